```python
import math
import jax, jax.numpy as jnp
from jax import lax
import numpy as np

D_MODEL = 1024
BATCH = 8
SEQ = 4096
DEPTH = 2
DEC_BATCH = 128
DEC_SEQ = 4
PAST_LEN = 16384
PAGE_SIZE = 128

N_EVEN = (DEPTH + 1) // 2
N_ODD = DEPTH // 2
CONV_K = 3
EPS = 1e-6
A_WIDTH = D_MODEL // 2
B_HEADS = 8
B_KV_HEADS = 2
B_GROUP = B_HEADS // B_KV_HEADS
B_HEAD_DIM = (D_MODEL // 2) // B_HEADS
WINDOW = 128
AB_SPLITS = [A_WIDTH, 2 * A_WIDTH, 3 * A_WIDTH, 3 * A_WIDTH + B_HEADS * B_HEAD_DIM,
             3 * A_WIDTH + (B_HEADS + B_KV_HEADS) * B_HEAD_DIM]
D_IN_AB = 3 * A_WIDTH + (B_HEADS + 2 * B_KV_HEADS) * B_HEAD_DIM
D_MIX_AB = A_WIDTH + B_HEADS * B_HEAD_DIM
C_HEADS = 8
C_DK = 128
C_DV = D_MODEL // C_HEADS
C_DK_T = C_HEADS * C_DK
C_DV_T = C_HEADS * C_DV
D_IN_C = 2 * C_DK_T + 2 * C_DV_T
HGRN_CHUNK = 64
D_FF = ((8 * D_MODEL // 3 + 255) // 256) * 256

kernel_name = 'hybrid_shortconv_swa_hgrn2_convffn_step'


def rms_norm(x, g):
    x32 = x.astype(jnp.float32)
    y = x32 * lax.rsqrt(jnp.mean(x32 * x32, axis=-1, keepdims=True) + EPS) * g.astype(jnp.float32)
    return y.astype(x.dtype)


def causal_dwconv(u, prev, w):
    L = u.shape[1]
    full = jnp.concatenate([prev.astype(u.dtype), u], axis=1)
    y = full[:, 0:L] * w[0]
    for j in range(1, CONV_K):
        y = y + full[:, j:j + L] * w[j]
    return y, full[:, L:]


def alibi_slopes(n):
    return jnp.exp2(-8.0 * jnp.arange(1, n + 1, dtype=jnp.float32) / n)


def window_attention(qb, kb, vb, qpos, kpos, sinks):
    f32 = jnp.float32
    s = jnp.einsum('bnqkgd,bnskd->bnkgqs', qb.astype(f32), kb.astype(f32)) * (B_HEAD_DIM ** -0.5)
    dist = qpos[:, :, None] - kpos[:, None, :]
    valid = (kpos[:, None, :] >= 0) & (dist >= 0) & (dist < WINDOW)
    slopes = alibi_slopes(B_HEADS).reshape(B_KV_HEADS, B_GROUP)
    s = s - slopes[None, None, :, :, None, None] * dist.astype(f32)[None, :, None, None]
    s = jnp.where(valid[None, :, None, None], s, -jnp.inf)
    sink = sinks.astype(f32).reshape(1, 1, B_KV_HEADS, B_GROUP, 1)
    m = jnp.maximum(jnp.max(s, axis=-1), sink)
    p = jnp.exp(s - m[..., None])
    denom = jnp.sum(p, axis=-1) + jnp.exp(sink - m)
    o = jnp.einsum('bnkgqs,bnskd->bnqkgd', p / denom[..., None], vb.astype(f32))
    return o


def mixer_ab(xn, conv_prev, k_buf, v_buf, w_in, conv_w, sinks, w_out):
    Bsz, L, _ = xn.shape
    h, gb, gc, q, k, v = jnp.split(xn @ w_in, AB_SPLITS, axis=-1)
    ya, new_conv = causal_dwconv(gc * h, conv_prev, conv_w)
    ya = gb * ya
    q = q.reshape(Bsz, L, B_KV_HEADS, B_GROUP, B_HEAD_DIM)
    k = k.reshape(Bsz, L, B_KV_HEADS, B_HEAD_DIM)
    v = v.reshape(Bsz, L, B_KV_HEADS, B_HEAD_DIM)
    if k_buf is None:
        nb = L // WINDOW
        qb = q.reshape(Bsz, nb, WINDOW, B_KV_HEADS, B_GROUP, B_HEAD_DIM)
        pad = jnp.zeros((Bsz, WINDOW, B_KV_HEADS, B_HEAD_DIM), k.dtype)
        kp = jnp.concatenate([pad, k], axis=1).reshape(Bsz, nb + 1, WINDOW, B_KV_HEADS, B_HEAD_DIM)
        vp = jnp.concatenate([pad, v], axis=1).reshape(Bsz, nb + 1, WINDOW, B_KV_HEADS, B_HEAD_DIM)
        kb = jnp.concatenate([kp[:, :-1], kp[:, 1:]], axis=2)
        vb = jnp.concatenate([vp[:, :-1], vp[:, 1:]], axis=2)
        start = jnp.arange(nb, dtype=jnp.int32)[:, None] * WINDOW
        qpos = start + jnp.arange(WINDOW, dtype=jnp.int32)[None]
        kpos = start - WINDOW + jnp.arange(2 * WINDOW, dtype=jnp.int32)[None]
        keep = min(WINDOW, L)
        new_k, new_v = k[:, L - keep:], v[:, L - keep:]
    else:
        Wb = k_buf.shape[1]
        kc = jnp.concatenate([k_buf.astype(k.dtype), k], axis=1)
        vc = jnp.concatenate([v_buf.astype(v.dtype), v], axis=1)
        qb, kb, vb = q[:, None], kc[:, None], vc[:, None]
        qpos = (PAST_LEN + jnp.arange(L, dtype=jnp.int32))[None]
        kpos = (PAST_LEN - Wb + jnp.arange(Wb + L, dtype=jnp.int32))[None]
        new_k, new_v = kc[:, L:], vc[:, L:]
    yb = window_attention(qb, kb, vb, qpos, kpos, sinks).reshape(Bsz, L, B_HEADS * B_HEAD_DIM).astype(xn.dtype)
    out = jnp.concatenate([ya, yb], axis=-1) @ w_out
    return out, new_conv, new_k, new_v


def gla_chunked(q, k, v, log_f, S0):
    Bsz, L, H, DK = q.shape
    DV = v.shape[-1]
    C = math.gcd(HGRN_CHUNK, L)
    n = L // C
    def to_chunks(a):
        return a.reshape(Bsz, n, C, H, a.shape[-1]).transpose(1, 0, 3, 2, 4)
    causal = jnp.tril(jnp.ones((C, C), dtype=bool))
    def step(S, inp):
        qc, kc, vc, lc = inp
        G = jnp.cumsum(lc, axis=2)
        diff = G[:, :, :, None, :] - G[:, :, None, :, :]
        decay = jnp.exp(jnp.where(causal[:, :, None], diff, -jnp.inf))
        A = jnp.einsum('bhtd,bhsd,bhtsd->bhts', qc, kc, decay)
        o = jnp.einsum('bhts,bhse->bhte', A, vc) + jnp.einsum('bhtd,bhde->bhte', qc * jnp.exp(G), S)
        Gl = G[:, :, -1]
        S = jnp.exp(Gl)[..., None] * S + jnp.einsum('bhsd,bhse->bhde', kc * jnp.exp(Gl[:, :, None] - G), vc)
        return S, o
    S, o = lax.scan(step, S0, (to_chunks(q), to_chunks(k), to_chunks(v), to_chunks(log_f)))
    o = o.transpose(1, 0, 3, 2, 4).reshape(Bsz, L, H, DV)
    return o, S


def hgrn_lower_bounds(logits):
    pr = jax.nn.softmax(logits.astype(jnp.float32), axis=0)
    return jnp.clip(jnp.cumsum(pr, axis=0) - pr[0:1], 0.0, 1.0)


def mixer_c(xn, S0, w_in, lb, norm_g, w_out):
    Bsz, L, _ = xn.shape
    f32 = jnp.float32
    q, f, i, g = jnp.split(xn @ w_in, [C_DK_T, 2 * C_DK_T, 2 * C_DK_T + C_DV_T], axis=-1)
    fl = f.astype(f32)
    log_f = jnp.logaddexp(jnp.log(lb), jnp.log1p(-lb) + jax.nn.log_sigmoid(fl))
    k_in = (1.0 - lb) * jax.nn.sigmoid(-fl)
    shp = (Bsz, L, C_HEADS, C_DK)
    o, S = gla_chunked(q.astype(f32).reshape(shp), k_in.reshape(shp),
                       i.astype(f32).reshape(Bsz, L, C_HEADS, C_DV), log_f.reshape(shp), S0.astype(f32))
    o = rms_norm(o, norm_g).reshape(Bsz, L, C_DV_T) * jax.nn.silu(g.astype(f32))
    return o.astype(xn.dtype) @ w_out, S.astype(S0.dtype)


def conv_ffn(xn, prev, w_up, conv_w, w_down):
    a, b = jnp.split(xn @ w_up, 2, axis=-1)
    a, new_prev = causal_dwconv(a, prev, conv_w)
    return (jax.nn.gelu(a) * b) @ w_down, new_prev


def run_trunk(x, conv_a_prev, k_buf, v_buf, hgrn_prev, ffn_prev, p):
    conv_new, k_new, v_new, s_new, ffn_new = [], [], [], [], []
    lbs = hgrn_lower_bounds(p['hgrn_lb_logits'])
    for l in range(DEPTH):
        j = l // 2
        h = rms_norm(x, p['norm_mix_pre'][l])
        if l % 2 == 0:
            mix, c, kk, vv = mixer_ab(h, conv_a_prev[j], None if k_buf is None else k_buf[j],
                                      None if v_buf is None else v_buf[j], p['w_in_ab'][j],
                                      p['conv_a_w'][j], p['attn_sinks'][j], p['w_out_ab'][j])
            conv_new.append(c); k_new.append(kk); v_new.append(vv)
        else:
            mix, S = mixer_c(h, hgrn_prev[j], p['w_in_c'][j], lbs[l], p['hgrn_norm_g'][j], p['w_out_c'][j])
            s_new.append(S)
        x = x + rms_norm(mix, p['norm_mix_post'][l])
        h = rms_norm(x, p['norm_ffn_pre'][l])
        fo, fs = conv_ffn(h, ffn_prev[l], p['w_ffn_up'][l], p['ffn_conv_w'][l], p['w_ffn_down'][l])
        ffn_new.append(fs)
        x = x + rms_norm(fo, p['norm_ffn_post'][l])
    return x, jnp.stack(conv_new), jnp.stack(k_new), jnp.stack(v_new), jnp.stack(s_new), jnp.stack(ffn_new)


def setup_inputs(seed: int = 0) -> dict:
    key = jax.random.key(seed)
    ks = jax.random.split(key, 24)
    f32 = jnp.float32
    def nrm(k, shape, scale=1.0):
        return scale * jax.random.normal(k, shape, f32)
    win_buf = min(WINDOW, PAST_LEN)
    return {
        'x_prompt': nrm(ks[0], (BATCH, SEQ, D_MODEL)),
        'x_sample': nrm(ks[1], (DEC_BATCH, DEC_SEQ, D_MODEL)),
        'state_conv_a': nrm(ks[2], (N_EVEN, DEC_BATCH, CONV_K - 1, A_WIDTH)),
        'cache_swa_k': nrm(ks[3], (N_EVEN, DEC_BATCH, win_buf, B_KV_HEADS, B_HEAD_DIM)),
        'cache_swa_v': nrm(ks[4], (N_EVEN, DEC_BATCH, win_buf, B_KV_HEADS, B_HEAD_DIM)),
        'state_hgrn': nrm(ks[5], (N_ODD, DEC_BATCH, C_HEADS, C_DK, C_DV), 0.3),
        'state_ffn_conv': nrm(ks[6], (DEPTH, DEC_BATCH, CONV_K - 1, D_FF), 0.5),
        'norm_mix_pre': 1.0 + nrm(ks[7], (DEPTH, D_MODEL), 0.02),
        'norm_mix_post': 1.0 + nrm(ks[8], (DEPTH, D_MODEL), 0.02),
        'norm_ffn_pre': 1.0 + nrm(ks[9], (DEPTH, D_MODEL), 0.02),
        'norm_ffn_post': 1.0 + nrm(ks[10], (DEPTH, D_MODEL), 0.02),
        'w_in_ab': nrm(ks[11], (N_EVEN, D_MODEL, D_IN_AB), D_MODEL ** -0.5),
        'conv_a_w': nrm(ks[12], (N_EVEN, CONV_K, A_WIDTH), CONV_K ** -0.5),
        'attn_sinks': nrm(ks[13], (N_EVEN, B_HEADS)),
        'w_out_ab': nrm(ks[14], (N_EVEN, D_MIX_AB, D_MODEL), D_MIX_AB ** -0.5),
        'w_in_c': nrm(ks[15], (N_ODD, D_MODEL, D_IN_C), D_MODEL ** -0.5),
        'hgrn_lb_logits': nrm(ks[16], (DEPTH, C_DK_T)),
        'hgrn_norm_g': 1.0 + nrm(ks[17], (N_ODD, C_DV), 0.02),
        'w_out_c': nrm(ks[18], (N_ODD, C_DV_T, D_MODEL), C_DV_T ** -0.5),
        'w_ffn_up': nrm(ks[19], (DEPTH, D_MODEL, 2 * D_FF), D_MODEL ** -0.5),
        'ffn_conv_w': nrm(ks[20], (DEPTH, CONV_K, D_FF), CONV_K ** -0.5),
        'w_ffn_down': nrm(ks[21], (DEPTH, D_FF, D_MODEL), D_FF ** -0.5),
    }


def reference(x_prompt, x_sample, state_conv_a, cache_swa_k, cache_swa_v, state_hgrn, state_ffn_conv,
              norm_mix_pre, norm_mix_post, norm_ffn_pre, norm_ffn_post, w_in_ab, conv_a_w, attn_sinks,
              w_out_ab, w_in_c, hgrn_lb_logits, hgrn_norm_g, w_out_c, w_ffn_up, ffn_conv_w, w_ffn_down):
    p = dict(norm_mix_pre=norm_mix_pre, norm_mix_post=norm_mix_post, norm_ffn_pre=norm_ffn_pre,
             norm_ffn_post=norm_ffn_post, w_in_ab=w_in_ab, conv_a_w=conv_a_w, attn_sinks=attn_sinks,
             w_out_ab=w_out_ab, w_in_c=w_in_c, hgrn_lb_logits=hgrn_lb_logits, hgrn_norm_g=hgrn_norm_g,
             w_out_c=w_out_c, w_ffn_up=w_ffn_up, ffn_conv_w=ffn_conv_w, w_ffn_down=w_ffn_down)
    Bp = x_prompt.shape[0]
    dt = x_prompt.dtype
    y_prompt, conv_p, k_p, v_p, hgrn_p, ffn_p = run_trunk(
        x_prompt,
        jnp.zeros((N_EVEN, Bp, CONV_K - 1, A_WIDTH), dt), None, None,
        jnp.zeros((N_ODD, Bp, C_HEADS, C_DK, C_DV), dt),
        jnp.zeros((DEPTH, Bp, CONV_K - 1, D_FF), dt), p)
    y_sample, conv_s, k_s, v_s, hgrn_s, ffn_s = run_trunk(
        x_sample, state_conv_a, cache_swa_k, cache_swa_v, state_hgrn, state_ffn_conv, p)
    return (y_prompt, y_sample, conv_p, conv_s, k_p, k_s, v_p, v_s, hgrn_p, hgrn_s, ffn_p, ffn_s)
```

```python
import functools
import math

import jax
import jax.numpy as jnp
from jax import lax
from jax.experimental import pallas as pl
from jax.experimental.pallas import tpu as pltpu

F32 = jnp.float32
BF16 = jnp.bfloat16

EPS = 1e-6
WINDOW = 128
KV_HEADS = 2
GROUP = 4
HEAD_DIM = 64
C_HEADS = 8
C_DK = 128
NEG = -1e30
SUBLANES = 8
PROMPT_TILE = 512
HGRN_CHUNK = 64
FFN_CHUNK = 256
SAMPLE_SEQ_BLOCK = 8
VMEM_LIMIT_V7X = 56 * 2**20


def _dot(a, b):
    return jnp.dot(a, b, preferred_element_type=F32)


def _dot_nt(a, b):
    return lax.dot_general(a, b, (((1,), (1,)), ((), ())), preferred_element_type=F32)


def _dot_tn(a, b):
    return lax.dot_general(a, b, (((0,), (0,)), ((), ())), preferred_element_type=F32)


def _rms(x, g):
    return x * lax.rsqrt(jnp.mean(x * x, axis=-1, keepdims=True) + EPS) * g


def _gelu_tanh(x):
    c = math.sqrt(2.0 / math.pi)
    return x * (0.5 * (1.0 + jnp.tanh(c * (x + 0.044715 * (x * x * x)))))


def _sigmoid(x):
    return 1.0 / (1.0 + jnp.exp(-x))


def _resident(shape):
    nd = len(shape)
    return pl.BlockSpec(shape, lambda *_: (0,) * nd, pipeline_mode=pl.Buffered(1))


def _params(sem):
    return pltpu.CompilerParams(dimension_semantics=sem, vmem_limit_bytes=VMEM_LIMIT_V7X)


def _head_slope(h):
    return 2.0 ** (-8.0 * (h + 1) / (KV_HEADS * GROUP))


def _softmax_pv(sc, sink, v_bf16):
    m = jnp.maximum(jnp.max(sc, axis=-1, keepdims=True), sink)
    p = jnp.exp(sc - m)
    den = jnp.sum(p, axis=-1, keepdims=True) + jnp.exp(sink - m)
    return _dot(p.astype(BF16), v_bf16) / den


def _mix_ab_prompt_kernel(x_ref, gpre_ref, gpost_ref, win_ref, cw_ref, sink_ref, wout_ref,
                          y_ref, conv_ref, kout_ref, vout_ref,
                          ubuf, qbuf, kbuf, vbuf, mixbuf):
    s = pl.program_id(1)
    ts = x_ref.shape[0]
    aw = cw_ref.shape[1]
    nq = GROUP * KV_HEADS * HEAD_DIM
    nkv = KV_HEADS * HEAD_DIM

    @pl.when(s == 0)
    def _():
        ubuf[0:SUBLANES, :] = jnp.zeros((SUBLANES, aw), F32)
        kbuf[0:WINDOW, :] = jnp.zeros((WINDOW, nkv), BF16)
        vbuf[0:WINDOW, :] = jnp.zeros((WINDOW, nkv), BF16)

    x = x_ref[...]
    h = _rms(x, gpre_ref[...]).astype(BF16)

    u = _dot(h, win_ref[:, 2 * aw:3 * aw]) * _dot(h, win_ref[:, 0:aw])
    ubuf[SUBLANES:SUBLANES + ts, :] = u
    cw = cw_ref[...]
    conv = (cw[2:3, :] * u + cw[1:2, :] * ubuf[SUBLANES - 1:SUBLANES - 1 + ts, :]
            + cw[0:1, :] * ubuf[SUBLANES - 2:SUBLANES - 2 + ts, :])
    mixbuf[:, 0:aw] = (_dot(h, win_ref[:, aw:2 * aw]) * conv).astype(BF16)
    tail = ubuf[ts + SUBLANES - 2:ts + SUBLANES, :]
    conv_ref[...] = tail
    ubuf[SUBLANES - 2:SUBLANES, :] = tail

    q0 = 3 * aw
    q = _dot(h, win_ref[:, q0:q0 + nq]) * (HEAD_DIM ** -0.5)
    lo_q = lax.broadcasted_iota(jnp.int32, (1, nq), 1) % nkv < HEAD_DIM
    qbuf[0] = jnp.where(lo_q, q, 0.0).astype(BF16)
    qbuf[1] = jnp.where(lo_q, 0.0, q).astype(BF16)
    k = _dot(h, win_ref[:, q0 + nq:q0 + nq + nkv])
    v = _dot(h, win_ref[:, q0 + nq + nkv:q0 + nq + 2 * nkv])
    kbuf[WINDOW:WINDOW + ts, :] = k.astype(BF16)
    vbuf[WINDOW:WINDOW + ts, :] = v.astype(BF16)
    kout_ref[...] = k[ts - WINDOW:, :]
    vout_ref[...] = v[ts - WINDOW:, :]

    qi = lax.broadcasted_iota(jnp.int32, (WINDOW, 2 * WINDOW), 0)
    kj = lax.broadcasted_iota(jnp.int32, (WINDOW, 2 * WINDOW), 1)
    dist = qi + WINDOW - kj
    valid = (dist >= 0) & (dist < WINDOW)
    negdist = jnp.where(valid, -dist.astype(F32), 0.0)
    maskadd = jnp.where(valid, 0.0, NEG)
    first = jnp.where(kj < WINDOW, jnp.where(s == 0, NEG, 0.0), 0.0)
    lane = lax.broadcasted_iota(jnp.int32, (1, 2 * HEAD_DIM), 1)
    lo = lane < HEAD_DIM

    for n in range(ts // WINDOW):
        r0 = n * WINDOW
        kc = kbuf[r0:r0 + 2 * WINDOW, :]
        vc = vbuf[r0:r0 + 2 * WINDOW, :]
        for g in range(GROUP):
            outs = []
            for kv in range(KV_HEADS):
                hidx = kv * GROUP + g
                qm = qbuf[kv, r0:r0 + WINDOW, g * nkv:(g + 1) * nkv]
                sc = _dot_nt(qm, kc) + (_head_slope(hidx) * negdist + maskadd)
                if n == 0:
                    sc = sc + first
                outs.append(_softmax_pv(sc, sink_ref[:, hidx:hidx + 1], vc))
            og = jnp.where(lo, outs[0], outs[1])
            mixbuf[r0:r0 + WINDOW, aw + g * 2 * HEAD_DIM:aw + (g + 1) * 2 * HEAD_DIM] = og.astype(BF16)

    kbuf[0:WINDOW, :] = kbuf[ts:ts + WINDOW, :]
    vbuf[0:WINDOW, :] = vbuf[ts:ts + WINDOW, :]

    out = _dot(mixbuf[...], wout_ref[...])
    y_ref[...] = x + _rms(out, gpost_ref[...])


def _mix_ab_prompt(x, gpre, gpost, win, cw, sinks, wout):
    b, seq, d = x.shape
    ts = PROMPT_TILE
    aw = cw.shape[1]
    nq = GROUP * KV_HEADS * HEAD_DIM
    nkv = KV_HEADS * HEAD_DIM
    tile = pl.BlockSpec((None, ts, d), lambda i, j: (i, j, 0))
    return pl.pallas_call(
        _mix_ab_prompt_kernel,
        grid=(b, seq // ts),
        in_specs=[tile, _resident(gpre.shape), _resident(gpost.shape), _resident(win.shape),
                  _resident(cw.shape), _resident(sinks.shape), _resident(wout.shape)],
        out_specs=[tile,
                   pl.BlockSpec((None, 2, aw), lambda i, j: (i, 0, 0)),
                   pl.BlockSpec((None, WINDOW, nkv), lambda i, j: (i, 0, 0)),
                   pl.BlockSpec((None, WINDOW, nkv), lambda i, j: (i, 0, 0))],
        out_shape=[jax.ShapeDtypeStruct((b, seq, d), F32),
                   jax.ShapeDtypeStruct((b, 2, aw), F32),
                   jax.ShapeDtypeStruct((b, WINDOW, nkv), F32),
                   jax.ShapeDtypeStruct((b, WINDOW, nkv), F32)],
        scratch_shapes=[pltpu.VMEM((ts + SUBLANES, aw), F32),
                        pltpu.VMEM((KV_HEADS, ts, nq), BF16),
                        pltpu.VMEM((ts + WINDOW, nkv), BF16),
                        pltpu.VMEM((ts + WINDOW, nkv), BF16),
                        pltpu.VMEM((ts, d), BF16)],
        compiler_params=_params(("arbitrary", "arbitrary")),
        name="mix_ab_prompt",
    )(x, gpre, gpost, win, cw, sinks, wout)


def _ffn_prompt_kernel(x_ref, gpre_ref, gpost_ref, wup_ref, cw_ref, wdown_ref,
                       y_ref, st_ref, abuf, gbuf):
    s = pl.program_id(1)
    rows = x_ref.shape[0]
    ff = cw_ref.shape[1]

    @pl.when(s == 0)
    def _():
        abuf[0:SUBLANES, :] = jnp.zeros((SUBLANES, ff), F32)

    x = x_ref[...]
    h = _rms(x, gpre_ref[...]).astype(BF16)
    for c in range(ff // FFN_CHUNK):
        c0 = c * FFN_CHUNK
        a = _dot(h, wup_ref[:, c0:c0 + FFN_CHUNK])
        b = _dot(h, wup_ref[:, ff + c0:ff + c0 + FFN_CHUNK])
        abuf[SUBLANES:SUBLANES + rows, c0:c0 + FFN_CHUNK] = a
        cw = cw_ref[:, c0:c0 + FFN_CHUNK]
        conv = (cw[2:3, :] * a + cw[1:2, :] * abuf[SUBLANES - 1:SUBLANES - 1 + rows, c0:c0 + FFN_CHUNK]
                + cw[0:1, :] * abuf[SUBLANES - 2:SUBLANES - 2 + rows, c0:c0 + FFN_CHUNK])
        tail = a[rows - 2:rows, :]
        st_ref[:, c0:c0 + FFN_CHUNK] = tail
        abuf[SUBLANES - 2:SUBLANES, c0:c0 + FFN_CHUNK] = tail
        gbuf[:, c0:c0 + FFN_CHUNK] = (_gelu_tanh(conv) * b).astype(BF16)
    out = _dot(gbuf[...], wdown_ref[...])
    y_ref[...] = x + _rms(out, gpost_ref[...])


def _ffn_prompt(x, gpre, gpost, wup, cw, wdown):
    b, seq, d = x.shape
    ts = PROMPT_TILE
    ff = cw.shape[1]
    tile = pl.BlockSpec((None, ts, d), lambda i, j: (i, j, 0))
    return pl.pallas_call(
        _ffn_prompt_kernel,
        grid=(b, seq // ts),
        in_specs=[tile, _resident(gpre.shape), _resident(gpost.shape), _resident(wup.shape),
                  _resident(cw.shape), _resident(wdown.shape)],
        out_specs=[tile, pl.BlockSpec((None, 2, ff), lambda i, j: (i, 0, 0))],
        out_shape=[jax.ShapeDtypeStruct((b, seq, d), F32), jax.ShapeDtypeStruct((b, 2, ff), F32)],
        scratch_shapes=[pltpu.VMEM((ts + SUBLANES, ff), F32), pltpu.VMEM((ts, ff), BF16)],
        compiler_params=_params(("arbitrary", "arbitrary")),
        name="ffn_prompt",
    )(x, gpre, gpost, wup, cw, wdown)


def _ffn_sample_kernel(seg, x_ref, p1_ref, p2_ref, gpre_ref, gpost_ref, wa_ref, wb_ref, cw_ref, wdown_ref,
                       y_ref, a_ref, hbuf, abuf, acc):
    c = pl.program_id(0)
    rows = x_ref.shape[0]

    @pl.when(c == 0)
    def _():
        hbuf[...] = _rms(x_ref[...], gpre_ref[...]).astype(BF16)
        acc[...] = jnp.zeros(acc.shape, F32)
        abuf[0:SUBLANES, :] = jnp.zeros((SUBLANES, abuf.shape[1]), F32)

    h = hbuf[...]
    t = lax.broadcasted_iota(jnp.int32, (rows, 1), 0) % seg
    a = _dot(h, wa_ref[...])
    b = _dot(h, wb_ref[...])
    a_ref[...] = a
    abuf[SUBLANES:SUBLANES + rows, :] = a
    am1 = jnp.where(t >= 1, abuf[SUBLANES - 1:SUBLANES - 1 + rows, :], p1_ref[...])
    am2 = jnp.where(t >= 2, abuf[SUBLANES - 2:SUBLANES - 2 + rows, :], p2_ref[...])
    cw = cw_ref[...]
    conv = cw[2:3, :] * a + cw[1:2, :] * am1 + cw[0:1, :] * am2
    acc[...] += _dot((_gelu_tanh(conv) * b).astype(BF16), wdown_ref[...])

    @pl.when(c == pl.num_programs(0) - 1)
    def _():
        y_ref[...] = x_ref[...] + _rms(acc[...], gpost_ref[...])


def _ffn_sample(x, prev, seg, gpre, gpost, wup, cw, wdown):
    rows, d = x.shape
    ff = cw.shape[1]
    cf = FFN_CHUNK
    nc = ff // cf
    p1 = jnp.pad(prev[:, 1:2], ((0, 0), (0, seg - 1), (0, 0))).reshape(rows, ff)
    p2 = jnp.pad(prev, ((0, 0), (0, seg - 2), (0, 0))).reshape(rows, ff)
    cols = lambda r: pl.BlockSpec((r, cf), lambda c: (0, c))
    y, a = pl.pallas_call(
        functools.partial(_ffn_sample_kernel, seg),
        grid=(nc,),
        in_specs=[_resident(x.shape), cols(rows), cols(rows), _resident(gpre.shape), _resident(gpost.shape),
                  cols(d), pl.BlockSpec((d, cf), lambda c: (0, c + nc)), cols(3),
                  pl.BlockSpec((cf, d), lambda c: (c, 0))],
        out_specs=[pl.BlockSpec((rows, d), lambda c: (0, 0)), cols(rows)],
        out_shape=[jax.ShapeDtypeStruct((rows, d), F32), jax.ShapeDtypeStruct((rows, ff), F32)],
        scratch_shapes=[pltpu.VMEM((rows, d), BF16), pltpu.VMEM((rows + SUBLANES, cf), F32),
                        pltpu.VMEM((rows, d), F32)],
        compiler_params=_params(("arbitrary",)),
        name="ffn_sample",
    )(x, p1, p2, gpre, gpost, wup, wup, cw, wdown)
    return y, a.reshape(rows // seg, seg, ff)[:, seg - 2:]


def _lower_bound_terms(logits_ref, layer):
    lg = logits_ref[...]
    e = jnp.exp(lg - jnp.max(lg, axis=0, keepdims=True))
    pr = e / jnp.sum(e, axis=0, keepdims=True)
    cum = pr[0:1, :]
    for r in range(1, layer + 1):
        cum = cum + pr[r:r + 1, :]
    lb = jnp.clip(cum - pr[0:1, :], 0.0, 1.0)
    return jnp.log(lb), jnp.log1p(-lb), 1.0 - lb


def _log_forget(f, log_lb, log1m_lb):
    log_sig = -(jnp.maximum(-f, 0.0) + jnp.log1p(jnp.exp(-jnp.abs(f))))
    b = log1m_lb + log_sig
    return jnp.maximum(log_lb, b) + jnp.log1p(jnp.exp(-jnp.abs(log_lb - b)))


def _split_bf16(x):
    hi = x.astype(BF16)
    return hi, (x - hi.astype(F32)).astype(BF16)


def _hgrn_gate_out(x, h, ob, mixbuf, win_ref, ng_ref, wout_ref, gpost_ref):
    dkt = C_HEADS * C_DK
    g = _dot(h, win_ref[:, 3 * dkt:4 * dkt])
    ng = ng_ref[...]
    for hh in range(C_HEADS):
        sl = slice(hh * C_DK, (hh + 1) * C_DK)
        mixbuf[:, sl] = (_rms(ob[:, sl], ng) * (g[:, sl] * _sigmoid(g[:, sl]))).astype(BF16)
    out = _dot(mixbuf[...], wout_ref[...])
    return x + _rms(out, gpost_ref[...])


def _hgrn_prompt_kernel(layer, x_ref, gpre_ref, gpost_ref, win_ref, lbl_ref, ng_ref, wout_ref,
                        y_ref, sout_ref, st, qb, kb, vb, lfb, ob, mixbuf):
    s = pl.program_id(1)
    ts = x_ref.shape[0]
    dkt = C_HEADS * C_DK
    ck = HGRN_CHUNK

    @pl.when(s == 0)
    def _():
        st[...] = jnp.zeros(st.shape, F32)

    x = x_ref[...]
    h = _rms(x, gpre_ref[...]).astype(BF16)
    log_lb, log1m_lb, one_m_lb = _lower_bound_terms(lbl_ref, layer)
    f = _dot(h, win_ref[:, dkt:2 * dkt])
    lfb[...] = _log_forget(f, log_lb, log1m_lb)
    kb[...] = one_m_lb * _sigmoid(-f)
    qb[...] = _dot(h, win_ref[:, 0:dkt])
    vb[...] = _dot(h, win_ref[:, 2 * dkt:3 * dkt]).astype(BF16)

    ri = lax.broadcasted_iota(jnp.int32, (ck, ck), 0)
    ci = lax.broadcasted_iota(jnp.int32, (ck, ck), 1)
    causal = ri >= ci
    tri = jnp.where(causal, 1.0, 0.0).astype(BF16)
    mid = ck // 2 - 1

    def chunk(c, carry):
        r0 = pl.multiple_of(c * ck, ck)
        hi, lo = _split_bf16(lfb[pl.ds(r0, ck), :])
        gcum = _dot(tri, hi) + _dot(tri, lo)
        for hh in range(C_HEADS):
            sl = slice(hh * C_DK, (hh + 1) * C_DK)
            gh = gcum[:, sl]
            gr = gh[mid:mid + 1, :]
            gl = gh[ck - 1:ck, :]
            qh = qb[pl.ds(r0, ck), sl]
            kh = kb[pl.ds(r0, ck), sl]
            vh = vb[pl.ds(r0, ck), sl]
            qt = (qh * jnp.exp(gh - gr)).astype(BF16)
            kt = (kh * jnp.exp(gr - gh)).astype(BF16)
            att = jnp.where(causal, _dot_nt(qt, kt), 0.0).astype(BF16)
            st_h = st[hh]
            qi = (qh * jnp.exp(gh)).astype(BF16)
            ob[pl.ds(r0, ck), sl] = _dot(att, vh) + _dot_nt(qi, st_h.astype(BF16))
            kl = (kh * jnp.exp(gl - gh)).astype(BF16)
            st[hh] = jnp.exp(gl) * st_h + _dot_tn(vh, kl)
        return carry

    lax.fori_loop(0, ts // ck, chunk, 0)

    y_ref[...] = _hgrn_gate_out(x, h, ob, mixbuf, win_ref, ng_ref, wout_ref, gpost_ref)

    @pl.when(s == pl.num_programs(1) - 1)
    def _():
        for hh in range(C_HEADS):
            sout_ref[hh] = st[hh].T


def _hgrn_prompt(layer, x, gpre, gpost, win, lbl, ng, wout):
    b, seq, d = x.shape
    ts = PROMPT_TILE
    dkt = C_HEADS * C_DK
    tile = pl.BlockSpec((None, ts, d), lambda i, j: (i, j, 0))
    return pl.pallas_call(
        functools.partial(_hgrn_prompt_kernel, layer),
        grid=(b, seq // ts),
        in_specs=[tile, _resident(gpre.shape), _resident(gpost.shape), _resident(win.shape),
                  _resident(lbl.shape), _resident(ng.shape), _resident(wout.shape)],
        out_specs=[tile, pl.BlockSpec((None, C_HEADS, C_DK, C_DK), lambda i, j: (i, 0, 0, 0))],
        out_shape=[jax.ShapeDtypeStruct((b, seq, d), F32),
                   jax.ShapeDtypeStruct((b, C_HEADS, C_DK, C_DK), F32)],
        scratch_shapes=[pltpu.VMEM((C_HEADS, C_DK, C_DK), F32),
                        pltpu.VMEM((ts, dkt), F32), pltpu.VMEM((ts, dkt), F32),
                        pltpu.VMEM((ts, dkt), BF16), pltpu.VMEM((ts, dkt), F32),
                        pltpu.VMEM((ts, dkt), F32), pltpu.VMEM((ts, dkt), BF16)],
        compiler_params=_params(("arbitrary", "arbitrary")),
        name="hgrn_prompt",
    )(x, gpre, gpost, win, lbl, ng, wout)


def _mix_ab_sample_kernel(seg, x_ref, p1_ref, p2_ref, kc_ref, vc_ref, gpre_ref, gpost_ref, win_ref,
                          cw_ref, sink_ref, wout_ref,
                          y_ref, u_ref, knew_ref, vnew_ref,
                          ubuf, qbuf, yabuf, ybbuf, kall, vall, bias_s, sink_s):
    i = pl.program_id(0)
    rows = x_ref.shape[0]
    aw = cw_ref.shape[1]
    nq = GROUP * KV_HEADS * HEAD_DIM
    nkv = KV_HEADS * HEAD_DIM
    pair = 2 * seg
    qrows = KV_HEADS * GROUP * pair
    tk = WINDOW + pair

    @pl.when(i == 0)
    def _():
        x = x_ref[...]
        h = _rms(x, gpre_ref[...]).astype(BF16)
        t = lax.broadcasted_iota(jnp.int32, (rows, 1), 0) % seg
        u = _dot(h, win_ref[:, 2 * aw:3 * aw]) * _dot(h, win_ref[:, 0:aw])
        u_ref[...] = u
        ubuf[0:SUBLANES, :] = jnp.zeros((SUBLANES, aw), F32)
        ubuf[SUBLANES:SUBLANES + rows, :] = u
        um1 = jnp.where(t >= 1, ubuf[SUBLANES - 1:SUBLANES - 1 + rows, :], p1_ref[...])
        um2 = jnp.where(t >= 2, ubuf[SUBLANES - 2:SUBLANES - 2 + rows, :], p2_ref[...])
        cw = cw_ref[...]
        conv = cw[2:3, :] * u + cw[1:2, :] * um1 + cw[0:1, :] * um2
        yabuf[...] = (_dot(h, win_ref[:, aw:2 * aw]) * conv).astype(BF16)
        q0 = 3 * aw
        qbuf[...] = _dot(h, win_ref[:, q0:q0 + nq]) * (HEAD_DIM ** -0.5)
        knew_ref[...] = _dot(h, win_ref[:, q0 + nq:q0 + nq + nkv])
        vnew_ref[...] = _dot(h, win_ref[:, q0 + nq + nkv:q0 + nq + 2 * nkv])

        r = lax.broadcasted_iota(jnp.int32, (qrows, tk), 0)
        j = lax.broadcasted_iota(jnp.int32, (qrows, tk), 1)
        tq = r % seg
        second = (r // seg) % 2
        jn = j - WINDOW - seg * second
        dist = jnp.where(j < WINDOW, tq + WINDOW - j, tq - jn)
        valid = (dist >= 0) & (dist < WINDOW) & ((j < WINDOW) | ((jn >= 0) & (jn < seg)))
        rc = lax.broadcasted_iota(jnp.int32, (qrows, 1), 0)
        head = (rc // (GROUP * pair)) * GROUP + (rc // pair) % GROUP
        slope = jnp.zeros((qrows, 1), F32)
        sink = jnp.zeros((qrows, 1), F32)
        for hidx in range(KV_HEADS * GROUP):
            slope = jnp.where(head == hidx, _head_slope(hidx), slope)
            sink = jnp.where(head == hidx, sink_ref[:, hidx:hidx + 1], sink)
        bias_s[...] = jnp.where(valid, -slope * dist.astype(F32), NEG)
        sink_s[...] = sink

    nblk = kc_ref.shape[0]
    lane = lax.broadcasted_iota(jnp.int32, (1, nkv), 1)
    lo = lane < HEAD_DIM
    rq = lax.broadcasted_iota(jnp.int32, (qrows, 1), 0)
    first_seq = (rq // seg) % 2 == 0
    kv0 = rq < GROUP * pair

    def body(p, carry):
        r0 = pl.multiple_of((i * nblk + 2 * p) * seg, pair)
        q8 = qbuf[pl.ds(r0, pair), :]
        qg = jnp.concatenate([q8[:, g * nkv:(g + 1) * nkv] for g in range(GROUP)], axis=0)
        qs = jnp.concatenate([jnp.where(lo, qg, 0.0), jnp.where(lo, 0.0, qg)], axis=0).astype(BF16)
        knew = knew_ref[pl.ds(r0, pair), :]
        vnew = vnew_ref[pl.ds(r0, pair), :]
        sc, ov = [], []
        for a in range(2):
            kall[a, 0:WINDOW, :] = kc_ref[2 * p + a]
            kall[a, WINDOW:tk, :] = knew
            vall[a, 0:WINDOW, :] = vc_ref[2 * p + a]
            vall[a, WINDOW:tk, :] = vnew
            sc.append(_dot_nt(qs, kall[a].astype(BF16)))
        s2 = jnp.where(first_seq, sc[0], sc[1]) + bias_s[...]
        sink = sink_s[...]
        m = jnp.maximum(jnp.max(s2, axis=-1, keepdims=True), sink)
        pr = jnp.exp(s2 - m)
        den = jnp.sum(pr, axis=-1, keepdims=True) + jnp.exp(sink - m)
        prb = pr.astype(BF16)
        for a in range(2):
            ov.append(_dot(prb, vall[a].astype(BF16)))
        o = jnp.where(first_seq, ov[0], ov[1]) / den
        og = jnp.where(lo, o[0:GROUP * pair, :], o[GROUP * pair:, :])
        for g in range(GROUP):
            ybbuf[pl.ds(r0, pair), g * nkv:(g + 1) * nkv] = og[g * pair:(g + 1) * pair, :]
        return carry

    lax.fori_loop(0, nblk // 2, body, 0)

    @pl.when(i == pl.num_programs(0) - 1)
    def _():
        out = _dot(yabuf[...], wout_ref[0:aw, :]) + _dot(ybbuf[...].astype(BF16), wout_ref[aw:, :])
        y_ref[...] = x_ref[...] + _rms(out, gpost_ref[...])


def _mix_ab_sample(x, conv_prev, kcache, vcache, seg, gpre, gpost, win, cw, sinks, wout):
    rows, d = x.shape
    nseq = rows // seg
    aw = cw.shape[1]
    nq = GROUP * KV_HEADS * HEAD_DIM
    nkv = KV_HEADS * HEAD_DIM
    sb = SAMPLE_SEQ_BLOCK
    pair = 2 * seg
    qrows = KV_HEADS * GROUP * pair
    tk = WINDOW + pair
    p1 = jnp.pad(conv_prev[:, 1:2], ((0, 0), (0, seg - 1), (0, 0))).reshape(rows, aw)
    p2 = jnp.pad(conv_prev, ((0, 0), (0, seg - 2), (0, 0))).reshape(rows, aw)
    full = lambda shape: pl.BlockSpec(shape, lambda i: (0,) * len(shape))
    cache = pl.BlockSpec((sb, WINDOW, nkv), lambda i: (i, 0, 0))
    y, u, knew, vnew = pl.pallas_call(
        functools.partial(_mix_ab_sample_kernel, seg),
        grid=(nseq // sb,),
        in_specs=[_resident(x.shape), _resident(p1.shape), _resident(p2.shape), cache, cache,
                  _resident(gpre.shape), _resident(gpost.shape), _resident(win.shape), _resident(cw.shape),
                  _resident(sinks.shape), _resident(wout.shape)],
        out_specs=[full((rows, d)), full((rows, aw)), full((rows, nkv)), full((rows, nkv))],
        out_shape=[jax.ShapeDtypeStruct((rows, d), F32), jax.ShapeDtypeStruct((rows, aw), F32),
                   jax.ShapeDtypeStruct((rows, nkv), F32), jax.ShapeDtypeStruct((rows, nkv), F32)],
        scratch_shapes=[pltpu.VMEM((rows + SUBLANES, aw), F32),
                        pltpu.VMEM((rows, nq), F32),
                        pltpu.VMEM((rows, aw), BF16),
                        pltpu.VMEM((rows, nq), F32),
                        pltpu.VMEM((2, tk, nkv), F32),
                        pltpu.VMEM((2, tk, nkv), F32),
                        pltpu.VMEM((qrows, tk), F32),
                        pltpu.VMEM((qrows, 1), F32)],
        compiler_params=_params(("arbitrary",)),
        name="mix_ab_sample",
    )(x, p1, p2, kcache, vcache, gpre, gpost, win, cw, sinks, wout)
    conv_new = u.reshape(nseq, seg, aw)[:, seg - 2:]
    k_new = jnp.concatenate([kcache, knew.reshape(nseq, seg, nkv)], axis=1)[:, seg:]
    v_new = jnp.concatenate([vcache, vnew.reshape(nseq, seg, nkv)], axis=1)[:, seg:]
    return y, conv_new, k_new, v_new


def _hgrn_sample_kernel(layer, seg, x_ref, s0_ref, gpre_ref, gpost_ref, win_ref, lbl_ref, ng_ref, wout_ref,
                        y_ref, s1_ref, qb, kb, vb, gb, ob, mixbuf):
    i = pl.program_id(0)
    rows = x_ref.shape[0]
    dkt = C_HEADS * C_DK
    pair = 2 * seg

    @pl.when(i == 0)
    def _():
        x = x_ref[...]
        h = _rms(x, gpre_ref[...]).astype(BF16)
        log_lb, log1m_lb, one_m_lb = _lower_bound_terms(lbl_ref, layer)
        f = _dot(h, win_ref[:, dkt:2 * dkt])
        hi, lo = _split_bf16(_log_forget(f, log_lb, log1m_lb))
        ri = lax.broadcasted_iota(jnp.int32, (rows, rows), 0)
        ci = lax.broadcasted_iota(jnp.int32, (rows, rows), 1)
        tri = jnp.where((ri // seg == ci // seg) & (ri >= ci), 1.0, 0.0).astype(BF16)
        gb[...] = _dot(tri, hi) + _dot(tri, lo)
        kb[...] = one_m_lb * _sigmoid(-f)
        qb[...] = _dot(h, win_ref[:, 0:dkt])
        vb[...] = _dot(h, win_ref[:, 2 * dkt:3 * dkt])

    nblk = s0_ref.shape[0]
    rr = lax.broadcasted_iota(jnp.int32, (pair, 1), 0)
    first_seq = rr < seg
    second_seq = rr >= seg
    ri = lax.broadcasted_iota(jnp.int32, (pair, pair), 0)
    ci = lax.broadcasted_iota(jnp.int32, (pair, pair), 1)
    causal = (ri // seg == ci // seg) & (ri >= ci)
    mid = seg // 2 - 1

    def body(p, carry):
        r0 = pl.multiple_of((i * nblk + 2 * p) * seg, pair)
        for hh in range(C_HEADS):
            sl = slice(hh * C_DK, (hh + 1) * C_DK)
            gh = gb[pl.ds(r0, pair), sl]
            gr = jnp.where(first_seq, gh[mid:mid + 1, :], gh[seg + mid:seg + mid + 1, :])
            gl = jnp.where(first_seq, gh[seg - 1:seg, :], gh[pair - 1:pair, :])
            qh = qb[pl.ds(r0, pair), sl]
            kh = kb[pl.ds(r0, pair), sl]
            vh = vb[pl.ds(r0, pair), sl].astype(BF16)
            qt = (qh * jnp.exp(gh - gr)).astype(BF16)
            kt = (kh * jnp.exp(gr - gh)).astype(BF16)
            att = jnp.where(causal, _dot_nt(qt, kt), 0.0).astype(BF16)
            qi = (qh * jnp.exp(gh)).astype(BF16)
            kl = kh * jnp.exp(gl - gh)
            o_prev = []
            for a in range(2):
                s_a = s0_ref[2 * p + a, hh]
                o_prev.append(_dot(qi, s_a.astype(BF16)))
                kl_a = jnp.where(first_seq if a == 0 else second_seq, kl, 0.0).astype(BF16)
                e = jnp.exp(gh[(a + 1) * seg - 1:(a + 1) * seg, :])
                e1 = e.astype(BF16).astype(F32)
                e2 = (e - e1).astype(BF16).astype(F32)
                e3 = (e - e1 - e2).astype(BF16).astype(F32)
                dec_l = jnp.where(rr == 0, e1, jnp.where(rr == 1, e2, jnp.where(rr == 2, e3, 0.0)))
                dec = _dot_tn(dec_l.astype(BF16), jnp.ones((pair, C_DK), BF16))
                s1_ref[2 * p + a, hh] = dec * s_a + _dot_tn(kl_a, vh)
            ob[pl.ds(r0, pair), sl] = _dot(att, vh) + jnp.where(first_seq, o_prev[0], o_prev[1])
        return carry

    lax.fori_loop(0, nblk // 2, body, 0)

    @pl.when(i == pl.num_programs(0) - 1)
    def _():
        x = x_ref[...]
        h = _rms(x, gpre_ref[...]).astype(BF16)
        y_ref[...] = _hgrn_gate_out(x, h, ob, mixbuf, win_ref, ng_ref, wout_ref, gpost_ref)


def _hgrn_sample(layer, x, s0, seg, gpre, gpost, win, lbl, ng, wout):
    rows, d = x.shape
    nseq = rows // seg
    dkt = C_HEADS * C_DK
    sb = SAMPLE_SEQ_BLOCK
    full = lambda shape: pl.BlockSpec(shape, lambda i: (0,) * len(shape))
    state = pl.BlockSpec((sb, C_HEADS, C_DK, C_DK), lambda i: (i, 0, 0, 0))
    return pl.pallas_call(
        functools.partial(_hgrn_sample_kernel, layer, seg),
        grid=(nseq // sb,),
        in_specs=[_resident(x.shape), state, _resident(gpre.shape), _resident(gpost.shape), _resident(win.shape),
                  _resident(lbl.shape), _resident(ng.shape), _resident(wout.shape)],
        out_specs=[full((rows, d)), state],
        out_shape=[jax.ShapeDtypeStruct((rows, d), F32), jax.ShapeDtypeStruct(s0.shape, F32)],
        scratch_shapes=[pltpu.VMEM((rows, dkt), F32), pltpu.VMEM((rows, dkt), F32),
                        pltpu.VMEM((rows, dkt), F32), pltpu.VMEM((rows, dkt), F32),
                        pltpu.VMEM((rows, dkt), F32), pltpu.VMEM((rows, dkt), BF16)],
        compiler_params=_params(("arbitrary",)),
        name="hgrn_sample",
    )(x, s0, gpre, gpost, win, lbl, ng, wout)


def _head_permutation():
    cols = []
    for g in range(GROUP):
        for kv in range(KV_HEADS):
            hidx = kv * GROUP + g
            cols.extend(range(hidx * HEAD_DIM, (hidx + 1) * HEAD_DIM))
    return jnp.asarray(cols, dtype=jnp.int32)


def kernel(x_prompt, x_sample, state_conv_a, cache_swa_k, cache_swa_v, state_hgrn, state_ffn_conv, norm_mix_pre, norm_mix_post, norm_ffn_pre, norm_ffn_post, w_in_ab, conv_a_w, attn_sinks, w_out_ab, w_in_c, hgrn_lb_logits, hgrn_norm_g, w_out_c, w_ffn_up, ffn_conv_w, w_ffn_down):
    bp, seq, d = x_prompt.shape
    ns, seg, _ = x_sample.shape
    aw = conv_a_w.shape[-1]
    nq = GROUP * KV_HEADS * HEAD_DIM
    nkv = KV_HEADS * HEAD_DIM

    perm = _head_permutation()
    q0 = 3 * aw
    win_ab = w_in_ab[0]
    win_ab = jnp.concatenate([win_ab[:, :q0], win_ab[:, q0:q0 + nq][:, perm], win_ab[:, q0 + nq:]], axis=1).astype(BF16)
    wout_ab = w_out_ab[0]
    wout_ab = jnp.concatenate([wout_ab[:aw], wout_ab[aw:][perm]], axis=0).astype(BF16)
    win_c = w_in_c[0].astype(BF16)
    wout_c = w_out_c[0].astype(BF16)
    wup = w_ffn_up.astype(BF16)
    wdown = w_ffn_down.astype(BF16)
    row = lambda a: a.reshape(1, -1)
    sinks = row(attn_sinks[0])
    ng = row(hgrn_norm_g[0])

    x1, conv_p, k_p, v_p = _mix_ab_prompt(x_prompt, row(norm_mix_pre[0]), row(norm_mix_post[0]), win_ab,
                                          conv_a_w[0], sinks, wout_ab)
    x2, ffn_p0 = _ffn_prompt(x1, row(norm_ffn_pre[0]), row(norm_ffn_post[0]), wup[0], ffn_conv_w[0], wdown[0])
    x3, hgrn_p = _hgrn_prompt(1, x2, row(norm_mix_pre[1]), row(norm_mix_post[1]), win_c, hgrn_lb_logits,
                              ng, wout_c)
    y_prompt, ffn_p1 = _ffn_prompt(x3, row(norm_ffn_pre[1]), row(norm_ffn_post[1]), wup[1], ffn_conv_w[1], wdown[1])

    xs = x_sample.reshape(ns * seg, d)
    kc = cache_swa_k[0].reshape(ns, WINDOW, nkv)
    vc = cache_swa_v[0].reshape(ns, WINDOW, nkv)
    s1, conv_s, k_s, v_s = _mix_ab_sample(xs, state_conv_a[0], kc, vc, seg, row(norm_mix_pre[0]),
                                          row(norm_mix_post[0]), win_ab, conv_a_w[0], sinks, wout_ab)
    s2, ffn_s0 = _ffn_sample(s1, state_ffn_conv[0], seg, row(norm_ffn_pre[0]), row(norm_ffn_post[0]),
                             wup[0], ffn_conv_w[0], wdown[0])
    s3, hgrn_s = _hgrn_sample(1, s2, state_hgrn[0], seg, row(norm_mix_pre[1]), row(norm_mix_post[1]),
                              win_c, hgrn_lb_logits, ng, wout_c)
    y_sample, ffn_s1 = _ffn_sample(s3, state_ffn_conv[1], seg, row(norm_ffn_pre[1]), row(norm_ffn_post[1]),
                                   wup[1], ffn_conv_w[1], wdown[1])

    kvshape = (KV_HEADS, HEAD_DIM)
    return (y_prompt, y_sample.reshape(ns, seg, d),
            conv_p[None], conv_s[None],
            k_p.reshape(1, bp, WINDOW, *kvshape), k_s.reshape(1, ns, WINDOW, *kvshape),
            v_p.reshape(1, bp, WINDOW, *kvshape), v_s.reshape(1, ns, WINDOW, *kvshape),
            hgrn_p[None], hgrn_s[None],
            jnp.stack([ffn_p0, ffn_p1]), jnp.stack([ffn_s0, ffn_s1]))
```

```python
import functools
import math

import jax
import jax.numpy as jnp
from jax import lax
from jax.experimental import pallas as pl
from jax.experimental.pallas import tpu as pltpu

F32 = jnp.float32
BF16 = jnp.bfloat16

EPS = 1e-6
WINDOW = 128
KV_HEADS = 2
GROUP = 4
HEAD_DIM = 64
C_HEADS = 8
C_DK = 128
NEG = -1e30
SUBLANES = 8
PROMPT_TILE = 512
HGRN_CHUNK = 64
FFN_CHUNK = 256
SAMPLE_SEQ_BLOCK = 8
VMEM_LIMIT_V7X = 56 * 2**20


def _dot(a, b):
    return jnp.dot(a, b, preferred_element_type=F32)


def _dot_nt(a, b):
    return lax.dot_general(a, b, (((1,), (1,)), ((), ())), preferred_element_type=F32)


def _dot_tn(a, b):
    return lax.dot_general(a, b, (((0,), (0,)), ((), ())), preferred_element_type=F32)


def _rms(x, g):
    return x * lax.rsqrt(jnp.mean(x * x, axis=-1, keepdims=True) + EPS) * g


def _gelu_tanh(x):
    c = math.sqrt(2.0 / math.pi)
    return x * (0.5 * (1.0 + jnp.tanh(c * (x + 0.044715 * (x * x * x)))))


def _sigmoid(x):
    return 1.0 / (1.0 + jnp.exp(-x))


def _resident(shape):
    nd = len(shape)
    return pl.BlockSpec(shape, lambda *_: (0,) * nd, pipeline_mode=pl.Buffered(1))


def _params(sem):
    return pltpu.CompilerParams(dimension_semantics=sem, vmem_limit_bytes=VMEM_LIMIT_V7X)


def _head_slope(h):
    return 2.0 ** (-8.0 * (h + 1) / (KV_HEADS * GROUP))


def _mix_ab_prompt_kernel(x_ref, gpre_ref, gpost_ref, win_ref, cw_ref, sink_ref, wout_ref,
                          y_ref, conv_ref, kout_ref, vout_ref,
                          ubuf, qbuf, kbuf, vbuf, mixbuf, bias_s):
    s = pl.program_id(1)
    ts = x_ref.shape[0]
    aw = cw_ref.shape[1]
    nq = GROUP * KV_HEADS * HEAD_DIM
    nkv = KV_HEADS * HEAD_DIM

    @pl.when(s == 0)
    def _():
        ubuf[0:SUBLANES, :] = jnp.zeros((SUBLANES, aw), F32)
        kbuf[0:WINDOW, :] = jnp.zeros((WINDOW, nkv), BF16)
        vbuf[0:WINDOW, 0:nkv] = jnp.zeros((WINDOW, nkv), BF16)
        vbuf[:, nkv:2 * nkv] = jnp.ones((ts + WINDOW, nkv), BF16)

    x = x_ref[...]
    h = _rms(x, gpre_ref[...]).astype(BF16)

    u = _dot(h, win_ref[:, 2 * aw:3 * aw]) * _dot(h, win_ref[:, 0:aw])
    ubuf[SUBLANES:SUBLANES + ts, :] = u
    cw = cw_ref[...]
    conv = (cw[2:3, :] * u + cw[1:2, :] * ubuf[SUBLANES - 1:SUBLANES - 1 + ts, :]
            + cw[0:1, :] * ubuf[SUBLANES - 2:SUBLANES - 2 + ts, :])
    mixbuf[:, 0:aw] = (_dot(h, win_ref[:, aw:2 * aw]) * conv).astype(BF16)
    tail = ubuf[ts + SUBLANES - 2:ts + SUBLANES, :]
    conv_ref[...] = tail
    ubuf[SUBLANES - 2:SUBLANES, :] = tail

    q0 = 3 * aw
    q = _dot(h, win_ref[:, q0:q0 + nq]) * (HEAD_DIM ** -0.5)
    lane = lax.broadcasted_iota(jnp.int32, (1, nkv), 1)
    lo = lane < HEAD_DIM
    nblk = ts // WINDOW
    for n in range(nblk):
        for g in range(GROUP):
            qg = q[n * WINDOW:(n + 1) * WINDOW, g * nkv:(g + 1) * nkv]
            qbuf[n, g * WINDOW:(g + 1) * WINDOW, :] = jnp.where(lo, qg, 0.0).astype(BF16)
            qbuf[n, (GROUP + g) * WINDOW:(GROUP + g + 1) * WINDOW, :] = jnp.where(lo, 0.0, qg).astype(BF16)
    kv_new = _dot(h, win_ref[:, q0 + nq:q0 + nq + 2 * nkv])
    k = kv_new[:, 0:nkv]
    v = kv_new[:, nkv:2 * nkv]
    kbuf[WINDOW:WINDOW + ts, :] = k.astype(BF16)
    vbuf[WINDOW:WINDOW + ts, 0:nkv] = v.astype(BF16)
    kout_ref[...] = k[ts - WINDOW:, :]
    vout_ref[...] = v[ts - WINDOW:, :]

    qi = lax.broadcasted_iota(jnp.int32, (WINDOW, 2 * WINDOW), 0)
    kj = lax.broadcasted_iota(jnp.int32, (WINDOW, 2 * WINDOW), 1)
    dist = qi + WINDOW - kj
    valid = (dist >= 0) & (dist < WINDOW)
    negdist = jnp.where(valid, -dist.astype(F32), 0.0)
    maskadd = jnp.where(valid, 0.0, NEG)
    nheads = KV_HEADS * GROUP
    for hidx in range(nheads):
        bias_s[hidx] = _head_slope(hidx) * negdist + maskadd
    first = jnp.where(kj < WINDOW, jnp.where(s == 0, NEG, 0.0), 0.0)

    for n in range(nblk):
        r0 = n * WINDOW
        sc_all = _dot_nt(qbuf[n], kbuf[r0:r0 + 2 * WINDOW, :])
        ps, sink_terms = [], []
        for hidx in range(nheads):
            sc = sc_all[hidx * WINDOW:(hidx + 1) * WINDOW, :] + bias_s[hidx]
            if n == 0:
                sc = sc + first
            sink = sink_ref[hidx]
            m = jnp.maximum(jnp.max(sc, axis=-1, keepdims=True), sink)
            ps.append(jnp.exp(sc - m).astype(BF16))
            sink_terms.append(jnp.exp(sink - m))
        pv = _dot(jnp.concatenate(ps, axis=0), vbuf[r0:r0 + 2 * WINDOW, :])
        outs = []
        for hidx in range(nheads):
            pv_h = pv[hidx * WINDOW:(hidx + 1) * WINDOW, :]
            outs.append(pv_h[:, 0:nkv] / (pv_h[:, nkv:2 * nkv] + sink_terms[hidx]))
        for g in range(GROUP):
            og = jnp.where(lo, outs[g], outs[GROUP + g])
            mixbuf[r0:r0 + WINDOW, aw + g * nkv:aw + (g + 1) * nkv] = og.astype(BF16)

    kbuf[0:WINDOW, :] = kbuf[ts:ts + WINDOW, :]
    vbuf[0:WINDOW, :] = vbuf[ts:ts + WINDOW, :]

    out = _dot(mixbuf[...], wout_ref[...])
    y_ref[...] = x + _rms(out, gpost_ref[...])


def _mix_ab_prompt(x, gpre, gpost, win, cw, sinks, wout):
    b, seq, d = x.shape
    ts = PROMPT_TILE
    aw = cw.shape[1]
    nq = GROUP * KV_HEADS * HEAD_DIM
    nkv = KV_HEADS * HEAD_DIM
    tile = pl.BlockSpec((None, ts, d), lambda i, j: (i, j, 0))
    return pl.pallas_call(
        _mix_ab_prompt_kernel,
        grid=(b, seq // ts),
        in_specs=[tile, _resident(gpre.shape), _resident(gpost.shape), _resident(win.shape),
                  _resident(cw.shape), pl.BlockSpec(memory_space=pltpu.SMEM), _resident(wout.shape)],
        out_specs=[tile,
                   pl.BlockSpec((None, 2, aw), lambda i, j: (i, 0, 0)),
                   pl.BlockSpec((None, WINDOW, nkv), lambda i, j: (i, 0, 0)),
                   pl.BlockSpec((None, WINDOW, nkv), lambda i, j: (i, 0, 0))],
        out_shape=[jax.ShapeDtypeStruct((b, seq, d), F32),
                   jax.ShapeDtypeStruct((b, 2, aw), F32),
                   jax.ShapeDtypeStruct((b, WINDOW, nkv), F32),
                   jax.ShapeDtypeStruct((b, WINDOW, nkv), F32)],
        scratch_shapes=[pltpu.VMEM((ts + SUBLANES, aw), F32),
                        pltpu.VMEM((ts // WINDOW, KV_HEADS * GROUP * WINDOW, nkv), BF16),
                        pltpu.VMEM((ts + WINDOW, nkv), BF16),
                        pltpu.VMEM((ts + WINDOW, 2 * nkv), BF16),
                        pltpu.VMEM((ts, d), BF16),
                        pltpu.VMEM((KV_HEADS * GROUP, WINDOW, 2 * WINDOW), F32)],
        compiler_params=_params(("arbitrary", "arbitrary")),
        name="mix_ab_prompt",
    )(x, gpre, gpost, win, cw, sinks.reshape(-1), wout)


def _ffn_prompt_kernel(x_ref, gpre_ref, gpost_ref, wup_ref, cw_ref, wdown_ref,
                       y_ref, st_ref, abuf, gbuf):
    s = pl.program_id(1)
    rows = x_ref.shape[0]
    ff = cw_ref.shape[1]

    @pl.when(s == 0)
    def _():
        abuf[0:SUBLANES, :] = jnp.zeros((SUBLANES, ff), F32)

    x = x_ref[...]
    h = _rms(x, gpre_ref[...]).astype(BF16)
    for c in range(ff // FFN_CHUNK):
        c0 = c * FFN_CHUNK
        a = _dot(h, wup_ref[:, c0:c0 + FFN_CHUNK])
        b = _dot(h, wup_ref[:, ff + c0:ff + c0 + FFN_CHUNK])
        abuf[SUBLANES:SUBLANES + rows, c0:c0 + FFN_CHUNK] = a
        cw = cw_ref[:, c0:c0 + FFN_CHUNK]
        conv = (cw[2:3, :] * a + cw[1:2, :] * abuf[SUBLANES - 1:SUBLANES - 1 + rows, c0:c0 + FFN_CHUNK]
                + cw[0:1, :] * abuf[SUBLANES - 2:SUBLANES - 2 + rows, c0:c0 + FFN_CHUNK])
        tail = a[rows - 2:rows, :]
        st_ref[:, c0:c0 + FFN_CHUNK] = tail
        abuf[SUBLANES - 2:SUBLANES, c0:c0 + FFN_CHUNK] = tail
        gbuf[:, c0:c0 + FFN_CHUNK] = (_gelu_tanh(conv) * b).astype(BF16)
    out = _dot(gbuf[...], wdown_ref[...])
    y_ref[...] = x + _rms(out, gpost_ref[...])


def _ffn_prompt(x, gpre, gpost, wup, cw, wdown):
    b, seq, d = x.shape
    ts = PROMPT_TILE
    ff = cw.shape[1]
    tile = pl.BlockSpec((None, ts, d), lambda i, j: (i, j, 0))
    return pl.pallas_call(
        _ffn_prompt_kernel,
        grid=(b, seq // ts),
        in_specs=[tile, _resident(gpre.shape), _resident(gpost.shape), _resident(wup.shape),
                  _resident(cw.shape), _resident(wdown.shape)],
        out_specs=[tile, pl.BlockSpec((None, 2, ff), lambda i, j: (i, 0, 0))],
        out_shape=[jax.ShapeDtypeStruct((b, seq, d), F32), jax.ShapeDtypeStruct((b, 2, ff), F32)],
        scratch_shapes=[pltpu.VMEM((ts + SUBLANES, ff), F32), pltpu.VMEM((ts, ff), BF16)],
        compiler_params=_params(("arbitrary", "arbitrary")),
        name="ffn_prompt",
    )(x, gpre, gpost, wup, cw, wdown)


def _ffn_sample_kernel(seg, x_ref, p1_ref, p2_ref, gpre_ref, gpost_ref, wa_ref, wb_ref, cw_ref, wdown_ref,
                       y_ref, a_ref, hbuf, abuf, acc):
    c = pl.program_id(0)
    rows = x_ref.shape[0]

    @pl.when(c == 0)
    def _():
        hbuf[...] = _rms(x_ref[...], gpre_ref[...]).astype(BF16)
        acc[...] = jnp.zeros(acc.shape, F32)
        abuf[0:SUBLANES, :] = jnp.zeros((SUBLANES, abuf.shape[1]), F32)

    h = hbuf[...]
    t = lax.broadcasted_iota(jnp.int32, (rows, 1), 0) % seg
    a = _dot(h, wa_ref[...])
    b = _dot(h, wb_ref[...])
    a_ref[...] = a
    abuf[SUBLANES:SUBLANES + rows, :] = a
    am1 = jnp.where(t >= 1, abuf[SUBLANES - 1:SUBLANES - 1 + rows, :], p1_ref[...])
    am2 = jnp.where(t >= 2, abuf[SUBLANES - 2:SUBLANES - 2 + rows, :], p2_ref[...])
    cw = cw_ref[...]
    conv = cw[2:3, :] * a + cw[1:2, :] * am1 + cw[0:1, :] * am2
    acc[...] += _dot((_gelu_tanh(conv) * b).astype(BF16), wdown_ref[...])

    @pl.when(c == pl.num_programs(0) - 1)
    def _():
        y_ref[...] = x_ref[...] + _rms(acc[...], gpost_ref[...])


def _ffn_sample(x, prev, seg, gpre, gpost, wup, cw, wdown):
    rows, d = x.shape
    ff = cw.shape[1]
    cf = FFN_CHUNK
    nc = ff // cf
    p1 = jnp.pad(prev[:, 1:2], ((0, 0), (0, seg - 1), (0, 0))).reshape(rows, ff)
    p2 = jnp.pad(prev, ((0, 0), (0, seg - 2), (0, 0))).reshape(rows, ff)
    cols = lambda r: pl.BlockSpec((r, cf), lambda c: (0, c))
    y, a = pl.pallas_call(
        functools.partial(_ffn_sample_kernel, seg),
        grid=(nc,),
        in_specs=[_resident(x.shape), cols(rows), cols(rows), _resident(gpre.shape), _resident(gpost.shape),
                  cols(d), pl.BlockSpec((d, cf), lambda c: (0, c + nc)), cols(3),
                  pl.BlockSpec((cf, d), lambda c: (c, 0))],
        out_specs=[pl.BlockSpec((rows, d), lambda c: (0, 0)), cols(rows)],
        out_shape=[jax.ShapeDtypeStruct((rows, d), F32), jax.ShapeDtypeStruct((rows, ff), F32)],
        scratch_shapes=[pltpu.VMEM((rows, d), BF16), pltpu.VMEM((rows + SUBLANES, cf), F32),
                        pltpu.VMEM((rows, d), F32)],
        compiler_params=_params(("arbitrary",)),
        name="ffn_sample",
    )(x, p1, p2, gpre, gpost, wup, wup, cw, wdown)
    return y, a.reshape(rows // seg, seg, ff)[:, seg - 2:]


def _lower_bound_terms(logits_ref, layer):
    lg = logits_ref[...]
    e = jnp.exp(lg - jnp.max(lg, axis=0, keepdims=True))
    pr = e / jnp.sum(e, axis=0, keepdims=True)
    cum = pr[0:1, :]
    for r in range(1, layer + 1):
        cum = cum + pr[r:r + 1, :]
    lb = jnp.clip(cum - pr[0:1, :], 0.0, 1.0)
    return jnp.log(lb), jnp.log1p(-lb), 1.0 - lb


def _log_forget(f, log_lb, log1m_lb):
    log_sig = -(jnp.maximum(-f, 0.0) + jnp.log(1.0 + jnp.exp(-jnp.abs(f))))
    b = log1m_lb + log_sig
    return jnp.maximum(log_lb, b) + jnp.log(1.0 + jnp.exp(-jnp.abs(log_lb - b)))


def _split_bf16(x):
    hi = x.astype(BF16)
    return hi, (x - hi.astype(F32)).astype(BF16)


def _hgrn_gate_out(x, h, ob, mixbuf, win_ref, ng_ref, wout_ref, gpost_ref):
    dkt = C_HEADS * C_DK
    g = _dot(h, win_ref[:, 3 * dkt:4 * dkt])
    ng = ng_ref[...]
    for hh in range(C_HEADS):
        sl = slice(hh * C_DK, (hh + 1) * C_DK)
        mixbuf[:, sl] = (_rms(ob[:, sl], ng) * (g[:, sl] * _sigmoid(g[:, sl]))).astype(BF16)
    out = _dot(mixbuf[...], wout_ref[...])
    return x + _rms(out, gpost_ref[...])


def _hgrn_prompt_kernel(layer, x_ref, gpre_ref, gpost_ref, win_ref, lbl_ref, ng_ref, wout_ref,
                        y_ref, sout_ref, st, qb, kb, vb, lfb, ob, mixbuf):
    s = pl.program_id(1)
    ts = x_ref.shape[0]
    dkt = C_HEADS * C_DK
    ck = HGRN_CHUNK

    @pl.when(s == 0)
    def _():
        st[...] = jnp.zeros(st.shape, F32)

    x = x_ref[...]
    h = _rms(x, gpre_ref[...]).astype(BF16)
    log_lb, log1m_lb, one_m_lb = _lower_bound_terms(lbl_ref, layer)
    f = _dot(h, win_ref[:, dkt:2 * dkt])
    lfb[...] = _log_forget(f, log_lb, log1m_lb)
    kb[...] = one_m_lb * _sigmoid(-f)
    qb[...] = _dot(h, win_ref[:, 0:dkt])
    vb[...] = _dot(h, win_ref[:, 2 * dkt:3 * dkt]).astype(BF16)

    ri = lax.broadcasted_iota(jnp.int32, (ck, ck), 0)
    ci = lax.broadcasted_iota(jnp.int32, (ck, ck), 1)
    causal = ri >= ci
    tri = jnp.where(causal, 1.0, 0.0).astype(BF16)
    mid = ck // 2 - 1

    for c in range(ts // ck):
        rows = slice(c * ck, (c + 1) * ck)
        hi, lo = _split_bf16(lfb[rows, :])
        gcum = _dot(tri, hi) + _dot(tri, lo)
        for hh in range(C_HEADS):
            sl = slice(hh * C_DK, (hh + 1) * C_DK)
            gh = gcum[:, sl]
            gr = gh[mid:mid + 1, :]
            gl = gh[ck - 1:ck, :]
            qh = qb[rows, sl]
            kh = kb[rows, sl]
            vh = vb[rows, sl]
            qt = (qh * jnp.exp(gh - gr)).astype(BF16)
            kt = (kh * jnp.exp(gr - gh)).astype(BF16)
            att = jnp.where(causal, _dot_nt(qt, kt), 0.0).astype(BF16)
            st_h = st[hh]
            qi = (qh * jnp.exp(gh)).astype(BF16)
            ob[rows, sl] = _dot(att, vh) + _dot_nt(qi, st_h.astype(BF16))
            kl = (kh * jnp.exp(gl - gh)).astype(BF16)
            st[hh] = jnp.exp(gl) * st_h + _dot_tn(vh, kl)

    y_ref[...] = _hgrn_gate_out(x, h, ob, mixbuf, win_ref, ng_ref, wout_ref, gpost_ref)

    @pl.when(s == pl.num_programs(1) - 1)
    def _():
        for hh in range(C_HEADS):
            sout_ref[hh] = st[hh].T


def _hgrn_prompt(layer, x, gpre, gpost, win, lbl, ng, wout):
    b, seq, d = x.shape
    ts = PROMPT_TILE
    dkt = C_HEADS * C_DK
    tile = pl.BlockSpec((None, ts, d), lambda i, j: (i, j, 0))
    return pl.pallas_call(
        functools.partial(_hgrn_prompt_kernel, layer),
        grid=(b, seq // ts),
        in_specs=[tile, _resident(gpre.shape), _resident(gpost.shape), _resident(win.shape),
                  _resident(lbl.shape), _resident(ng.shape), _resident(wout.shape)],
        out_specs=[tile, pl.BlockSpec((None, C_HEADS, C_DK, C_DK), lambda i, j: (i, 0, 0, 0))],
        out_shape=[jax.ShapeDtypeStruct((b, seq, d), F32),
                   jax.ShapeDtypeStruct((b, C_HEADS, C_DK, C_DK), F32)],
        scratch_shapes=[pltpu.VMEM((C_HEADS, C_DK, C_DK), F32),
                        pltpu.VMEM((ts, dkt), F32), pltpu.VMEM((ts, dkt), F32),
                        pltpu.VMEM((ts, dkt), BF16), pltpu.VMEM((ts, dkt), F32),
                        pltpu.VMEM((ts, dkt), F32), pltpu.VMEM((ts, dkt), BF16)],
        compiler_params=_params(("arbitrary", "arbitrary")),
        name="hgrn_prompt",
    )(x, gpre, gpost, win, lbl, ng, wout)


def _mix_ab_sample_kernel(seg, x_ref, p1_ref, p2_ref, kc_ref, vc_ref, gpre_ref, gpost_ref, win_ref,
                          cw_ref, sink_ref, wout_ref,
                          y_ref, u_ref, knew_ref, vnew_ref,
                          ubuf, qbuf, yabuf, ybbuf, kall, vall, bias_s, sink_s):
    i = pl.program_id(0)
    rows = x_ref.shape[0]
    aw = cw_ref.shape[1]
    nq = GROUP * KV_HEADS * HEAD_DIM
    nkv = KV_HEADS * HEAD_DIM
    pair = 2 * seg
    qrows = KV_HEADS * GROUP * pair
    tk = WINDOW + pair

    @pl.when(i == 0)
    def _():
        x = x_ref[...]
        h = _rms(x, gpre_ref[...]).astype(BF16)
        t = lax.broadcasted_iota(jnp.int32, (rows, 1), 0) % seg
        u = _dot(h, win_ref[:, 2 * aw:3 * aw]) * _dot(h, win_ref[:, 0:aw])
        u_ref[...] = u
        ubuf[0:SUBLANES, :] = jnp.zeros((SUBLANES, aw), F32)
        ubuf[SUBLANES:SUBLANES + rows, :] = u
        um1 = jnp.where(t >= 1, ubuf[SUBLANES - 1:SUBLANES - 1 + rows, :], p1_ref[...])
        um2 = jnp.where(t >= 2, ubuf[SUBLANES - 2:SUBLANES - 2 + rows, :], p2_ref[...])
        cw = cw_ref[...]
        conv = cw[2:3, :] * u + cw[1:2, :] * um1 + cw[0:1, :] * um2
        yabuf[...] = (_dot(h, win_ref[:, aw:2 * aw]) * conv).astype(BF16)
        q0 = 3 * aw
        qbuf[...] = _dot(h, win_ref[:, q0:q0 + nq]) * (HEAD_DIM ** -0.5)
        knew_ref[...] = _dot(h, win_ref[:, q0 + nq:q0 + nq + nkv])
        vnew_ref[...] = _dot(h, win_ref[:, q0 + nq + nkv:q0 + nq + 2 * nkv])

        r = lax.broadcasted_iota(jnp.int32, (qrows, tk), 0)
        j = lax.broadcasted_iota(jnp.int32, (qrows, tk), 1)
        tq = r % seg
        second = (r // seg) % 2
        jn = j - WINDOW - seg * second
        dist = jnp.where(j < WINDOW, tq + WINDOW - j, tq - jn)
        valid = (dist >= 0) & (dist < WINDOW) & ((j < WINDOW) | ((jn >= 0) & (jn < seg)))
        rc = lax.broadcasted_iota(jnp.int32, (qrows, 1), 0)
        head = (rc // (GROUP * pair)) * GROUP + (rc // pair) % GROUP
        slope = jnp.zeros((qrows, 1), F32)
        sink = jnp.zeros((qrows, 1), F32)
        for hidx in range(KV_HEADS * GROUP):
            slope = jnp.where(head == hidx, _head_slope(hidx), slope)
            sink = jnp.where(head == hidx, sink_ref[:, hidx:hidx + 1], sink)
        bias_s[...] = jnp.where(valid, -slope * dist.astype(F32), NEG)
        sink_s[...] = sink

    nblk = kc_ref.shape[0]
    lane = lax.broadcasted_iota(jnp.int32, (1, nkv), 1)
    lo = lane < HEAD_DIM
    rq = lax.broadcasted_iota(jnp.int32, (qrows, 1), 0)
    first_seq = (rq // seg) % 2 == 0
    kv0 = rq < GROUP * pair

    def body(p, carry):
        r0 = pl.multiple_of((i * nblk + 2 * p) * seg, pair)
        q8 = qbuf[pl.ds(r0, pair), :]
        qg = jnp.concatenate([q8[:, g * nkv:(g + 1) * nkv] for g in range(GROUP)], axis=0)
        qs = jnp.concatenate([jnp.where(lo, qg, 0.0), jnp.where(lo, 0.0, qg)], axis=0).astype(BF16)
        knew = knew_ref[pl.ds(r0, pair), :]
        vnew = vnew_ref[pl.ds(r0, pair), :]
        sc, ov = [], []
        for a in range(2):
            kall[a, 0:WINDOW, :] = kc_ref[2 * p + a]
            kall[a, WINDOW:tk, :] = knew
            vall[a, 0:WINDOW, :] = vc_ref[2 * p + a]
            vall[a, WINDOW:tk, :] = vnew
            sc.append(_dot_nt(qs, kall[a].astype(BF16)))
        s2 = jnp.where(first_seq, sc[0], sc[1]) + bias_s[...]
        sink = sink_s[...]
        m = jnp.maximum(jnp.max(s2, axis=-1, keepdims=True), sink)
        pr = jnp.exp(s2 - m)
        den = jnp.sum(pr, axis=-1, keepdims=True) + jnp.exp(sink - m)
        prb = pr.astype(BF16)
        for a in range(2):
            ov.append(_dot(prb, vall[a].astype(BF16)))
        o = jnp.where(first_seq, ov[0], ov[1]) / den
        og = jnp.where(lo, o[0:GROUP * pair, :], o[GROUP * pair:, :])
        for g in range(GROUP):
            ybbuf[pl.ds(r0, pair), g * nkv:(g + 1) * nkv] = og[g * pair:(g + 1) * pair, :]
        return carry

    lax.fori_loop(0, nblk // 2, body, 0)

    @pl.when(i == pl.num_programs(0) - 1)
    def _():
        out = _dot(yabuf[...], wout_ref[0:aw, :]) + _dot(ybbuf[...].astype(BF16), wout_ref[aw:, :])
        y_ref[...] = x_ref[...] + _rms(out, gpost_ref[...])


def _mix_ab_sample(x, conv_prev, kcache, vcache, seg, gpre, gpost, win, cw, sinks, wout):
    rows, d = x.shape
    nseq = rows // seg
    aw = cw.shape[1]
    nq = GROUP * KV_HEADS * HEAD_DIM
    nkv = KV_HEADS * HEAD_DIM
    sb = SAMPLE_SEQ_BLOCK
    pair = 2 * seg
    qrows = KV_HEADS * GROUP * pair
    tk = WINDOW + pair
    p1 = jnp.pad(conv_prev[:, 1:2], ((0, 0), (0, seg - 1), (0, 0))).reshape(rows, aw)
    p2 = jnp.pad(conv_prev, ((0, 0), (0, seg - 2), (0, 0))).reshape(rows, aw)
    full = lambda shape: pl.BlockSpec(shape, lambda i: (0,) * len(shape))
    cache = pl.BlockSpec((sb, WINDOW, nkv), lambda i: (i, 0, 0))
    y, u, knew, vnew = pl.pallas_call(
        functools.partial(_mix_ab_sample_kernel, seg),
        grid=(nseq // sb,),
        in_specs=[_resident(x.shape), _resident(p1.shape), _resident(p2.shape), cache, cache,
                  _resident(gpre.shape), _resident(gpost.shape), _resident(win.shape), _resident(cw.shape),
                  _resident(sinks.shape), _resident(wout.shape)],
        out_specs=[full((rows, d)), full((rows, aw)), full((rows, nkv)), full((rows, nkv))],
        out_shape=[jax.ShapeDtypeStruct((rows, d), F32), jax.ShapeDtypeStruct((rows, aw), F32),
                   jax.ShapeDtypeStruct((rows, nkv), F32), jax.ShapeDtypeStruct((rows, nkv), F32)],
        scratch_shapes=[pltpu.VMEM((rows + SUBLANES, aw), F32),
                        pltpu.VMEM((rows, nq), F32),
                        pltpu.VMEM((rows, aw), BF16),
                        pltpu.VMEM((rows, nq), F32),
                        pltpu.VMEM((2, tk, nkv), F32),
                        pltpu.VMEM((2, tk, nkv), F32),
                        pltpu.VMEM((qrows, tk), F32),
                        pltpu.VMEM((qrows, 1), F32)],
        compiler_params=_params(("arbitrary",)),
        name="mix_ab_sample",
    )(x, p1, p2, kcache, vcache, gpre, gpost, win, cw, sinks, wout)
    conv_new = u.reshape(nseq, seg, aw)[:, seg - 2:]
    k_new = jnp.concatenate([kcache, knew.reshape(nseq, seg, nkv)], axis=1)[:, seg:]
    v_new = jnp.concatenate([vcache, vnew.reshape(nseq, seg, nkv)], axis=1)[:, seg:]
    return y, conv_new, k_new, v_new


def _hgrn_sample_kernel(layer, seg, x_ref, s0_ref, gpre_ref, gpost_ref, win_ref, lbl_ref, ng_ref, wout_ref,
                        y_ref, s1_ref, qb, kb, vb, gb, ob, mixbuf):
    i = pl.program_id(0)
    rows = x_ref.shape[0]
    dkt = C_HEADS * C_DK
    pair = 2 * seg

    @pl.when(i == 0)
    def _():
        x = x_ref[...]
        h = _rms(x, gpre_ref[...]).astype(BF16)
        log_lb, log1m_lb, one_m_lb = _lower_bound_terms(lbl_ref, layer)
        f = _dot(h, win_ref[:, dkt:2 * dkt])
        hi, lo = _split_bf16(_log_forget(f, log_lb, log1m_lb))
        ri = lax.broadcasted_iota(jnp.int32, (rows, rows), 0)
        ci = lax.broadcasted_iota(jnp.int32, (rows, rows), 1)
        tri = jnp.where((ri // seg == ci // seg) & (ri >= ci), 1.0, 0.0).astype(BF16)
        gb[...] = _dot(tri, hi) + _dot(tri, lo)
        kb[...] = one_m_lb * _sigmoid(-f)
        qb[...] = _dot(h, win_ref[:, 0:dkt])
        vb[...] = _dot(h, win_ref[:, 2 * dkt:3 * dkt])

    nblk = s0_ref.shape[0]
    rr = lax.broadcasted_iota(jnp.int32, (pair, 1), 0)
    first_seq = rr < seg
    second_seq = rr >= seg
    ri = lax.broadcasted_iota(jnp.int32, (pair, pair), 0)
    ci = lax.broadcasted_iota(jnp.int32, (pair, pair), 1)
    causal = (ri // seg == ci // seg) & (ri >= ci)
    mid = seg // 2 - 1

    def body(p, carry):
        r0 = pl.multiple_of((i * nblk + 2 * p) * seg, pair)
        for hh in range(C_HEADS):
            sl = slice(hh * C_DK, (hh + 1) * C_DK)
            gh = gb[pl.ds(r0, pair), sl]
            gr = jnp.where(first_seq, gh[mid:mid + 1, :], gh[seg + mid:seg + mid + 1, :])
            gl = jnp.where(first_seq, gh[seg - 1:seg, :], gh[pair - 1:pair, :])
            qh = qb[pl.ds(r0, pair), sl]
            kh = kb[pl.ds(r0, pair), sl]
            vh = vb[pl.ds(r0, pair), sl].astype(BF16)
            qt = (qh * jnp.exp(gh - gr)).astype(BF16)
            kt = (kh * jnp.exp(gr - gh)).astype(BF16)
            att = jnp.where(causal, _dot_nt(qt, kt), 0.0).astype(BF16)
            qi = (qh * jnp.exp(gh)).astype(BF16)
            kl = kh * jnp.exp(gl - gh)
            o_prev = []
            for a in range(2):
                s_a = s0_ref[2 * p + a, hh]
                o_prev.append(_dot(qi, s_a.astype(BF16)))
                kl_a = jnp.where(first_seq if a == 0 else second_seq, kl, 0.0).astype(BF16)
                e = jnp.exp(gh[(a + 1) * seg - 1:(a + 1) * seg, :])
                e1 = e.astype(BF16).astype(F32)
                e2 = (e - e1).astype(BF16).astype(F32)
                e3 = (e - e1 - e2).astype(BF16).astype(F32)
                dec_l = jnp.where(rr == 0, e1, jnp.where(rr == 1, e2, jnp.where(rr == 2, e3, 0.0)))
                dec = _dot_tn(dec_l.astype(BF16), jnp.ones((pair, C_DK), BF16))
                s1_ref[2 * p + a, hh] = dec * s_a + _dot_tn(kl_a, vh)
            ob[pl.ds(r0, pair), sl] = _dot(att, vh) + jnp.where(first_seq, o_prev[0], o_prev[1])
        return carry

    lax.fori_loop(0, nblk // 2, body, 0)

    @pl.when(i == pl.num_programs(0) - 1)
    def _():
        x = x_ref[...]
        h = _rms(x, gpre_ref[...]).astype(BF16)
        y_ref[...] = _hgrn_gate_out(x, h, ob, mixbuf, win_ref, ng_ref, wout_ref, gpost_ref)


def _hgrn_sample(layer, x, s0, seg, gpre, gpost, win, lbl, ng, wout):
    rows, d = x.shape
    nseq = rows // seg
    dkt = C_HEADS * C_DK
    sb = SAMPLE_SEQ_BLOCK
    full = lambda shape: pl.BlockSpec(shape, lambda i: (0,) * len(shape))
    state = pl.BlockSpec((sb, C_HEADS, C_DK, C_DK), lambda i: (i, 0, 0, 0))
    return pl.pallas_call(
        functools.partial(_hgrn_sample_kernel, layer, seg),
        grid=(nseq // sb,),
        in_specs=[_resident(x.shape), state, _resident(gpre.shape), _resident(gpost.shape), _resident(win.shape),
                  _resident(lbl.shape), _resident(ng.shape), _resident(wout.shape)],
        out_specs=[full((rows, d)), state],
        out_shape=[jax.ShapeDtypeStruct((rows, d), F32), jax.ShapeDtypeStruct(s0.shape, F32)],
        scratch_shapes=[pltpu.VMEM((rows, dkt), F32), pltpu.VMEM((rows, dkt), F32),
                        pltpu.VMEM((rows, dkt), F32), pltpu.VMEM((rows, dkt), F32),
                        pltpu.VMEM((rows, dkt), F32), pltpu.VMEM((rows, dkt), BF16)],
        compiler_params=_params(("arbitrary",)),
        name="hgrn_sample",
    )(x, s0, gpre, gpost, win, lbl, ng, wout)


def _head_permutation():
    cols = []
    for g in range(GROUP):
        for kv in range(KV_HEADS):
            hidx = kv * GROUP + g
            cols.extend(range(hidx * HEAD_DIM, (hidx + 1) * HEAD_DIM))
    return jnp.asarray(cols, dtype=jnp.int32)


def kernel(x_prompt, x_sample, state_conv_a, cache_swa_k, cache_swa_v, state_hgrn, state_ffn_conv, norm_mix_pre, norm_mix_post, norm_ffn_pre, norm_ffn_post, w_in_ab, conv_a_w, attn_sinks, w_out_ab, w_in_c, hgrn_lb_logits, hgrn_norm_g, w_out_c, w_ffn_up, ffn_conv_w, w_ffn_down):
    bp, seq, d = x_prompt.shape
    ns, seg, _ = x_sample.shape
    aw = conv_a_w.shape[-1]
    nq = GROUP * KV_HEADS * HEAD_DIM
    nkv = KV_HEADS * HEAD_DIM

    perm = _head_permutation()
    q0 = 3 * aw
    win_ab = w_in_ab[0]
    win_ab = jnp.concatenate([win_ab[:, :q0], win_ab[:, q0:q0 + nq][:, perm], win_ab[:, q0 + nq:]], axis=1).astype(BF16)
    wout_ab = w_out_ab[0]
    wout_ab = jnp.concatenate([wout_ab[:aw], wout_ab[aw:][perm]], axis=0).astype(BF16)
    win_c = w_in_c[0].astype(BF16)
    wout_c = w_out_c[0].astype(BF16)
    wup = w_ffn_up.astype(BF16)
    wdown = w_ffn_down.astype(BF16)
    row = lambda a: a.reshape(1, -1)
    sinks = row(attn_sinks[0])
    ng = row(hgrn_norm_g[0])

    x1, conv_p, k_p, v_p = _mix_ab_prompt(x_prompt, row(norm_mix_pre[0]), row(norm_mix_post[0]), win_ab,
                                          conv_a_w[0], sinks, wout_ab)
    x2, ffn_p0 = _ffn_prompt(x1, row(norm_ffn_pre[0]), row(norm_ffn_post[0]), wup[0], ffn_conv_w[0], wdown[0])
    x3, hgrn_p = _hgrn_prompt(1, x2, row(norm_mix_pre[1]), row(norm_mix_post[1]), win_c, hgrn_lb_logits,
                              ng, wout_c)
    y_prompt, ffn_p1 = _ffn_prompt(x3, row(norm_ffn_pre[1]), row(norm_ffn_post[1]), wup[1], ffn_conv_w[1], wdown[1])

    xs = x_sample.reshape(ns * seg, d)
    kc = cache_swa_k[0].reshape(ns, WINDOW, nkv)
    vc = cache_swa_v[0].reshape(ns, WINDOW, nkv)
    s1, conv_s, k_s, v_s = _mix_ab_sample(xs, state_conv_a[0], kc, vc, seg, row(norm_mix_pre[0]),
                                          row(norm_mix_post[0]), win_ab, conv_a_w[0], sinks, wout_ab)
    s2, ffn_s0 = _ffn_sample(s1, state_ffn_conv[0], seg, row(norm_ffn_pre[0]), row(norm_ffn_post[0]),
                             wup[0], ffn_conv_w[0], wdown[0])
    s3, hgrn_s = _hgrn_sample(1, s2, state_hgrn[0], seg, row(norm_mix_pre[1]), row(norm_mix_post[1]),
                              win_c, hgrn_lb_logits, ng, wout_c)
    y_sample, ffn_s1 = _ffn_sample(s3, state_ffn_conv[1], seg, row(norm_ffn_pre[1]), row(norm_ffn_post[1]),
                                   wup[1], ffn_conv_w[1], wdown[1])

    kvshape = (KV_HEADS, HEAD_DIM)
    return (y_prompt, y_sample.reshape(ns, seg, d),
            conv_p[None], conv_s[None],
            k_p.reshape(1, bp, WINDOW, *kvshape), k_s.reshape(1, ns, WINDOW, *kvshape),
            v_p.reshape(1, bp, WINDOW, *kvshape), v_s.reshape(1, ns, WINDOW, *kvshape),
            hgrn_p[None], hgrn_s[None],
            jnp.stack([ffn_p0, ffn_p1]), jnp.stack([ffn_s0, ffn_s1]))
```

```python
import functools
import math

import jax
import jax.numpy as jnp
from jax import lax
from jax.experimental import pallas as pl
from jax.experimental.pallas import tpu as pltpu

F32 = jnp.float32
BF16 = jnp.bfloat16

EPS = 1e-6
WINDOW = 128
KV_HEADS = 2
GROUP = 4
HEAD_DIM = 64
C_HEADS = 8
C_DK = 128
NEG = -1e30
SUBLANES = 8
PROMPT_TILE = 512
HGRN_CHUNK = 64
FFN_CHUNK = 256
SAMPLE_SEQ_BLOCK = 8
VMEM_LIMIT_V7X = 56 * 2**20


def _dot(a, b):
    return jnp.dot(a, b, preferred_element_type=F32)


def _dot_nt(a, b):
    return lax.dot_general(a, b, (((1,), (1,)), ((), ())), preferred_element_type=F32)


def _dot_tn(a, b):
    return lax.dot_general(a, b, (((0,), (0,)), ((), ())), preferred_element_type=F32)


def _rms(x, g):
    return x * lax.rsqrt(jnp.mean(x * x, axis=-1, keepdims=True) + EPS) * g


def _gelu_tanh(x):
    c = math.sqrt(2.0 / math.pi)
    return x * (0.5 * (1.0 + jnp.tanh(c * (x + 0.044715 * (x * x * x)))))


def _sigmoid(x):
    return 1.0 / (1.0 + jnp.exp(-x))


def _resident(shape):
    nd = len(shape)
    return pl.BlockSpec(shape, lambda *_: (0,) * nd, pipeline_mode=pl.Buffered(1))


def _params(sem):
    return pltpu.CompilerParams(dimension_semantics=sem, vmem_limit_bytes=VMEM_LIMIT_V7X)


def _head_slope(h):
    return 2.0 ** (-8.0 * (h + 1) / (KV_HEADS * GROUP))


def _mix_ab_prompt_kernel(x_ref, gpre_ref, gpost_ref, win_ref, cw_ref, sink_ref, wout_ref,
                          y_ref, conv_ref, kout_ref, vout_ref,
                          ubuf, qbuf, kbuf, vbuf, mixbuf, bias_s):
    s = pl.program_id(1)
    ts = x_ref.shape[0]
    aw = cw_ref.shape[1]
    nq = GROUP * KV_HEADS * HEAD_DIM
    nkv = KV_HEADS * HEAD_DIM

    @pl.when(s == 0)
    def _():
        ubuf[0:SUBLANES, :] = jnp.zeros((SUBLANES, aw), F32)
        kbuf[0:WINDOW, :] = jnp.zeros((WINDOW, nkv), BF16)
        vbuf[0:WINDOW, 0:nkv] = jnp.zeros((WINDOW, nkv), BF16)
        vbuf[:, nkv:2 * nkv] = jnp.ones((ts + WINDOW, nkv), BF16)

    x = x_ref[...]
    h = _rms(x, gpre_ref[...]).astype(BF16)

    u = _dot(h, win_ref[:, 2 * aw:3 * aw]) * _dot(h, win_ref[:, 0:aw])
    ubuf[SUBLANES:SUBLANES + ts, :] = u
    cw = cw_ref[...]
    conv = (cw[2:3, :] * u + cw[1:2, :] * ubuf[SUBLANES - 1:SUBLANES - 1 + ts, :]
            + cw[0:1, :] * ubuf[SUBLANES - 2:SUBLANES - 2 + ts, :])
    mixbuf[:, 0:aw] = (_dot(h, win_ref[:, aw:2 * aw]) * conv).astype(BF16)
    tail = ubuf[ts + SUBLANES - 2:ts + SUBLANES, :]
    conv_ref[...] = tail
    ubuf[SUBLANES - 2:SUBLANES, :] = tail

    q0 = 3 * aw
    q = _dot(h, win_ref[:, q0:q0 + nq]) * (HEAD_DIM ** -0.5)
    lane = lax.broadcasted_iota(jnp.int32, (1, nkv), 1)
    lo = lane < HEAD_DIM
    nblk = ts // WINDOW
    for n in range(nblk):
        for g in range(GROUP):
            qg = q[n * WINDOW:(n + 1) * WINDOW, g * nkv:(g + 1) * nkv]
            qbuf[n, g * WINDOW:(g + 1) * WINDOW, :] = jnp.where(lo, qg, 0.0).astype(BF16)
            qbuf[n, (GROUP + g) * WINDOW:(GROUP + g + 1) * WINDOW, :] = jnp.where(lo, 0.0, qg).astype(BF16)
    kv_new = _dot(h, win_ref[:, q0 + nq:q0 + nq + 2 * nkv])
    k = kv_new[:, 0:nkv]
    v = kv_new[:, nkv:2 * nkv]
    kbuf[WINDOW:WINDOW + ts, :] = k.astype(BF16)
    vbuf[WINDOW:WINDOW + ts, 0:nkv] = v.astype(BF16)
    kout_ref[...] = k[ts - WINDOW:, :]
    vout_ref[...] = v[ts - WINDOW:, :]

    qi = lax.broadcasted_iota(jnp.int32, (WINDOW, 2 * WINDOW), 0)
    kj = lax.broadcasted_iota(jnp.int32, (WINDOW, 2 * WINDOW), 1)
    dist = qi + WINDOW - kj
    valid = (dist >= 0) & (dist < WINDOW)
    negdist = jnp.where(valid, -dist.astype(F32), 0.0)
    maskadd = jnp.where(valid, 0.0, NEG)
    nheads = KV_HEADS * GROUP
    for hidx in range(nheads):
        bias_s[hidx] = _head_slope(hidx) * negdist + maskadd
    first = jnp.where(kj < WINDOW, jnp.where(s == 0, NEG, 0.0), 0.0)

    for n in range(nblk):
        r0 = n * WINDOW
        sc_all = _dot_nt(qbuf[n], kbuf[r0:r0 + 2 * WINDOW, :])
        ps, sink_terms = [], []
        for hidx in range(nheads):
            sc = sc_all[hidx * WINDOW:(hidx + 1) * WINDOW, :] + bias_s[hidx]
            if n == 0:
                sc = sc + first
            sink = sink_ref[hidx]
            m = jnp.maximum(jnp.max(sc, axis=-1, keepdims=True), sink)
            ps.append(jnp.exp(sc - m).astype(BF16))
            sink_terms.append(jnp.exp(sink - m))
        pv = _dot(jnp.concatenate(ps, axis=0), vbuf[r0:r0 + 2 * WINDOW, :])
        outs = []
        for hidx in range(nheads):
            pv_h = pv[hidx * WINDOW:(hidx + 1) * WINDOW, :]
            outs.append(pv_h[:, 0:nkv] / (pv_h[:, nkv:2 * nkv] + sink_terms[hidx]))
        for g in range(GROUP):
            og = jnp.where(lo, outs[g], outs[GROUP + g])
            mixbuf[r0:r0 + WINDOW, aw + g * nkv:aw + (g + 1) * nkv] = og.astype(BF16)

    kbuf[0:WINDOW, :] = kbuf[ts:ts + WINDOW, :]
    vbuf[0:WINDOW, :] = vbuf[ts:ts + WINDOW, :]

    out = _dot(mixbuf[...], wout_ref[...])
    y_ref[...] = x + _rms(out, gpost_ref[...])


def _mix_ab_prompt(x, gpre, gpost, win, cw, sinks, wout):
    b, seq, d = x.shape
    ts = PROMPT_TILE
    aw = cw.shape[1]
    nq = GROUP * KV_HEADS * HEAD_DIM
    nkv = KV_HEADS * HEAD_DIM
    tile = pl.BlockSpec((None, ts, d), lambda i, j: (i, j, 0))
    return pl.pallas_call(
        _mix_ab_prompt_kernel,
        grid=(b, seq // ts),
        in_specs=[tile, _resident(gpre.shape), _resident(gpost.shape), _resident(win.shape),
                  _resident(cw.shape), pl.BlockSpec(memory_space=pltpu.SMEM), _resident(wout.shape)],
        out_specs=[tile,
                   pl.BlockSpec((None, 2, aw), lambda i, j: (i, 0, 0)),
                   pl.BlockSpec((None, WINDOW, nkv), lambda i, j: (i, 0, 0)),
                   pl.BlockSpec((None, WINDOW, nkv), lambda i, j: (i, 0, 0))],
        out_shape=[jax.ShapeDtypeStruct((b, seq, d), F32),
                   jax.ShapeDtypeStruct((b, 2, aw), F32),
                   jax.ShapeDtypeStruct((b, WINDOW, nkv), F32),
                   jax.ShapeDtypeStruct((b, WINDOW, nkv), F32)],
        scratch_shapes=[pltpu.VMEM((ts + SUBLANES, aw), F32),
                        pltpu.VMEM((ts // WINDOW, KV_HEADS * GROUP * WINDOW, nkv), BF16),
                        pltpu.VMEM((ts + WINDOW, nkv), BF16),
                        pltpu.VMEM((ts + WINDOW, 2 * nkv), BF16),
                        pltpu.VMEM((ts, d), BF16),
                        pltpu.VMEM((KV_HEADS * GROUP, WINDOW, 2 * WINDOW), F32)],
        compiler_params=_params(("arbitrary", "arbitrary")),
        name="mix_ab_prompt",
    )(x, gpre, gpost, win, cw, sinks.reshape(-1), wout)


def _ffn_prompt_kernel(x_ref, gpre_ref, gpost_ref, wup_ref, cw_ref, wdown_ref,
                       y_ref, st_ref, abuf, gbuf):
    s = pl.program_id(1)
    rows = x_ref.shape[0]
    ff = cw_ref.shape[1]

    @pl.when(s == 0)
    def _():
        abuf[0:SUBLANES, :] = jnp.zeros((SUBLANES, ff), F32)

    x = x_ref[...]
    h = _rms(x, gpre_ref[...]).astype(BF16)
    for c in range(ff // FFN_CHUNK):
        c0 = c * FFN_CHUNK
        a = _dot(h, wup_ref[:, c0:c0 + FFN_CHUNK])
        b = _dot(h, wup_ref[:, ff + c0:ff + c0 + FFN_CHUNK])
        abuf[SUBLANES:SUBLANES + rows, c0:c0 + FFN_CHUNK] = a
        cw = cw_ref[:, c0:c0 + FFN_CHUNK]
        conv = (cw[2:3, :] * a + cw[1:2, :] * abuf[SUBLANES - 1:SUBLANES - 1 + rows, c0:c0 + FFN_CHUNK]
                + cw[0:1, :] * abuf[SUBLANES - 2:SUBLANES - 2 + rows, c0:c0 + FFN_CHUNK])
        tail = a[rows - 2:rows, :]
        st_ref[:, c0:c0 + FFN_CHUNK] = tail
        abuf[SUBLANES - 2:SUBLANES, c0:c0 + FFN_CHUNK] = tail
        gbuf[:, c0:c0 + FFN_CHUNK] = (_gelu_tanh(conv) * b).astype(BF16)
    out = _dot(gbuf[...], wdown_ref[...])
    y_ref[...] = x + _rms(out, gpost_ref[...])


def _ffn_prompt(x, gpre, gpost, wup, cw, wdown):
    b, seq, d = x.shape
    ts = PROMPT_TILE
    ff = cw.shape[1]
    tile = pl.BlockSpec((None, ts, d), lambda i, j: (i, j, 0))
    return pl.pallas_call(
        _ffn_prompt_kernel,
        grid=(b, seq // ts),
        in_specs=[tile, _resident(gpre.shape), _resident(gpost.shape), _resident(wup.shape),
                  _resident(cw.shape), _resident(wdown.shape)],
        out_specs=[tile, pl.BlockSpec((None, 2, ff), lambda i, j: (i, 0, 0))],
        out_shape=[jax.ShapeDtypeStruct((b, seq, d), F32), jax.ShapeDtypeStruct((b, 2, ff), F32)],
        scratch_shapes=[pltpu.VMEM((ts + SUBLANES, ff), F32), pltpu.VMEM((ts, ff), BF16)],
        compiler_params=_params(("arbitrary", "arbitrary")),
        name="ffn_prompt",
    )(x, gpre, gpost, wup, cw, wdown)


def _ffn_sample_kernel(seg, x_ref, p1_ref, p2_ref, gpre_ref, gpost_ref, wa_ref, wb_ref, cw_ref, wdown_ref,
                       y_ref, a_ref, hbuf, abuf, acc):
    c = pl.program_id(0)
    rows = x_ref.shape[0]

    @pl.when(c == 0)
    def _():
        hbuf[...] = _rms(x_ref[...], gpre_ref[...]).astype(BF16)
        acc[...] = jnp.zeros(acc.shape, F32)
        abuf[0:SUBLANES, :] = jnp.zeros((SUBLANES, abuf.shape[1]), F32)

    h = hbuf[...]
    t = lax.broadcasted_iota(jnp.int32, (rows, 1), 0) % seg
    a = _dot(h, wa_ref[...])
    b = _dot(h, wb_ref[...])
    a_ref[...] = a
    abuf[SUBLANES:SUBLANES + rows, :] = a
    am1 = jnp.where(t >= 1, abuf[SUBLANES - 1:SUBLANES - 1 + rows, :], p1_ref[...])
    am2 = jnp.where(t >= 2, abuf[SUBLANES - 2:SUBLANES - 2 + rows, :], p2_ref[...])
    cw = cw_ref[...]
    conv = cw[2:3, :] * a + cw[1:2, :] * am1 + cw[0:1, :] * am2
    acc[...] += _dot((_gelu_tanh(conv) * b).astype(BF16), wdown_ref[...])

    @pl.when(c == pl.num_programs(0) - 1)
    def _():
        y_ref[...] = x_ref[...] + _rms(acc[...], gpost_ref[...])


def _ffn_sample(x, prev, seg, gpre, gpost, wup, cw, wdown):
    rows, d = x.shape
    ff = cw.shape[1]
    cf = FFN_CHUNK
    nc = ff // cf
    p1 = jnp.pad(prev[:, 1:2], ((0, 0), (0, seg - 1), (0, 0))).reshape(rows, ff)
    p2 = jnp.pad(prev, ((0, 0), (0, seg - 2), (0, 0))).reshape(rows, ff)
    cols = lambda r: pl.BlockSpec((r, cf), lambda c: (0, c))
    y, a = pl.pallas_call(
        functools.partial(_ffn_sample_kernel, seg),
        grid=(nc,),
        in_specs=[_resident(x.shape), cols(rows), cols(rows), _resident(gpre.shape), _resident(gpost.shape),
                  cols(d), pl.BlockSpec((d, cf), lambda c: (0, c + nc)), cols(3),
                  pl.BlockSpec((cf, d), lambda c: (c, 0))],
        out_specs=[pl.BlockSpec((rows, d), lambda c: (0, 0)), cols(rows)],
        out_shape=[jax.ShapeDtypeStruct((rows, d), F32), jax.ShapeDtypeStruct((rows, ff), F32)],
        scratch_shapes=[pltpu.VMEM((rows, d), BF16), pltpu.VMEM((rows + SUBLANES, cf), F32),
                        pltpu.VMEM((rows, d), F32)],
        compiler_params=_params(("arbitrary",)),
        name="ffn_sample",
    )(x, p1, p2, gpre, gpost, wup, wup, cw, wdown)
    return y, a.reshape(rows // seg, seg, ff)[:, seg - 2:]


LOG2E = 1.4426950408889634


def _lower_bound_terms(logits_ref, layer):
    lg = logits_ref[...]
    e = jnp.exp(lg - jnp.max(lg, axis=0, keepdims=True))
    pr = e / jnp.sum(e, axis=0, keepdims=True)
    cum = pr[0:1, :]
    for r in range(1, layer + 1):
        cum = cum + pr[r:r + 1, :]
    lb = jnp.clip(cum - pr[0:1, :], 0.0, 1.0)
    return jnp.log(lb) * LOG2E, jnp.log1p(-lb) * LOG2E, 1.0 - lb


def _forget_gate(f, lb_terms):
    log2_lb, log2_1m_lb, one_m_lb = lb_terms
    f2 = f * LOG2E
    e = jnp.exp2(jnp.minimum(f2, -f2))
    w = 1.0 + e
    b = log2_1m_lb + (jnp.minimum(f2, 0.0) - jnp.log2(w))
    d = log2_lb - b
    log2_ft = jnp.maximum(log2_lb, b) + jnp.log2(1.0 + jnp.exp2(jnp.minimum(d, -d)))
    return log2_ft, one_m_lb * (jnp.where(f2 >= 0.0, e, 1.0) / w)


def _split_bf16(x):
    hi = x.astype(BF16)
    return hi, (x - hi.astype(F32)).astype(BF16)


def _hgrn_gate_out(x, h, ob, mixbuf, win_ref, ng_ref, wout_ref, gpost_ref):
    dkt = C_HEADS * C_DK
    g = _dot(h, win_ref[:, 3 * dkt:4 * dkt])
    ng = ng_ref[...]
    for hh in range(C_HEADS):
        sl = slice(hh * C_DK, (hh + 1) * C_DK)
        mixbuf[:, sl] = (_rms(ob[:, sl], ng) * (g[:, sl] * _sigmoid(g[:, sl]))).astype(BF16)
    out = _dot(mixbuf[...], wout_ref[...])
    return x + _rms(out, gpost_ref[...])


def _hgrn_prompt_kernel(layer, x_ref, gpre_ref, gpost_ref, win_ref, lbl_ref, ng_ref, wout_ref,
                        y_ref, sout_ref, st, qb, kb, vb, lfb, ob, mixbuf):
    s = pl.program_id(1)
    ts = x_ref.shape[0]
    dkt = C_HEADS * C_DK
    ck = HGRN_CHUNK

    @pl.when(s == 0)
    def _():
        st[...] = jnp.zeros(st.shape, F32)

    x = x_ref[...]
    h = _rms(x, gpre_ref[...]).astype(BF16)
    f = _dot(h, win_ref[:, dkt:2 * dkt])
    lfb[...], kb[...] = _forget_gate(f, _lower_bound_terms(lbl_ref, layer))
    qb[...] = _dot(h, win_ref[:, 0:dkt])
    vb[...] = _dot(h, win_ref[:, 2 * dkt:3 * dkt]).astype(BF16)

    ri = lax.broadcasted_iota(jnp.int32, (ck, ck), 0)
    ci = lax.broadcasted_iota(jnp.int32, (ck, ck), 1)
    causal = ri >= ci
    tri = jnp.where(causal, 1.0, 0.0).astype(BF16)
    mid = ck // 2 - 1

    for c in range(ts // ck):
        rows = slice(c * ck, (c + 1) * ck)
        hi, lo = _split_bf16(lfb[rows, :])
        gcum = _dot(tri, hi) + _dot(tri, lo)
        for hh in range(C_HEADS):
            sl = slice(hh * C_DK, (hh + 1) * C_DK)
            gh = gcum[:, sl]
            gr = gh[mid:mid + 1, :]
            gl = gh[ck - 1:ck, :]
            qh = qb[rows, sl]
            kh = kb[rows, sl]
            vh = vb[rows, sl]
            qt = (qh * jnp.exp2(gh - gr)).astype(BF16)
            kt = (kh * jnp.exp2(gr - gh)).astype(BF16)
            att = jnp.where(causal, _dot_nt(qt, kt), 0.0).astype(BF16)
            st_h = st[hh]
            qi = (qh * jnp.exp2(gh)).astype(BF16)
            ob[rows, sl] = _dot(jnp.concatenate([qi, att], axis=1),
                                jnp.concatenate([st_h.astype(BF16).T, vh], axis=0))
            kl = (kh * jnp.exp2(gl - gh)).astype(BF16)
            st[hh] = jnp.exp2(gl) * st_h + _dot_tn(vh, kl)

    y_ref[...] = _hgrn_gate_out(x, h, ob, mixbuf, win_ref, ng_ref, wout_ref, gpost_ref)

    @pl.when(s == pl.num_programs(1) - 1)
    def _():
        for hh in range(C_HEADS):
            sout_ref[hh] = st[hh].T


def _hgrn_prompt(layer, x, gpre, gpost, win, lbl, ng, wout):
    b, seq, d = x.shape
    ts = PROMPT_TILE
    dkt = C_HEADS * C_DK
    tile = pl.BlockSpec((None, ts, d), lambda i, j: (i, j, 0))
    return pl.pallas_call(
        functools.partial(_hgrn_prompt_kernel, layer),
        grid=(b, seq // ts),
        in_specs=[tile, _resident(gpre.shape), _resident(gpost.shape), _resident(win.shape),
                  _resident(lbl.shape), _resident(ng.shape), _resident(wout.shape)],
        out_specs=[tile, pl.BlockSpec((None, C_HEADS, C_DK, C_DK), lambda i, j: (i, 0, 0, 0))],
        out_shape=[jax.ShapeDtypeStruct((b, seq, d), F32),
                   jax.ShapeDtypeStruct((b, C_HEADS, C_DK, C_DK), F32)],
        scratch_shapes=[pltpu.VMEM((C_HEADS, C_DK, C_DK), F32),
                        pltpu.VMEM((ts, dkt), F32), pltpu.VMEM((ts, dkt), F32),
                        pltpu.VMEM((ts, dkt), BF16), pltpu.VMEM((ts, dkt), F32),
                        pltpu.VMEM((ts, dkt), F32), pltpu.VMEM((ts, dkt), BF16)],
        compiler_params=_params(("arbitrary", "arbitrary")),
        name="hgrn_prompt",
    )(x, gpre, gpost, win, lbl, ng, wout)


def _mix_ab_sample_kernel(seg, x_ref, p1_ref, p2_ref, kc_ref, vc_ref, gpre_ref, gpost_ref, win_ref,
                          cw_ref, sink_ref, wout_ref,
                          y_ref, u_ref, knew_ref, vnew_ref,
                          ubuf, qbuf, yabuf, ybbuf, kall, vall, bias_s, sink_s):
    i = pl.program_id(0)
    rows = x_ref.shape[0]
    aw = cw_ref.shape[1]
    nq = GROUP * KV_HEADS * HEAD_DIM
    nkv = KV_HEADS * HEAD_DIM
    pair = 2 * seg
    qrows = KV_HEADS * GROUP * pair
    tk = WINDOW + pair

    @pl.when(i == 0)
    def _():
        x = x_ref[...]
        h = _rms(x, gpre_ref[...]).astype(BF16)
        t = lax.broadcasted_iota(jnp.int32, (rows, 1), 0) % seg
        u = _dot(h, win_ref[:, 2 * aw:3 * aw]) * _dot(h, win_ref[:, 0:aw])
        u_ref[...] = u
        ubuf[0:SUBLANES, :] = jnp.zeros((SUBLANES, aw), F32)
        ubuf[SUBLANES:SUBLANES + rows, :] = u
        um1 = jnp.where(t >= 1, ubuf[SUBLANES - 1:SUBLANES - 1 + rows, :], p1_ref[...])
        um2 = jnp.where(t >= 2, ubuf[SUBLANES - 2:SUBLANES - 2 + rows, :], p2_ref[...])
        cw = cw_ref[...]
        conv = cw[2:3, :] * u + cw[1:2, :] * um1 + cw[0:1, :] * um2
        yabuf[...] = (_dot(h, win_ref[:, aw:2 * aw]) * conv).astype(BF16)
        q0 = 3 * aw
        qbuf[...] = _dot(h, win_ref[:, q0:q0 + nq]) * (HEAD_DIM ** -0.5)
        knew_ref[...] = _dot(h, win_ref[:, q0 + nq:q0 + nq + nkv])
        vnew_ref[...] = _dot(h, win_ref[:, q0 + nq + nkv:q0 + nq + 2 * nkv])

        r = lax.broadcasted_iota(jnp.int32, (qrows, tk), 0)
        j = lax.broadcasted_iota(jnp.int32, (qrows, tk), 1)
        tq = r % seg
        second = (r // seg) % 2
        jn = j - WINDOW - seg * second
        dist = jnp.where(j < WINDOW, tq + WINDOW - j, tq - jn)
        valid = (dist >= 0) & (dist < WINDOW) & ((j < WINDOW) | ((jn >= 0) & (jn < seg)))
        rc = lax.broadcasted_iota(jnp.int32, (qrows, 1), 0)
        head = (rc // (GROUP * pair)) * GROUP + (rc // pair) % GROUP
        slope = jnp.zeros((qrows, 1), F32)
        sink = jnp.zeros((qrows, 1), F32)
        for hidx in range(KV_HEADS * GROUP):
            slope = jnp.where(head == hidx, _head_slope(hidx), slope)
            sink = jnp.where(head == hidx, sink_ref[:, hidx:hidx + 1], sink)
        bias_s[...] = jnp.where(valid, -slope * dist.astype(F32), NEG)
        sink_s[...] = sink

    nblk = kc_ref.shape[0]
    lane = lax.broadcasted_iota(jnp.int32, (1, nkv), 1)
    lo = lane < HEAD_DIM
    rq = lax.broadcasted_iota(jnp.int32, (qrows, 1), 0)
    first_seq = (rq // seg) % 2 == 0
    kv0 = rq < GROUP * pair

    def body(p, carry):
        r0 = pl.multiple_of((i * nblk + 2 * p) * seg, pair)
        q8 = qbuf[pl.ds(r0, pair), :]
        qg = jnp.concatenate([q8[:, g * nkv:(g + 1) * nkv] for g in range(GROUP)], axis=0)
        qs = jnp.concatenate([jnp.where(lo, qg, 0.0), jnp.where(lo, 0.0, qg)], axis=0).astype(BF16)
        knew = knew_ref[pl.ds(r0, pair), :]
        vnew = vnew_ref[pl.ds(r0, pair), :]
        sc, ov = [], []
        for a in range(2):
            kall[a, 0:WINDOW, :] = kc_ref[2 * p + a]
            kall[a, WINDOW:tk, :] = knew
            vall[a, 0:WINDOW, :] = vc_ref[2 * p + a]
            vall[a, WINDOW:tk, :] = vnew
            sc.append(_dot_nt(qs, kall[a].astype(BF16)))
        s2 = jnp.where(first_seq, sc[0], sc[1]) + bias_s[...]
        sink = sink_s[...]
        m = jnp.maximum(jnp.max(s2, axis=-1, keepdims=True), sink)
        pr = jnp.exp(s2 - m)
        den = jnp.sum(pr, axis=-1, keepdims=True) + jnp.exp(sink - m)
        prb = pr.astype(BF16)
        for a in range(2):
            ov.append(_dot(prb, vall[a].astype(BF16)))
        o = jnp.where(first_seq, ov[0], ov[1]) / den
        og = jnp.where(lo, o[0:GROUP * pair, :], o[GROUP * pair:, :])
        for g in range(GROUP):
            ybbuf[pl.ds(r0, pair), g * nkv:(g + 1) * nkv] = og[g * pair:(g + 1) * pair, :]
        return carry

    lax.fori_loop(0, nblk // 2, body, 0)

    @pl.when(i == pl.num_programs(0) - 1)
    def _():
        out = _dot(yabuf[...], wout_ref[0:aw, :]) + _dot(ybbuf[...].astype(BF16), wout_ref[aw:, :])
        y_ref[...] = x_ref[...] + _rms(out, gpost_ref[...])


def _mix_ab_sample(x, conv_prev, kcache, vcache, seg, gpre, gpost, win, cw, sinks, wout):
    rows, d = x.shape
    nseq = rows // seg
    aw = cw.shape[1]
    nq = GROUP * KV_HEADS * HEAD_DIM
    nkv = KV_HEADS * HEAD_DIM
    sb = SAMPLE_SEQ_BLOCK
    pair = 2 * seg
    qrows = KV_HEADS * GROUP * pair
    tk = WINDOW + pair
    p1 = jnp.pad(conv_prev[:, 1:2], ((0, 0), (0, seg - 1), (0, 0))).reshape(rows, aw)
    p2 = jnp.pad(conv_prev, ((0, 0), (0, seg - 2), (0, 0))).reshape(rows, aw)
    full = lambda shape: pl.BlockSpec(shape, lambda i: (0,) * len(shape))
    cache = pl.BlockSpec((sb, WINDOW, nkv), lambda i: (i, 0, 0))
    y, u, knew, vnew = pl.pallas_call(
        functools.partial(_mix_ab_sample_kernel, seg),
        grid=(nseq // sb,),
        in_specs=[_resident(x.shape), _resident(p1.shape), _resident(p2.shape), cache, cache,
                  _resident(gpre.shape), _resident(gpost.shape), _resident(win.shape), _resident(cw.shape),
                  _resident(sinks.shape), _resident(wout.shape)],
        out_specs=[full((rows, d)), full((rows, aw)), full((rows, nkv)), full((rows, nkv))],
        out_shape=[jax.ShapeDtypeStruct((rows, d), F32), jax.ShapeDtypeStruct((rows, aw), F32),
                   jax.ShapeDtypeStruct((rows, nkv), F32), jax.ShapeDtypeStruct((rows, nkv), F32)],
        scratch_shapes=[pltpu.VMEM((rows + SUBLANES, aw), F32),
                        pltpu.VMEM((rows, nq), F32),
                        pltpu.VMEM((rows, aw), BF16),
                        pltpu.VMEM((rows, nq), F32),
                        pltpu.VMEM((2, tk, nkv), F32),
                        pltpu.VMEM((2, tk, nkv), F32),
                        pltpu.VMEM((qrows, tk), F32),
                        pltpu.VMEM((qrows, 1), F32)],
        compiler_params=_params(("arbitrary",)),
        name="mix_ab_sample",
    )(x, p1, p2, kcache, vcache, gpre, gpost, win, cw, sinks, wout)
    conv_new = u.reshape(nseq, seg, aw)[:, seg - 2:]
    k_new = jnp.concatenate([kcache, knew.reshape(nseq, seg, nkv)], axis=1)[:, seg:]
    v_new = jnp.concatenate([vcache, vnew.reshape(nseq, seg, nkv)], axis=1)[:, seg:]
    return y, conv_new, k_new, v_new


def _hgrn_sample_kernel(layer, seg, x_ref, s0_ref, gpre_ref, gpost_ref, win_ref, lbl_ref, ng_ref, wout_ref,
                        y_ref, s1_ref, qb, kb, vb, gb, ob, mixbuf):
    i = pl.program_id(0)
    rows = x_ref.shape[0]
    dkt = C_HEADS * C_DK
    pair = 2 * seg

    @pl.when(i == 0)
    def _():
        x = x_ref[...]
        h = _rms(x, gpre_ref[...]).astype(BF16)
        f = _dot(h, win_ref[:, dkt:2 * dkt])
        log2_ft, kb[...] = _forget_gate(f, _lower_bound_terms(lbl_ref, layer))
        hi, lo = _split_bf16(log2_ft)
        ri = lax.broadcasted_iota(jnp.int32, (rows, rows), 0)
        ci = lax.broadcasted_iota(jnp.int32, (rows, rows), 1)
        tri = jnp.where((ri // seg == ci // seg) & (ri >= ci), 1.0, 0.0).astype(BF16)
        gb[...] = _dot(tri, hi) + _dot(tri, lo)
        qb[...] = _dot(h, win_ref[:, 0:dkt])
        vb[...] = _dot(h, win_ref[:, 2 * dkt:3 * dkt])

    nblk = s0_ref.shape[0]
    rr = lax.broadcasted_iota(jnp.int32, (pair, 1), 0)
    first_seq = rr < seg
    second_seq = rr >= seg
    ri = lax.broadcasted_iota(jnp.int32, (pair, pair), 0)
    ci = lax.broadcasted_iota(jnp.int32, (pair, pair), 1)
    causal = (ri // seg == ci // seg) & (ri >= ci)
    mid = seg // 2 - 1

    def body(p, carry):
        r0 = pl.multiple_of((i * nblk + 2 * p) * seg, pair)
        for hh in range(C_HEADS):
            sl = slice(hh * C_DK, (hh + 1) * C_DK)
            gh = gb[pl.ds(r0, pair), sl]
            gr = jnp.where(first_seq, gh[mid:mid + 1, :], gh[seg + mid:seg + mid + 1, :])
            gl = jnp.where(first_seq, gh[seg - 1:seg, :], gh[pair - 1:pair, :])
            qh = qb[pl.ds(r0, pair), sl]
            kh = kb[pl.ds(r0, pair), sl]
            vh = vb[pl.ds(r0, pair), sl].astype(BF16)
            qt = (qh * jnp.exp2(gh - gr)).astype(BF16)
            kt = (kh * jnp.exp2(gr - gh)).astype(BF16)
            att = jnp.where(causal, _dot_nt(qt, kt), 0.0).astype(BF16)
            qi = (qh * jnp.exp2(gh)).astype(BF16)
            kl = kh * jnp.exp2(gl - gh)
            o_prev = []
            for a in range(2):
                s_a = s0_ref[2 * p + a, hh]
                o_prev.append(_dot(qi, s_a.astype(BF16)))
                kl_a = jnp.where(first_seq if a == 0 else second_seq, kl, 0.0).astype(BF16)
                e = jnp.exp2(gh[(a + 1) * seg - 1:(a + 1) * seg, :])
                e1 = e.astype(BF16).astype(F32)
                e2 = (e - e1).astype(BF16).astype(F32)
                e3 = (e - e1 - e2).astype(BF16).astype(F32)
                dec_l = jnp.where(rr == 0, e1, jnp.where(rr == 1, e2, jnp.where(rr == 2, e3, 0.0)))
                dec = _dot_tn(dec_l.astype(BF16), jnp.ones((pair, C_DK), BF16))
                s1_ref[2 * p + a, hh] = dec * s_a + _dot_tn(kl_a, vh)
            ob[pl.ds(r0, pair), sl] = _dot(att, vh) + jnp.where(first_seq, o_prev[0], o_prev[1])
        return carry

    lax.fori_loop(0, nblk // 2, body, 0)

    @pl.when(i == pl.num_programs(0) - 1)
    def _():
        x = x_ref[...]
        h = _rms(x, gpre_ref[...]).astype(BF16)
        y_ref[...] = _hgrn_gate_out(x, h, ob, mixbuf, win_ref, ng_ref, wout_ref, gpost_ref)


def _hgrn_sample(layer, x, s0, seg, gpre, gpost, win, lbl, ng, wout):
    rows, d = x.shape
    nseq = rows // seg
    dkt = C_HEADS * C_DK
    sb = SAMPLE_SEQ_BLOCK
    full = lambda shape: pl.BlockSpec(shape, lambda i: (0,) * len(shape))
    state = pl.BlockSpec((sb, C_HEADS, C_DK, C_DK), lambda i: (i, 0, 0, 0))
    return pl.pallas_call(
        functools.partial(_hgrn_sample_kernel, layer, seg),
        grid=(nseq // sb,),
        in_specs=[_resident(x.shape), state, _resident(gpre.shape), _resident(gpost.shape), _resident(win.shape),
                  _resident(lbl.shape), _resident(ng.shape), _resident(wout.shape)],
        out_specs=[full((rows, d)), state],
        out_shape=[jax.ShapeDtypeStruct((rows, d), F32), jax.ShapeDtypeStruct(s0.shape, F32)],
        scratch_shapes=[pltpu.VMEM((rows, dkt), F32), pltpu.VMEM((rows, dkt), F32),
                        pltpu.VMEM((rows, dkt), F32), pltpu.VMEM((rows, dkt), F32),
                        pltpu.VMEM((rows, dkt), F32), pltpu.VMEM((rows, dkt), BF16)],
        compiler_params=_params(("arbitrary",)),
        name="hgrn_sample",
    )(x, s0, gpre, gpost, win, lbl, ng, wout)


def _head_permutation():
    cols = []
    for g in range(GROUP):
        for kv in range(KV_HEADS):
            hidx = kv * GROUP + g
            cols.extend(range(hidx * HEAD_DIM, (hidx + 1) * HEAD_DIM))
    return jnp.asarray(cols, dtype=jnp.int32)


def kernel(x_prompt, x_sample, state_conv_a, cache_swa_k, cache_swa_v, state_hgrn, state_ffn_conv, norm_mix_pre, norm_mix_post, norm_ffn_pre, norm_ffn_post, w_in_ab, conv_a_w, attn_sinks, w_out_ab, w_in_c, hgrn_lb_logits, hgrn_norm_g, w_out_c, w_ffn_up, ffn_conv_w, w_ffn_down):
    bp, seq, d = x_prompt.shape
    ns, seg, _ = x_sample.shape
    aw = conv_a_w.shape[-1]
    nq = GROUP * KV_HEADS * HEAD_DIM
    nkv = KV_HEADS * HEAD_DIM

    perm = _head_permutation()
    q0 = 3 * aw
    win_ab = w_in_ab[0]
    win_ab = jnp.concatenate([win_ab[:, :q0], win_ab[:, q0:q0 + nq][:, perm], win_ab[:, q0 + nq:]], axis=1).astype(BF16)
    wout_ab = w_out_ab[0]
    wout_ab = jnp.concatenate([wout_ab[:aw], wout_ab[aw:][perm]], axis=0).astype(BF16)
    win_c = w_in_c[0].astype(BF16)
    wout_c = w_out_c[0].astype(BF16)
    wup = w_ffn_up.astype(BF16)
    wdown = w_ffn_down.astype(BF16)
    row = lambda a: a.reshape(1, -1)
    sinks = row(attn_sinks[0])
    ng = row(hgrn_norm_g[0])

    x1, conv_p, k_p, v_p = _mix_ab_prompt(x_prompt, row(norm_mix_pre[0]), row(norm_mix_post[0]), win_ab,
                                          conv_a_w[0], sinks, wout_ab)
    x2, ffn_p0 = _ffn_prompt(x1, row(norm_ffn_pre[0]), row(norm_ffn_post[0]), wup[0], ffn_conv_w[0], wdown[0])
    x3, hgrn_p = _hgrn_prompt(1, x2, row(norm_mix_pre[1]), row(norm_mix_post[1]), win_c, hgrn_lb_logits,
                              ng, wout_c)
    y_prompt, ffn_p1 = _ffn_prompt(x3, row(norm_ffn_pre[1]), row(norm_ffn_post[1]), wup[1], ffn_conv_w[1], wdown[1])

    xs = x_sample.reshape(ns * seg, d)
    kc = cache_swa_k[0].reshape(ns, WINDOW, nkv)
    vc = cache_swa_v[0].reshape(ns, WINDOW, nkv)
    s1, conv_s, k_s, v_s = _mix_ab_sample(xs, state_conv_a[0], kc, vc, seg, row(norm_mix_pre[0]),
                                          row(norm_mix_post[0]), win_ab, conv_a_w[0], sinks, wout_ab)
    s2, ffn_s0 = _ffn_sample(s1, state_ffn_conv[0], seg, row(norm_ffn_pre[0]), row(norm_ffn_post[0]),
                             wup[0], ffn_conv_w[0], wdown[0])
    s3, hgrn_s = _hgrn_sample(1, s2, state_hgrn[0], seg, row(norm_mix_pre[1]), row(norm_mix_post[1]),
                              win_c, hgrn_lb_logits, ng, wout_c)
    y_sample, ffn_s1 = _ffn_sample(s3, state_ffn_conv[1], seg, row(norm_ffn_pre[1]), row(norm_ffn_post[1]),
                                   wup[1], ffn_conv_w[1], wdown[1])

    kvshape = (KV_HEADS, HEAD_DIM)
    return (y_prompt, y_sample.reshape(ns, seg, d),
            conv_p[None], conv_s[None],
            k_p.reshape(1, bp, WINDOW, *kvshape), k_s.reshape(1, ns, WINDOW, *kvshape),
            v_p.reshape(1, bp, WINDOW, *kvshape), v_s.reshape(1, ns, WINDOW, *kvshape),
            hgrn_p[None], hgrn_s[None],
            jnp.stack([ffn_p0, ffn_p1]), jnp.stack([ffn_s0, ffn_s1]))
```

```python
import functools
import math

import jax
import jax.numpy as jnp
from jax import lax
from jax.experimental import pallas as pl
from jax.experimental.pallas import tpu as pltpu

F32 = jnp.float32
BF16 = jnp.bfloat16

EPS = 1e-6
WINDOW = 128
KV_HEADS = 2
GROUP = 4
HEAD_DIM = 64
C_HEADS = 8
C_DK = 128
NEG = -1e30
SUBLANES = 8
LANES = 128
PROMPT_TILE = 512
HGRN_CHUNK = 64
FFN_CHUNK = 256
SAMPLE_SEQ_BLOCK = 8
VMEM_LIMIT_V7X = 56 * 2**20


def _dot(a, b):
    return jnp.dot(a, b, preferred_element_type=F32)


def _dot_nt(a, b):
    return lax.dot_general(a, b, (((1,), (1,)), ((), ())), preferred_element_type=F32)


def _dot_tn(a, b):
    return lax.dot_general(a, b, (((0,), (0,)), ((), ())), preferred_element_type=F32)


def _rms(x, g):
    return x * lax.rsqrt(jnp.mean(x * x, axis=-1, keepdims=True) + EPS) * g


def _gelu_tanh(x):
    c = math.sqrt(2.0 / math.pi)
    return x * (0.5 * (1.0 + jnp.tanh(c * (x + 0.044715 * (x * x * x)))))


def _sigmoid(x):
    return 1.0 / (1.0 + jnp.exp(-x))


def _resident(shape):
    nd = len(shape)
    return pl.BlockSpec(shape, lambda *_: (0,) * nd, pipeline_mode=pl.Buffered(1))


def _params(sem):
    return pltpu.CompilerParams(dimension_semantics=sem, vmem_limit_bytes=VMEM_LIMIT_V7X)


def _head_slope(h):
    return 2.0 ** (-8.0 * (h + 1) / (KV_HEADS * GROUP))


def _mix_ab_prompt_kernel(x_ref, gpre_ref, gpost_ref, win_ref, cw_ref, sink_ref, wout_ref,
                          y_ref, conv_ref, kout_ref, vout_ref,
                          ubuf, qbuf, kbuf, vbuf, mixbuf, bias_s):
    s = pl.program_id(1)
    ts = x_ref.shape[0]
    aw = cw_ref.shape[1]
    nq = GROUP * KV_HEADS * HEAD_DIM
    nkv = KV_HEADS * HEAD_DIM

    @pl.when(s == 0)
    def _():
        ubuf[0:SUBLANES, :] = jnp.zeros((SUBLANES, aw), F32)
        kbuf[0:WINDOW, :] = jnp.zeros((WINDOW, nkv), BF16)
        vbuf[0:WINDOW, 0:nkv] = jnp.zeros((WINDOW, nkv), BF16)
        vbuf[:, nkv:2 * nkv] = jnp.ones((ts + WINDOW, nkv), BF16)

    x = x_ref[...]
    h = _rms(x, gpre_ref[...]).astype(BF16)

    u = _dot(h, win_ref[:, 2 * aw:3 * aw]) * _dot(h, win_ref[:, 0:aw])
    ubuf[SUBLANES:SUBLANES + ts, :] = u
    cw = cw_ref[...]
    conv = (cw[2:3, :] * u + cw[1:2, :] * ubuf[SUBLANES - 1:SUBLANES - 1 + ts, :]
            + cw[0:1, :] * ubuf[SUBLANES - 2:SUBLANES - 2 + ts, :])
    mixbuf[:, 0:aw] = (_dot(h, win_ref[:, aw:2 * aw]) * conv).astype(BF16)
    tail = ubuf[ts + SUBLANES - 2:ts + SUBLANES, :]
    conv_ref[...] = tail
    ubuf[SUBLANES - 2:SUBLANES, :] = tail

    q0 = 3 * aw
    q = _dot(h, win_ref[:, q0:q0 + nq]) * (HEAD_DIM ** -0.5)
    lane = lax.broadcasted_iota(jnp.int32, (1, nkv), 1)
    lo = lane < HEAD_DIM
    nblk = ts // WINDOW
    for n in range(nblk):
        for g in range(GROUP):
            qg = q[n * WINDOW:(n + 1) * WINDOW, g * nkv:(g + 1) * nkv]
            qbuf[n, g * WINDOW:(g + 1) * WINDOW, :] = jnp.where(lo, qg, 0.0).astype(BF16)
            qbuf[n, (GROUP + g) * WINDOW:(GROUP + g + 1) * WINDOW, :] = jnp.where(lo, 0.0, qg).astype(BF16)
    kv_new = _dot(h, win_ref[:, q0 + nq:q0 + nq + 2 * nkv])
    k = kv_new[:, 0:nkv]
    v = kv_new[:, nkv:2 * nkv]
    kbuf[WINDOW:WINDOW + ts, :] = k.astype(BF16)
    vbuf[WINDOW:WINDOW + ts, 0:nkv] = v.astype(BF16)
    kout_ref[...] = k[ts - WINDOW:, :]
    vout_ref[...] = v[ts - WINDOW:, :]

    qi = lax.broadcasted_iota(jnp.int32, (WINDOW, 2 * WINDOW), 0)
    kj = lax.broadcasted_iota(jnp.int32, (WINDOW, 2 * WINDOW), 1)
    dist = qi + WINDOW - kj
    valid = (dist >= 0) & (dist < WINDOW)
    negdist = jnp.where(valid, -dist.astype(F32), 0.0)
    maskadd = jnp.where(valid, 0.0, NEG)
    nheads = KV_HEADS * GROUP
    for hidx in range(nheads):
        bias_s[hidx] = _head_slope(hidx) * negdist + maskadd
    first = jnp.where(kj < WINDOW, jnp.where(s == 0, NEG, 0.0), 0.0)

    for n in range(nblk):
        r0 = n * WINDOW
        sc_all = _dot_nt(qbuf[n], kbuf[r0:r0 + 2 * WINDOW, :])
        ps, sink_terms = [], []
        for hidx in range(nheads):
            sc = sc_all[hidx * WINDOW:(hidx + 1) * WINDOW, :] + bias_s[hidx]
            if n == 0:
                sc = sc + first
            sink = sink_ref[hidx]
            m = jnp.maximum(jnp.max(sc, axis=-1, keepdims=True), sink)
            ps.append(jnp.exp(sc - m).astype(BF16))
            sink_terms.append(jnp.exp(sink - m))
        pv = _dot(jnp.concatenate(ps, axis=0), vbuf[r0:r0 + 2 * WINDOW, :])
        outs = []
        for hidx in range(nheads):
            pv_h = pv[hidx * WINDOW:(hidx + 1) * WINDOW, :]
            outs.append(pv_h[:, 0:nkv] / (pv_h[:, nkv:2 * nkv] + sink_terms[hidx]))
        for g in range(GROUP):
            og = jnp.where(lo, outs[g], outs[GROUP + g])
            mixbuf[r0:r0 + WINDOW, aw + g * nkv:aw + (g + 1) * nkv] = og.astype(BF16)

    kbuf[0:WINDOW, :] = kbuf[ts:ts + WINDOW, :]
    vbuf[0:WINDOW, :] = vbuf[ts:ts + WINDOW, :]

    out = _dot(mixbuf[...], wout_ref[...])
    y_ref[...] = x + _rms(out, gpost_ref[...])


def _mix_ab_prompt(x, gpre, gpost, win, cw, sinks, wout):
    b, seq, d = x.shape
    ts = PROMPT_TILE
    aw = cw.shape[1]
    nq = GROUP * KV_HEADS * HEAD_DIM
    nkv = KV_HEADS * HEAD_DIM
    tile = pl.BlockSpec((None, ts, d), lambda i, j: (i, j, 0))
    return pl.pallas_call(
        _mix_ab_prompt_kernel,
        grid=(b, seq // ts),
        in_specs=[tile, _resident(gpre.shape), _resident(gpost.shape), _resident(win.shape),
                  _resident(cw.shape), pl.BlockSpec(memory_space=pltpu.SMEM), _resident(wout.shape)],
        out_specs=[tile,
                   pl.BlockSpec((None, 2, aw), lambda i, j: (i, 0, 0)),
                   pl.BlockSpec((None, WINDOW, nkv), lambda i, j: (i, 0, 0)),
                   pl.BlockSpec((None, WINDOW, nkv), lambda i, j: (i, 0, 0))],
        out_shape=[jax.ShapeDtypeStruct((b, seq, d), F32),
                   jax.ShapeDtypeStruct((b, 2, aw), F32),
                   jax.ShapeDtypeStruct((b, WINDOW, nkv), F32),
                   jax.ShapeDtypeStruct((b, WINDOW, nkv), F32)],
        scratch_shapes=[pltpu.VMEM((ts + SUBLANES, aw), F32),
                        pltpu.VMEM((ts // WINDOW, KV_HEADS * GROUP * WINDOW, nkv), BF16),
                        pltpu.VMEM((ts + WINDOW, nkv), BF16),
                        pltpu.VMEM((ts + WINDOW, 2 * nkv), BF16),
                        pltpu.VMEM((ts, d), BF16),
                        pltpu.VMEM((KV_HEADS * GROUP, WINDOW, 2 * WINDOW), F32)],
        compiler_params=_params(("arbitrary", "arbitrary")),
        name="mix_ab_prompt",
    )(x, gpre, gpost, win, cw, sinks.reshape(-1), wout)


def _ffn_prompt_kernel(x_ref, gpre_ref, gpost_ref, wup_ref, cw_ref, wdown_ref,
                       y_ref, st_ref, abuf, gbuf):
    s = pl.program_id(1)
    rows = x_ref.shape[0]
    ff = cw_ref.shape[1]

    @pl.when(s == 0)
    def _():
        abuf[0:SUBLANES, :] = jnp.zeros((SUBLANES, ff), F32)

    x = x_ref[...]
    h = _rms(x, gpre_ref[...]).astype(BF16)
    for c in range(ff // FFN_CHUNK):
        c0 = c * FFN_CHUNK
        a = _dot(h, wup_ref[:, c0:c0 + FFN_CHUNK])
        b = _dot(h, wup_ref[:, ff + c0:ff + c0 + FFN_CHUNK])
        abuf[SUBLANES:SUBLANES + rows, c0:c0 + FFN_CHUNK] = a
        cw = cw_ref[:, c0:c0 + FFN_CHUNK]
        conv = (cw[2:3, :] * a + cw[1:2, :] * abuf[SUBLANES - 1:SUBLANES - 1 + rows, c0:c0 + FFN_CHUNK]
                + cw[0:1, :] * abuf[SUBLANES - 2:SUBLANES - 2 + rows, c0:c0 + FFN_CHUNK])
        tail = a[rows - 2:rows, :]
        st_ref[:, c0:c0 + FFN_CHUNK] = tail
        abuf[SUBLANES - 2:SUBLANES, c0:c0 + FFN_CHUNK] = tail
        gbuf[:, c0:c0 + FFN_CHUNK] = (_gelu_tanh(conv) * b).astype(BF16)
    out = _dot(gbuf[...], wdown_ref[...])
    y_ref[...] = x + _rms(out, gpost_ref[...])


def _ffn_prompt(x, gpre, gpost, wup, cw, wdown):
    b, seq, d = x.shape
    ts = PROMPT_TILE
    ff = cw.shape[1]
    tile = pl.BlockSpec((None, ts, d), lambda i, j: (i, j, 0))
    return pl.pallas_call(
        _ffn_prompt_kernel,
        grid=(b, seq // ts),
        in_specs=[tile, _resident(gpre.shape), _resident(gpost.shape), _resident(wup.shape),
                  _resident(cw.shape), _resident(wdown.shape)],
        out_specs=[tile, pl.BlockSpec((None, 2, ff), lambda i, j: (i, 0, 0))],
        out_shape=[jax.ShapeDtypeStruct((b, seq, d), F32), jax.ShapeDtypeStruct((b, 2, ff), F32)],
        scratch_shapes=[pltpu.VMEM((ts + SUBLANES, ff), F32), pltpu.VMEM((ts, ff), BF16)],
        compiler_params=_params(("arbitrary", "arbitrary")),
        name="ffn_prompt",
    )(x, gpre, gpost, wup, cw, wdown)


def _segment_conv_taps(a, t, seg, prev0_ref, prev1_ref, abuf, fix1, fix2):
    rows, c = a.shape
    nseq = rows // seg
    am1, am2 = [], []
    for k in range(c // LANES):
        sl = slice(k * LANES, (k + 1) * LANES)
        abuf[k, SUBLANES:SUBLANES + rows, :] = a[:, sl]
        fix1[k, pl.ds(0, nseq, stride=seg), :] = prev1_ref[:, sl]
        fix2[k, pl.ds(0, nseq, stride=seg), :] = prev0_ref[:, sl]
        fix2[k, pl.ds(1, nseq, stride=seg), :] = prev1_ref[:, sl]
        am1.append(jnp.where(t >= 1, abuf[k, SUBLANES - 1:SUBLANES - 1 + rows, :], fix1[k]))
        am2.append(jnp.where(t >= 2, abuf[k, SUBLANES - 2:SUBLANES - 2 + rows, :], fix2[k]))
    return jnp.concatenate(am1, axis=1), jnp.concatenate(am2, axis=1)


def _segment_tails(abuf, nseq, seg):
    tail = lambda back: jnp.concatenate(
        [abuf[k, pl.ds(SUBLANES + seg - back, nseq, stride=seg), :] for k in range(abuf.shape[0])], axis=1)
    return tail(2), tail(1)


def _ffn_sample_kernel(seg, x_ref, prev0_ref, prev1_ref, gpre_ref, gpost_ref, wa_ref, wb_ref, cw_ref, wdown_ref,
                       y_ref, st0_ref, st1_ref, hbuf, abuf, fix1, fix2, acc):
    c = pl.program_id(0)
    rows = x_ref.shape[0]

    @pl.when(c == 0)
    def _():
        hbuf[...] = _rms(x_ref[...], gpre_ref[...]).astype(BF16)
        acc[...] = jnp.zeros(acc.shape, F32)
        abuf[:, 0:SUBLANES, :] = jnp.zeros((abuf.shape[0], SUBLANES, LANES), F32)
        fix1[...] = jnp.zeros(fix1.shape, F32)
        fix2[...] = jnp.zeros(fix2.shape, F32)

    h = hbuf[...]
    t = lax.broadcasted_iota(jnp.int32, (rows, 1), 0) % seg
    a = _dot(h, wa_ref[...])
    b = _dot(h, wb_ref[...])
    am1, am2 = _segment_conv_taps(a, t, seg, prev0_ref, prev1_ref, abuf, fix1, fix2)
    st0_ref[...], st1_ref[...] = _segment_tails(abuf, rows // seg, seg)
    cw = cw_ref[...]
    conv = cw[2:3, :] * a + cw[1:2, :] * am1 + cw[0:1, :] * am2
    acc[...] += _dot((_gelu_tanh(conv) * b).astype(BF16), wdown_ref[...])

    @pl.when(c == pl.num_programs(0) - 1)
    def _():
        y_ref[...] = x_ref[...] + _rms(acc[...], gpost_ref[...])


def _ffn_sample(x, prev, seg, gpre, gpost, wup, cw, wdown):
    rows, d = x.shape
    nseq = rows // seg
    ff = cw.shape[1]
    cf = FFN_CHUNK
    nc = ff // cf
    cols = lambda r: pl.BlockSpec((r, cf), lambda c: (0, c))
    y, st0, st1 = pl.pallas_call(
        functools.partial(_ffn_sample_kernel, seg),
        grid=(nc,),
        in_specs=[_resident(x.shape), cols(nseq), cols(nseq), _resident(gpre.shape), _resident(gpost.shape),
                  cols(d), pl.BlockSpec((d, cf), lambda c: (0, c + nc)), cols(3),
                  pl.BlockSpec((cf, d), lambda c: (c, 0))],
        out_specs=[pl.BlockSpec((rows, d), lambda c: (0, 0)), cols(nseq), cols(nseq)],
        out_shape=[jax.ShapeDtypeStruct((rows, d), F32), jax.ShapeDtypeStruct((nseq, ff), F32),
                   jax.ShapeDtypeStruct((nseq, ff), F32)],
        scratch_shapes=[pltpu.VMEM((rows, d), BF16), pltpu.VMEM((cf // LANES, rows + SUBLANES, LANES), F32),
                        pltpu.VMEM((cf // LANES, rows, LANES), F32), pltpu.VMEM((cf // LANES, rows, LANES), F32),
                        pltpu.VMEM((rows, d), F32)],
        compiler_params=_params(("arbitrary",)),
        name="ffn_sample",
    )(x, prev[:, 0], prev[:, 1], gpre, gpost, wup, wup, cw, wdown)
    return y, jnp.stack([st0, st1], axis=1)


LOG2E = 1.4426950408889634


def _lower_bound_terms(logits_ref, layer):
    lg = logits_ref[...]
    e = jnp.exp(lg - jnp.max(lg, axis=0, keepdims=True))
    pr = e / jnp.sum(e, axis=0, keepdims=True)
    cum = pr[0:1, :]
    for r in range(1, layer + 1):
        cum = cum + pr[r:r + 1, :]
    lb = jnp.clip(cum - pr[0:1, :], 0.0, 1.0)
    return jnp.log(lb) * LOG2E, jnp.log1p(-lb) * LOG2E, 1.0 - lb


def _forget_gate(f, lb_terms):
    log2_lb, log2_1m_lb, one_m_lb = lb_terms
    f2 = f * LOG2E
    e = jnp.exp2(jnp.minimum(f2, -f2))
    w = 1.0 + e
    b = log2_1m_lb + (jnp.minimum(f2, 0.0) - jnp.log2(w))
    d = log2_lb - b
    log2_ft = jnp.maximum(log2_lb, b) + jnp.log2(1.0 + jnp.exp2(jnp.minimum(d, -d)))
    return log2_ft, one_m_lb * (jnp.where(f2 >= 0.0, e, 1.0) / w)


def _split_bf16(x):
    hi = x.astype(BF16)
    return hi, (x - hi.astype(F32)).astype(BF16)


def _hgrn_gate_out(x, h, ob, mixbuf, win_ref, ng_ref, wout_ref, gpost_ref):
    dkt = C_HEADS * C_DK
    g = _dot(h, win_ref[:, 3 * dkt:4 * dkt])
    ng = ng_ref[...]
    for hh in range(C_HEADS):
        sl = slice(hh * C_DK, (hh + 1) * C_DK)
        mixbuf[:, sl] = (_rms(ob[:, sl], ng) * (g[:, sl] * _sigmoid(g[:, sl]))).astype(BF16)
    out = _dot(mixbuf[...], wout_ref[...])
    return x + _rms(out, gpost_ref[...])


def _hgrn_prompt_kernel(layer, x_ref, gpre_ref, gpost_ref, win_ref, lbl_ref, ng_ref, wout_ref,
                        y_ref, sout_ref, st, qb, kb, vb, lfb, ob, mixbuf):
    s = pl.program_id(1)
    ts = x_ref.shape[0]
    dkt = C_HEADS * C_DK
    ck = HGRN_CHUNK

    @pl.when(s == 0)
    def _():
        st[...] = jnp.zeros(st.shape, F32)

    x = x_ref[...]
    h = _rms(x, gpre_ref[...]).astype(BF16)
    f = _dot(h, win_ref[:, dkt:2 * dkt])
    lfb[...], kb[...] = _forget_gate(f, _lower_bound_terms(lbl_ref, layer))
    qb[...] = _dot(h, win_ref[:, 0:dkt])
    vb[...] = _dot(h, win_ref[:, 2 * dkt:3 * dkt]).astype(BF16)

    ri = lax.broadcasted_iota(jnp.int32, (ck, ck), 0)
    ci = lax.broadcasted_iota(jnp.int32, (ck, ck), 1)
    causal = ri >= ci
    tri = jnp.where(causal, 1.0, 0.0).astype(BF16)
    mid = ck // 2 - 1

    for c in range(ts // ck):
        rows = slice(c * ck, (c + 1) * ck)
        hi, lo = _split_bf16(lfb[rows, :])
        gcum = _dot(tri, hi) + _dot(tri, lo)
        for hh in range(C_HEADS):
            sl = slice(hh * C_DK, (hh + 1) * C_DK)
            gh = gcum[:, sl]
            gr = gh[mid:mid + 1, :]
            gl = gh[ck - 1:ck, :]
            qh = qb[rows, sl]
            kh = kb[rows, sl]
            vh = vb[rows, sl]
            qt = (qh * jnp.exp2(gh - gr)).astype(BF16)
            kt = (kh * jnp.exp2(gr - gh)).astype(BF16)
            att = jnp.where(causal, _dot_nt(qt, kt), 0.0).astype(BF16)
            st_h = st[hh]
            qi = (qh * jnp.exp2(gh)).astype(BF16)
            ob[rows, sl] = _dot(jnp.concatenate([qi, att], axis=1),
                                jnp.concatenate([st_h.astype(BF16).T, vh], axis=0))
            kl = (kh * jnp.exp2(gl - gh)).astype(BF16)
            st[hh] = jnp.exp2(gl) * st_h + _dot_tn(vh, kl)

    y_ref[...] = _hgrn_gate_out(x, h, ob, mixbuf, win_ref, ng_ref, wout_ref, gpost_ref)

    @pl.when(s == pl.num_programs(1) - 1)
    def _():
        for hh in range(C_HEADS):
            sout_ref[hh] = st[hh].T


def _hgrn_prompt(layer, x, gpre, gpost, win, lbl, ng, wout):
    b, seq, d = x.shape
    ts = PROMPT_TILE
    dkt = C_HEADS * C_DK
    tile = pl.BlockSpec((None, ts, d), lambda i, j: (i, j, 0))
    return pl.pallas_call(
        functools.partial(_hgrn_prompt_kernel, layer),
        grid=(b, seq // ts),
        in_specs=[tile, _resident(gpre.shape), _resident(gpost.shape), _resident(win.shape),
                  _resident(lbl.shape), _resident(ng.shape), _resident(wout.shape)],
        out_specs=[tile, pl.BlockSpec((None, C_HEADS, C_DK, C_DK), lambda i, j: (i, 0, 0, 0))],
        out_shape=[jax.ShapeDtypeStruct((b, seq, d), F32),
                   jax.ShapeDtypeStruct((b, C_HEADS, C_DK, C_DK), F32)],
        scratch_shapes=[pltpu.VMEM((C_HEADS, C_DK, C_DK), F32),
                        pltpu.VMEM((ts, dkt), F32), pltpu.VMEM((ts, dkt), F32),
                        pltpu.VMEM((ts, dkt), BF16), pltpu.VMEM((ts, dkt), F32),
                        pltpu.VMEM((ts, dkt), F32), pltpu.VMEM((ts, dkt), BF16)],
        compiler_params=_params(("arbitrary", "arbitrary")),
        name="hgrn_prompt",
    )(x, gpre, gpost, win, lbl, ng, wout)


def _mix_ab_sample_kernel(seg, x_ref, prev0_ref, prev1_ref, kc_ref, vc_ref, gpre_ref, gpost_ref, win_ref,
                          cw_ref, sink_ref, wout_ref,
                          y_ref, conv0_ref, conv1_ref, kout_ref, vout_ref,
                          ubuf, fix1, fix2, qbuf, knew_s, vnew_s, yabuf, ybbuf, kall, vall, bias_s, sink_s):
    i = pl.program_id(0)
    rows = x_ref.shape[0]
    aw = cw_ref.shape[1]
    nq = GROUP * KV_HEADS * HEAD_DIM
    nkv = KV_HEADS * HEAD_DIM
    pair = 2 * seg
    qrows = KV_HEADS * GROUP * pair
    tk = WINDOW + pair

    @pl.when(i == 0)
    def _():
        x = x_ref[...]
        h = _rms(x, gpre_ref[...]).astype(BF16)
        t = lax.broadcasted_iota(jnp.int32, (rows, 1), 0) % seg
        u = _dot(h, win_ref[:, 2 * aw:3 * aw]) * _dot(h, win_ref[:, 0:aw])
        ubuf[:, 0:SUBLANES, :] = jnp.zeros((ubuf.shape[0], SUBLANES, LANES), F32)
        fix1[...] = jnp.zeros(fix1.shape, F32)
        fix2[...] = jnp.zeros(fix2.shape, F32)
        um1, um2 = _segment_conv_taps(u, t, seg, prev0_ref, prev1_ref, ubuf, fix1, fix2)
        conv0_ref[...], conv1_ref[...] = _segment_tails(ubuf, rows // seg, seg)
        cw = cw_ref[...]
        conv = cw[2:3, :] * u + cw[1:2, :] * um1 + cw[0:1, :] * um2
        yabuf[...] = (_dot(h, win_ref[:, aw:2 * aw]) * conv).astype(BF16)
        q0 = 3 * aw
        qbuf[...] = _dot(h, win_ref[:, q0:q0 + nq]) * (HEAD_DIM ** -0.5)
        kv_new = _dot(h, win_ref[:, q0 + nq:q0 + nq + 2 * nkv])
        knew_s[...] = kv_new[:, 0:nkv]
        vnew_s[...] = kv_new[:, nkv:2 * nkv]

        r = lax.broadcasted_iota(jnp.int32, (qrows, tk), 0)
        j = lax.broadcasted_iota(jnp.int32, (qrows, tk), 1)
        dist = r % seg + WINDOW - j
        valid = (dist >= 0) & (dist < WINDOW) & (j < WINDOW + seg)
        rc = lax.broadcasted_iota(jnp.int32, (qrows, 1), 0)
        head = (rc // (GROUP * pair)) * GROUP + (rc // pair) % GROUP
        slope = jnp.zeros((qrows, 1), F32)
        sink = jnp.zeros((qrows, 1), F32)
        for hidx in range(KV_HEADS * GROUP):
            slope = jnp.where(head == hidx, _head_slope(hidx), slope)
            sink = jnp.where(head == hidx, sink_ref[:, hidx:hidx + 1], sink)
        bias_s[...] = jnp.where(valid, -slope * dist.astype(F32), NEG)
        sink_s[...] = sink

    nblk = kc_ref.shape[0]
    lane = lax.broadcasted_iota(jnp.int32, (1, nkv), 1)
    lo = lane < HEAD_DIM
    rq = lax.broadcasted_iota(jnp.int32, (qrows, 1), 0)
    first_seq = (rq // seg) % 2 == 0

    def body(p, carry):
        r0 = pl.multiple_of((i * nblk + 2 * p) * seg, pair)
        q8 = qbuf[pl.ds(r0, pair), :]
        qg = jnp.concatenate([q8[:, g * nkv:(g + 1) * nkv] for g in range(GROUP)], axis=0)
        qs = jnp.concatenate([jnp.where(lo, qg, 0.0), jnp.where(lo, 0.0, qg)], axis=0).astype(BF16)
        knew = knew_s[pl.ds(r0, pair), :]
        vnew = vnew_s[pl.ds(r0, pair), :]
        sc, ov = [], []
        for a in range(2):
            own_first = (lambda z: z) if a == 0 else (lambda z: jnp.concatenate([z[seg:], z[:seg]], axis=0))
            kall[a, 0:WINDOW, :] = kc_ref[2 * p + a]
            kall[a, WINDOW:tk, :] = own_first(knew)
            vall[a, 0:WINDOW, :] = vc_ref[2 * p + a]
            vall[a, WINDOW:tk, :] = own_first(vnew)
            kout_ref[2 * p + a] = kall[a, seg:seg + WINDOW, :]
            vout_ref[2 * p + a] = vall[a, seg:seg + WINDOW, :]
            sc.append(_dot_nt(qs, kall[a].astype(BF16)))
        s2 = jnp.where(first_seq, sc[0], sc[1]) + bias_s[...]
        sink = sink_s[...]
        m = jnp.maximum(jnp.max(s2, axis=-1, keepdims=True), sink)
        pr = jnp.exp(s2 - m)
        den = jnp.sum(pr, axis=-1, keepdims=True) + jnp.exp(sink - m)
        prb = pr.astype(BF16)
        for a in range(2):
            ov.append(_dot(prb, vall[a].astype(BF16)))
        o = jnp.where(first_seq, ov[0], ov[1]) / den
        og = jnp.where(lo, o[0:GROUP * pair, :], o[GROUP * pair:, :])
        for g in range(GROUP):
            ybbuf[pl.ds(r0, pair), g * nkv:(g + 1) * nkv] = og[g * pair:(g + 1) * pair, :]
        return carry

    lax.fori_loop(0, nblk // 2, body, 0)

    @pl.when(i == pl.num_programs(0) - 1)
    def _():
        out = _dot(yabuf[...], wout_ref[0:aw, :]) + _dot(ybbuf[...].astype(BF16), wout_ref[aw:, :])
        y_ref[...] = x_ref[...] + _rms(out, gpost_ref[...])


def _mix_ab_sample(x, conv_prev, kcache, vcache, seg, gpre, gpost, win, cw, sinks, wout):
    rows, d = x.shape
    nseq = rows // seg
    aw = cw.shape[1]
    nq = GROUP * KV_HEADS * HEAD_DIM
    nkv = KV_HEADS * HEAD_DIM
    sb = SAMPLE_SEQ_BLOCK
    pair = 2 * seg
    qrows = KV_HEADS * GROUP * pair
    tk = WINDOW + pair
    full = lambda shape: pl.BlockSpec(shape, lambda i: (0,) * len(shape))
    cache = pl.BlockSpec((sb, WINDOW, nkv), lambda i: (i, 0, 0))
    prev0, prev1 = conv_prev[:, 0], conv_prev[:, 1]
    y, conv0, conv1, k_new, v_new = pl.pallas_call(
        functools.partial(_mix_ab_sample_kernel, seg),
        grid=(nseq // sb,),
        in_specs=[_resident(x.shape), _resident(prev0.shape), _resident(prev1.shape), cache, cache,
                  _resident(gpre.shape), _resident(gpost.shape), _resident(win.shape), _resident(cw.shape),
                  _resident(sinks.shape), _resident(wout.shape)],
        out_specs=[full((rows, d)), full((nseq, aw)), full((nseq, aw)), cache, cache],
        out_shape=[jax.ShapeDtypeStruct((rows, d), F32), jax.ShapeDtypeStruct((nseq, aw), F32),
                   jax.ShapeDtypeStruct((nseq, aw), F32), jax.ShapeDtypeStruct(kcache.shape, F32),
                   jax.ShapeDtypeStruct(vcache.shape, F32)],
        scratch_shapes=[pltpu.VMEM((aw // LANES, rows + SUBLANES, LANES), F32),
                        pltpu.VMEM((aw // LANES, rows, LANES), F32),
                        pltpu.VMEM((aw // LANES, rows, LANES), F32),
                        pltpu.VMEM((rows, nq), F32),
                        pltpu.VMEM((rows, nkv), F32),
                        pltpu.VMEM((rows, nkv), F32),
                        pltpu.VMEM((rows, aw), BF16),
                        pltpu.VMEM((rows, nq), F32),
                        pltpu.VMEM((2, tk, nkv), F32),
                        pltpu.VMEM((2, tk, nkv), F32),
                        pltpu.VMEM((qrows, tk), F32),
                        pltpu.VMEM((qrows, 1), F32)],
        compiler_params=_params(("arbitrary",)),
        name="mix_ab_sample",
    )(x, prev0, prev1, kcache, vcache, gpre, gpost, win, cw, sinks, wout)
    return y, jnp.stack([conv0, conv1], axis=1), k_new, v_new


def _hgrn_sample_kernel(layer, seg, x_ref, s0_ref, gpre_ref, gpost_ref, win_ref, lbl_ref, ng_ref, wout_ref,
                        y_ref, s1_ref, qb, kb, vb, gb, ob, mixbuf):
    i = pl.program_id(0)
    rows = x_ref.shape[0]
    dkt = C_HEADS * C_DK
    pair = 2 * seg

    @pl.when(i == 0)
    def _():
        x = x_ref[...]
        h = _rms(x, gpre_ref[...]).astype(BF16)
        f = _dot(h, win_ref[:, dkt:2 * dkt])
        log2_ft, kb[...] = _forget_gate(f, _lower_bound_terms(lbl_ref, layer))
        hi, lo = _split_bf16(log2_ft)
        ri = lax.broadcasted_iota(jnp.int32, (rows, rows), 0)
        ci = lax.broadcasted_iota(jnp.int32, (rows, rows), 1)
        tri = jnp.where((ri // seg == ci // seg) & (ri >= ci), 1.0, 0.0).astype(BF16)
        gb[...] = _dot(tri, hi) + _dot(tri, lo)
        qb[...] = _dot(h, win_ref[:, 0:dkt])
        vb[...] = _dot(h, win_ref[:, 2 * dkt:3 * dkt])

    nblk = s0_ref.shape[0]
    rr = lax.broadcasted_iota(jnp.int32, (pair, 1), 0)
    first_seq = rr < seg
    second_seq = rr >= seg
    ri = lax.broadcasted_iota(jnp.int32, (pair, pair), 0)
    ci = lax.broadcasted_iota(jnp.int32, (pair, pair), 1)
    causal = (ri // seg == ci // seg) & (ri >= ci)
    mid = seg // 2 - 1

    def body(p, carry):
        r0 = pl.multiple_of((i * nblk + 2 * p) * seg, pair)
        for hh in range(C_HEADS):
            sl = slice(hh * C_DK, (hh + 1) * C_DK)
            gh = gb[pl.ds(r0, pair), sl]
            gr = jnp.where(first_seq, gh[mid:mid + 1, :], gh[seg + mid:seg + mid + 1, :])
            gl = jnp.where(first_seq, gh[seg - 1:seg, :], gh[pair - 1:pair, :])
            qh = qb[pl.ds(r0, pair), sl]
            kh = kb[pl.ds(r0, pair), sl]
            vh = vb[pl.ds(r0, pair), sl].astype(BF16)
            qt = (qh * jnp.exp2(gh - gr)).astype(BF16)
            kt = (kh * jnp.exp2(gr - gh)).astype(BF16)
            att = jnp.where(causal, _dot_nt(qt, kt), 0.0).astype(BF16)
            qi = (qh * jnp.exp2(gh)).astype(BF16)
            kl = kh * jnp.exp2(gl - gh)
            o_prev = []
            for a in range(2):
                s_a = s0_ref[2 * p + a, hh]
                o_prev.append(_dot(qi, s_a.astype(BF16)))
                kl_a = jnp.where(first_seq if a == 0 else second_seq, kl, 0.0).astype(BF16)
                e = jnp.exp2(gh[(a + 1) * seg - 1:(a + 1) * seg, :])
                e1 = e.astype(BF16).astype(F32)
                e2 = (e - e1).astype(BF16).astype(F32)
                e3 = (e - e1 - e2).astype(BF16).astype(F32)
                dec_l = jnp.where(rr == 0, e1, jnp.where(rr == 1, e2, jnp.where(rr == 2, e3, 0.0)))
                dec = _dot_tn(dec_l.astype(BF16), jnp.ones((pair, C_DK), BF16))
                s1_ref[2 * p + a, hh] = dec * s_a + _dot_tn(kl_a, vh)
            ob[pl.ds(r0, pair), sl] = _dot(att, vh) + jnp.where(first_seq, o_prev[0], o_prev[1])
        return carry

    lax.fori_loop(0, nblk // 2, body, 0)

    @pl.when(i == pl.num_programs(0) - 1)
    def _():
        x = x_ref[...]
        h = _rms(x, gpre_ref[...]).astype(BF16)
        y_ref[...] = _hgrn_gate_out(x, h, ob, mixbuf, win_ref, ng_ref, wout_ref, gpost_ref)


def _hgrn_sample(layer, x, s0, seg, gpre, gpost, win, lbl, ng, wout):
    rows, d = x.shape
    nseq = rows // seg
    dkt = C_HEADS * C_DK
    sb = SAMPLE_SEQ_BLOCK
    full = lambda shape: pl.BlockSpec(shape, lambda i: (0,) * len(shape))
    state = pl.BlockSpec((sb, C_HEADS, C_DK, C_DK), lambda i: (i, 0, 0, 0))
    return pl.pallas_call(
        functools.partial(_hgrn_sample_kernel, layer, seg),
        grid=(nseq // sb,),
        in_specs=[_resident(x.shape), state, _resident(gpre.shape), _resident(gpost.shape), _resident(win.shape),
                  _resident(lbl.shape), _resident(ng.shape), _resident(wout.shape)],
        out_specs=[full((rows, d)), state],
        out_shape=[jax.ShapeDtypeStruct((rows, d), F32), jax.ShapeDtypeStruct(s0.shape, F32)],
        scratch_shapes=[pltpu.VMEM((rows, dkt), F32), pltpu.VMEM((rows, dkt), F32),
                        pltpu.VMEM((rows, dkt), F32), pltpu.VMEM((rows, dkt), F32),
                        pltpu.VMEM((rows, dkt), F32), pltpu.VMEM((rows, dkt), BF16)],
        compiler_params=_params(("arbitrary",)),
        name="hgrn_sample",
    )(x, s0, gpre, gpost, win, lbl, ng, wout)


def _head_permutation():
    cols = []
    for g in range(GROUP):
        for kv in range(KV_HEADS):
            hidx = kv * GROUP + g
            cols.extend(range(hidx * HEAD_DIM, (hidx + 1) * HEAD_DIM))
    return jnp.asarray(cols, dtype=jnp.int32)


def kernel(x_prompt, x_sample, state_conv_a, cache_swa_k, cache_swa_v, state_hgrn, state_ffn_conv, norm_mix_pre, norm_mix_post, norm_ffn_pre, norm_ffn_post, w_in_ab, conv_a_w, attn_sinks, w_out_ab, w_in_c, hgrn_lb_logits, hgrn_norm_g, w_out_c, w_ffn_up, ffn_conv_w, w_ffn_down):
    bp, seq, d = x_prompt.shape
    ns, seg, _ = x_sample.shape
    aw = conv_a_w.shape[-1]
    nq = GROUP * KV_HEADS * HEAD_DIM
    nkv = KV_HEADS * HEAD_DIM

    perm = _head_permutation()
    q0 = 3 * aw
    win_ab = w_in_ab[0]
    win_ab = jnp.concatenate([win_ab[:, :q0], win_ab[:, q0:q0 + nq][:, perm], win_ab[:, q0 + nq:]], axis=1).astype(BF16)
    wout_ab = w_out_ab[0]
    wout_ab = jnp.concatenate([wout_ab[:aw], wout_ab[aw:][perm]], axis=0).astype(BF16)
    win_c = w_in_c[0].astype(BF16)
    wout_c = w_out_c[0].astype(BF16)
    wup = w_ffn_up.astype(BF16)
    wdown = w_ffn_down.astype(BF16)
    row = lambda a: a.reshape(1, -1)
    sinks = row(attn_sinks[0])
    ng = row(hgrn_norm_g[0])

    x1, conv_p, k_p, v_p = _mix_ab_prompt(x_prompt, row(norm_mix_pre[0]), row(norm_mix_post[0]), win_ab,
                                          conv_a_w[0], sinks, wout_ab)
    x2, ffn_p0 = _ffn_prompt(x1, row(norm_ffn_pre[0]), row(norm_ffn_post[0]), wup[0], ffn_conv_w[0], wdown[0])
    x3, hgrn_p = _hgrn_prompt(1, x2, row(norm_mix_pre[1]), row(norm_mix_post[1]), win_c, hgrn_lb_logits,
                              ng, wout_c)
    y_prompt, ffn_p1 = _ffn_prompt(x3, row(norm_ffn_pre[1]), row(norm_ffn_post[1]), wup[1], ffn_conv_w[1], wdown[1])

    xs = x_sample.reshape(ns * seg, d)
    kc = cache_swa_k[0].reshape(ns, WINDOW, nkv)
    vc = cache_swa_v[0].reshape(ns, WINDOW, nkv)
    s1, conv_s, k_s, v_s = _mix_ab_sample(xs, state_conv_a[0], kc, vc, seg, row(norm_mix_pre[0]),
                                          row(norm_mix_post[0]), win_ab, conv_a_w[0], sinks, wout_ab)
    s2, ffn_s0 = _ffn_sample(s1, state_ffn_conv[0], seg, row(norm_ffn_pre[0]), row(norm_ffn_post[0]),
                             wup[0], ffn_conv_w[0], wdown[0])
    s3, hgrn_s = _hgrn_sample(1, s2, state_hgrn[0], seg, row(norm_mix_pre[1]), row(norm_mix_post[1]),
                              win_c, hgrn_lb_logits, ng, wout_c)
    y_sample, ffn_s1 = _ffn_sample(s3, state_ffn_conv[1], seg, row(norm_ffn_pre[1]), row(norm_ffn_post[1]),
                                   wup[1], ffn_conv_w[1], wdown[1])

    kvshape = (KV_HEADS, HEAD_DIM)
    return (y_prompt, y_sample.reshape(ns, seg, d),
            conv_p[None], conv_s[None],
            k_p.reshape(1, bp, WINDOW, *kvshape), k_s.reshape(1, ns, WINDOW, *kvshape),
            v_p.reshape(1, bp, WINDOW, *kvshape), v_s.reshape(1, ns, WINDOW, *kvshape),
            hgrn_p[None], hgrn_s[None],
            jnp.stack([ffn_p0, ffn_p1]), jnp.stack([ffn_s0, ffn_s1]))
```

```python
import functools
import math

import jax
import jax.numpy as jnp
from jax import lax
from jax.experimental import pallas as pl
from jax.experimental.pallas import tpu as pltpu

F32 = jnp.float32
BF16 = jnp.bfloat16

EPS = 1e-6
WINDOW = 128
KV_HEADS = 2
GROUP = 4
HEAD_DIM = 64
C_HEADS = 8
C_DK = 128
NEG = -1e30
SUBLANES = 8
LANES = 128
PROMPT_TILE = 1024
FFN_TILE = 1024
HGRN_CHUNK = 64
FFN_CHUNK = 256
SAMPLE_SEQ_BLOCK = 8
VMEM_LIMIT_V7X = 56 * 2**20


def _dot(a, b):
    return jnp.dot(a, b, preferred_element_type=F32)


def _dot_nt(a, b):
    return lax.dot_general(a, b, (((1,), (1,)), ((), ())), preferred_element_type=F32)


def _dot_tn(a, b):
    return lax.dot_general(a, b, (((0,), (0,)), ((), ())), preferred_element_type=F32)


def _rms(x, g):
    return x * lax.rsqrt(jnp.mean(x * x, axis=-1, keepdims=True) + EPS) * g


def _gelu_tanh(x):
    c = math.sqrt(2.0 / math.pi)
    return x * (0.5 * (1.0 + jnp.tanh(c * (x + 0.044715 * (x * x * x)))))


def _sigmoid(x):
    return 1.0 / (1.0 + jnp.exp(-x))


def _resident(shape):
    nd = len(shape)
    return pl.BlockSpec(shape, lambda *_: (0,) * nd, pipeline_mode=pl.Buffered(1))


def _params(sem):
    return pltpu.CompilerParams(dimension_semantics=sem, vmem_limit_bytes=VMEM_LIMIT_V7X)


def _head_slope(h):
    return 2.0 ** (-8.0 * (h + 1) / (KV_HEADS * GROUP))


def _mix_ab_prompt_kernel(x_ref, gpre_ref, gpost_ref, win_ref, cw_ref, sink_ref, wout_ref,
                          y_ref, conv_ref, kout_ref, vout_ref,
                          ubuf, qbuf, kbuf, vbuf, mixbuf, bias_s):
    s = pl.program_id(1)
    ts = x_ref.shape[0]
    aw = cw_ref.shape[1]
    nq = GROUP * KV_HEADS * HEAD_DIM
    nkv = KV_HEADS * HEAD_DIM

    @pl.when(s == 0)
    def _():
        ubuf[0:SUBLANES, :] = jnp.zeros((SUBLANES, aw), F32)
        kbuf[0:WINDOW, :] = jnp.zeros((WINDOW, nkv), BF16)
        vbuf[0:WINDOW, 0:nkv] = jnp.zeros((WINDOW, nkv), BF16)
        vbuf[:, nkv:2 * nkv] = jnp.ones((ts + WINDOW, nkv), BF16)

    x = x_ref[...]
    h = _rms(x, gpre_ref[...]).astype(BF16)

    u = _dot(h, win_ref[:, 2 * aw:3 * aw]) * _dot(h, win_ref[:, 0:aw])
    ubuf[SUBLANES:SUBLANES + ts, :] = u
    cw = cw_ref[...]
    conv = (cw[2:3, :] * u + cw[1:2, :] * ubuf[SUBLANES - 1:SUBLANES - 1 + ts, :]
            + cw[0:1, :] * ubuf[SUBLANES - 2:SUBLANES - 2 + ts, :])
    mixbuf[:, 0:aw] = (_dot(h, win_ref[:, aw:2 * aw]) * conv).astype(BF16)
    tail = ubuf[ts + SUBLANES - 2:ts + SUBLANES, :]
    conv_ref[...] = tail
    ubuf[SUBLANES - 2:SUBLANES, :] = tail

    q0 = 3 * aw
    q = _dot(h, win_ref[:, q0:q0 + nq]) * (HEAD_DIM ** -0.5)
    lane = lax.broadcasted_iota(jnp.int32, (1, nkv), 1)
    lo = lane < HEAD_DIM
    nblk = ts // WINDOW
    for n in range(nblk):
        for g in range(GROUP):
            qg = q[n * WINDOW:(n + 1) * WINDOW, g * nkv:(g + 1) * nkv]
            qbuf[n, g * WINDOW:(g + 1) * WINDOW, :] = jnp.where(lo, qg, 0.0).astype(BF16)
            qbuf[n, (GROUP + g) * WINDOW:(GROUP + g + 1) * WINDOW, :] = jnp.where(lo, 0.0, qg).astype(BF16)
    kv_new = _dot(h, win_ref[:, q0 + nq:q0 + nq + 2 * nkv])
    k = kv_new[:, 0:nkv]
    v = kv_new[:, nkv:2 * nkv]
    kbuf[WINDOW:WINDOW + ts, :] = k.astype(BF16)
    vbuf[WINDOW:WINDOW + ts, 0:nkv] = v.astype(BF16)
    kout_ref[...] = k[ts - WINDOW:, :]
    vout_ref[...] = v[ts - WINDOW:, :]

    qi = lax.broadcasted_iota(jnp.int32, (WINDOW, 2 * WINDOW), 0)
    kj = lax.broadcasted_iota(jnp.int32, (WINDOW, 2 * WINDOW), 1)
    dist = qi + WINDOW - kj
    valid = (dist >= 0) & (dist < WINDOW)
    negdist = jnp.where(valid, -dist.astype(F32), 0.0)
    maskadd = jnp.where(valid, 0.0, NEG)
    nheads = KV_HEADS * GROUP
    for hidx in range(nheads):
        bias_s[hidx] = _head_slope(hidx) * negdist + maskadd
    first = jnp.where(kj < WINDOW, jnp.where(s == 0, NEG, 0.0), 0.0)

    for n in range(nblk):
        r0 = n * WINDOW
        sc_all = _dot_nt(qbuf[n], kbuf[r0:r0 + 2 * WINDOW, :])
        ps, sink_terms = [], []
        for hidx in range(nheads):
            sc = sc_all[hidx * WINDOW:(hidx + 1) * WINDOW, :] + bias_s[hidx]
            if n == 0:
                sc = sc + first
            sink = sink_ref[hidx]
            m = jnp.maximum(jnp.max(sc, axis=-1, keepdims=True), sink)
            ps.append(jnp.exp(sc - m).astype(BF16))
            sink_terms.append(jnp.exp(sink - m))
        pv = _dot(jnp.concatenate(ps, axis=0), vbuf[r0:r0 + 2 * WINDOW, :])
        outs = []
        for hidx in range(nheads):
            pv_h = pv[hidx * WINDOW:(hidx + 1) * WINDOW, :]
            outs.append(pv_h[:, 0:nkv] / (pv_h[:, nkv:2 * nkv] + sink_terms[hidx]))
        for g in range(GROUP):
            og = jnp.where(lo, outs[g], outs[GROUP + g])
            mixbuf[r0:r0 + WINDOW, aw + g * nkv:aw + (g + 1) * nkv] = og.astype(BF16)

    kbuf[0:WINDOW, :] = kbuf[ts:ts + WINDOW, :]
    vbuf[0:WINDOW, :] = vbuf[ts:ts + WINDOW, :]

    out = _dot(mixbuf[...], wout_ref[...])
    y_ref[...] = x + _rms(out, gpost_ref[...])


def _mix_ab_prompt(x, gpre, gpost, win, cw, sinks, wout):
    b, seq, d = x.shape
    ts = PROMPT_TILE
    aw = cw.shape[1]
    nq = GROUP * KV_HEADS * HEAD_DIM
    nkv = KV_HEADS * HEAD_DIM
    tile = pl.BlockSpec((None, ts, d), lambda i, j: (i, j, 0))
    return pl.pallas_call(
        _mix_ab_prompt_kernel,
        grid=(b, seq // ts),
        in_specs=[tile, _resident(gpre.shape), _resident(gpost.shape), _resident(win.shape),
                  _resident(cw.shape), pl.BlockSpec(memory_space=pltpu.SMEM), _resident(wout.shape)],
        out_specs=[tile,
                   pl.BlockSpec((None, 2, aw), lambda i, j: (i, 0, 0)),
                   pl.BlockSpec((None, WINDOW, nkv), lambda i, j: (i, 0, 0)),
                   pl.BlockSpec((None, WINDOW, nkv), lambda i, j: (i, 0, 0))],
        out_shape=[jax.ShapeDtypeStruct((b, seq, d), F32),
                   jax.ShapeDtypeStruct((b, 2, aw), F32),
                   jax.ShapeDtypeStruct((b, WINDOW, nkv), F32),
                   jax.ShapeDtypeStruct((b, WINDOW, nkv), F32)],
        scratch_shapes=[pltpu.VMEM((ts + SUBLANES, aw), F32),
                        pltpu.VMEM((ts // WINDOW, KV_HEADS * GROUP * WINDOW, nkv), BF16),
                        pltpu.VMEM((ts + WINDOW, nkv), BF16),
                        pltpu.VMEM((ts + WINDOW, 2 * nkv), BF16),
                        pltpu.VMEM((ts, d), BF16),
                        pltpu.VMEM((KV_HEADS * GROUP, WINDOW, 2 * WINDOW), F32)],
        compiler_params=_params(("arbitrary", "arbitrary")),
        name="mix_ab_prompt",
    )(x, gpre, gpost, win, cw, sinks.reshape(-1), wout)


def _ffn_prompt_kernel(x_ref, gpre_ref, gpost_ref, wup_ref, cw_ref, wdown_ref,
                       y_ref, st_ref, abuf, carry, gbuf):
    s = pl.program_id(1)
    rows = x_ref.shape[0]
    ff = cw_ref.shape[1]

    @pl.when(s == 0)
    def _():
        carry[...] = jnp.zeros(carry.shape, F32)

    x = x_ref[...]
    h = _rms(x, gpre_ref[...]).astype(BF16)
    for c in range(ff // FFN_CHUNK):
        c0 = c * FFN_CHUNK
        buf = abuf.at[c % 2]
        a = _dot(h, wup_ref[:, c0:c0 + FFN_CHUNK])
        b = _dot(h, wup_ref[:, ff + c0:ff + c0 + FFN_CHUNK])
        buf[0:SUBLANES, :] = carry[:, c0:c0 + FFN_CHUNK]
        buf[SUBLANES:SUBLANES + rows, :] = a
        cw = cw_ref[:, c0:c0 + FFN_CHUNK]
        conv = (cw[2:3, :] * a + cw[1:2, :] * buf[SUBLANES - 1:SUBLANES - 1 + rows, :]
                + cw[0:1, :] * buf[SUBLANES - 2:SUBLANES - 2 + rows, :])
        tail = a[rows - 2:rows, :]
        st_ref[:, c0:c0 + FFN_CHUNK] = tail
        carry[SUBLANES - 2:SUBLANES, c0:c0 + FFN_CHUNK] = tail
        gbuf[:, c0:c0 + FFN_CHUNK] = (_gelu_tanh(conv) * b).astype(BF16)
    out = _dot(gbuf[...], wdown_ref[...])
    y_ref[...] = x + _rms(out, gpost_ref[...])


def _ffn_prompt(x, gpre, gpost, wup, cw, wdown):
    b, seq, d = x.shape
    ts = FFN_TILE
    ff = cw.shape[1]
    tile = pl.BlockSpec((None, ts, d), lambda i, j: (i, j, 0))
    return pl.pallas_call(
        _ffn_prompt_kernel,
        grid=(b, seq // ts),
        in_specs=[tile, _resident(gpre.shape), _resident(gpost.shape), _resident(wup.shape),
                  _resident(cw.shape), _resident(wdown.shape)],
        out_specs=[tile, pl.BlockSpec((None, 2, ff), lambda i, j: (i, 0, 0))],
        out_shape=[jax.ShapeDtypeStruct((b, seq, d), F32), jax.ShapeDtypeStruct((b, 2, ff), F32)],
        scratch_shapes=[pltpu.VMEM((2, ts + SUBLANES, FFN_CHUNK), F32), pltpu.VMEM((SUBLANES, ff), F32),
                        pltpu.VMEM((ts, ff), BF16)],
        compiler_params=_params(("arbitrary", "arbitrary")),
        name="ffn_prompt",
    )(x, gpre, gpost, wup, cw, wdown)


def _segment_conv_taps(a, t, seg, prev0_ref, prev1_ref, abuf, fix1, fix2):
    rows, c = a.shape
    nseq = rows // seg
    am1, am2 = [], []
    for k in range(c // LANES):
        sl = slice(k * LANES, (k + 1) * LANES)
        abuf[k, SUBLANES:SUBLANES + rows, :] = a[:, sl]
        fix1[k, pl.ds(0, nseq, stride=seg), :] = prev1_ref[:, sl]
        fix2[k, pl.ds(0, nseq, stride=seg), :] = prev0_ref[:, sl]
        fix2[k, pl.ds(1, nseq, stride=seg), :] = prev1_ref[:, sl]
        am1.append(jnp.where(t >= 1, abuf[k, SUBLANES - 1:SUBLANES - 1 + rows, :], fix1[k]))
        am2.append(jnp.where(t >= 2, abuf[k, SUBLANES - 2:SUBLANES - 2 + rows, :], fix2[k]))
    return jnp.concatenate(am1, axis=1), jnp.concatenate(am2, axis=1)


def _segment_tails(abuf, nseq, seg):
    tail = lambda back: jnp.concatenate(
        [abuf[k, pl.ds(SUBLANES + seg - back, nseq, stride=seg), :] for k in range(abuf.shape[0])], axis=1)
    return tail(2), tail(1)


def _ffn_sample_kernel(seg, x_ref, prev0_ref, prev1_ref, gpre_ref, gpost_ref, wa_ref, wb_ref, cw_ref, wdown_ref,
                       y_ref, st0_ref, st1_ref, hbuf, abuf, fix1, fix2, acc):
    c = pl.program_id(0)
    rows = x_ref.shape[0]

    @pl.when(c == 0)
    def _():
        hbuf[...] = _rms(x_ref[...], gpre_ref[...]).astype(BF16)
        acc[...] = jnp.zeros(acc.shape, F32)
        abuf[:, 0:SUBLANES, :] = jnp.zeros((abuf.shape[0], SUBLANES, LANES), F32)
        fix1[...] = jnp.zeros(fix1.shape, F32)
        fix2[...] = jnp.zeros(fix2.shape, F32)

    h = hbuf[...]
    t = lax.broadcasted_iota(jnp.int32, (rows, 1), 0) % seg
    a = _dot(h, wa_ref[...])
    b = _dot(h, wb_ref[...])
    am1, am2 = _segment_conv_taps(a, t, seg, prev0_ref, prev1_ref, abuf, fix1, fix2)
    st0_ref[...], st1_ref[...] = _segment_tails(abuf, rows // seg, seg)
    cw = cw_ref[...]
    conv = cw[2:3, :] * a + cw[1:2, :] * am1 + cw[0:1, :] * am2
    acc[...] += _dot((_gelu_tanh(conv) * b).astype(BF16), wdown_ref[...])

    @pl.when(c == pl.num_programs(0) - 1)
    def _():
        y_ref[...] = x_ref[...] + _rms(acc[...], gpost_ref[...])


def _ffn_sample(x, prev, seg, gpre, gpost, wup, cw, wdown):
    rows, d = x.shape
    nseq = rows // seg
    ff = cw.shape[1]
    cf = FFN_CHUNK
    nc = ff // cf
    cols = lambda r: pl.BlockSpec((r, cf), lambda c: (0, c))
    y, st0, st1 = pl.pallas_call(
        functools.partial(_ffn_sample_kernel, seg),
        grid=(nc,),
        in_specs=[_resident(x.shape), cols(nseq), cols(nseq), _resident(gpre.shape), _resident(gpost.shape),
                  cols(d), pl.BlockSpec((d, cf), lambda c: (0, c + nc)), cols(3),
                  pl.BlockSpec((cf, d), lambda c: (c, 0))],
        out_specs=[pl.BlockSpec((rows, d), lambda c: (0, 0)), cols(nseq), cols(nseq)],
        out_shape=[jax.ShapeDtypeStruct((rows, d), F32), jax.ShapeDtypeStruct((nseq, ff), F32),
                   jax.ShapeDtypeStruct((nseq, ff), F32)],
        scratch_shapes=[pltpu.VMEM((rows, d), BF16), pltpu.VMEM((cf // LANES, rows + SUBLANES, LANES), F32),
                        pltpu.VMEM((cf // LANES, rows, LANES), F32), pltpu.VMEM((cf // LANES, rows, LANES), F32),
                        pltpu.VMEM((rows, d), F32)],
        compiler_params=_params(("arbitrary",)),
        name="ffn_sample",
    )(x, prev[:, 0], prev[:, 1], gpre, gpost, wup, wup, cw, wdown)
    return y, jnp.stack([st0, st1], axis=1)


LOG2E = 1.4426950408889634


def _lower_bound_terms(logits_ref, layer):
    lg = logits_ref[...]
    e = jnp.exp(lg - jnp.max(lg, axis=0, keepdims=True))
    pr = e / jnp.sum(e, axis=0, keepdims=True)
    cum = pr[0:1, :]
    for r in range(1, layer + 1):
        cum = cum + pr[r:r + 1, :]
    lb = jnp.clip(cum - pr[0:1, :], 0.0, 1.0)
    return jnp.log(lb) * LOG2E, jnp.log1p(-lb) * LOG2E, 1.0 - lb


def _forget_gate(f, lb_terms):
    log2_lb, log2_1m_lb, one_m_lb = lb_terms
    f2 = f * LOG2E
    e = jnp.exp2(jnp.minimum(f2, -f2))
    w = 1.0 + e
    b = log2_1m_lb + (jnp.minimum(f2, 0.0) - jnp.log2(w))
    d = log2_lb - b
    log2_ft = jnp.maximum(log2_lb, b) + jnp.log2(1.0 + jnp.exp2(jnp.minimum(d, -d)))
    return log2_ft, one_m_lb * (jnp.where(f2 >= 0.0, e, 1.0) / w)


def _split_bf16(x):
    hi = x.astype(BF16)
    return hi, (x - hi.astype(F32)).astype(BF16)


def _hgrn_gate_out(x, h, ob, mixbuf, win_ref, ng_ref, wout_ref, gpost_ref):
    dkt = C_HEADS * C_DK
    g = _dot(h, win_ref[:, 3 * dkt:4 * dkt])
    ng = ng_ref[...]
    for hh in range(C_HEADS):
        sl = slice(hh * C_DK, (hh + 1) * C_DK)
        mixbuf[:, sl] = (_rms(ob[:, sl], ng) * (g[:, sl] * _sigmoid(g[:, sl]))).astype(BF16)
    out = _dot(mixbuf[...], wout_ref[...])
    return x + _rms(out, gpost_ref[...])


def _hgrn_prompt_kernel(layer, x_ref, gpre_ref, gpost_ref, win_ref, lbl_ref, ng_ref, wout_ref,
                        y_ref, sout_ref, st, qb, kb, vb, lfb, ob, mixbuf):
    s = pl.program_id(1)
    ts = x_ref.shape[0]
    dkt = C_HEADS * C_DK
    ck = HGRN_CHUNK

    @pl.when(s == 0)
    def _():
        st[...] = jnp.zeros(st.shape, F32)

    x = x_ref[...]
    h = _rms(x, gpre_ref[...]).astype(BF16)
    f = _dot(h, win_ref[:, dkt:2 * dkt])
    lfb[...], kb[...] = _forget_gate(f, _lower_bound_terms(lbl_ref, layer))
    qb[...] = _dot(h, win_ref[:, 0:dkt])
    vb[...] = _dot(h, win_ref[:, 2 * dkt:3 * dkt]).astype(BF16)

    ri = lax.broadcasted_iota(jnp.int32, (ck, ck), 0)
    ci = lax.broadcasted_iota(jnp.int32, (ck, ck), 1)
    causal = ri >= ci
    tri = jnp.where(causal, 1.0, 0.0).astype(BF16)
    mid = ck // 2 - 1

    for c in range(ts // ck):
        rows = slice(c * ck, (c + 1) * ck)
        hi, lo = _split_bf16(lfb[rows, :])
        gcum = _dot(tri, hi) + _dot(tri, lo)
        for hh in range(C_HEADS):
            sl = slice(hh * C_DK, (hh + 1) * C_DK)
            gh = gcum[:, sl]
            gr = gh[mid:mid + 1, :]
            gl = gh[ck - 1:ck, :]
            qh = qb[rows, sl]
            kh = kb[rows, sl]
            vh = vb[rows, sl]
            qt = (qh * jnp.exp2(gh - gr)).astype(BF16)
            kt = (kh * jnp.exp2(gr - gh)).astype(BF16)
            att = jnp.where(causal, _dot_nt(qt, kt), 0.0).astype(BF16)
            st_h = st[hh]
            qi = (qh * jnp.exp2(gh)).astype(BF16)
            ob[rows, sl] = _dot(jnp.concatenate([qi, att], axis=1),
                                jnp.concatenate([st_h.astype(BF16).T, vh], axis=0))
            kl = (kh * jnp.exp2(gl - gh)).astype(BF16)
            st[hh] = jnp.exp2(gl) * st_h + _dot_tn(vh, kl)

    y_ref[...] = _hgrn_gate_out(x, h, ob, mixbuf, win_ref, ng_ref, wout_ref, gpost_ref)

    @pl.when(s == pl.num_programs(1) - 1)
    def _():
        for hh in range(C_HEADS):
            sout_ref[hh] = st[hh].T


def _hgrn_prompt(layer, x, gpre, gpost, win, lbl, ng, wout):
    b, seq, d = x.shape
    ts = PROMPT_TILE
    dkt = C_HEADS * C_DK
    tile = pl.BlockSpec((None, ts, d), lambda i, j: (i, j, 0))
    return pl.pallas_call(
        functools.partial(_hgrn_prompt_kernel, layer),
        grid=(b, seq // ts),
        in_specs=[tile, _resident(gpre.shape), _resident(gpost.shape), _resident(win.shape),
                  _resident(lbl.shape), _resident(ng.shape), _resident(wout.shape)],
        out_specs=[tile, pl.BlockSpec((None, C_HEADS, C_DK, C_DK), lambda i, j: (i, 0, 0, 0))],
        out_shape=[jax.ShapeDtypeStruct((b, seq, d), F32),
                   jax.ShapeDtypeStruct((b, C_HEADS, C_DK, C_DK), F32)],
        scratch_shapes=[pltpu.VMEM((C_HEADS, C_DK, C_DK), F32),
                        pltpu.VMEM((ts, dkt), F32), pltpu.VMEM((ts, dkt), F32),
                        pltpu.VMEM((ts, dkt), BF16), pltpu.VMEM((ts, dkt), F32),
                        pltpu.VMEM((ts, dkt), F32), pltpu.VMEM((ts, dkt), BF16)],
        compiler_params=_params(("arbitrary", "arbitrary")),
        name="hgrn_prompt",
    )(x, gpre, gpost, win, lbl, ng, wout)


def _mix_ab_sample_kernel(seg, x_ref, prev0_ref, prev1_ref, kc_ref, vc_ref, gpre_ref, gpost_ref, win_ref,
                          cw_ref, sink_ref, wout_ref,
                          y_ref, conv0_ref, conv1_ref, kout_ref, vout_ref,
                          ubuf, fix1, fix2, qbuf, knew_s, vnew_s, yabuf, ybbuf, kall, vall, bias_s, sink_s):
    i = pl.program_id(0)
    rows = x_ref.shape[0]
    aw = cw_ref.shape[1]
    nq = GROUP * KV_HEADS * HEAD_DIM
    nkv = KV_HEADS * HEAD_DIM
    pair = 2 * seg
    qrows = KV_HEADS * GROUP * pair
    tk = WINDOW + pair

    @pl.when(i == 0)
    def _():
        x = x_ref[...]
        h = _rms(x, gpre_ref[...]).astype(BF16)
        t = lax.broadcasted_iota(jnp.int32, (rows, 1), 0) % seg
        u = _dot(h, win_ref[:, 2 * aw:3 * aw]) * _dot(h, win_ref[:, 0:aw])
        ubuf[:, 0:SUBLANES, :] = jnp.zeros((ubuf.shape[0], SUBLANES, LANES), F32)
        fix1[...] = jnp.zeros(fix1.shape, F32)
        fix2[...] = jnp.zeros(fix2.shape, F32)
        um1, um2 = _segment_conv_taps(u, t, seg, prev0_ref, prev1_ref, ubuf, fix1, fix2)
        conv0_ref[...], conv1_ref[...] = _segment_tails(ubuf, rows // seg, seg)
        cw = cw_ref[...]
        conv = cw[2:3, :] * u + cw[1:2, :] * um1 + cw[0:1, :] * um2
        yabuf[...] = (_dot(h, win_ref[:, aw:2 * aw]) * conv).astype(BF16)
        q0 = 3 * aw
        qbuf[...] = _dot(h, win_ref[:, q0:q0 + nq]) * (HEAD_DIM ** -0.5)
        kv_new = _dot(h, win_ref[:, q0 + nq:q0 + nq + 2 * nkv])
        knew_s[...] = kv_new[:, 0:nkv]
        vnew_s[...] = kv_new[:, nkv:2 * nkv]

        r = lax.broadcasted_iota(jnp.int32, (qrows, tk), 0)
        j = lax.broadcasted_iota(jnp.int32, (qrows, tk), 1)
        dist = r % seg + WINDOW - j
        valid = (dist >= 0) & (dist < WINDOW) & (j < WINDOW + seg)
        rc = lax.broadcasted_iota(jnp.int32, (qrows, 1), 0)
        head = (rc // (GROUP * pair)) * GROUP + (rc // pair) % GROUP
        slope = jnp.zeros((qrows, 1), F32)
        sink = jnp.zeros((qrows, 1), F32)
        for hidx in range(KV_HEADS * GROUP):
            slope = jnp.where(head == hidx, _head_slope(hidx), slope)
            sink = jnp.where(head == hidx, sink_ref[:, hidx:hidx + 1], sink)
        bias_s[...] = jnp.where(valid, -slope * dist.astype(F32), NEG)
        sink_s[...] = sink

    nblk = kc_ref.shape[0]
    lane = lax.broadcasted_iota(jnp.int32, (1, nkv), 1)
    lo = lane < HEAD_DIM
    rq = lax.broadcasted_iota(jnp.int32, (qrows, 1), 0)
    first_seq = (rq // seg) % 2 == 0

    def body(p, carry):
        r0 = pl.multiple_of((i * nblk + 2 * p) * seg, pair)
        q8 = qbuf[pl.ds(r0, pair), :]
        qg = jnp.concatenate([q8[:, g * nkv:(g + 1) * nkv] for g in range(GROUP)], axis=0)
        qs = jnp.concatenate([jnp.where(lo, qg, 0.0), jnp.where(lo, 0.0, qg)], axis=0).astype(BF16)
        knew = knew_s[pl.ds(r0, pair), :]
        vnew = vnew_s[pl.ds(r0, pair), :]
        sc, ov = [], []
        for a in range(2):
            own_first = (lambda z: z) if a == 0 else (lambda z: jnp.concatenate([z[seg:], z[:seg]], axis=0))
            kall[a, 0:WINDOW, :] = kc_ref[2 * p + a]
            kall[a, WINDOW:tk, :] = own_first(knew)
            vall[a, 0:WINDOW, :] = vc_ref[2 * p + a]
            vall[a, WINDOW:tk, :] = own_first(vnew)
            kout_ref[2 * p + a] = kall[a, seg:seg + WINDOW, :]
            vout_ref[2 * p + a] = vall[a, seg:seg + WINDOW, :]
            sc.append(_dot_nt(qs, kall[a].astype(BF16)))
        s2 = jnp.where(first_seq, sc[0], sc[1]) + bias_s[...]
        sink = sink_s[...]
        m = jnp.maximum(jnp.max(s2, axis=-1, keepdims=True), sink)
        pr = jnp.exp(s2 - m)
        den = jnp.sum(pr, axis=-1, keepdims=True) + jnp.exp(sink - m)
        prb = pr.astype(BF16)
        for a in range(2):
            ov.append(_dot(prb, vall[a].astype(BF16)))
        o = jnp.where(first_seq, ov[0], ov[1]) / den
        og = jnp.where(lo, o[0:GROUP * pair, :], o[GROUP * pair:, :])
        for g in range(GROUP):
            ybbuf[pl.ds(r0, pair), g * nkv:(g + 1) * nkv] = og[g * pair:(g + 1) * pair, :]
        return carry

    lax.fori_loop(0, nblk // 2, body, 0)

    @pl.when(i == pl.num_programs(0) - 1)
    def _():
        out = _dot(yabuf[...], wout_ref[0:aw, :]) + _dot(ybbuf[...].astype(BF16), wout_ref[aw:, :])
        y_ref[...] = x_ref[...] + _rms(out, gpost_ref[...])


def _mix_ab_sample(x, conv_prev, kcache, vcache, seg, gpre, gpost, win, cw, sinks, wout):
    rows, d = x.shape
    nseq = rows // seg
    aw = cw.shape[1]
    nq = GROUP * KV_HEADS * HEAD_DIM
    nkv = KV_HEADS * HEAD_DIM
    sb = SAMPLE_SEQ_BLOCK
    pair = 2 * seg
    qrows = KV_HEADS * GROUP * pair
    tk = WINDOW + pair
    full = lambda shape: pl.BlockSpec(shape, lambda i: (0,) * len(shape))
    cache = pl.BlockSpec((sb, WINDOW, nkv), lambda i: (i, 0, 0))
    prev0, prev1 = conv_prev[:, 0], conv_prev[:, 1]
    y, conv0, conv1, k_new, v_new = pl.pallas_call(
        functools.partial(_mix_ab_sample_kernel, seg),
        grid=(nseq // sb,),
        in_specs=[_resident(x.shape), _resident(prev0.shape), _resident(prev1.shape), cache, cache,
                  _resident(gpre.shape), _resident(gpost.shape), _resident(win.shape), _resident(cw.shape),
                  _resident(sinks.shape), _resident(wout.shape)],
        out_specs=[full((rows, d)), full((nseq, aw)), full((nseq, aw)), cache, cache],
        out_shape=[jax.ShapeDtypeStruct((rows, d), F32), jax.ShapeDtypeStruct((nseq, aw), F32),
                   jax.ShapeDtypeStruct((nseq, aw), F32), jax.ShapeDtypeStruct(kcache.shape, F32),
                   jax.ShapeDtypeStruct(vcache.shape, F32)],
        scratch_shapes=[pltpu.VMEM((aw // LANES, rows + SUBLANES, LANES), F32),
                        pltpu.VMEM((aw // LANES, rows, LANES), F32),
                        pltpu.VMEM((aw // LANES, rows, LANES), F32),
                        pltpu.VMEM((rows, nq), F32),
                        pltpu.VMEM((rows, nkv), F32),
                        pltpu.VMEM((rows, nkv), F32),
                        pltpu.VMEM((rows, aw), BF16),
                        pltpu.VMEM((rows, nq), F32),
                        pltpu.VMEM((2, tk, nkv), F32),
                        pltpu.VMEM((2, tk, nkv), F32),
                        pltpu.VMEM((qrows, tk), F32),
                        pltpu.VMEM((qrows, 1), F32)],
        compiler_params=_params(("arbitrary",)),
        name="mix_ab_sample",
    )(x, prev0, prev1, kcache, vcache, gpre, gpost, win, cw, sinks, wout)
    return y, jnp.stack([conv0, conv1], axis=1), k_new, v_new


def _hgrn_sample_kernel(layer, seg, x_ref, s0_ref, gpre_ref, gpost_ref, win_ref, lbl_ref, ng_ref, wout_ref,
                        y_ref, s1_ref, qb, kb, vb, gb, ob, mixbuf):
    i = pl.program_id(0)
    rows = x_ref.shape[0]
    dkt = C_HEADS * C_DK
    pair = 2 * seg

    @pl.when(i == 0)
    def _():
        x = x_ref[...]
        h = _rms(x, gpre_ref[...]).astype(BF16)
        f = _dot(h, win_ref[:, dkt:2 * dkt])
        log2_ft, kb[...] = _forget_gate(f, _lower_bound_terms(lbl_ref, layer))
        hi, lo = _split_bf16(log2_ft)
        ri = lax.broadcasted_iota(jnp.int32, (rows, rows), 0)
        ci = lax.broadcasted_iota(jnp.int32, (rows, rows), 1)
        tri = jnp.where((ri // seg == ci // seg) & (ri >= ci), 1.0, 0.0).astype(BF16)
        gb[...] = _dot(tri, hi) + _dot(tri, lo)
        qb[...] = _dot(h, win_ref[:, 0:dkt])
        vb[...] = _dot(h, win_ref[:, 2 * dkt:3 * dkt])

    nblk = s0_ref.shape[0]
    rr = lax.broadcasted_iota(jnp.int32, (pair, 1), 0)
    first_seq = rr < seg
    second_seq = rr >= seg
    ri = lax.broadcasted_iota(jnp.int32, (pair, pair), 0)
    ci = lax.broadcasted_iota(jnp.int32, (pair, pair), 1)
    causal = (ri // seg == ci // seg) & (ri >= ci)
    mid = seg // 2 - 1

    def body(p, carry):
        r0 = pl.multiple_of((i * nblk + 2 * p) * seg, pair)
        for hh in range(C_HEADS):
            sl = slice(hh * C_DK, (hh + 1) * C_DK)
            gh = gb[pl.ds(r0, pair), sl]
            gr = jnp.where(first_seq, gh[mid:mid + 1, :], gh[seg + mid:seg + mid + 1, :])
            gl = jnp.where(first_seq, gh[seg - 1:seg, :], gh[pair - 1:pair, :])
            qh = qb[pl.ds(r0, pair), sl]
            kh = kb[pl.ds(r0, pair), sl]
            vh = vb[pl.ds(r0, pair), sl].astype(BF16)
            qt = (qh * jnp.exp2(gh - gr)).astype(BF16)
            kt = (kh * jnp.exp2(gr - gh)).astype(BF16)
            att = jnp.where(causal, _dot_nt(qt, kt), 0.0).astype(BF16)
            qi = (qh * jnp.exp2(gh)).astype(BF16)
            kl = kh * jnp.exp2(gl - gh)
            o_prev = []
            for a in range(2):
                s_a = s0_ref[2 * p + a, hh]
                o_prev.append(_dot(qi, s_a.astype(BF16)))
                kl_a = jnp.where(first_seq if a == 0 else second_seq, kl, 0.0).astype(BF16)
                e = jnp.exp2(gh[(a + 1) * seg - 1:(a + 1) * seg, :])
                e1 = e.astype(BF16).astype(F32)
                e2 = (e - e1).astype(BF16).astype(F32)
                e3 = (e - e1 - e2).astype(BF16).astype(F32)
                dec_l = jnp.where(rr == 0, e1, jnp.where(rr == 1, e2, jnp.where(rr == 2, e3, 0.0)))
                dec = _dot_tn(dec_l.astype(BF16), jnp.ones((pair, C_DK), BF16))
                s1_ref[2 * p + a, hh] = dec * s_a + _dot_tn(kl_a, vh)
            ob[pl.ds(r0, pair), sl] = _dot(att, vh) + jnp.where(first_seq, o_prev[0], o_prev[1])
        return carry

    lax.fori_loop(0, nblk // 2, body, 0)

    @pl.when(i == pl.num_programs(0) - 1)
    def _():
        x = x_ref[...]
        h = _rms(x, gpre_ref[...]).astype(BF16)
        y_ref[...] = _hgrn_gate_out(x, h, ob, mixbuf, win_ref, ng_ref, wout_ref, gpost_ref)


def _hgrn_sample(layer, x, s0, seg, gpre, gpost, win, lbl, ng, wout):
    rows, d = x.shape
    nseq = rows // seg
    dkt = C_HEADS * C_DK
    sb = SAMPLE_SEQ_BLOCK
    full = lambda shape: pl.BlockSpec(shape, lambda i: (0,) * len(shape))
    state = pl.BlockSpec((sb, C_HEADS, C_DK, C_DK), lambda i: (i, 0, 0, 0))
    return pl.pallas_call(
        functools.partial(_hgrn_sample_kernel, layer, seg),
        grid=(nseq // sb,),
        in_specs=[_resident(x.shape), state, _resident(gpre.shape), _resident(gpost.shape), _resident(win.shape),
                  _resident(lbl.shape), _resident(ng.shape), _resident(wout.shape)],
        out_specs=[full((rows, d)), state],
        out_shape=[jax.ShapeDtypeStruct((rows, d), F32), jax.ShapeDtypeStruct(s0.shape, F32)],
        scratch_shapes=[pltpu.VMEM((rows, dkt), F32), pltpu.VMEM((rows, dkt), F32),
                        pltpu.VMEM((rows, dkt), F32), pltpu.VMEM((rows, dkt), F32),
                        pltpu.VMEM((rows, dkt), F32), pltpu.VMEM((rows, dkt), BF16)],
        compiler_params=_params(("arbitrary",)),
        name="hgrn_sample",
    )(x, s0, gpre, gpost, win, lbl, ng, wout)


def _head_permutation():
    cols = []
    for g in range(GROUP):
        for kv in range(KV_HEADS):
            hidx = kv * GROUP + g
            cols.extend(range(hidx * HEAD_DIM, (hidx + 1) * HEAD_DIM))
    return jnp.asarray(cols, dtype=jnp.int32)


def kernel(x_prompt, x_sample, state_conv_a, cache_swa_k, cache_swa_v, state_hgrn, state_ffn_conv, norm_mix_pre, norm_mix_post, norm_ffn_pre, norm_ffn_post, w_in_ab, conv_a_w, attn_sinks, w_out_ab, w_in_c, hgrn_lb_logits, hgrn_norm_g, w_out_c, w_ffn_up, ffn_conv_w, w_ffn_down):
    bp, seq, d = x_prompt.shape
    ns, seg, _ = x_sample.shape
    aw = conv_a_w.shape[-1]
    nq = GROUP * KV_HEADS * HEAD_DIM
    nkv = KV_HEADS * HEAD_DIM

    perm = _head_permutation()
    q0 = 3 * aw
    win_ab = w_in_ab[0]
    win_ab = jnp.concatenate([win_ab[:, :q0], win_ab[:, q0:q0 + nq][:, perm], win_ab[:, q0 + nq:]], axis=1).astype(BF16)
    wout_ab = w_out_ab[0]
    wout_ab = jnp.concatenate([wout_ab[:aw], wout_ab[aw:][perm]], axis=0).astype(BF16)
    win_c = w_in_c[0].astype(BF16)
    wout_c = w_out_c[0].astype(BF16)
    wup = [w.astype(BF16) for w in w_ffn_up]
    wdown = [w.astype(BF16) for w in w_ffn_down]
    row = lambda a: a.reshape(1, -1)
    sinks = row(attn_sinks[0])
    ng = row(hgrn_norm_g[0])

    x1, conv_p, k_p, v_p = _mix_ab_prompt(x_prompt, row(norm_mix_pre[0]), row(norm_mix_post[0]), win_ab,
                                          conv_a_w[0], sinks, wout_ab)
    x2, ffn_p0 = _ffn_prompt(x1, row(norm_ffn_pre[0]), row(norm_ffn_post[0]), wup[0], ffn_conv_w[0], wdown[0])
    x3, hgrn_p = _hgrn_prompt(1, x2, row(norm_mix_pre[1]), row(norm_mix_post[1]), win_c, hgrn_lb_logits,
                              ng, wout_c)
    y_prompt, ffn_p1 = _ffn_prompt(x3, row(norm_ffn_pre[1]), row(norm_ffn_post[1]), wup[1], ffn_conv_w[1], wdown[1])

    xs = x_sample.reshape(ns * seg, d)
    kc = cache_swa_k[0].reshape(ns, WINDOW, nkv)
    vc = cache_swa_v[0].reshape(ns, WINDOW, nkv)
    s1, conv_s, k_s, v_s = _mix_ab_sample(xs, state_conv_a[0], kc, vc, seg, row(norm_mix_pre[0]),
                                          row(norm_mix_post[0]), win_ab, conv_a_w[0], sinks, wout_ab)
    s2, ffn_s0 = _ffn_sample(s1, state_ffn_conv[0], seg, row(norm_ffn_pre[0]), row(norm_ffn_post[0]),
                             wup[0], ffn_conv_w[0], wdown[0])
    s3, hgrn_s = _hgrn_sample(1, s2, state_hgrn[0], seg, row(norm_mix_pre[1]), row(norm_mix_post[1]),
                              win_c, hgrn_lb_logits, ng, wout_c)
    y_sample, ffn_s1 = _ffn_sample(s3, state_ffn_conv[1], seg, row(norm_ffn_pre[1]), row(norm_ffn_post[1]),
                                   wup[1], ffn_conv_w[1], wdown[1])

    kvshape = (KV_HEADS, HEAD_DIM)
    return (y_prompt, y_sample.reshape(ns, seg, d),
            conv_p[None], conv_s[None],
            k_p.reshape(1, bp, WINDOW, *kvshape), k_s.reshape(1, ns, WINDOW, *kvshape),
            v_p.reshape(1, bp, WINDOW, *kvshape), v_s.reshape(1, ns, WINDOW, *kvshape),
            hgrn_p[None], hgrn_s[None],
            jnp.stack([ffn_p0, ffn_p1]), jnp.stack([ffn_s0, ffn_s1]))
```

```python
import functools
import math

import jax
import jax.numpy as jnp
from jax import lax
from jax.experimental import pallas as pl
from jax.experimental.pallas import tpu as pltpu

F32 = jnp.float32
BF16 = jnp.bfloat16

EPS = 1e-6
WINDOW = 128
KV_HEADS = 2
GROUP = 4
HEAD_DIM = 64
C_HEADS = 8
C_DK = 128
NEG = -1e30
SUBLANES = 8
LANES = 128
MIX_TILE = 1024
HGRN_TILE = 512
FFN_TILE = 1024
HGRN_CHUNK = 64
FFN_CHUNK = 256
SAMPLE_SEQ_BLOCK = 8
VMEM_LIMIT_V7X = 56 * 2**20


def _dot(a, b):
    return jnp.dot(a, b, preferred_element_type=F32)


def _dot_nt(a, b):
    return lax.dot_general(a, b, (((1,), (1,)), ((), ())), preferred_element_type=F32)


def _dot_tn(a, b):
    return lax.dot_general(a, b, (((0,), (0,)), ((), ())), preferred_element_type=F32)


def _rms(x, g):
    return x * lax.rsqrt(jnp.mean(x * x, axis=-1, keepdims=True) + EPS) * g


def _gelu_tanh(x):
    c = math.sqrt(2.0 / math.pi)
    return x * (0.5 * (1.0 + jnp.tanh(c * (x + 0.044715 * (x * x * x)))))


def _sigmoid(x):
    return 1.0 / (1.0 + jnp.exp(-x))


def _resident(shape):
    nd = len(shape)
    return pl.BlockSpec(shape, lambda *_: (0,) * nd, pipeline_mode=pl.Buffered(1))


def _params(sem):
    return pltpu.CompilerParams(dimension_semantics=sem, vmem_limit_bytes=VMEM_LIMIT_V7X)


def _head_slope(h):
    return 2.0 ** (-8.0 * (h + 1) / (KV_HEADS * GROUP))


def _mix_ab_prompt_kernel(x_ref, gpre_ref, gpost_ref, win_ref, cw_ref, sink_ref, wout_ref,
                          y_ref, conv_ref, kout_ref, vout_ref,
                          ubuf, qbuf, kbuf, vbuf, mixbuf, bias_s):
    s = pl.program_id(1)
    ts = x_ref.shape[0]
    aw = cw_ref.shape[1]
    nq = GROUP * KV_HEADS * HEAD_DIM
    nkv = KV_HEADS * HEAD_DIM

    @pl.when(s == 0)
    def _():
        ubuf[0:SUBLANES, :] = jnp.zeros((SUBLANES, aw), F32)
        kbuf[0:WINDOW, :] = jnp.zeros((WINDOW, nkv), BF16)
        vbuf[0:WINDOW, 0:nkv] = jnp.zeros((WINDOW, nkv), BF16)
        vbuf[:, nkv:2 * nkv] = jnp.ones((ts + WINDOW, nkv), BF16)

    x = x_ref[...]
    h = _rms(x, gpre_ref[...]).astype(BF16)

    u = _dot(h, win_ref[:, 2 * aw:3 * aw]) * _dot(h, win_ref[:, 0:aw])
    ubuf[SUBLANES:SUBLANES + ts, :] = u
    cw = cw_ref[...]
    conv = (cw[2:3, :] * u + cw[1:2, :] * ubuf[SUBLANES - 1:SUBLANES - 1 + ts, :]
            + cw[0:1, :] * ubuf[SUBLANES - 2:SUBLANES - 2 + ts, :])
    mixbuf[:, 0:aw] = (_dot(h, win_ref[:, aw:2 * aw]) * conv).astype(BF16)
    tail = ubuf[ts + SUBLANES - 2:ts + SUBLANES, :]
    conv_ref[...] = tail
    ubuf[SUBLANES - 2:SUBLANES, :] = tail

    q0 = 3 * aw
    q = _dot(h, win_ref[:, q0:q0 + nq]) * (HEAD_DIM ** -0.5)
    lane = lax.broadcasted_iota(jnp.int32, (1, nkv), 1)
    lo = lane < HEAD_DIM
    nblk = ts // WINDOW
    for n in range(nblk):
        for g in range(GROUP):
            qg = q[n * WINDOW:(n + 1) * WINDOW, g * nkv:(g + 1) * nkv]
            qbuf[n, g * WINDOW:(g + 1) * WINDOW, :] = jnp.where(lo, qg, 0.0).astype(BF16)
            qbuf[n, (GROUP + g) * WINDOW:(GROUP + g + 1) * WINDOW, :] = jnp.where(lo, 0.0, qg).astype(BF16)
    kv_new = _dot(h, win_ref[:, q0 + nq:q0 + nq + 2 * nkv])
    k = kv_new[:, 0:nkv]
    v = kv_new[:, nkv:2 * nkv]
    kbuf[WINDOW:WINDOW + ts, :] = k.astype(BF16)
    vbuf[WINDOW:WINDOW + ts, 0:nkv] = v.astype(BF16)
    kout_ref[...] = k[ts - WINDOW:, :]
    vout_ref[...] = v[ts - WINDOW:, :]

    qi = lax.broadcasted_iota(jnp.int32, (WINDOW, 2 * WINDOW), 0)
    kj = lax.broadcasted_iota(jnp.int32, (WINDOW, 2 * WINDOW), 1)
    dist = qi + WINDOW - kj
    valid = (dist >= 0) & (dist < WINDOW)
    negdist = jnp.where(valid, -dist.astype(F32), 0.0)
    maskadd = jnp.where(valid, 0.0, NEG)
    nheads = KV_HEADS * GROUP
    for hidx in range(nheads):
        bias_s[hidx] = _head_slope(hidx) * negdist + maskadd
    first = jnp.where(kj < WINDOW, jnp.where(s == 0, NEG, 0.0), 0.0)

    for n in range(nblk):
        r0 = n * WINDOW
        sc_all = _dot_nt(qbuf[n], kbuf[r0:r0 + 2 * WINDOW, :])
        ps, sink_terms = [], []
        for hidx in range(nheads):
            sc = sc_all[hidx * WINDOW:(hidx + 1) * WINDOW, :] + bias_s[hidx]
            if n == 0:
                sc = sc + first
            sink = sink_ref[hidx]
            m = jnp.maximum(jnp.max(sc, axis=-1, keepdims=True), sink)
            ps.append(jnp.exp(sc - m).astype(BF16))
            sink_terms.append(jnp.exp(sink - m))
        pv = _dot(jnp.concatenate(ps, axis=0), vbuf[r0:r0 + 2 * WINDOW, :])
        outs = []
        for hidx in range(nheads):
            pv_h = pv[hidx * WINDOW:(hidx + 1) * WINDOW, :]
            outs.append(pv_h[:, 0:nkv] / (pv_h[:, nkv:2 * nkv] + sink_terms[hidx]))
        for g in range(GROUP):
            og = jnp.where(lo, outs[g], outs[GROUP + g])
            mixbuf[r0:r0 + WINDOW, aw + g * nkv:aw + (g + 1) * nkv] = og.astype(BF16)

    kbuf[0:WINDOW, :] = kbuf[ts:ts + WINDOW, :]
    vbuf[0:WINDOW, :] = vbuf[ts:ts + WINDOW, :]

    out = _dot(mixbuf[...], wout_ref[...])
    y_ref[...] = x + _rms(out, gpost_ref[...])


def _mix_ab_prompt(x, gpre, gpost, win, cw, sinks, wout):
    b, seq, d = x.shape
    ts = MIX_TILE
    aw = cw.shape[1]
    nq = GROUP * KV_HEADS * HEAD_DIM
    nkv = KV_HEADS * HEAD_DIM
    tile = pl.BlockSpec((None, ts, d), lambda i, j: (i, j, 0))
    return pl.pallas_call(
        _mix_ab_prompt_kernel,
        grid=(b, seq // ts),
        in_specs=[tile, _resident(gpre.shape), _resident(gpost.shape), _resident(win.shape),
                  _resident(cw.shape), pl.BlockSpec(memory_space=pltpu.SMEM), _resident(wout.shape)],
        out_specs=[tile,
                   pl.BlockSpec((None, 2, aw), lambda i, j: (i, 0, 0)),
                   pl.BlockSpec((None, WINDOW, nkv), lambda i, j: (i, 0, 0)),
                   pl.BlockSpec((None, WINDOW, nkv), lambda i, j: (i, 0, 0))],
        out_shape=[jax.ShapeDtypeStruct((b, seq, d), F32),
                   jax.ShapeDtypeStruct((b, 2, aw), F32),
                   jax.ShapeDtypeStruct((b, WINDOW, nkv), F32),
                   jax.ShapeDtypeStruct((b, WINDOW, nkv), F32)],
        scratch_shapes=[pltpu.VMEM((ts + SUBLANES, aw), F32),
                        pltpu.VMEM((ts // WINDOW, KV_HEADS * GROUP * WINDOW, nkv), BF16),
                        pltpu.VMEM((ts + WINDOW, nkv), BF16),
                        pltpu.VMEM((ts + WINDOW, 2 * nkv), BF16),
                        pltpu.VMEM((ts, d), BF16),
                        pltpu.VMEM((KV_HEADS * GROUP, WINDOW, 2 * WINDOW), F32)],
        compiler_params=_params(("arbitrary", "arbitrary")),
        name="mix_ab_prompt",
    )(x, gpre, gpost, win, cw, sinks.reshape(-1), wout)


def _ffn_prompt_kernel(x_ref, gpre_ref, gpost_ref, wup_ref, cw_ref, wdown_ref,
                       y_ref, st_ref, abuf, carry, gbuf):
    s = pl.program_id(1)
    rows = x_ref.shape[0]
    ff = cw_ref.shape[1]

    @pl.when(s == 0)
    def _():
        carry[...] = jnp.zeros(carry.shape, F32)

    x = x_ref[...]
    h = _rms(x, gpre_ref[...]).astype(BF16)
    for c in range(ff // FFN_CHUNK):
        c0 = c * FFN_CHUNK
        buf = abuf.at[c % 2]
        a = _dot(h, wup_ref[:, c0:c0 + FFN_CHUNK])
        b = _dot(h, wup_ref[:, ff + c0:ff + c0 + FFN_CHUNK])
        buf[0:SUBLANES, :] = carry[:, c0:c0 + FFN_CHUNK]
        buf[SUBLANES:SUBLANES + rows, :] = a
        cw = cw_ref[:, c0:c0 + FFN_CHUNK]
        conv = (cw[2:3, :] * a + cw[1:2, :] * buf[SUBLANES - 1:SUBLANES - 1 + rows, :]
                + cw[0:1, :] * buf[SUBLANES - 2:SUBLANES - 2 + rows, :])
        tail = a[rows - 2:rows, :]
        st_ref[:, c0:c0 + FFN_CHUNK] = tail
        carry[SUBLANES - 2:SUBLANES, c0:c0 + FFN_CHUNK] = tail
        gbuf[:, c0:c0 + FFN_CHUNK] = (_gelu_tanh(conv) * b).astype(BF16)
    out = _dot(gbuf[...], wdown_ref[...])
    y_ref[...] = x + _rms(out, gpost_ref[...])


def _ffn_prompt(layer, x, gpre, gpost, wup, cw, wdown):
    b, seq, d = x.shape
    ts = FFN_TILE
    ff = cw.shape[1]
    tile = pl.BlockSpec((None, ts, d), lambda i, j: (i, j, 0))
    layer_block = lambda w: pl.BlockSpec((None,) + w.shape[1:], lambda i, j: (layer, 0, 0),
                                         pipeline_mode=pl.Buffered(1))
    return pl.pallas_call(
        _ffn_prompt_kernel,
        grid=(b, seq // ts),
        in_specs=[tile, _resident(gpre.shape), _resident(gpost.shape), layer_block(wup),
                  _resident(cw.shape), layer_block(wdown)],
        out_specs=[tile, pl.BlockSpec((None, 2, ff), lambda i, j: (i, 0, 0))],
        out_shape=[jax.ShapeDtypeStruct((b, seq, d), F32), jax.ShapeDtypeStruct((b, 2, ff), F32)],
        scratch_shapes=[pltpu.VMEM((2, ts + SUBLANES, FFN_CHUNK), F32), pltpu.VMEM((SUBLANES, ff), F32),
                        pltpu.VMEM((ts, ff), BF16)],
        compiler_params=_params(("arbitrary", "arbitrary")),
        name="ffn_prompt",
    )(x, gpre, gpost, wup, cw, wdown)


def _segment_conv_taps(a, t, seg, prev0_ref, prev1_ref, abuf, fix1, fix2):
    rows, c = a.shape
    nseq = rows // seg
    am1, am2 = [], []
    for k in range(c // LANES):
        sl = slice(k * LANES, (k + 1) * LANES)
        abuf[k, SUBLANES:SUBLANES + rows, :] = a[:, sl]
        fix1[k, pl.ds(0, nseq, stride=seg), :] = prev1_ref[:, sl]
        fix2[k, pl.ds(0, nseq, stride=seg), :] = prev0_ref[:, sl]
        fix2[k, pl.ds(1, nseq, stride=seg), :] = prev1_ref[:, sl]
        am1.append(jnp.where(t >= 1, abuf[k, SUBLANES - 1:SUBLANES - 1 + rows, :], fix1[k]))
        am2.append(jnp.where(t >= 2, abuf[k, SUBLANES - 2:SUBLANES - 2 + rows, :], fix2[k]))
    return jnp.concatenate(am1, axis=1), jnp.concatenate(am2, axis=1)


def _segment_tails(abuf, nseq, seg):
    tail = lambda back: jnp.concatenate(
        [abuf[k, pl.ds(SUBLANES + seg - back, nseq, stride=seg), :] for k in range(abuf.shape[0])], axis=1)
    return tail(2), tail(1)


def _ffn_sample_kernel(seg, x_ref, prev0_ref, prev1_ref, gpre_ref, gpost_ref, wa_ref, wb_ref, cw_ref, wdown_ref,
                       y_ref, st0_ref, st1_ref, hbuf, abuf, fix1, fix2, acc):
    c = pl.program_id(0)
    rows = x_ref.shape[0]

    @pl.when(c == 0)
    def _():
        hbuf[...] = _rms(x_ref[...], gpre_ref[...]).astype(BF16)
        acc[...] = jnp.zeros(acc.shape, F32)
        abuf[:, 0:SUBLANES, :] = jnp.zeros((abuf.shape[0], SUBLANES, LANES), F32)
        fix1[...] = jnp.zeros(fix1.shape, F32)
        fix2[...] = jnp.zeros(fix2.shape, F32)

    h = hbuf[...]
    t = lax.broadcasted_iota(jnp.int32, (rows, 1), 0) % seg
    a = _dot(h, wa_ref[...])
    b = _dot(h, wb_ref[...])
    am1, am2 = _segment_conv_taps(a, t, seg, prev0_ref, prev1_ref, abuf, fix1, fix2)
    st0_ref[...], st1_ref[...] = _segment_tails(abuf, rows // seg, seg)
    cw = cw_ref[...]
    conv = cw[2:3, :] * a + cw[1:2, :] * am1 + cw[0:1, :] * am2
    acc[...] += _dot((_gelu_tanh(conv) * b).astype(BF16), wdown_ref[...])

    @pl.when(c == pl.num_programs(0) - 1)
    def _():
        y_ref[...] = x_ref[...] + _rms(acc[...], gpost_ref[...])


def _ffn_sample(layer, x, prev, seg, gpre, gpost, wup, cw, wdown):
    rows, d = x.shape
    nseq = rows // seg
    ff = cw.shape[1]
    cf = FFN_CHUNK
    nc = ff // cf
    cols = lambda r: pl.BlockSpec((r, cf), lambda c: (0, c))
    y, st0, st1 = pl.pallas_call(
        functools.partial(_ffn_sample_kernel, seg),
        grid=(nc,),
        in_specs=[_resident(x.shape), cols(nseq), cols(nseq), _resident(gpre.shape), _resident(gpost.shape),
                  pl.BlockSpec((None, d, cf), lambda c: (layer, 0, c)),
                  pl.BlockSpec((None, d, cf), lambda c: (layer, 0, c + nc)), cols(3),
                  pl.BlockSpec((None, cf, d), lambda c: (layer, c, 0))],
        out_specs=[pl.BlockSpec((rows, d), lambda c: (0, 0)), cols(nseq), cols(nseq)],
        out_shape=[jax.ShapeDtypeStruct((rows, d), F32), jax.ShapeDtypeStruct((nseq, ff), F32),
                   jax.ShapeDtypeStruct((nseq, ff), F32)],
        scratch_shapes=[pltpu.VMEM((rows, d), BF16), pltpu.VMEM((cf // LANES, rows + SUBLANES, LANES), F32),
                        pltpu.VMEM((cf // LANES, rows, LANES), F32), pltpu.VMEM((cf // LANES, rows, LANES), F32),
                        pltpu.VMEM((rows, d), F32)],
        compiler_params=_params(("arbitrary",)),
        name="ffn_sample",
    )(x, prev[:, 0], prev[:, 1], gpre, gpost, wup, wup, cw, wdown)
    return y, jnp.stack([st0, st1], axis=1)


LOG2E = 1.4426950408889634


def _lower_bound_terms(logits_ref, layer):
    lg = logits_ref[...]
    e = jnp.exp(lg - jnp.max(lg, axis=0, keepdims=True))
    pr = e / jnp.sum(e, axis=0, keepdims=True)
    cum = pr[0:1, :]
    for r in range(1, layer + 1):
        cum = cum + pr[r:r + 1, :]
    lb = jnp.clip(cum - pr[0:1, :], 0.0, 1.0)
    return jnp.log(lb) * LOG2E, jnp.log1p(-lb) * LOG2E, 1.0 - lb


def _forget_gate(f, lb_terms):
    log2_lb, log2_1m_lb, one_m_lb = lb_terms
    f2 = f * LOG2E
    e = jnp.exp2(jnp.minimum(f2, -f2))
    w = 1.0 + e
    b = log2_1m_lb + (jnp.minimum(f2, 0.0) - jnp.log2(w))
    d = log2_lb - b
    log2_ft = jnp.maximum(log2_lb, b) + jnp.log2(1.0 + jnp.exp2(jnp.minimum(d, -d)))
    return log2_ft, one_m_lb * (jnp.where(f2 >= 0.0, e, 1.0) / w)


def _split_bf16(x):
    hi = x.astype(BF16)
    return hi, (x - hi.astype(F32)).astype(BF16)


def _hgrn_gate_out(x, h, ob, mixbuf, win_ref, ng_ref, wout_ref, gpost_ref):
    dkt = C_HEADS * C_DK
    g = _dot(h, win_ref[:, 3 * dkt:4 * dkt])
    ng = ng_ref[...]
    for hh in range(C_HEADS):
        sl = slice(hh * C_DK, (hh + 1) * C_DK)
        mixbuf[:, sl] = (_rms(ob[:, sl], ng) * (g[:, sl] * _sigmoid(g[:, sl]))).astype(BF16)
    out = _dot(mixbuf[...], wout_ref[...])
    return x + _rms(out, gpost_ref[...])


def _hgrn_prompt_kernel(layer, x_ref, gpre_ref, gpost_ref, win_ref, lbl_ref, ng_ref, wout_ref,
                        y_ref, sout_ref, st, qtb, ktb, qib, klb, eglb, vb, ob, mixbuf):
    s = pl.program_id(1)
    ts = x_ref.shape[0]
    dkt = C_HEADS * C_DK
    ck = HGRN_CHUNK

    @pl.when(s == 0)
    def _():
        st[...] = jnp.zeros(st.shape, F32)

    x = x_ref[...]
    h = _rms(x, gpre_ref[...]).astype(BF16)
    f = _dot(h, win_ref[:, dkt:2 * dkt])
    log2_ft, k_in = _forget_gate(f, _lower_bound_terms(lbl_ref, layer))
    q = _dot(h, win_ref[:, 0:dkt])
    vb[...] = _dot(h, win_ref[:, 2 * dkt:3 * dkt]).astype(BF16)

    ri = lax.broadcasted_iota(jnp.int32, (ck, ck), 0)
    ci = lax.broadcasted_iota(jnp.int32, (ck, ck), 1)
    causal = ri >= ci
    tri = jnp.where(causal, 1.0, 0.0).astype(BF16)
    mid = ck // 2 - 1
    nck = ts // ck

    for c in range(nck):
        rows = slice(c * ck, (c + 1) * ck)
        hi, lo = _split_bf16(log2_ft[rows, :])
        gcum = _dot(tri, hi) + _dot(tri, lo)
        gr = gcum[mid:mid + 1, :]
        gl = gcum[ck - 1:ck, :]
        qc = q[rows, :]
        kc = k_in[rows, :]
        qtb[rows, :] = (qc * jnp.exp2(gcum - gr)).astype(BF16)
        ktb[rows, :] = (kc * jnp.exp2(gr - gcum)).astype(BF16)
        qib[rows, :] = (qc * jnp.exp2(gcum)).astype(BF16)
        klb[rows, :] = (kc * jnp.exp2(gl - gcum)).astype(BF16)
        eglb[c] = jnp.exp2(gl)

    for c in range(nck):
        rows = slice(c * ck, (c + 1) * ck)
        for hh in range(C_HEADS):
            sl = slice(hh * C_DK, (hh + 1) * C_DK)
            vh = vb[rows, sl]
            att = jnp.where(causal, _dot_nt(qtb[rows, sl], ktb[rows, sl]), 0.0).astype(BF16)
            st_h = st[hh]
            ob[rows, sl] = _dot(jnp.concatenate([qib[rows, sl], att], axis=1),
                                jnp.concatenate([st_h.astype(BF16).T, vh], axis=0))
            st[hh] = eglb[c, :, sl] * st_h + _dot_tn(vh, klb[rows, sl])

    y_ref[...] = _hgrn_gate_out(x, h, ob, mixbuf, win_ref, ng_ref, wout_ref, gpost_ref)

    @pl.when(s == pl.num_programs(1) - 1)
    def _():
        for hh in range(C_HEADS):
            sout_ref[hh] = st[hh].T


def _hgrn_prompt(layer, x, gpre, gpost, win, lbl, ng, wout):
    b, seq, d = x.shape
    ts = HGRN_TILE
    dkt = C_HEADS * C_DK
    tile = pl.BlockSpec((None, ts, d), lambda i, j: (i, j, 0))
    return pl.pallas_call(
        functools.partial(_hgrn_prompt_kernel, layer),
        grid=(b, seq // ts),
        in_specs=[tile, _resident(gpre.shape), _resident(gpost.shape), _resident(win.shape),
                  _resident(lbl.shape), _resident(ng.shape), _resident(wout.shape)],
        out_specs=[tile, pl.BlockSpec((None, C_HEADS, C_DK, C_DK), lambda i, j: (i, 0, 0, 0))],
        out_shape=[jax.ShapeDtypeStruct((b, seq, d), F32),
                   jax.ShapeDtypeStruct((b, C_HEADS, C_DK, C_DK), F32)],
        scratch_shapes=[pltpu.VMEM((C_HEADS, C_DK, C_DK), F32),
                        pltpu.VMEM((ts, dkt), BF16), pltpu.VMEM((ts, dkt), BF16),
                        pltpu.VMEM((ts, dkt), BF16), pltpu.VMEM((ts, dkt), BF16),
                        pltpu.VMEM((ts // HGRN_CHUNK, 1, dkt), F32),
                        pltpu.VMEM((ts, dkt), BF16), pltpu.VMEM((ts, dkt), F32),
                        pltpu.VMEM((ts, dkt), BF16)],
        compiler_params=_params(("arbitrary", "arbitrary")),
        name="hgrn_prompt",
    )(x, gpre, gpost, win, lbl, ng, wout)


def _mix_ab_sample_kernel(seg, x_ref, prev0_ref, prev1_ref, kc_ref, vc_ref, gpre_ref, gpost_ref, win_ref,
                          cw_ref, sink_ref, wout_ref,
                          y_ref, conv0_ref, conv1_ref, kout_ref, vout_ref,
                          ubuf, fix1, fix2, qbuf, knew_s, vnew_s, yabuf, ybbuf, kall, vall, bias_s, sink_s):
    i = pl.program_id(0)
    rows = x_ref.shape[0]
    aw = cw_ref.shape[1]
    nq = GROUP * KV_HEADS * HEAD_DIM
    nkv = KV_HEADS * HEAD_DIM
    pair = 2 * seg
    qrows = KV_HEADS * GROUP * pair
    tk = WINDOW + pair

    @pl.when(i == 0)
    def _():
        x = x_ref[...]
        h = _rms(x, gpre_ref[...]).astype(BF16)
        t = lax.broadcasted_iota(jnp.int32, (rows, 1), 0) % seg
        u = _dot(h, win_ref[:, 2 * aw:3 * aw]) * _dot(h, win_ref[:, 0:aw])
        ubuf[:, 0:SUBLANES, :] = jnp.zeros((ubuf.shape[0], SUBLANES, LANES), F32)
        fix1[...] = jnp.zeros(fix1.shape, F32)
        fix2[...] = jnp.zeros(fix2.shape, F32)
        um1, um2 = _segment_conv_taps(u, t, seg, prev0_ref, prev1_ref, ubuf, fix1, fix2)
        conv0_ref[...], conv1_ref[...] = _segment_tails(ubuf, rows // seg, seg)
        cw = cw_ref[...]
        conv = cw[2:3, :] * u + cw[1:2, :] * um1 + cw[0:1, :] * um2
        yabuf[...] = (_dot(h, win_ref[:, aw:2 * aw]) * conv).astype(BF16)
        q0 = 3 * aw
        qbuf[...] = _dot(h, win_ref[:, q0:q0 + nq]) * (HEAD_DIM ** -0.5)
        kv_new = _dot(h, win_ref[:, q0 + nq:q0 + nq + 2 * nkv])
        knew_s[...] = kv_new[:, 0:nkv]
        vnew_s[...] = kv_new[:, nkv:2 * nkv]

        r = lax.broadcasted_iota(jnp.int32, (qrows, tk), 0)
        j = lax.broadcasted_iota(jnp.int32, (qrows, tk), 1)
        dist = r % seg + WINDOW - j
        valid = (dist >= 0) & (dist < WINDOW) & (j < WINDOW + seg)
        rc = lax.broadcasted_iota(jnp.int32, (qrows, 1), 0)
        head = (rc // (GROUP * pair)) * GROUP + (rc // pair) % GROUP
        slope = jnp.zeros((qrows, 1), F32)
        sink = jnp.zeros((qrows, 1), F32)
        for hidx in range(KV_HEADS * GROUP):
            slope = jnp.where(head == hidx, _head_slope(hidx), slope)
            sink = jnp.where(head == hidx, sink_ref[:, hidx:hidx + 1], sink)
        bias_s[...] = jnp.where(valid, -slope * dist.astype(F32), NEG)
        sink_s[...] = sink

    nblk = kc_ref.shape[0]
    lane = lax.broadcasted_iota(jnp.int32, (1, nkv), 1)
    lo = lane < HEAD_DIM
    rq = lax.broadcasted_iota(jnp.int32, (qrows, 1), 0)
    first_seq = (rq // seg) % 2 == 0

    def body(p, carry):
        r0 = pl.multiple_of((i * nblk + 2 * p) * seg, pair)
        q8 = qbuf[pl.ds(r0, pair), :]
        qg = jnp.concatenate([q8[:, g * nkv:(g + 1) * nkv] for g in range(GROUP)], axis=0)
        qs = jnp.concatenate([jnp.where(lo, qg, 0.0), jnp.where(lo, 0.0, qg)], axis=0).astype(BF16)
        knew = knew_s[pl.ds(r0, pair), :]
        vnew = vnew_s[pl.ds(r0, pair), :]
        sc, ov = [], []
        for a in range(2):
            own_first = (lambda z: z) if a == 0 else (lambda z: jnp.concatenate([z[seg:], z[:seg]], axis=0))
            kall[a, 0:WINDOW, :] = kc_ref[2 * p + a]
            kall[a, WINDOW:tk, :] = own_first(knew)
            vall[a, 0:WINDOW, :] = vc_ref[2 * p + a]
            vall[a, WINDOW:tk, :] = own_first(vnew)
            kout_ref[2 * p + a] = kall[a, seg:seg + WINDOW, :]
            vout_ref[2 * p + a] = vall[a, seg:seg + WINDOW, :]
            sc.append(_dot_nt(qs, kall[a].astype(BF16)))
        s2 = jnp.where(first_seq, sc[0], sc[1]) + bias_s[...]
        sink = sink_s[...]
        m = jnp.maximum(jnp.max(s2, axis=-1, keepdims=True), sink)
        pr = jnp.exp(s2 - m)
        den = jnp.sum(pr, axis=-1, keepdims=True) + jnp.exp(sink - m)
        prb = pr.astype(BF16)
        for a in range(2):
            ov.append(_dot(prb, vall[a].astype(BF16)))
        o = jnp.where(first_seq, ov[0], ov[1]) / den
        og = jnp.where(lo, o[0:GROUP * pair, :], o[GROUP * pair:, :])
        for g in range(GROUP):
            ybbuf[pl.ds(r0, pair), g * nkv:(g + 1) * nkv] = og[g * pair:(g + 1) * pair, :]
        return carry

    lax.fori_loop(0, nblk // 2, body, 0)

    @pl.when(i == pl.num_programs(0) - 1)
    def _():
        out = _dot(yabuf[...], wout_ref[0:aw, :]) + _dot(ybbuf[...].astype(BF16), wout_ref[aw:, :])
        y_ref[...] = x_ref[...] + _rms(out, gpost_ref[...])


def _mix_ab_sample(x, conv_prev, kcache, vcache, seg, gpre, gpost, win, cw, sinks, wout):
    rows, d = x.shape
    nseq = rows // seg
    aw = cw.shape[1]
    nq = GROUP * KV_HEADS * HEAD_DIM
    nkv = KV_HEADS * HEAD_DIM
    sb = SAMPLE_SEQ_BLOCK
    pair = 2 * seg
    qrows = KV_HEADS * GROUP * pair
    tk = WINDOW + pair
    full = lambda shape: pl.BlockSpec(shape, lambda i: (0,) * len(shape))
    cache = pl.BlockSpec((sb, WINDOW, nkv), lambda i: (i, 0, 0))
    prev0, prev1 = conv_prev[:, 0], conv_prev[:, 1]
    y, conv0, conv1, k_new, v_new = pl.pallas_call(
        functools.partial(_mix_ab_sample_kernel, seg),
        grid=(nseq // sb,),
        in_specs=[_resident(x.shape), _resident(prev0.shape), _resident(prev1.shape), cache, cache,
                  _resident(gpre.shape), _resident(gpost.shape), _resident(win.shape), _resident(cw.shape),
                  _resident(sinks.shape), _resident(wout.shape)],
        out_specs=[full((rows, d)), full((nseq, aw)), full((nseq, aw)), cache, cache],
        out_shape=[jax.ShapeDtypeStruct((rows, d), F32), jax.ShapeDtypeStruct((nseq, aw), F32),
                   jax.ShapeDtypeStruct((nseq, aw), F32), jax.ShapeDtypeStruct(kcache.shape, F32),
                   jax.ShapeDtypeStruct(vcache.shape, F32)],
        scratch_shapes=[pltpu.VMEM((aw // LANES, rows + SUBLANES, LANES), F32),
                        pltpu.VMEM((aw // LANES, rows, LANES), F32),
                        pltpu.VMEM((aw // LANES, rows, LANES), F32),
                        pltpu.VMEM((rows, nq), F32),
                        pltpu.VMEM((rows, nkv), F32),
                        pltpu.VMEM((rows, nkv), F32),
                        pltpu.VMEM((rows, aw), BF16),
                        pltpu.VMEM((rows, nq), F32),
                        pltpu.VMEM((2, tk, nkv), F32),
                        pltpu.VMEM((2, tk, nkv), F32),
                        pltpu.VMEM((qrows, tk), F32),
                        pltpu.VMEM((qrows, 1), F32)],
        compiler_params=_params(("arbitrary",)),
        name="mix_ab_sample",
    )(x, prev0, prev1, kcache, vcache, gpre, gpost, win, cw, sinks, wout)
    return y, jnp.stack([conv0, conv1], axis=1), k_new, v_new


def _hgrn_sample_kernel(layer, seg, x_ref, s0_ref, gpre_ref, gpost_ref, win_ref, lbl_ref, ng_ref, wout_ref,
                        y_ref, s1_ref, qb, kb, vb, gb, ob, mixbuf):
    i = pl.program_id(0)
    rows = x_ref.shape[0]
    dkt = C_HEADS * C_DK
    pair = 2 * seg

    @pl.when(i == 0)
    def _():
        x = x_ref[...]
        h = _rms(x, gpre_ref[...]).astype(BF16)
        f = _dot(h, win_ref[:, dkt:2 * dkt])
        log2_ft, kb[...] = _forget_gate(f, _lower_bound_terms(lbl_ref, layer))
        hi, lo = _split_bf16(log2_ft)
        ri = lax.broadcasted_iota(jnp.int32, (rows, rows), 0)
        ci = lax.broadcasted_iota(jnp.int32, (rows, rows), 1)
        tri = jnp.where((ri // seg == ci // seg) & (ri >= ci), 1.0, 0.0).astype(BF16)
        gb[...] = _dot(tri, hi) + _dot(tri, lo)
        qb[...] = _dot(h, win_ref[:, 0:dkt])
        vb[...] = _dot(h, win_ref[:, 2 * dkt:3 * dkt])

    nblk = s0_ref.shape[0]
    rr = lax.broadcasted_iota(jnp.int32, (pair, 1), 0)
    first_seq = rr < seg
    second_seq = rr >= seg
    ri = lax.broadcasted_iota(jnp.int32, (pair, pair), 0)
    ci = lax.broadcasted_iota(jnp.int32, (pair, pair), 1)
    causal = (ri // seg == ci // seg) & (ri >= ci)
    mid = seg // 2 - 1

    def body(p, carry):
        r0 = pl.multiple_of((i * nblk + 2 * p) * seg, pair)
        for hh in range(C_HEADS):
            sl = slice(hh * C_DK, (hh + 1) * C_DK)
            gh = gb[pl.ds(r0, pair), sl]
            gr = jnp.where(first_seq, gh[mid:mid + 1, :], gh[seg + mid:seg + mid + 1, :])
            gl = jnp.where(first_seq, gh[seg - 1:seg, :], gh[pair - 1:pair, :])
            qh = qb[pl.ds(r0, pair), sl]
            kh = kb[pl.ds(r0, pair), sl]
            vh = vb[pl.ds(r0, pair), sl].astype(BF16)
            qt = (qh * jnp.exp2(gh - gr)).astype(BF16)
            kt = (kh * jnp.exp2(gr - gh)).astype(BF16)
            att = jnp.where(causal, _dot_nt(qt, kt), 0.0).astype(BF16)
            qi = (qh * jnp.exp2(gh)).astype(BF16)
            kl = kh * jnp.exp2(gl - gh)
            o_prev = []
            for a in range(2):
                s_a = s0_ref[2 * p + a, hh]
                o_prev.append(_dot(qi, s_a.astype(BF16)))
                kl_a = jnp.where(first_seq if a == 0 else second_seq, kl, 0.0).astype(BF16)
                e = jnp.exp2(gh[(a + 1) * seg - 1:(a + 1) * seg, :])
                e1 = e.astype(BF16).astype(F32)
                e2 = (e - e1).astype(BF16).astype(F32)
                e3 = (e - e1 - e2).astype(BF16).astype(F32)
                dec_l = jnp.where(rr == 0, e1, jnp.where(rr == 1, e2, jnp.where(rr == 2, e3, 0.0)))
                dec = _dot_tn(dec_l.astype(BF16), jnp.ones((pair, C_DK), BF16))
                s1_ref[2 * p + a, hh] = dec * s_a + _dot_tn(kl_a, vh)
            ob[pl.ds(r0, pair), sl] = _dot(att, vh) + jnp.where(first_seq, o_prev[0], o_prev[1])
        return carry

    lax.fori_loop(0, nblk // 2, body, 0)

    @pl.when(i == pl.num_programs(0) - 1)
    def _():
        x = x_ref[...]
        h = _rms(x, gpre_ref[...]).astype(BF16)
        y_ref[...] = _hgrn_gate_out(x, h, ob, mixbuf, win_ref, ng_ref, wout_ref, gpost_ref)


def _hgrn_sample(layer, x, s0, seg, gpre, gpost, win, lbl, ng, wout):
    rows, d = x.shape
    nseq = rows // seg
    dkt = C_HEADS * C_DK
    sb = SAMPLE_SEQ_BLOCK
    full = lambda shape: pl.BlockSpec(shape, lambda i: (0,) * len(shape))
    state = pl.BlockSpec((sb, C_HEADS, C_DK, C_DK), lambda i: (i, 0, 0, 0))
    return pl.pallas_call(
        functools.partial(_hgrn_sample_kernel, layer, seg),
        grid=(nseq // sb,),
        in_specs=[_resident(x.shape), state, _resident(gpre.shape), _resident(gpost.shape), _resident(win.shape),
                  _resident(lbl.shape), _resident(ng.shape), _resident(wout.shape)],
        out_specs=[full((rows, d)), state],
        out_shape=[jax.ShapeDtypeStruct((rows, d), F32), jax.ShapeDtypeStruct(s0.shape, F32)],
        scratch_shapes=[pltpu.VMEM((rows, dkt), F32), pltpu.VMEM((rows, dkt), F32),
                        pltpu.VMEM((rows, dkt), F32), pltpu.VMEM((rows, dkt), F32),
                        pltpu.VMEM((rows, dkt), F32), pltpu.VMEM((rows, dkt), BF16)],
        compiler_params=_params(("arbitrary",)),
        name="hgrn_sample",
    )(x, s0, gpre, gpost, win, lbl, ng, wout)


def _head_permutation():
    cols = []
    for g in range(GROUP):
        for kv in range(KV_HEADS):
            hidx = kv * GROUP + g
            cols.extend(range(hidx * HEAD_DIM, (hidx + 1) * HEAD_DIM))
    return jnp.asarray(cols, dtype=jnp.int32)


def kernel(x_prompt, x_sample, state_conv_a, cache_swa_k, cache_swa_v, state_hgrn, state_ffn_conv, norm_mix_pre, norm_mix_post, norm_ffn_pre, norm_ffn_post, w_in_ab, conv_a_w, attn_sinks, w_out_ab, w_in_c, hgrn_lb_logits, hgrn_norm_g, w_out_c, w_ffn_up, ffn_conv_w, w_ffn_down):
    bp, seq, d = x_prompt.shape
    ns, seg, _ = x_sample.shape
    aw = conv_a_w.shape[-1]
    nq = GROUP * KV_HEADS * HEAD_DIM
    nkv = KV_HEADS * HEAD_DIM

    perm = _head_permutation()
    q0 = 3 * aw
    win_ab = w_in_ab[0]
    win_ab = jnp.concatenate([win_ab[:, :q0], win_ab[:, q0:q0 + nq][:, perm], win_ab[:, q0 + nq:]], axis=1).astype(BF16)
    wout_ab = w_out_ab[0]
    wout_ab = jnp.concatenate([wout_ab[:aw], wout_ab[aw:][perm]], axis=0).astype(BF16)
    win_c = w_in_c[0].astype(BF16)
    wout_c = w_out_c[0].astype(BF16)
    wup = w_ffn_up.astype(BF16)
    wdown = w_ffn_down.astype(BF16)
    row = lambda a: a.reshape(1, -1)
    sinks = row(attn_sinks[0])
    ng = row(hgrn_norm_g[0])

    x1, conv_p, k_p, v_p = _mix_ab_prompt(x_prompt, row(norm_mix_pre[0]), row(norm_mix_post[0]), win_ab,
                                          conv_a_w[0], sinks, wout_ab)
    x2, ffn_p0 = _ffn_prompt(0, x1, row(norm_ffn_pre[0]), row(norm_ffn_post[0]), wup, ffn_conv_w[0], wdown)
    x3, hgrn_p = _hgrn_prompt(1, x2, row(norm_mix_pre[1]), row(norm_mix_post[1]), win_c, hgrn_lb_logits,
                              ng, wout_c)
    y_prompt, ffn_p1 = _ffn_prompt(1, x3, row(norm_ffn_pre[1]), row(norm_ffn_post[1]), wup, ffn_conv_w[1], wdown)

    xs = x_sample.reshape(ns * seg, d)
    kc = cache_swa_k[0].reshape(ns, WINDOW, nkv)
    vc = cache_swa_v[0].reshape(ns, WINDOW, nkv)
    s1, conv_s, k_s, v_s = _mix_ab_sample(xs, state_conv_a[0], kc, vc, seg, row(norm_mix_pre[0]),
                                          row(norm_mix_post[0]), win_ab, conv_a_w[0], sinks, wout_ab)
    s2, ffn_s0 = _ffn_sample(0, s1, state_ffn_conv[0], seg, row(norm_ffn_pre[0]), row(norm_ffn_post[0]),
                             wup, ffn_conv_w[0], wdown)
    s3, hgrn_s = _hgrn_sample(1, s2, state_hgrn[0], seg, row(norm_mix_pre[1]), row(norm_mix_post[1]),
                              win_c, hgrn_lb_logits, ng, wout_c)
    y_sample, ffn_s1 = _ffn_sample(1, s3, state_ffn_conv[1], seg, row(norm_ffn_pre[1]), row(norm_ffn_post[1]),
                                   wup, ffn_conv_w[1], wdown)

    kvshape = (KV_HEADS, HEAD_DIM)
    return (y_prompt, y_sample.reshape(ns, seg, d),
            conv_p[None], conv_s[None],
            k_p.reshape(1, bp, WINDOW, *kvshape), k_s.reshape(1, ns, WINDOW, *kvshape),
            v_p.reshape(1, bp, WINDOW, *kvshape), v_s.reshape(1, ns, WINDOW, *kvshape),
            hgrn_p[None], hgrn_s[None],
            jnp.stack([ffn_p0, ffn_p1]), jnp.stack([ffn_s0, ffn_s1]))
```

```python
import functools
import math

import jax
import jax.numpy as jnp
from jax import lax
from jax.experimental import pallas as pl
from jax.experimental.pallas import tpu as pltpu

F32 = jnp.float32
BF16 = jnp.bfloat16

EPS = 1e-6
WINDOW = 128
KV_HEADS = 2
GROUP = 4
HEAD_DIM = 64
C_HEADS = 8
C_DK = 128
NEG = -1e30
SUBLANES = 8
LANES = 128
MIX_TILE = 1024
HGRN_TILE = 512
FFN_TILE = 1024
HGRN_CHUNK = 64
DECAY_LOG2_LIMIT = 100.0
FFN_CHUNK = 256
SAMPLE_SEQ_BLOCK = 8
VMEM_LIMIT_V7X = 56 * 2**20


def _dot(a, b):
    return jnp.dot(a, b, preferred_element_type=F32)


def _dot_nt(a, b):
    return lax.dot_general(a, b, (((1,), (1,)), ((), ())), preferred_element_type=F32)


def _dot_tn(a, b):
    return lax.dot_general(a, b, (((0,), (0,)), ((), ())), preferred_element_type=F32)


def _rms(x, g):
    return x * lax.rsqrt(jnp.mean(x * x, axis=-1, keepdims=True) + EPS) * g


def _gelu_tanh(x):
    c = math.sqrt(2.0 / math.pi)
    return x * (0.5 * (1.0 + jnp.tanh(c * (x + 0.044715 * (x * x * x)))))


def _sigmoid(x):
    return 1.0 / (1.0 + jnp.exp(-x))


def _resident(shape):
    nd = len(shape)
    return pl.BlockSpec(shape, lambda *_: (0,) * nd, pipeline_mode=pl.Buffered(1))


def _params(sem):
    return pltpu.CompilerParams(dimension_semantics=sem, vmem_limit_bytes=VMEM_LIMIT_V7X)


def _head_slope(h):
    return 2.0 ** (-8.0 * (h + 1) / (KV_HEADS * GROUP))


def _mix_ab_prompt_kernel(x_ref, gpre_ref, gpost_ref, win_ref, cw_ref, sink_ref, wout_ref,
                          y_ref, conv_ref, kout_ref, vout_ref,
                          ubuf, qbuf, kbuf, vbuf, mixbuf, bias_s):
    s = pl.program_id(1)
    ts = x_ref.shape[0]
    aw = cw_ref.shape[1]
    nq = GROUP * KV_HEADS * HEAD_DIM
    nkv = KV_HEADS * HEAD_DIM

    @pl.when(s == 0)
    def _():
        ubuf[0:SUBLANES, :] = jnp.zeros((SUBLANES, aw), F32)
        kbuf[0:WINDOW, :] = jnp.zeros((WINDOW, nkv), BF16)
        vbuf[0:WINDOW, 0:nkv] = jnp.zeros((WINDOW, nkv), BF16)
        vbuf[:, nkv:2 * nkv] = jnp.ones((ts + WINDOW, nkv), BF16)

    x = x_ref[...]
    h = _rms(x, gpre_ref[...]).astype(BF16)

    u = _dot(h, win_ref[:, 2 * aw:3 * aw]) * _dot(h, win_ref[:, 0:aw])
    ubuf[SUBLANES:SUBLANES + ts, :] = u
    cw = cw_ref[...]
    conv = (cw[2:3, :] * u + cw[1:2, :] * ubuf[SUBLANES - 1:SUBLANES - 1 + ts, :]
            + cw[0:1, :] * ubuf[SUBLANES - 2:SUBLANES - 2 + ts, :])
    mixbuf[:, 0:aw] = (_dot(h, win_ref[:, aw:2 * aw]) * conv).astype(BF16)
    tail = ubuf[ts + SUBLANES - 2:ts + SUBLANES, :]
    conv_ref[...] = tail
    ubuf[SUBLANES - 2:SUBLANES, :] = tail

    q0 = 3 * aw
    q = _dot(h, win_ref[:, q0:q0 + nq]) * (HEAD_DIM ** -0.5)
    lane = lax.broadcasted_iota(jnp.int32, (1, nkv), 1)
    lo = lane < HEAD_DIM
    nblk = ts // WINDOW
    for n in range(nblk):
        for g in range(GROUP):
            qg = q[n * WINDOW:(n + 1) * WINDOW, g * nkv:(g + 1) * nkv]
            qbuf[n, g * WINDOW:(g + 1) * WINDOW, :] = jnp.where(lo, qg, 0.0).astype(BF16)
            qbuf[n, (GROUP + g) * WINDOW:(GROUP + g + 1) * WINDOW, :] = jnp.where(lo, 0.0, qg).astype(BF16)
    kv_new = _dot(h, win_ref[:, q0 + nq:q0 + nq + 2 * nkv])
    k = kv_new[:, 0:nkv]
    v = kv_new[:, nkv:2 * nkv]
    kbuf[WINDOW:WINDOW + ts, :] = k.astype(BF16)
    vbuf[WINDOW:WINDOW + ts, 0:nkv] = v.astype(BF16)
    kout_ref[...] = k[ts - WINDOW:, :]
    vout_ref[...] = v[ts - WINDOW:, :]

    qi = lax.broadcasted_iota(jnp.int32, (WINDOW, 2 * WINDOW), 0)
    kj = lax.broadcasted_iota(jnp.int32, (WINDOW, 2 * WINDOW), 1)
    dist = qi + WINDOW - kj
    valid = (dist >= 0) & (dist < WINDOW)
    negdist = jnp.where(valid, -dist.astype(F32), 0.0)
    maskadd = jnp.where(valid, 0.0, NEG)
    nheads = KV_HEADS * GROUP
    for hidx in range(nheads):
        bias_s[hidx] = _head_slope(hidx) * negdist + maskadd
    first = jnp.where(kj < WINDOW, jnp.where(s == 0, NEG, 0.0), 0.0)

    for n in range(nblk):
        r0 = n * WINDOW
        sc_all = _dot_nt(qbuf[n], kbuf[r0:r0 + 2 * WINDOW, :])
        ps, sink_terms = [], []
        for hidx in range(nheads):
            sc = sc_all[hidx * WINDOW:(hidx + 1) * WINDOW, :] + bias_s[hidx]
            if n == 0:
                sc = sc + first
            sink = sink_ref[hidx]
            m = jnp.maximum(jnp.max(sc, axis=-1, keepdims=True), sink)
            ps.append(jnp.exp(sc - m).astype(BF16))
            sink_terms.append(jnp.exp(sink - m))
        pv = _dot(jnp.concatenate(ps, axis=0), vbuf[r0:r0 + 2 * WINDOW, :])
        outs = []
        for hidx in range(nheads):
            pv_h = pv[hidx * WINDOW:(hidx + 1) * WINDOW, :]
            outs.append(pv_h[:, 0:nkv] / (pv_h[:, nkv:2 * nkv] + sink_terms[hidx]))
        for g in range(GROUP):
            og = jnp.where(lo, outs[g], outs[GROUP + g])
            mixbuf[r0:r0 + WINDOW, aw + g * nkv:aw + (g + 1) * nkv] = og.astype(BF16)

    kbuf[0:WINDOW, :] = kbuf[ts:ts + WINDOW, :]
    vbuf[0:WINDOW, :] = vbuf[ts:ts + WINDOW, :]

    out = _dot(mixbuf[...], wout_ref[...])
    y_ref[...] = x + _rms(out, gpost_ref[...])


def _mix_ab_prompt(x, gpre, gpost, win, cw, sinks, wout):
    b, seq, d = x.shape
    ts = MIX_TILE
    assert seq % ts == 0 and ts % WINDOW == 0, (seq, ts)
    aw = cw.shape[1]
    nq = GROUP * KV_HEADS * HEAD_DIM
    nkv = KV_HEADS * HEAD_DIM
    tile = pl.BlockSpec((None, ts, d), lambda i, j: (i, j, 0))
    return pl.pallas_call(
        _mix_ab_prompt_kernel,
        grid=(b, seq // ts),
        in_specs=[tile, _resident(gpre.shape), _resident(gpost.shape), _resident(win.shape),
                  _resident(cw.shape), pl.BlockSpec(memory_space=pltpu.SMEM), _resident(wout.shape)],
        out_specs=[tile,
                   pl.BlockSpec((None, 2, aw), lambda i, j: (i, 0, 0)),
                   pl.BlockSpec((None, WINDOW, nkv), lambda i, j: (i, 0, 0)),
                   pl.BlockSpec((None, WINDOW, nkv), lambda i, j: (i, 0, 0))],
        out_shape=[jax.ShapeDtypeStruct((b, seq, d), F32),
                   jax.ShapeDtypeStruct((b, 2, aw), F32),
                   jax.ShapeDtypeStruct((b, WINDOW, nkv), F32),
                   jax.ShapeDtypeStruct((b, WINDOW, nkv), F32)],
        scratch_shapes=[pltpu.VMEM((ts + SUBLANES, aw), F32),
                        pltpu.VMEM((ts // WINDOW, KV_HEADS * GROUP * WINDOW, nkv), BF16),
                        pltpu.VMEM((ts + WINDOW, nkv), BF16),
                        pltpu.VMEM((ts + WINDOW, 2 * nkv), BF16),
                        pltpu.VMEM((ts, d), BF16),
                        pltpu.VMEM((KV_HEADS * GROUP, WINDOW, 2 * WINDOW), F32)],
        compiler_params=_params(("arbitrary", "arbitrary")),
        name="mix_ab_prompt",
    )(x, gpre, gpost, win, cw, sinks.reshape(-1), wout)


def _ffn_prompt_kernel(x_ref, gpre_ref, gpost_ref, wup_ref, cw_ref, wdown_ref,
                       y_ref, st_ref, abuf, carry, gbuf):
    s = pl.program_id(1)
    rows = x_ref.shape[0]
    ff = cw_ref.shape[1]

    @pl.when(s == 0)
    def _():
        carry[...] = jnp.zeros(carry.shape, F32)

    x = x_ref[...]
    h = _rms(x, gpre_ref[...]).astype(BF16)
    for c in range(ff // FFN_CHUNK):
        c0 = c * FFN_CHUNK
        buf = abuf.at[c % 2]
        a = _dot(h, wup_ref[:, c0:c0 + FFN_CHUNK])
        b = _dot(h, wup_ref[:, ff + c0:ff + c0 + FFN_CHUNK])
        buf[0:SUBLANES, :] = carry[:, c0:c0 + FFN_CHUNK]
        buf[SUBLANES:SUBLANES + rows, :] = a
        cw = cw_ref[:, c0:c0 + FFN_CHUNK]
        conv = (cw[2:3, :] * a + cw[1:2, :] * buf[SUBLANES - 1:SUBLANES - 1 + rows, :]
                + cw[0:1, :] * buf[SUBLANES - 2:SUBLANES - 2 + rows, :])
        tail = a[rows - 2:rows, :]
        st_ref[:, c0:c0 + FFN_CHUNK] = tail
        carry[SUBLANES - 2:SUBLANES, c0:c0 + FFN_CHUNK] = tail
        gbuf[:, c0:c0 + FFN_CHUNK] = (_gelu_tanh(conv) * b).astype(BF16)
    out = _dot(gbuf[...], wdown_ref[...])
    y_ref[...] = x + _rms(out, gpost_ref[...])


def _ffn_prompt(layer, x, gpre, gpost, wup, cw, wdown):
    b, seq, d = x.shape
    ts = FFN_TILE
    ff = cw.shape[1]
    assert seq % ts == 0 and ff % FFN_CHUNK == 0, (seq, ts, ff)
    tile = pl.BlockSpec((None, ts, d), lambda i, j: (i, j, 0))
    layer_block = lambda w: pl.BlockSpec((None,) + w.shape[1:], lambda i, j: (layer, 0, 0),
                                         pipeline_mode=pl.Buffered(1))
    return pl.pallas_call(
        _ffn_prompt_kernel,
        grid=(b, seq // ts),
        in_specs=[tile, _resident(gpre.shape), _resident(gpost.shape), layer_block(wup),
                  _resident(cw.shape), layer_block(wdown)],
        out_specs=[tile, pl.BlockSpec((None, 2, ff), lambda i, j: (i, 0, 0))],
        out_shape=[jax.ShapeDtypeStruct((b, seq, d), F32), jax.ShapeDtypeStruct((b, 2, ff), F32)],
        scratch_shapes=[pltpu.VMEM((2, ts + SUBLANES, FFN_CHUNK), F32), pltpu.VMEM((SUBLANES, ff), F32),
                        pltpu.VMEM((ts, ff), BF16)],
        compiler_params=_params(("arbitrary", "arbitrary")),
        name="ffn_prompt",
    )(x, gpre, gpost, wup, cw, wdown)


def _segment_conv_taps(a, t, seg, prev0_ref, prev1_ref, abuf, fix1, fix2):
    rows, c = a.shape
    nseq = rows // seg
    am1, am2 = [], []
    for k in range(c // LANES):
        sl = slice(k * LANES, (k + 1) * LANES)
        abuf[k, SUBLANES:SUBLANES + rows, :] = a[:, sl]
        fix1[k, pl.ds(0, nseq, stride=seg), :] = prev1_ref[:, sl]
        fix2[k, pl.ds(0, nseq, stride=seg), :] = prev0_ref[:, sl]
        fix2[k, pl.ds(1, nseq, stride=seg), :] = prev1_ref[:, sl]
        am1.append(jnp.where(t >= 1, abuf[k, SUBLANES - 1:SUBLANES - 1 + rows, :], fix1[k]))
        am2.append(jnp.where(t >= 2, abuf[k, SUBLANES - 2:SUBLANES - 2 + rows, :], fix2[k]))
    return jnp.concatenate(am1, axis=1), jnp.concatenate(am2, axis=1)


def _segment_tails(abuf, nseq, seg):
    tail = lambda back: jnp.concatenate(
        [abuf[k, pl.ds(SUBLANES + seg - back, nseq, stride=seg), :] for k in range(abuf.shape[0])], axis=1)
    return tail(2), tail(1)


def _ffn_sample_kernel(seg, x_ref, prev0_ref, prev1_ref, gpre_ref, gpost_ref, wa_ref, wb_ref, cw_ref, wdown_ref,
                       y_ref, st0_ref, st1_ref, hbuf, abuf, fix1, fix2, acc):
    c = pl.program_id(0)
    rows = x_ref.shape[0]

    @pl.when(c == 0)
    def _():
        hbuf[...] = _rms(x_ref[...], gpre_ref[...]).astype(BF16)
        acc[...] = jnp.zeros(acc.shape, F32)
        abuf[:, 0:SUBLANES, :] = jnp.zeros((abuf.shape[0], SUBLANES, LANES), F32)
        fix1[...] = jnp.zeros(fix1.shape, F32)
        fix2[...] = jnp.zeros(fix2.shape, F32)

    h = hbuf[...]
    t = lax.broadcasted_iota(jnp.int32, (rows, 1), 0) % seg
    a = _dot(h, wa_ref[...])
    b = _dot(h, wb_ref[...])
    am1, am2 = _segment_conv_taps(a, t, seg, prev0_ref, prev1_ref, abuf, fix1, fix2)
    st0_ref[...], st1_ref[...] = _segment_tails(abuf, rows // seg, seg)
    cw = cw_ref[...]
    conv = cw[2:3, :] * a + cw[1:2, :] * am1 + cw[0:1, :] * am2
    acc[...] += _dot((_gelu_tanh(conv) * b).astype(BF16), wdown_ref[...])

    @pl.when(c == pl.num_programs(0) - 1)
    def _():
        y_ref[...] = x_ref[...] + _rms(acc[...], gpost_ref[...])


def _ffn_sample(layer, x, prev, seg, gpre, gpost, wup, cw, wdown):
    rows, d = x.shape
    nseq = rows // seg
    ff = cw.shape[1]
    cf = FFN_CHUNK
    nc = ff // cf
    cols = lambda r: pl.BlockSpec((r, cf), lambda c: (0, c))
    y, st0, st1 = pl.pallas_call(
        functools.partial(_ffn_sample_kernel, seg),
        grid=(nc,),
        in_specs=[_resident(x.shape), cols(nseq), cols(nseq), _resident(gpre.shape), _resident(gpost.shape),
                  pl.BlockSpec((None, d, cf), lambda c: (layer, 0, c)),
                  pl.BlockSpec((None, d, cf), lambda c: (layer, 0, c + nc)), cols(3),
                  pl.BlockSpec((None, cf, d), lambda c: (layer, c, 0))],
        out_specs=[pl.BlockSpec((rows, d), lambda c: (0, 0)), cols(nseq), cols(nseq)],
        out_shape=[jax.ShapeDtypeStruct((rows, d), F32), jax.ShapeDtypeStruct((nseq, ff), F32),
                   jax.ShapeDtypeStruct((nseq, ff), F32)],
        scratch_shapes=[pltpu.VMEM((rows, d), BF16), pltpu.VMEM((cf // LANES, rows + SUBLANES, LANES), F32),
                        pltpu.VMEM((cf // LANES, rows, LANES), F32), pltpu.VMEM((cf // LANES, rows, LANES), F32),
                        pltpu.VMEM((rows, d), F32)],
        compiler_params=_params(("arbitrary",)),
        name="ffn_sample",
    )(x, prev[:, 0], prev[:, 1], gpre, gpost, wup, wup, cw, wdown)
    return y, jnp.stack([st0, st1], axis=1)


LOG2E = 1.4426950408889634


def _lower_bound_terms(logits_ref, layer):
    lg = logits_ref[...]
    e = jnp.exp(lg - jnp.max(lg, axis=0, keepdims=True))
    pr = e / jnp.sum(e, axis=0, keepdims=True)
    cum = pr[0:1, :]
    for r in range(1, layer + 1):
        cum = cum + pr[r:r + 1, :]
    lb = jnp.clip(cum - pr[0:1, :], 0.0, 1.0)
    return jnp.log(lb) * LOG2E, jnp.log1p(-lb) * LOG2E, 1.0 - lb


def _forget_gate(f, lb_terms):
    log2_lb, log2_1m_lb, one_m_lb = lb_terms
    f2 = f * LOG2E
    e = jnp.exp2(jnp.minimum(f2, -f2))
    w = 1.0 + e
    b = log2_1m_lb + (jnp.minimum(f2, 0.0) - jnp.log2(w))
    d = log2_lb - b
    log2_ft = jnp.maximum(log2_lb, b) + jnp.log2(1.0 + jnp.exp2(jnp.minimum(d, -d)))
    return log2_ft, one_m_lb * (jnp.where(f2 >= 0.0, e, 1.0) / w)


def _split_bf16(x):
    hi = x.astype(BF16)
    return hi, (x - hi.astype(F32)).astype(BF16)


def _hgrn_gate_out(x, h, ob, mixbuf, win_ref, ng_ref, wout_ref, gpost_ref):
    dkt = C_HEADS * C_DK
    g = _dot(h, win_ref[:, 3 * dkt:4 * dkt])
    ng = ng_ref[...]
    for hh in range(C_HEADS):
        sl = slice(hh * C_DK, (hh + 1) * C_DK)
        mixbuf[:, sl] = (_rms(ob[:, sl], ng) * (g[:, sl] * _sigmoid(g[:, sl]))).astype(BF16)
    out = _dot(mixbuf[...], wout_ref[...])
    return x + _rms(out, gpost_ref[...])


def _hgrn_exact_tile(q, k_in, log2_ft, vb, st, ob, sq, sk, slg):
    ts = sq.shape[0]
    sq[...] = q
    sk[...] = k_in
    slg[...] = log2_ft
    ob[...] = jnp.zeros(ob.shape, F32)
    grp = 2 * SUBLANES
    row = lax.broadcasted_iota(jnp.int32, (grp, 1), 0)

    def body(t, carry):
        t0 = pl.multiple_of(lax.shift_left(lax.shift_right_logical(t, 4), 4), grp)
        sel = row == lax.bitwise_and(t, grp - 1)
        for hh in range(C_HEADS):
            sl = slice(hh * C_DK, (hh + 1) * C_DK)
            rows_of = lambda ref: jnp.where(sel, ref[pl.ds(t0, grp), sl].astype(F32), 0.0)
            pick = lambda ref: rows_of(ref).astype(BF16)
            log2_f = jnp.sum(rows_of(slg), axis=0, keepdims=True)
            s_new = st[hh] * jnp.exp2(log2_f) + _dot_tn(pick(vb), pick(sk))
            st[hh] = s_new
            ob[pl.ds(t0, grp), sl] += _dot_nt(pick(sq), s_new.astype(BF16))
        return carry

    lax.fori_loop(0, ts, body, 0)


def _hgrn_prompt_kernel(layer, x_ref, gpre_ref, gpost_ref, win_ref, lbl_ref, ng_ref, wout_ref,
                        y_ref, sout_ref, st, st0, qtb, ktb, qib, klb, eglb, vb, ob, mixbuf, sq, sk, slg):
    s = pl.program_id(1)
    ts = x_ref.shape[0]
    dkt = C_HEADS * C_DK
    ck = HGRN_CHUNK

    @pl.when(s == 0)
    def _():
        st[...] = jnp.zeros(st.shape, F32)

    x = x_ref[...]
    h = _rms(x, gpre_ref[...]).astype(BF16)
    f = _dot(h, win_ref[:, dkt:2 * dkt])
    log2_ft, k_in = _forget_gate(f, _lower_bound_terms(lbl_ref, layer))
    q = _dot(h, win_ref[:, 0:dkt])
    vb[...] = _dot(h, win_ref[:, 2 * dkt:3 * dkt]).astype(BF16)

    ri = lax.broadcasted_iota(jnp.int32, (ck, ck), 0)
    ci = lax.broadcasted_iota(jnp.int32, (ck, ck), 1)
    causal = ri >= ci
    tri = jnp.where(causal, 1.0, 0.0).astype(BF16)
    mid = ck // 2 - 1
    nck = ts // ck

    spread = jnp.zeros((1, dkt), F32)
    for c in range(nck):
        rows = slice(c * ck, (c + 1) * ck)
        hi, lo = _split_bf16(log2_ft[rows, :])
        gcum = _dot(tri, hi) + _dot(tri, lo)
        gr = gcum[mid:mid + 1, :]
        gl = gcum[ck - 1:ck, :]
        spread = jnp.maximum(spread, jnp.maximum(gcum[0:1, :] - gr, gr - gl))
        qc = q[rows, :]
        kc = k_in[rows, :]
        qtb[rows, :] = (qc * jnp.exp2(gcum - gr)).astype(BF16)
        ktb[rows, :] = (kc * jnp.exp2(gr - gcum)).astype(BF16)
        qib[rows, :] = (qc * jnp.exp2(gcum)).astype(BF16)
        klb[rows, :] = (kc * jnp.exp2(gl - gcum)).astype(BF16)
        eglb[c] = jnp.exp2(gl)

    st0[...] = st[...]
    for c in range(nck):
        rows = slice(c * ck, (c + 1) * ck)
        for hh in range(C_HEADS):
            sl = slice(hh * C_DK, (hh + 1) * C_DK)
            vh = vb[rows, sl]
            att = jnp.where(causal, _dot_nt(qtb[rows, sl], ktb[rows, sl]), 0.0).astype(BF16)
            st_h = st[hh]
            ob[rows, sl] = _dot(jnp.concatenate([qib[rows, sl], att], axis=1),
                                jnp.concatenate([st_h.astype(BF16).T, vh], axis=0))
            st[hh] = eglb[c, :, sl] * st_h + _dot_tn(vh, klb[rows, sl])

    @pl.when(jnp.max(spread) > DECAY_LOG2_LIMIT)
    def _():
        st[...] = st0[...]
        _hgrn_exact_tile(q, k_in, log2_ft, vb, st, ob, sq, sk, slg)

    y_ref[...] = _hgrn_gate_out(x, h, ob, mixbuf, win_ref, ng_ref, wout_ref, gpost_ref)

    @pl.when(s == pl.num_programs(1) - 1)
    def _():
        for hh in range(C_HEADS):
            sout_ref[hh] = st[hh].T


def _hgrn_prompt(layer, x, gpre, gpost, win, lbl, ng, wout):
    b, seq, d = x.shape
    ts = HGRN_TILE
    assert seq % ts == 0 and ts % HGRN_CHUNK == 0, (seq, ts)
    dkt = C_HEADS * C_DK
    tile = pl.BlockSpec((None, ts, d), lambda i, j: (i, j, 0))
    return pl.pallas_call(
        functools.partial(_hgrn_prompt_kernel, layer),
        grid=(b, seq // ts),
        in_specs=[tile, _resident(gpre.shape), _resident(gpost.shape), _resident(win.shape),
                  _resident(lbl.shape), _resident(ng.shape), _resident(wout.shape)],
        out_specs=[tile, pl.BlockSpec((None, C_HEADS, C_DK, C_DK), lambda i, j: (i, 0, 0, 0))],
        out_shape=[jax.ShapeDtypeStruct((b, seq, d), F32),
                   jax.ShapeDtypeStruct((b, C_HEADS, C_DK, C_DK), F32)],
        scratch_shapes=[pltpu.VMEM((C_HEADS, C_DK, C_DK), F32), pltpu.VMEM((C_HEADS, C_DK, C_DK), F32),
                        pltpu.VMEM((ts, dkt), BF16), pltpu.VMEM((ts, dkt), BF16),
                        pltpu.VMEM((ts, dkt), BF16), pltpu.VMEM((ts, dkt), BF16),
                        pltpu.VMEM((ts // HGRN_CHUNK, 1, dkt), F32),
                        pltpu.VMEM((ts, dkt), BF16), pltpu.VMEM((ts, dkt), F32),
                        pltpu.VMEM((ts, dkt), BF16),
                        pltpu.VMEM((ts, dkt), F32), pltpu.VMEM((ts, dkt), F32), pltpu.VMEM((ts, dkt), F32)],
        compiler_params=_params(("arbitrary", "arbitrary")),
        name="hgrn_prompt",
    )(x, gpre, gpost, win, lbl, ng, wout)


def _mix_ab_sample_kernel(seg, x_ref, prev0_ref, prev1_ref, kc_ref, vc_ref, gpre_ref, gpost_ref, win_ref,
                          cw_ref, sink_ref, wout_ref,
                          y_ref, conv0_ref, conv1_ref, kout_ref, vout_ref,
                          ubuf, fix1, fix2, qbuf, knew_s, vnew_s, yabuf, ybbuf, kall, vall, bias_s, sink_s):
    i = pl.program_id(0)
    rows = x_ref.shape[0]
    aw = cw_ref.shape[1]
    nq = GROUP * KV_HEADS * HEAD_DIM
    nkv = KV_HEADS * HEAD_DIM
    pair = 2 * seg
    qrows = KV_HEADS * GROUP * pair
    tk = WINDOW + pair

    @pl.when(i == 0)
    def _():
        x = x_ref[...]
        h = _rms(x, gpre_ref[...]).astype(BF16)
        t = lax.broadcasted_iota(jnp.int32, (rows, 1), 0) % seg
        u = _dot(h, win_ref[:, 2 * aw:3 * aw]) * _dot(h, win_ref[:, 0:aw])
        ubuf[:, 0:SUBLANES, :] = jnp.zeros((ubuf.shape[0], SUBLANES, LANES), F32)
        fix1[...] = jnp.zeros(fix1.shape, F32)
        fix2[...] = jnp.zeros(fix2.shape, F32)
        um1, um2 = _segment_conv_taps(u, t, seg, prev0_ref, prev1_ref, ubuf, fix1, fix2)
        conv0_ref[...], conv1_ref[...] = _segment_tails(ubuf, rows // seg, seg)
        cw = cw_ref[...]
        conv = cw[2:3, :] * u + cw[1:2, :] * um1 + cw[0:1, :] * um2
        yabuf[...] = (_dot(h, win_ref[:, aw:2 * aw]) * conv).astype(BF16)
        q0 = 3 * aw
        qbuf[...] = _dot(h, win_ref[:, q0:q0 + nq]) * (HEAD_DIM ** -0.5)
        kv_new = _dot(h, win_ref[:, q0 + nq:q0 + nq + 2 * nkv])
        knew_s[...] = kv_new[:, 0:nkv]
        vnew_s[...] = kv_new[:, nkv:2 * nkv]

        r = lax.broadcasted_iota(jnp.int32, (qrows, tk), 0)
        j = lax.broadcasted_iota(jnp.int32, (qrows, tk), 1)
        dist = r % seg + WINDOW - j
        valid = (dist >= 0) & (dist < WINDOW) & (j < WINDOW + seg)
        rc = lax.broadcasted_iota(jnp.int32, (qrows, 1), 0)
        head = (rc // (GROUP * pair)) * GROUP + (rc // pair) % GROUP
        slope = jnp.zeros((qrows, 1), F32)
        sink = jnp.zeros((qrows, 1), F32)
        for hidx in range(KV_HEADS * GROUP):
            slope = jnp.where(head == hidx, _head_slope(hidx), slope)
            sink = jnp.where(head == hidx, sink_ref[:, hidx:hidx + 1], sink)
        bias_s[...] = jnp.where(valid, -slope * dist.astype(F32), NEG)
        sink_s[...] = sink

    nblk = kc_ref.shape[0]
    lane = lax.broadcasted_iota(jnp.int32, (1, nkv), 1)
    lo = lane < HEAD_DIM
    rq = lax.broadcasted_iota(jnp.int32, (qrows, 1), 0)
    first_seq = (rq // seg) % 2 == 0

    def body(p, carry):
        r0 = pl.multiple_of((i * nblk + 2 * p) * seg, pair)
        q8 = qbuf[pl.ds(r0, pair), :]
        qg = jnp.concatenate([q8[:, g * nkv:(g + 1) * nkv] for g in range(GROUP)], axis=0)
        qs = jnp.concatenate([jnp.where(lo, qg, 0.0), jnp.where(lo, 0.0, qg)], axis=0).astype(BF16)
        knew = knew_s[pl.ds(r0, pair), :]
        vnew = vnew_s[pl.ds(r0, pair), :]
        sc, ov = [], []
        for a in range(2):
            own_first = (lambda z: z) if a == 0 else (lambda z: jnp.concatenate([z[seg:], z[:seg]], axis=0))
            kall[a, 0:WINDOW, :] = kc_ref[2 * p + a]
            kall[a, WINDOW:tk, :] = own_first(knew)
            vall[a, 0:WINDOW, :] = vc_ref[2 * p + a]
            vall[a, WINDOW:tk, :] = own_first(vnew)
            kout_ref[2 * p + a] = kall[a, seg:seg + WINDOW, :]
            vout_ref[2 * p + a] = vall[a, seg:seg + WINDOW, :]
            sc.append(_dot_nt(qs, kall[a].astype(BF16)))
        s2 = jnp.where(first_seq, sc[0], sc[1]) + bias_s[...]
        sink = sink_s[...]
        m = jnp.maximum(jnp.max(s2, axis=-1, keepdims=True), sink)
        pr = jnp.exp(s2 - m)
        den = jnp.sum(pr, axis=-1, keepdims=True) + jnp.exp(sink - m)
        prb = pr.astype(BF16)
        for a in range(2):
            ov.append(_dot(prb, vall[a].astype(BF16)))
        o = jnp.where(first_seq, ov[0], ov[1]) / den
        og = jnp.where(lo, o[0:GROUP * pair, :], o[GROUP * pair:, :])
        for g in range(GROUP):
            ybbuf[pl.ds(r0, pair), g * nkv:(g + 1) * nkv] = og[g * pair:(g + 1) * pair, :]
        return carry

    lax.fori_loop(0, nblk // 2, body, 0)

    @pl.when(i == pl.num_programs(0) - 1)
    def _():
        out = _dot(yabuf[...], wout_ref[0:aw, :]) + _dot(ybbuf[...].astype(BF16), wout_ref[aw:, :])
        y_ref[...] = x_ref[...] + _rms(out, gpost_ref[...])


def _mix_ab_sample(x, conv_prev, kcache, vcache, seg, gpre, gpost, win, cw, sinks, wout):
    rows, d = x.shape
    nseq = rows // seg
    aw = cw.shape[1]
    nq = GROUP * KV_HEADS * HEAD_DIM
    nkv = KV_HEADS * HEAD_DIM
    sb = SAMPLE_SEQ_BLOCK
    pair = 2 * seg
    qrows = KV_HEADS * GROUP * pair
    tk = WINDOW + pair
    full = lambda shape: pl.BlockSpec(shape, lambda i: (0,) * len(shape))
    cache = pl.BlockSpec((sb, WINDOW, nkv), lambda i: (i, 0, 0))
    prev0, prev1 = conv_prev[:, 0], conv_prev[:, 1]
    y, conv0, conv1, k_new, v_new = pl.pallas_call(
        functools.partial(_mix_ab_sample_kernel, seg),
        grid=(nseq // sb,),
        in_specs=[_resident(x.shape), _resident(prev0.shape), _resident(prev1.shape), cache, cache,
                  _resident(gpre.shape), _resident(gpost.shape), _resident(win.shape), _resident(cw.shape),
                  _resident(sinks.shape), _resident(wout.shape)],
        out_specs=[full((rows, d)), full((nseq, aw)), full((nseq, aw)), cache, cache],
        out_shape=[jax.ShapeDtypeStruct((rows, d), F32), jax.ShapeDtypeStruct((nseq, aw), F32),
                   jax.ShapeDtypeStruct((nseq, aw), F32), jax.ShapeDtypeStruct(kcache.shape, F32),
                   jax.ShapeDtypeStruct(vcache.shape, F32)],
        scratch_shapes=[pltpu.VMEM((aw // LANES, rows + SUBLANES, LANES), F32),
                        pltpu.VMEM((aw // LANES, rows, LANES), F32),
                        pltpu.VMEM((aw // LANES, rows, LANES), F32),
                        pltpu.VMEM((rows, nq), F32),
                        pltpu.VMEM((rows, nkv), F32),
                        pltpu.VMEM((rows, nkv), F32),
                        pltpu.VMEM((rows, aw), BF16),
                        pltpu.VMEM((rows, nq), F32),
                        pltpu.VMEM((2, tk, nkv), F32),
                        pltpu.VMEM((2, tk, nkv), F32),
                        pltpu.VMEM((qrows, tk), F32),
                        pltpu.VMEM((qrows, 1), F32)],
        compiler_params=_params(("arbitrary",)),
        name="mix_ab_sample",
    )(x, prev0, prev1, kcache, vcache, gpre, gpost, win, cw, sinks, wout)
    return y, jnp.stack([conv0, conv1], axis=1), k_new, v_new


def _hgrn_sample_kernel(layer, seg, x_ref, s0_ref, gpre_ref, gpost_ref, win_ref, lbl_ref, ng_ref, wout_ref,
                        y_ref, s1_ref, qb, kb, vb, gb, lb2, ob, mixbuf):
    i = pl.program_id(0)
    rows = x_ref.shape[0]
    dkt = C_HEADS * C_DK
    pair = 2 * seg

    @pl.when(i == 0)
    def _():
        x = x_ref[...]
        h = _rms(x, gpre_ref[...]).astype(BF16)
        f = _dot(h, win_ref[:, dkt:2 * dkt])
        log2_ft, kb[...] = _forget_gate(f, _lower_bound_terms(lbl_ref, layer))
        lb2[...] = log2_ft
        hi, lo = _split_bf16(log2_ft)
        ri = lax.broadcasted_iota(jnp.int32, (rows, rows), 0)
        ci = lax.broadcasted_iota(jnp.int32, (rows, rows), 1)
        tri = jnp.where((ri // seg == ci // seg) & (ri >= ci), 1.0, 0.0).astype(BF16)
        gb[...] = _dot(tri, hi) + _dot(tri, lo)
        qb[...] = _dot(h, win_ref[:, 0:dkt])
        vb[...] = _dot(h, win_ref[:, 2 * dkt:3 * dkt])

    nblk = s0_ref.shape[0]
    rr = lax.broadcasted_iota(jnp.int32, (pair, 1), 0)
    first_seq = rr < seg
    second_seq = rr >= seg
    ri = lax.broadcasted_iota(jnp.int32, (pair, pair), 0)
    ci = lax.broadcasted_iota(jnp.int32, (pair, pair), 1)
    same = ri // seg == ci // seg
    tq = ri % seg
    pair_diag = ri == ci
    pair_step = same & (tq % 2 == 1) & (ci == ri - 1)
    pair_half = same & (tq >= seg // 2) & (ci % seg < seg // 2)
    mid = seg // 2 - 1

    def body(p, carry):
        r0 = pl.multiple_of((i * nblk + 2 * p) * seg, pair)
        for hh in range(C_HEADS):
            sl = slice(hh * C_DK, (hh + 1) * C_DK)
            gh = gb[pl.ds(r0, pair), sl]
            gr = jnp.where(first_seq, gh[mid:mid + 1, :], gh[seg + mid:seg + mid + 1, :])
            gl = jnp.where(first_seq, gh[seg - 1:seg, :], gh[pair - 1:pair, :])
            qh = qb[pl.ds(r0, pair), sl]
            kh = kb[pl.ds(r0, pair), sl]
            vh = vb[pl.ds(r0, pair), sl].astype(BF16)
            qt = (qh * jnp.exp2(gh - gr)).astype(BF16)
            kt = (kh * jnp.exp2(gr - gh)).astype(BF16)
            q_step = (qh * jnp.exp2(lb2[pl.ds(r0, pair), sl])).astype(BF16)
            near = _dot_nt(jnp.concatenate([qh.astype(BF16), q_step], axis=0), kh.astype(BF16))
            att = (jnp.where(pair_diag, near[0:pair, :], 0.0) + jnp.where(pair_step, near[pair:, :], 0.0)
                   + jnp.where(pair_half, _dot_nt(qt, kt), 0.0)).astype(BF16)
            qi = (qh * jnp.exp2(gh)).astype(BF16)
            kl = kh * jnp.exp2(gl - gh)
            o_prev = []
            for a in range(2):
                s_a = s0_ref[2 * p + a, hh]
                o_prev.append(_dot(qi, s_a.astype(BF16)))
                kl_a = jnp.where(first_seq if a == 0 else second_seq, kl, 0.0).astype(BF16)
                e = jnp.exp2(gh[(a + 1) * seg - 1:(a + 1) * seg, :])
                e1 = e.astype(BF16).astype(F32)
                e2 = (e - e1).astype(BF16).astype(F32)
                e3 = (e - e1 - e2).astype(BF16).astype(F32)
                dec_l = jnp.where(rr == 0, e1, jnp.where(rr == 1, e2, jnp.where(rr == 2, e3, 0.0)))
                dec = _dot_tn(dec_l.astype(BF16), jnp.ones((pair, C_DK), BF16))
                s1_ref[2 * p + a, hh] = dec * s_a + _dot_tn(kl_a, vh)
            ob[pl.ds(r0, pair), sl] = _dot(att, vh) + jnp.where(first_seq, o_prev[0], o_prev[1])
        return carry

    lax.fori_loop(0, nblk // 2, body, 0)

    @pl.when(i == pl.num_programs(0) - 1)
    def _():
        x = x_ref[...]
        h = _rms(x, gpre_ref[...]).astype(BF16)
        y_ref[...] = _hgrn_gate_out(x, h, ob, mixbuf, win_ref, ng_ref, wout_ref, gpost_ref)


def _hgrn_sample(layer, x, s0, seg, gpre, gpost, win, lbl, ng, wout):
    rows, d = x.shape
    assert seg == 4, "the in-block pair split of _hgrn_sample_kernel is written for 4 tokens per sequence"
    nseq = rows // seg
    dkt = C_HEADS * C_DK
    sb = SAMPLE_SEQ_BLOCK
    full = lambda shape: pl.BlockSpec(shape, lambda i: (0,) * len(shape))
    state = pl.BlockSpec((sb, C_HEADS, C_DK, C_DK), lambda i: (i, 0, 0, 0))
    return pl.pallas_call(
        functools.partial(_hgrn_sample_kernel, layer, seg),
        grid=(nseq // sb,),
        in_specs=[_resident(x.shape), state, _resident(gpre.shape), _resident(gpost.shape), _resident(win.shape),
                  _resident(lbl.shape), _resident(ng.shape), _resident(wout.shape)],
        out_specs=[full((rows, d)), state],
        out_shape=[jax.ShapeDtypeStruct((rows, d), F32), jax.ShapeDtypeStruct(s0.shape, F32)],
        scratch_shapes=[pltpu.VMEM((rows, dkt), F32), pltpu.VMEM((rows, dkt), F32),
                        pltpu.VMEM((rows, dkt), F32), pltpu.VMEM((rows, dkt), F32),
                        pltpu.VMEM((rows, dkt), F32),
                        pltpu.VMEM((rows, dkt), F32), pltpu.VMEM((rows, dkt), BF16)],
        compiler_params=_params(("arbitrary",)),
        name="hgrn_sample",
    )(x, s0, gpre, gpost, win, lbl, ng, wout)


def _head_permutation():
    cols = []
    for g in range(GROUP):
        for kv in range(KV_HEADS):
            hidx = kv * GROUP + g
            cols.extend(range(hidx * HEAD_DIM, (hidx + 1) * HEAD_DIM))
    return jnp.asarray(cols, dtype=jnp.int32)


def kernel(x_prompt, x_sample, state_conv_a, cache_swa_k, cache_swa_v, state_hgrn, state_ffn_conv, norm_mix_pre, norm_mix_post, norm_ffn_pre, norm_ffn_post, w_in_ab, conv_a_w, attn_sinks, w_out_ab, w_in_c, hgrn_lb_logits, hgrn_norm_g, w_out_c, w_ffn_up, ffn_conv_w, w_ffn_down):
    bp, seq, d = x_prompt.shape
    ns, seg, _ = x_sample.shape
    aw = conv_a_w.shape[-1]
    nq = GROUP * KV_HEADS * HEAD_DIM
    nkv = KV_HEADS * HEAD_DIM

    perm = _head_permutation()
    q0 = 3 * aw
    win_ab = w_in_ab[0]
    win_ab = jnp.concatenate([win_ab[:, :q0], win_ab[:, q0:q0 + nq][:, perm], win_ab[:, q0 + nq:]], axis=1).astype(BF16)
    wout_ab = w_out_ab[0]
    wout_ab = jnp.concatenate([wout_ab[:aw], wout_ab[aw:][perm]], axis=0).astype(BF16)
    win_c = w_in_c[0].astype(BF16)
    wout_c = w_out_c[0].astype(BF16)
    wup = w_ffn_up.astype(BF16)
    wdown = w_ffn_down.astype(BF16)
    row = lambda a: a.reshape(1, -1)
    sinks = row(attn_sinks[0])
    ng = row(hgrn_norm_g[0])

    x1, conv_p, k_p, v_p = _mix_ab_prompt(x_prompt, row(norm_mix_pre[0]), row(norm_mix_post[0]), win_ab,
                                          conv_a_w[0], sinks, wout_ab)
    x2, ffn_p0 = _ffn_prompt(0, x1, row(norm_ffn_pre[0]), row(norm_ffn_post[0]), wup, ffn_conv_w[0], wdown)
    x3, hgrn_p = _hgrn_prompt(1, x2, row(norm_mix_pre[1]), row(norm_mix_post[1]), win_c, hgrn_lb_logits,
                              ng, wout_c)
    y_prompt, ffn_p1 = _ffn_prompt(1, x3, row(norm_ffn_pre[1]), row(norm_ffn_post[1]), wup, ffn_conv_w[1], wdown)

    xs = x_sample.reshape(ns * seg, d)
    kc = cache_swa_k[0].reshape(ns, WINDOW, nkv)
    vc = cache_swa_v[0].reshape(ns, WINDOW, nkv)
    s1, conv_s, k_s, v_s = _mix_ab_sample(xs, state_conv_a[0], kc, vc, seg, row(norm_mix_pre[0]),
                                          row(norm_mix_post[0]), win_ab, conv_a_w[0], sinks, wout_ab)
    s2, ffn_s0 = _ffn_sample(0, s1, state_ffn_conv[0], seg, row(norm_ffn_pre[0]), row(norm_ffn_post[0]),
                             wup, ffn_conv_w[0], wdown)
    s3, hgrn_s = _hgrn_sample(1, s2, state_hgrn[0], seg, row(norm_mix_pre[1]), row(norm_mix_post[1]),
                              win_c, hgrn_lb_logits, ng, wout_c)
    y_sample, ffn_s1 = _ffn_sample(1, s3, state_ffn_conv[1], seg, row(norm_ffn_pre[1]), row(norm_ffn_post[1]),
                                   wup, ffn_conv_w[1], wdown)

    kvshape = (KV_HEADS, HEAD_DIM)
    return (y_prompt, y_sample.reshape(ns, seg, d),
            conv_p[None], conv_s[None],
            k_p.reshape(1, bp, WINDOW, *kvshape), k_s.reshape(1, ns, WINDOW, *kvshape),
            v_p.reshape(1, bp, WINDOW, *kvshape), v_s.reshape(1, ns, WINDOW, *kvshape),
            hgrn_p[None], hgrn_s[None],
            jnp.stack([ffn_p0, ffn_p1]), jnp.stack([ffn_s0, ffn_s1]))
```

```python
import functools
import math

import jax
import jax.numpy as jnp
from jax import lax
from jax.experimental import pallas as pl
from jax.experimental.pallas import tpu as pltpu

F32 = jnp.float32
BF16 = jnp.bfloat16

EPS = 1e-6
WINDOW = 128
KV_HEADS = 2
GROUP = 4
HEAD_DIM = 64
C_HEADS = 8
C_DK = 128
NEG = -1e30
SUBLANES = 8
LANES = 128
MIX_TILE = 1024
HGRN_TILE = 512
FFN_TILE = 1024
HGRN_CHUNK = 64
DECAY_LOG2_LIMIT = 100.0
FFN_CHUNK = 256
SAMPLE_SEQ_BLOCK = 8
VMEM_LIMIT_V7X = 56 * 2**20


def _dot(a, b):
    return jnp.dot(a, b, preferred_element_type=F32)


def _dot_nt(a, b):
    return lax.dot_general(a, b, (((1,), (1,)), ((), ())), preferred_element_type=F32)


def _dot_tn(a, b):
    return lax.dot_general(a, b, (((0,), (0,)), ((), ())), preferred_element_type=F32)


def _rms(x, g):
    return x * lax.rsqrt(jnp.mean(x * x, axis=-1, keepdims=True) + EPS) * g


def _gelu_tanh(x):
    c = math.sqrt(2.0 / math.pi)
    return x * (0.5 * (1.0 + jnp.tanh(c * (x + 0.044715 * (x * x * x)))))


def _sigmoid(x):
    return 1.0 / (1.0 + jnp.exp(-x))


def _resident(shape):
    nd = len(shape)
    return pl.BlockSpec(shape, lambda *_: (0,) * nd, pipeline_mode=pl.Buffered(1))


def _params(sem):
    return pltpu.CompilerParams(dimension_semantics=sem, vmem_limit_bytes=VMEM_LIMIT_V7X)


def _head_slope(h):
    return 2.0 ** (-8.0 * (h + 1) / (KV_HEADS * GROUP))


def _mix_ab_prompt_kernel(x_ref, gpre_ref, gpost_ref, win_ref, cw_ref, sink_ref, wout_ref,
                          y_ref, conv_ref, kout_ref, vout_ref,
                          ubuf, qbuf, kbuf, vbuf, mixbuf, bias_s):
    s = pl.program_id(1)
    ts = x_ref.shape[0]
    aw = cw_ref.shape[1]
    nq = GROUP * KV_HEADS * HEAD_DIM
    nkv = KV_HEADS * HEAD_DIM

    @pl.when(s == 0)
    def _():
        ubuf[0:SUBLANES, :] = jnp.zeros((SUBLANES, aw), F32)
        kbuf[0:WINDOW, :] = jnp.zeros((WINDOW, nkv), BF16)
        vbuf[0:WINDOW, 0:nkv] = jnp.zeros((WINDOW, nkv), BF16)
        vbuf[:, nkv:2 * nkv] = jnp.ones((ts + WINDOW, nkv), BF16)

    x = x_ref[...]
    h = _rms(x, gpre_ref[...]).astype(BF16)

    u = _dot(h, win_ref[:, 2 * aw:3 * aw]) * _dot(h, win_ref[:, 0:aw])
    ubuf[SUBLANES:SUBLANES + ts, :] = u
    cw = cw_ref[...]
    conv = (cw[2:3, :] * u + cw[1:2, :] * ubuf[SUBLANES - 1:SUBLANES - 1 + ts, :]
            + cw[0:1, :] * ubuf[SUBLANES - 2:SUBLANES - 2 + ts, :])
    mixbuf[:, 0:aw] = (_dot(h, win_ref[:, aw:2 * aw]) * conv).astype(BF16)
    tail = ubuf[ts + SUBLANES - 2:ts + SUBLANES, :]
    conv_ref[...] = tail
    ubuf[SUBLANES - 2:SUBLANES, :] = tail

    q0 = 3 * aw
    q = _dot(h, win_ref[:, q0:q0 + nq]) * (HEAD_DIM ** -0.5)
    lane = lax.broadcasted_iota(jnp.int32, (1, nkv), 1)
    lo = lane < HEAD_DIM
    nblk = ts // WINDOW
    for n in range(nblk):
        for g in range(GROUP):
            qg = q[n * WINDOW:(n + 1) * WINDOW, g * nkv:(g + 1) * nkv]
            qbuf[n, g * WINDOW:(g + 1) * WINDOW, :] = jnp.where(lo, qg, 0.0).astype(BF16)
            qbuf[n, (GROUP + g) * WINDOW:(GROUP + g + 1) * WINDOW, :] = jnp.where(lo, 0.0, qg).astype(BF16)
    kv_new = _dot(h, win_ref[:, q0 + nq:q0 + nq + 2 * nkv])
    k = kv_new[:, 0:nkv]
    v = kv_new[:, nkv:2 * nkv]
    kbuf[WINDOW:WINDOW + ts, :] = k.astype(BF16)
    vbuf[WINDOW:WINDOW + ts, 0:nkv] = v.astype(BF16)
    kout_ref[...] = k[ts - WINDOW:, :]
    vout_ref[...] = v[ts - WINDOW:, :]

    qi = lax.broadcasted_iota(jnp.int32, (WINDOW, 2 * WINDOW), 0)
    kj = lax.broadcasted_iota(jnp.int32, (WINDOW, 2 * WINDOW), 1)
    dist = qi + WINDOW - kj
    valid = (dist >= 0) & (dist < WINDOW)
    negdist = jnp.where(valid, -dist.astype(F32), 0.0)
    maskadd = jnp.where(valid, 0.0, NEG)
    nheads = KV_HEADS * GROUP
    for hidx in range(nheads):
        bias_s[hidx] = _head_slope(hidx) * negdist + maskadd
    first = jnp.where(kj < WINDOW, jnp.where(s == 0, NEG, 0.0), 0.0)

    for n in range(nblk):
        r0 = n * WINDOW
        sc_all = _dot_nt(qbuf[n], kbuf[r0:r0 + 2 * WINDOW, :])
        ps, sink_terms = [], []
        for hidx in range(nheads):
            sc = sc_all[hidx * WINDOW:(hidx + 1) * WINDOW, :] + bias_s[hidx]
            if n == 0:
                sc = sc + first
            sink = sink_ref[hidx]
            m = jnp.maximum(jnp.max(sc, axis=-1, keepdims=True), sink)
            ps.append(jnp.exp(sc - m).astype(BF16))
            sink_terms.append(jnp.exp(sink - m))
        pv = _dot(jnp.concatenate(ps, axis=0), vbuf[r0:r0 + 2 * WINDOW, :])
        outs = []
        for hidx in range(nheads):
            pv_h = pv[hidx * WINDOW:(hidx + 1) * WINDOW, :]
            outs.append(pv_h[:, 0:nkv] / (pv_h[:, nkv:2 * nkv] + sink_terms[hidx]))
        for g in range(GROUP):
            og = jnp.where(lo, outs[g], outs[GROUP + g])
            mixbuf[r0:r0 + WINDOW, aw + g * nkv:aw + (g + 1) * nkv] = og.astype(BF16)

    kbuf[0:WINDOW, :] = kbuf[ts:ts + WINDOW, :]
    vbuf[0:WINDOW, :] = vbuf[ts:ts + WINDOW, :]

    out = _dot(mixbuf[...], wout_ref[...])
    y_ref[...] = x + _rms(out, gpost_ref[...])


def _mix_ab_prompt(x, gpre, gpost, win, cw, sinks, wout):
    b, seq, d = x.shape
    ts = MIX_TILE
    assert seq % ts == 0 and ts % WINDOW == 0, (seq, ts)
    aw = cw.shape[1]
    nq = GROUP * KV_HEADS * HEAD_DIM
    nkv = KV_HEADS * HEAD_DIM
    tile = pl.BlockSpec((None, ts, d), lambda i, j: (i, j, 0))
    return pl.pallas_call(
        _mix_ab_prompt_kernel,
        grid=(b, seq // ts),
        in_specs=[tile, _resident(gpre.shape), _resident(gpost.shape), _resident(win.shape),
                  _resident(cw.shape), pl.BlockSpec(memory_space=pltpu.SMEM), _resident(wout.shape)],
        out_specs=[tile,
                   pl.BlockSpec((None, 2, aw), lambda i, j: (i, 0, 0)),
                   pl.BlockSpec((None, WINDOW, nkv), lambda i, j: (i, 0, 0)),
                   pl.BlockSpec((None, WINDOW, nkv), lambda i, j: (i, 0, 0))],
        out_shape=[jax.ShapeDtypeStruct((b, seq, d), F32),
                   jax.ShapeDtypeStruct((b, 2, aw), F32),
                   jax.ShapeDtypeStruct((b, WINDOW, nkv), F32),
                   jax.ShapeDtypeStruct((b, WINDOW, nkv), F32)],
        scratch_shapes=[pltpu.VMEM((ts + SUBLANES, aw), F32),
                        pltpu.VMEM((ts // WINDOW, KV_HEADS * GROUP * WINDOW, nkv), BF16),
                        pltpu.VMEM((ts + WINDOW, nkv), BF16),
                        pltpu.VMEM((ts + WINDOW, 2 * nkv), BF16),
                        pltpu.VMEM((ts, d), BF16),
                        pltpu.VMEM((KV_HEADS * GROUP, WINDOW, 2 * WINDOW), F32)],
        compiler_params=_params(("arbitrary", "arbitrary")),
        name="mix_ab_prompt",
    )(x, gpre, gpost, win, cw, sinks.reshape(-1), wout)


def _ffn_prompt_kernel(x_ref, gpre_ref, gpost_ref, wup_ref, cw_ref, wdown_ref,
                       y_ref, st_ref, abuf, carry, gbuf):
    s = pl.program_id(1)
    rows = x_ref.shape[0]
    ff = cw_ref.shape[1]

    @pl.when(s == 0)
    def _():
        carry[...] = jnp.zeros(carry.shape, F32)

    x = x_ref[...]
    h = _rms(x, gpre_ref[...]).astype(BF16)
    for c in range(ff // FFN_CHUNK):
        c0 = c * FFN_CHUNK
        buf = abuf.at[c % 2]
        a = _dot(h, wup_ref[:, c0:c0 + FFN_CHUNK])
        b = _dot(h, wup_ref[:, ff + c0:ff + c0 + FFN_CHUNK])
        buf[0:SUBLANES, :] = carry[:, c0:c0 + FFN_CHUNK]
        buf[SUBLANES:SUBLANES + rows, :] = a
        cw = cw_ref[:, c0:c0 + FFN_CHUNK]
        conv = (cw[2:3, :] * a + cw[1:2, :] * buf[SUBLANES - 1:SUBLANES - 1 + rows, :]
                + cw[0:1, :] * buf[SUBLANES - 2:SUBLANES - 2 + rows, :])
        tail = a[rows - 2:rows, :]
        st_ref[:, c0:c0 + FFN_CHUNK] = tail
        carry[SUBLANES - 2:SUBLANES, c0:c0 + FFN_CHUNK] = tail
        gbuf[:, c0:c0 + FFN_CHUNK] = (_gelu_tanh(conv) * b).astype(BF16)
    out = _dot(gbuf[...], wdown_ref[...])
    y_ref[...] = x + _rms(out, gpost_ref[...])


def _ffn_prompt(layer, x, gpre, gpost, wup, cw, wdown):
    b, seq, d = x.shape
    ts = FFN_TILE
    ff = cw.shape[1]
    assert seq % ts == 0 and ff % FFN_CHUNK == 0, (seq, ts, ff)
    tile = pl.BlockSpec((None, ts, d), lambda i, j: (i, j, 0))
    layer_block = lambda w: pl.BlockSpec((None,) + w.shape[1:], lambda i, j: (layer, 0, 0),
                                         pipeline_mode=pl.Buffered(1))
    return pl.pallas_call(
        _ffn_prompt_kernel,
        grid=(b, seq // ts),
        in_specs=[tile, _resident(gpre.shape), _resident(gpost.shape), layer_block(wup),
                  _resident(cw.shape), layer_block(wdown)],
        out_specs=[tile, pl.BlockSpec((None, 2, ff), lambda i, j: (i, 0, 0))],
        out_shape=[jax.ShapeDtypeStruct((b, seq, d), F32), jax.ShapeDtypeStruct((b, 2, ff), F32)],
        scratch_shapes=[pltpu.VMEM((2, ts + SUBLANES, FFN_CHUNK), F32), pltpu.VMEM((SUBLANES, ff), F32),
                        pltpu.VMEM((ts, ff), BF16)],
        compiler_params=_params(("arbitrary", "arbitrary")),
        name="ffn_prompt",
    )(x, gpre, gpost, wup, cw, wdown)


def _segment_conv_taps(a, t, seg, prev0_ref, prev1_ref, abuf, fix1, fix2):
    rows, c = a.shape
    nseq = rows // seg
    am1, am2 = [], []
    for k in range(c // LANES):
        sl = slice(k * LANES, (k + 1) * LANES)
        abuf[k, SUBLANES:SUBLANES + rows, :] = a[:, sl]
        fix1[k, pl.ds(0, nseq, stride=seg), :] = prev1_ref[:, sl]
        fix2[k, pl.ds(0, nseq, stride=seg), :] = prev0_ref[:, sl]
        fix2[k, pl.ds(1, nseq, stride=seg), :] = prev1_ref[:, sl]
        am1.append(jnp.where(t >= 1, abuf[k, SUBLANES - 1:SUBLANES - 1 + rows, :], fix1[k]))
        am2.append(jnp.where(t >= 2, abuf[k, SUBLANES - 2:SUBLANES - 2 + rows, :], fix2[k]))
    return jnp.concatenate(am1, axis=1), jnp.concatenate(am2, axis=1)


def _segment_tails(abuf, nseq, seg):
    tail = lambda back: jnp.concatenate(
        [abuf[k, pl.ds(SUBLANES + seg - back, nseq, stride=seg), :] for k in range(abuf.shape[0])], axis=1)
    return tail(2), tail(1)


def _ffn_sample_kernel(seg, x_ref, prev0_ref, prev1_ref, gpre_ref, gpost_ref, wa_ref, wb_ref, cw_ref, wdown_ref,
                       y_ref, st0_ref, st1_ref, hbuf, abuf, fix1, fix2, acc):
    c = pl.program_id(0)
    rows = x_ref.shape[0]

    @pl.when(c == 0)
    def _():
        hbuf[...] = _rms(x_ref[...], gpre_ref[...]).astype(BF16)
        acc[...] = jnp.zeros(acc.shape, F32)
        abuf[:, 0:SUBLANES, :] = jnp.zeros((abuf.shape[0], SUBLANES, LANES), F32)
        fix1[...] = jnp.zeros(fix1.shape, F32)
        fix2[...] = jnp.zeros(fix2.shape, F32)

    h = hbuf[...]
    t = lax.broadcasted_iota(jnp.int32, (rows, 1), 0) % seg
    a = _dot(h, wa_ref[...])
    b = _dot(h, wb_ref[...])
    am1, am2 = _segment_conv_taps(a, t, seg, prev0_ref, prev1_ref, abuf, fix1, fix2)
    st0_ref[...], st1_ref[...] = _segment_tails(abuf, rows // seg, seg)
    cw = cw_ref[...]
    conv = cw[2:3, :] * a + cw[1:2, :] * am1 + cw[0:1, :] * am2
    acc[...] += _dot((_gelu_tanh(conv) * b).astype(BF16), wdown_ref[...])

    @pl.when(c == pl.num_programs(0) - 1)
    def _():
        y_ref[...] = x_ref[...] + _rms(acc[...], gpost_ref[...])


def _ffn_sample(layer, x, prev, seg, gpre, gpost, wup, cw, wdown):
    rows, d = x.shape
    nseq = rows // seg
    ff = cw.shape[1]
    cf = FFN_CHUNK
    nc = ff // cf
    cols = lambda r: pl.BlockSpec((r, cf), lambda c: (0, c))
    y, st0, st1 = pl.pallas_call(
        functools.partial(_ffn_sample_kernel, seg),
        grid=(nc,),
        in_specs=[_resident(x.shape), cols(nseq), cols(nseq), _resident(gpre.shape), _resident(gpost.shape),
                  pl.BlockSpec((None, d, cf), lambda c: (layer, 0, c)),
                  pl.BlockSpec((None, d, cf), lambda c: (layer, 0, c + nc)), cols(3),
                  pl.BlockSpec((None, cf, d), lambda c: (layer, c, 0))],
        out_specs=[pl.BlockSpec((rows, d), lambda c: (0, 0)), cols(nseq), cols(nseq)],
        out_shape=[jax.ShapeDtypeStruct((rows, d), F32), jax.ShapeDtypeStruct((nseq, ff), F32),
                   jax.ShapeDtypeStruct((nseq, ff), F32)],
        scratch_shapes=[pltpu.VMEM((rows, d), BF16), pltpu.VMEM((cf // LANES, rows + SUBLANES, LANES), F32),
                        pltpu.VMEM((cf // LANES, rows, LANES), F32), pltpu.VMEM((cf // LANES, rows, LANES), F32),
                        pltpu.VMEM((rows, d), F32)],
        compiler_params=_params(("arbitrary",)),
        name="ffn_sample",
    )(x, prev[:, 0], prev[:, 1], gpre, gpost, wup, wup, cw, wdown)
    return y, jnp.stack([st0, st1], axis=1)


LOG2E = 1.4426950408889634


def _lower_bound_terms(logits_ref, layer):
    lg = logits_ref[...]
    e = jnp.exp(lg - jnp.max(lg, axis=0, keepdims=True))
    pr = e / jnp.sum(e, axis=0, keepdims=True)
    cum = pr[0:1, :]
    for r in range(1, layer + 1):
        cum = cum + pr[r:r + 1, :]
    lb = jnp.clip(cum - pr[0:1, :], 0.0, 1.0)
    return jnp.log(lb) * LOG2E, jnp.log1p(-lb) * LOG2E, 1.0 - lb


def _forget_gate(f, lb_terms):
    log2_lb, log2_1m_lb, one_m_lb = lb_terms
    f2 = f * LOG2E
    e = jnp.exp2(jnp.minimum(f2, -f2))
    w = 1.0 + e
    b = log2_1m_lb + (jnp.minimum(f2, 0.0) - jnp.log2(w))
    d = log2_lb - b
    log2_ft = jnp.maximum(log2_lb, b) + jnp.log2(1.0 + jnp.exp2(jnp.minimum(d, -d)))
    return log2_ft, one_m_lb * (jnp.where(f2 >= 0.0, e, 1.0) / w)


def _split_bf16(x):
    hi = x.astype(BF16)
    return hi, (x - hi.astype(F32)).astype(BF16)


def _hgrn_gate_out(x, h, ob, mixbuf, win_ref, ng_ref, wout_ref, gpost_ref):
    dkt = C_HEADS * C_DK
    g = _dot(h, win_ref[:, 3 * dkt:4 * dkt])
    ng = ng_ref[...]
    for hh in range(C_HEADS):
        sl = slice(hh * C_DK, (hh + 1) * C_DK)
        mixbuf[:, sl] = (_rms(ob[:, sl], ng) * (g[:, sl] * _sigmoid(g[:, sl]))).astype(BF16)
    out = _dot(mixbuf[...], wout_ref[...])
    return x + _rms(out, gpost_ref[...])


def _hgrn_exact_tile(q, k_in, log2_ft, vb, st, ob, sq, sk, slg):
    ts = sq.shape[0]
    sq[...] = q
    sk[...] = k_in
    slg[...] = log2_ft
    ob[...] = jnp.zeros(ob.shape, F32)
    grp = 2 * SUBLANES
    row = lax.broadcasted_iota(jnp.int32, (grp, 1), 0)

    def body(t, carry):
        t0 = pl.multiple_of(lax.shift_left(lax.shift_right_logical(t, 4), 4), grp)
        sel = row == lax.bitwise_and(t, grp - 1)
        for hh in range(C_HEADS):
            sl = slice(hh * C_DK, (hh + 1) * C_DK)
            rows_of = lambda ref: jnp.where(sel, ref[pl.ds(t0, grp), sl].astype(F32), 0.0)
            pick = lambda ref: rows_of(ref).astype(BF16)
            log2_f = jnp.sum(rows_of(slg), axis=0, keepdims=True)
            s_new = st[hh] * jnp.exp2(log2_f) + _dot_tn(pick(vb), pick(sk))
            st[hh] = s_new
            ob[pl.ds(t0, grp), sl] += _dot_nt(pick(sq), s_new.astype(BF16))
        return carry

    lax.fori_loop(0, ts, body, 0)


def _hgrn_prompt_kernel(layer, x_ref, gpre_ref, gpost_ref, win_ref, lbl_ref, ng_ref, wout_ref,
                        y_ref, sout_ref, st, st0, qtb, ktb, qib, klb, eglb, vb, ob, mixbuf, sq, sk, slg):
    s = pl.program_id(1)
    ts = x_ref.shape[0]
    dkt = C_HEADS * C_DK
    ck = HGRN_CHUNK

    @pl.when(s == 0)
    def _():
        st[...] = jnp.zeros(st.shape, F32)

    x = x_ref[...]
    h = _rms(x, gpre_ref[...]).astype(BF16)
    f = _dot(h, win_ref[:, dkt:2 * dkt])
    log2_ft, k_in = _forget_gate(f, _lower_bound_terms(lbl_ref, layer))
    q = _dot(h, win_ref[:, 0:dkt])
    vb[...] = _dot(h, win_ref[:, 2 * dkt:3 * dkt]).astype(BF16)

    ri = lax.broadcasted_iota(jnp.int32, (ck, ck), 0)
    ci = lax.broadcasted_iota(jnp.int32, (ck, ck), 1)
    causal = ri >= ci
    tri = jnp.where(causal, 1.0, 0.0).astype(BF16)
    mid = ck // 2 - 1
    nck = ts // ck

    spread = jnp.zeros((1, dkt), F32)
    for c in range(nck):
        rows = slice(c * ck, (c + 1) * ck)
        hi, lo = _split_bf16(log2_ft[rows, :])
        gcum = _dot(tri, hi) + _dot(tri, lo)
        gr = gcum[mid:mid + 1, :]
        gl = gcum[ck - 1:ck, :]
        spread = jnp.maximum(spread, jnp.maximum(gcum[0:1, :] - gr, gr - gl))
        qc = q[rows, :]
        kc = k_in[rows, :]
        qtb[rows, :] = (qc * jnp.exp2(gcum - gr)).astype(BF16)
        ktb[rows, :] = (kc * jnp.exp2(gr - gcum)).astype(BF16)
        qib[rows, :] = (qc * jnp.exp2(gcum)).astype(BF16)
        klb[rows, :] = (kc * jnp.exp2(gl - gcum)).astype(BF16)
        eglb[c] = jnp.exp2(gl)

    st0[...] = st[...]
    for c in range(nck):
        rows = slice(c * ck, (c + 1) * ck)
        for hh in range(C_HEADS):
            sl = slice(hh * C_DK, (hh + 1) * C_DK)
            vh = vb[rows, sl]
            att = jnp.where(causal, _dot_nt(qtb[rows, sl], ktb[rows, sl]), 0.0).astype(BF16)
            st_h = st[hh]
            ob[rows, sl] = _dot(jnp.concatenate([qib[rows, sl], att], axis=1),
                                jnp.concatenate([st_h.astype(BF16).T, vh], axis=0))
            st[hh] = eglb[c, :, sl] * st_h + _dot_tn(vh, klb[rows, sl])

    y_ref[...] = _hgrn_gate_out(x, h, ob, mixbuf, win_ref, ng_ref, wout_ref, gpost_ref)

    @pl.when(jnp.max(spread) > DECAY_LOG2_LIMIT)
    def _():
        st[...] = st0[...]
        _hgrn_exact_tile(q, k_in, log2_ft, vb, st, ob, sq, sk, slg)
        y_ref[...] = _hgrn_gate_out(x, h, ob, mixbuf, win_ref, ng_ref, wout_ref, gpost_ref)

    @pl.when(s == pl.num_programs(1) - 1)
    def _():
        for hh in range(C_HEADS):
            sout_ref[hh] = st[hh].T


def _hgrn_prompt(layer, x, gpre, gpost, win, lbl, ng, wout):
    b, seq, d = x.shape
    ts = HGRN_TILE
    assert seq % ts == 0 and ts % HGRN_CHUNK == 0, (seq, ts)
    dkt = C_HEADS * C_DK
    tile = pl.BlockSpec((None, ts, d), lambda i, j: (i, j, 0))
    return pl.pallas_call(
        functools.partial(_hgrn_prompt_kernel, layer),
        grid=(b, seq // ts),
        in_specs=[tile, _resident(gpre.shape), _resident(gpost.shape), _resident(win.shape),
                  _resident(lbl.shape), _resident(ng.shape), _resident(wout.shape)],
        out_specs=[tile, pl.BlockSpec((None, C_HEADS, C_DK, C_DK), lambda i, j: (i, 0, 0, 0))],
        out_shape=[jax.ShapeDtypeStruct((b, seq, d), F32),
                   jax.ShapeDtypeStruct((b, C_HEADS, C_DK, C_DK), F32)],
        scratch_shapes=[pltpu.VMEM((C_HEADS, C_DK, C_DK), F32), pltpu.VMEM((C_HEADS, C_DK, C_DK), F32),
                        pltpu.VMEM((ts, dkt), BF16), pltpu.VMEM((ts, dkt), BF16),
                        pltpu.VMEM((ts, dkt), BF16), pltpu.VMEM((ts, dkt), BF16),
                        pltpu.VMEM((ts // HGRN_CHUNK, 1, dkt), F32),
                        pltpu.VMEM((ts, dkt), BF16), pltpu.VMEM((ts, dkt), F32),
                        pltpu.VMEM((ts, dkt), BF16),
                        pltpu.VMEM((ts, dkt), F32), pltpu.VMEM((ts, dkt), F32), pltpu.VMEM((ts, dkt), F32)],
        compiler_params=_params(("arbitrary", "arbitrary")),
        name="hgrn_prompt",
    )(x, gpre, gpost, win, lbl, ng, wout)


def _mix_ab_sample_kernel(seg, x_ref, prev0_ref, prev1_ref, kc_ref, vc_ref, gpre_ref, gpost_ref, win_ref,
                          cw_ref, sink_ref, wout_ref,
                          y_ref, conv0_ref, conv1_ref, kout_ref, vout_ref,
                          ubuf, fix1, fix2, qbuf, knew_s, vnew_s, yabuf, ybbuf, kall, vall, bias_s, sink_s):
    i = pl.program_id(0)
    rows = x_ref.shape[0]
    aw = cw_ref.shape[1]
    nq = GROUP * KV_HEADS * HEAD_DIM
    nkv = KV_HEADS * HEAD_DIM
    pair = 2 * seg
    qrows = KV_HEADS * GROUP * pair
    tk = WINDOW + pair

    @pl.when(i == 0)
    def _():
        x = x_ref[...]
        h = _rms(x, gpre_ref[...]).astype(BF16)
        t = lax.broadcasted_iota(jnp.int32, (rows, 1), 0) % seg
        u = _dot(h, win_ref[:, 2 * aw:3 * aw]) * _dot(h, win_ref[:, 0:aw])
        ubuf[:, 0:SUBLANES, :] = jnp.zeros((ubuf.shape[0], SUBLANES, LANES), F32)
        fix1[...] = jnp.zeros(fix1.shape, F32)
        fix2[...] = jnp.zeros(fix2.shape, F32)
        um1, um2 = _segment_conv_taps(u, t, seg, prev0_ref, prev1_ref, ubuf, fix1, fix2)
        conv0_ref[...], conv1_ref[...] = _segment_tails(ubuf, rows // seg, seg)
        cw = cw_ref[...]
        conv = cw[2:3, :] * u + cw[1:2, :] * um1 + cw[0:1, :] * um2
        yabuf[...] = (_dot(h, win_ref[:, aw:2 * aw]) * conv).astype(BF16)
        q0 = 3 * aw
        qbuf[...] = _dot(h, win_ref[:, q0:q0 + nq]) * (HEAD_DIM ** -0.5)
        kv_new = _dot(h, win_ref[:, q0 + nq:q0 + nq + 2 * nkv])
        knew_s[...] = kv_new[:, 0:nkv]
        vnew_s[...] = kv_new[:, nkv:2 * nkv]

        r = lax.broadcasted_iota(jnp.int32, (qrows, tk), 0)
        j = lax.broadcasted_iota(jnp.int32, (qrows, tk), 1)
        dist = r % seg + WINDOW - j
        valid = (dist >= 0) & (dist < WINDOW) & (j < WINDOW + seg)
        rc = lax.broadcasted_iota(jnp.int32, (qrows, 1), 0)
        head = (rc // (GROUP * pair)) * GROUP + (rc // pair) % GROUP
        slope = jnp.zeros((qrows, 1), F32)
        sink = jnp.zeros((qrows, 1), F32)
        for hidx in range(KV_HEADS * GROUP):
            slope = jnp.where(head == hidx, _head_slope(hidx), slope)
            sink = jnp.where(head == hidx, sink_ref[:, hidx:hidx + 1], sink)
        bias_s[...] = jnp.where(valid, -slope * dist.astype(F32), NEG)
        sink_s[...] = sink

    nblk = kc_ref.shape[0]
    lane = lax.broadcasted_iota(jnp.int32, (1, nkv), 1)
    lo = lane < HEAD_DIM
    rq = lax.broadcasted_iota(jnp.int32, (qrows, 1), 0)
    first_seq = (rq // seg) % 2 == 0

    def body(p, carry):
        r0 = pl.multiple_of((i * nblk + 2 * p) * seg, pair)
        q8 = qbuf[pl.ds(r0, pair), :]
        qg = jnp.concatenate([q8[:, g * nkv:(g + 1) * nkv] for g in range(GROUP)], axis=0)
        qs = jnp.concatenate([jnp.where(lo, qg, 0.0), jnp.where(lo, 0.0, qg)], axis=0).astype(BF16)
        knew = knew_s[pl.ds(r0, pair), :]
        vnew = vnew_s[pl.ds(r0, pair), :]
        sc, ov = [], []
        for a in range(2):
            own_first = (lambda z: z) if a == 0 else (lambda z: jnp.concatenate([z[seg:], z[:seg]], axis=0))
            kall[a, 0:WINDOW, :] = kc_ref[2 * p + a]
            kall[a, WINDOW:tk, :] = own_first(knew)
            vall[a, 0:WINDOW, :] = vc_ref[2 * p + a]
            vall[a, WINDOW:tk, :] = own_first(vnew)
            kout_ref[2 * p + a] = kall[a, seg:seg + WINDOW, :]
            vout_ref[2 * p + a] = vall[a, seg:seg + WINDOW, :]
            sc.append(_dot_nt(qs, kall[a].astype(BF16)))
        s2 = jnp.where(first_seq, sc[0], sc[1]) + bias_s[...]
        sink = sink_s[...]
        m = jnp.maximum(jnp.max(s2, axis=-1, keepdims=True), sink)
        pr = jnp.exp(s2 - m)
        den = jnp.sum(pr, axis=-1, keepdims=True) + jnp.exp(sink - m)
        prb = pr.astype(BF16)
        for a in range(2):
            ov.append(_dot(prb, vall[a].astype(BF16)))
        o = jnp.where(first_seq, ov[0], ov[1]) / den
        og = jnp.where(lo, o[0:GROUP * pair, :], o[GROUP * pair:, :])
        for g in range(GROUP):
            ybbuf[pl.ds(r0, pair), g * nkv:(g + 1) * nkv] = og[g * pair:(g + 1) * pair, :]
        return carry

    lax.fori_loop(0, nblk // 2, body, 0)

    @pl.when(i == pl.num_programs(0) - 1)
    def _():
        out = _dot(yabuf[...], wout_ref[0:aw, :]) + _dot(ybbuf[...].astype(BF16), wout_ref[aw:, :])
        y_ref[...] = x_ref[...] + _rms(out, gpost_ref[...])


def _mix_ab_sample(x, conv_prev, kcache, vcache, seg, gpre, gpost, win, cw, sinks, wout):
    rows, d = x.shape
    nseq = rows // seg
    aw = cw.shape[1]
    nq = GROUP * KV_HEADS * HEAD_DIM
    nkv = KV_HEADS * HEAD_DIM
    sb = SAMPLE_SEQ_BLOCK
    pair = 2 * seg
    qrows = KV_HEADS * GROUP * pair
    tk = WINDOW + pair
    full = lambda shape: pl.BlockSpec(shape, lambda i: (0,) * len(shape))
    cache = pl.BlockSpec((sb, WINDOW, nkv), lambda i: (i, 0, 0))
    prev0, prev1 = conv_prev[:, 0], conv_prev[:, 1]
    y, conv0, conv1, k_new, v_new = pl.pallas_call(
        functools.partial(_mix_ab_sample_kernel, seg),
        grid=(nseq // sb,),
        in_specs=[_resident(x.shape), _resident(prev0.shape), _resident(prev1.shape), cache, cache,
                  _resident(gpre.shape), _resident(gpost.shape), _resident(win.shape), _resident(cw.shape),
                  _resident(sinks.shape), _resident(wout.shape)],
        out_specs=[full((rows, d)), full((nseq, aw)), full((nseq, aw)), cache, cache],
        out_shape=[jax.ShapeDtypeStruct((rows, d), F32), jax.ShapeDtypeStruct((nseq, aw), F32),
                   jax.ShapeDtypeStruct((nseq, aw), F32), jax.ShapeDtypeStruct(kcache.shape, F32),
                   jax.ShapeDtypeStruct(vcache.shape, F32)],
        scratch_shapes=[pltpu.VMEM((aw // LANES, rows + SUBLANES, LANES), F32),
                        pltpu.VMEM((aw // LANES, rows, LANES), F32),
                        pltpu.VMEM((aw // LANES, rows, LANES), F32),
                        pltpu.VMEM((rows, nq), F32),
                        pltpu.VMEM((rows, nkv), F32),
                        pltpu.VMEM((rows, nkv), F32),
                        pltpu.VMEM((rows, aw), BF16),
                        pltpu.VMEM((rows, nq), F32),
                        pltpu.VMEM((2, tk, nkv), F32),
                        pltpu.VMEM((2, tk, nkv), F32),
                        pltpu.VMEM((qrows, tk), F32),
                        pltpu.VMEM((qrows, 1), F32)],
        compiler_params=_params(("arbitrary",)),
        name="mix_ab_sample",
    )(x, prev0, prev1, kcache, vcache, gpre, gpost, win, cw, sinks, wout)
    return y, jnp.stack([conv0, conv1], axis=1), k_new, v_new


def _hgrn_sample_kernel(layer, seg, x_ref, s0_ref, gpre_ref, gpost_ref, win_ref, lbl_ref, ng_ref, wout_ref,
                        y_ref, s1_ref, qb, kb, vb, gb, ob, mixbuf):
    i = pl.program_id(0)
    rows = x_ref.shape[0]
    dkt = C_HEADS * C_DK
    pair = 2 * seg

    @pl.when(i == 0)
    def _():
        x = x_ref[...]
        h = _rms(x, gpre_ref[...]).astype(BF16)
        f = _dot(h, win_ref[:, dkt:2 * dkt])
        log2_ft, kb[...] = _forget_gate(f, _lower_bound_terms(lbl_ref, layer))
        hi, lo = _split_bf16(log2_ft)
        ri = lax.broadcasted_iota(jnp.int32, (rows, rows), 0)
        ci = lax.broadcasted_iota(jnp.int32, (rows, rows), 1)
        tri = jnp.where((ri // seg == ci // seg) & (ri >= ci), 1.0, 0.0).astype(BF16)
        gb[...] = _dot(tri, hi) + _dot(tri, lo)
        qb[...] = _dot(h, win_ref[:, 0:dkt])
        vb[...] = _dot(h, win_ref[:, 2 * dkt:3 * dkt])

    nblk = s0_ref.shape[0]
    rr = lax.broadcasted_iota(jnp.int32, (pair, 1), 0)
    first_seq = rr < seg
    second_seq = rr >= seg
    ri = lax.broadcasted_iota(jnp.int32, (pair, pair), 0)
    ci = lax.broadcasted_iota(jnp.int32, (pair, pair), 1)
    key_s = [(ri // seg == ci // seg) & (ci % seg == sk) & (ri % seg >= sk) for sk in range(seg)]

    def body(p, carry):
        r0 = pl.multiple_of((i * nblk + 2 * p) * seg, pair)
        for hh in range(C_HEADS):
            sl = slice(hh * C_DK, (hh + 1) * C_DK)
            gh = gb[pl.ds(r0, pair), sl]
            g_row = lambda t: jnp.where(first_seq, gh[t:t + 1, :], gh[seg + t:seg + t + 1, :])
            gl = g_row(seg - 1)
            qh = qb[pl.ds(r0, pair), sl]
            kh = kb[pl.ds(r0, pair), sl]
            vh = vb[pl.ds(r0, pair), sl].astype(BF16)
            q_rel = [(qh * jnp.exp2(gh - g_row(sk))).astype(BF16) for sk in range(seg - 1)] + [qh.astype(BF16)]
            rel = _dot_nt(jnp.concatenate(q_rel, axis=0), kh.astype(BF16))
            att = jnp.where(key_s[0], rel[0:pair, :], 0.0)
            for sk in range(1, seg):
                att = att + jnp.where(key_s[sk], rel[sk * pair:(sk + 1) * pair, :], 0.0)
            att = att.astype(BF16)
            qi = (qh * jnp.exp2(gh)).astype(BF16)
            kl = kh * jnp.exp2(gl - gh)
            o_prev = []
            for a in range(2):
                s_a = s0_ref[2 * p + a, hh]
                o_prev.append(_dot(qi, s_a.astype(BF16)))
                kl_a = jnp.where(first_seq if a == 0 else second_seq, kl, 0.0).astype(BF16)
                e = jnp.exp2(gh[(a + 1) * seg - 1:(a + 1) * seg, :])
                e1 = e.astype(BF16).astype(F32)
                e2 = (e - e1).astype(BF16).astype(F32)
                e3 = (e - e1 - e2).astype(BF16).astype(F32)
                dec_l = jnp.where(rr == 0, e1, jnp.where(rr == 1, e2, jnp.where(rr == 2, e3, 0.0)))
                dec = _dot_tn(dec_l.astype(BF16), jnp.ones((pair, C_DK), BF16))
                s1_ref[2 * p + a, hh] = dec * s_a + _dot_tn(kl_a, vh)
            ob[pl.ds(r0, pair), sl] = _dot(att, vh) + jnp.where(first_seq, o_prev[0], o_prev[1])
        return carry

    lax.fori_loop(0, nblk // 2, body, 0)

    @pl.when(i == pl.num_programs(0) - 1)
    def _():
        x = x_ref[...]
        h = _rms(x, gpre_ref[...]).astype(BF16)
        y_ref[...] = _hgrn_gate_out(x, h, ob, mixbuf, win_ref, ng_ref, wout_ref, gpost_ref)


def _hgrn_sample(layer, x, s0, seg, gpre, gpost, win, lbl, ng, wout):
    rows, d = x.shape
    assert 2 * seg == SUBLANES, "two sequences per iteration must fill one f32 vreg of sublanes"
    nseq = rows // seg
    dkt = C_HEADS * C_DK
    sb = SAMPLE_SEQ_BLOCK
    full = lambda shape: pl.BlockSpec(shape, lambda i: (0,) * len(shape))
    state = pl.BlockSpec((sb, C_HEADS, C_DK, C_DK), lambda i: (i, 0, 0, 0))
    return pl.pallas_call(
        functools.partial(_hgrn_sample_kernel, layer, seg),
        grid=(nseq // sb,),
        in_specs=[_resident(x.shape), state, _resident(gpre.shape), _resident(gpost.shape), _resident(win.shape),
                  _resident(lbl.shape), _resident(ng.shape), _resident(wout.shape)],
        out_specs=[full((rows, d)), state],
        out_shape=[jax.ShapeDtypeStruct((rows, d), F32), jax.ShapeDtypeStruct(s0.shape, F32)],
        scratch_shapes=[pltpu.VMEM((rows, dkt), F32), pltpu.VMEM((rows, dkt), F32),
                        pltpu.VMEM((rows, dkt), F32), pltpu.VMEM((rows, dkt), F32),
                        pltpu.VMEM((rows, dkt), F32), pltpu.VMEM((rows, dkt), BF16)],
        compiler_params=_params(("arbitrary",)),
        name="hgrn_sample",
    )(x, s0, gpre, gpost, win, lbl, ng, wout)


def _head_permutation():
    cols = []
    for g in range(GROUP):
        for kv in range(KV_HEADS):
            hidx = kv * GROUP + g
            cols.extend(range(hidx * HEAD_DIM, (hidx + 1) * HEAD_DIM))
    return jnp.asarray(cols, dtype=jnp.int32)


def kernel(x_prompt, x_sample, state_conv_a, cache_swa_k, cache_swa_v, state_hgrn, state_ffn_conv, norm_mix_pre, norm_mix_post, norm_ffn_pre, norm_ffn_post, w_in_ab, conv_a_w, attn_sinks, w_out_ab, w_in_c, hgrn_lb_logits, hgrn_norm_g, w_out_c, w_ffn_up, ffn_conv_w, w_ffn_down):
    bp, seq, d = x_prompt.shape
    ns, seg, _ = x_sample.shape
    aw = conv_a_w.shape[-1]
    nq = GROUP * KV_HEADS * HEAD_DIM
    nkv = KV_HEADS * HEAD_DIM

    perm = _head_permutation()
    q0 = 3 * aw
    win_ab = w_in_ab[0]
    win_ab = jnp.concatenate([win_ab[:, :q0], win_ab[:, q0:q0 + nq][:, perm], win_ab[:, q0 + nq:]], axis=1).astype(BF16)
    wout_ab = w_out_ab[0]
    wout_ab = jnp.concatenate([wout_ab[:aw], wout_ab[aw:][perm]], axis=0).astype(BF16)
    win_c = w_in_c[0].astype(BF16)
    wout_c = w_out_c[0].astype(BF16)
    wup = w_ffn_up.astype(BF16)
    wdown = w_ffn_down.astype(BF16)
    row = lambda a: a.reshape(1, -1)
    sinks = row(attn_sinks[0])
    ng = row(hgrn_norm_g[0])

    x1, conv_p, k_p, v_p = _mix_ab_prompt(x_prompt, row(norm_mix_pre[0]), row(norm_mix_post[0]), win_ab,
                                          conv_a_w[0], sinks, wout_ab)
    x2, ffn_p0 = _ffn_prompt(0, x1, row(norm_ffn_pre[0]), row(norm_ffn_post[0]), wup, ffn_conv_w[0], wdown)
    x3, hgrn_p = _hgrn_prompt(1, x2, row(norm_mix_pre[1]), row(norm_mix_post[1]), win_c, hgrn_lb_logits,
                              ng, wout_c)
    y_prompt, ffn_p1 = _ffn_prompt(1, x3, row(norm_ffn_pre[1]), row(norm_ffn_post[1]), wup, ffn_conv_w[1], wdown)

    xs = x_sample.reshape(ns * seg, d)
    kc = cache_swa_k[0].reshape(ns, WINDOW, nkv)
    vc = cache_swa_v[0].reshape(ns, WINDOW, nkv)
    s1, conv_s, k_s, v_s = _mix_ab_sample(xs, state_conv_a[0], kc, vc, seg, row(norm_mix_pre[0]),
                                          row(norm_mix_post[0]), win_ab, conv_a_w[0], sinks, wout_ab)
    s2, ffn_s0 = _ffn_sample(0, s1, state_ffn_conv[0], seg, row(norm_ffn_pre[0]), row(norm_ffn_post[0]),
                             wup, ffn_conv_w[0], wdown)
    s3, hgrn_s = _hgrn_sample(1, s2, state_hgrn[0], seg, row(norm_mix_pre[1]), row(norm_mix_post[1]),
                              win_c, hgrn_lb_logits, ng, wout_c)
    y_sample, ffn_s1 = _ffn_sample(1, s3, state_ffn_conv[1], seg, row(norm_ffn_pre[1]), row(norm_ffn_post[1]),
                                   wup, ffn_conv_w[1], wdown)

    kvshape = (KV_HEADS, HEAD_DIM)
    return (y_prompt, y_sample.reshape(ns, seg, d),
            conv_p[None], conv_s[None],
            k_p.reshape(1, bp, WINDOW, *kvshape), k_s.reshape(1, ns, WINDOW, *kvshape),
            v_p.reshape(1, bp, WINDOW, *kvshape), v_s.reshape(1, ns, WINDOW, *kvshape),
            hgrn_p[None], hgrn_s[None],
            jnp.stack([ffn_p0, ffn_p1]), jnp.stack([ffn_s0, ffn_s1]))
```

```python
import functools
import math

import jax
import jax.numpy as jnp
from jax import lax
from jax.experimental import pallas as pl
from jax.experimental.pallas import tpu as pltpu

F32 = jnp.float32
BF16 = jnp.bfloat16

EPS = 1e-6
WINDOW = 128
KV_HEADS = 2
GROUP = 4
HEAD_DIM = 64
C_HEADS = 8
C_DK = 128
NEG = -1e30
SUBLANES = 8
LANES = 128
MIX_TILE = 1024
HGRN_TILE = 512
FFN_TILE = 1024
HGRN_CHUNK = 64
DECAY_LOG2_LIMIT = 100.0
FFN_CHUNK = 256
SAMPLE_SEQ_BLOCK = 8
VMEM_LIMIT_V7X = 56 * 2**20


def _dot(a, b):
    return jnp.dot(a, b, preferred_element_type=F32)


def _dot_nt(a, b):
    return lax.dot_general(a, b, (((1,), (1,)), ((), ())), preferred_element_type=F32)


def _dot_tn(a, b):
    return lax.dot_general(a, b, (((0,), (0,)), ((), ())), preferred_element_type=F32)


def _rms(x, g):
    return x * lax.rsqrt(jnp.mean(x * x, axis=-1, keepdims=True) + EPS) * g


def _gelu_tanh(x):
    c = math.sqrt(2.0 / math.pi)
    return x * (0.5 * (1.0 + jnp.tanh(c * (x + 0.044715 * (x * x * x)))))


def _sigmoid(x):
    return 1.0 / (1.0 + jnp.exp(-x))


def _resident(shape):
    nd = len(shape)
    return pl.BlockSpec(shape, lambda *_: (0,) * nd, pipeline_mode=pl.Buffered(1))


def _params(sem):
    return pltpu.CompilerParams(dimension_semantics=sem, vmem_limit_bytes=VMEM_LIMIT_V7X)


def _head_slope(h):
    return 2.0 ** (-8.0 * (h + 1) / (KV_HEADS * GROUP))


def _mix_ab_prompt_kernel(x_ref, gpre_ref, gpost_ref, win_ref, cw_ref, sink_ref, wout_ref,
                          y_ref, conv_ref, kout_ref, vout_ref,
                          ubuf, qbuf, kbuf, vbuf, mixbuf, bias_s):
    s = pl.program_id(1)
    ts = x_ref.shape[0]
    aw = cw_ref.shape[1]
    nq = GROUP * KV_HEADS * HEAD_DIM
    nkv = KV_HEADS * HEAD_DIM

    @pl.when(s == 0)
    def _():
        ubuf[0:SUBLANES, :] = jnp.zeros((SUBLANES, aw), F32)
        kbuf[0:WINDOW, :] = jnp.zeros((WINDOW, nkv), BF16)
        vbuf[0:WINDOW, 0:nkv] = jnp.zeros((WINDOW, nkv), BF16)
        vbuf[:, nkv:2 * nkv] = jnp.ones((ts + WINDOW, nkv), BF16)

    x = x_ref[...]
    h = _rms(x, gpre_ref[...]).astype(BF16)

    u = _dot(h, win_ref[:, 2 * aw:3 * aw]) * _dot(h, win_ref[:, 0:aw])
    ubuf[SUBLANES:SUBLANES + ts, :] = u
    cw = cw_ref[...]
    conv = (cw[2:3, :] * u + cw[1:2, :] * ubuf[SUBLANES - 1:SUBLANES - 1 + ts, :]
            + cw[0:1, :] * ubuf[SUBLANES - 2:SUBLANES - 2 + ts, :])
    mixbuf[:, 0:aw] = (_dot(h, win_ref[:, aw:2 * aw]) * conv).astype(BF16)
    tail = ubuf[ts + SUBLANES - 2:ts + SUBLANES, :]
    conv_ref[...] = tail
    ubuf[SUBLANES - 2:SUBLANES, :] = tail

    q0 = 3 * aw
    q = _dot(h, win_ref[:, q0:q0 + nq]) * (HEAD_DIM ** -0.5 * LOG2E)
    lane = lax.broadcasted_iota(jnp.int32, (1, nkv), 1)
    lo = lane < HEAD_DIM
    nblk = ts // WINDOW
    for n in range(nblk):
        for g in range(GROUP):
            qg = q[n * WINDOW:(n + 1) * WINDOW, g * nkv:(g + 1) * nkv]
            qbuf[n, g * WINDOW:(g + 1) * WINDOW, :] = jnp.where(lo, qg, 0.0).astype(BF16)
            qbuf[n, (GROUP + g) * WINDOW:(GROUP + g + 1) * WINDOW, :] = jnp.where(lo, 0.0, qg).astype(BF16)
    wkv = win_ref[:, q0 + nq:q0 + nq + 2 * nkv]
    kv_new = jnp.concatenate([_dot(h[0:ts // 2, :], wkv), _dot(h[ts // 2:, :], wkv)], axis=0)
    k = kv_new[:, 0:nkv]
    v = kv_new[:, nkv:2 * nkv]
    kbuf[WINDOW:WINDOW + ts, :] = k.astype(BF16)
    vbuf[WINDOW:WINDOW + ts, 0:nkv] = v.astype(BF16)
    kout_ref[...] = k[ts - WINDOW:, :]
    vout_ref[...] = v[ts - WINDOW:, :]

    qi = lax.broadcasted_iota(jnp.int32, (WINDOW, 2 * WINDOW), 0)
    kj = lax.broadcasted_iota(jnp.int32, (WINDOW, 2 * WINDOW), 1)
    dist = qi + WINDOW - kj
    valid = (dist >= 0) & (dist < WINDOW)
    negdist = jnp.where(valid, -dist.astype(F32), 0.0)
    maskadd = jnp.where(valid, 0.0, NEG)
    nheads = KV_HEADS * GROUP
    for hidx in range(nheads):
        bias_s[hidx] = (_head_slope(hidx) * LOG2E) * negdist + maskadd
    first = jnp.where(kj < WINDOW, jnp.where(s == 0, NEG, 0.0), 0.0)

    for n in range(nblk):
        r0 = n * WINDOW
        half = GROUP * WINDOW
        keys = kbuf[r0:r0 + 2 * WINDOW, :]
        sc_kv = [_dot_nt(qbuf[n, kv * half:(kv + 1) * half, :], keys) for kv in range(KV_HEADS)]
        ps, sink_terms = [], []
        for hidx in range(nheads):
            g = hidx % GROUP
            sc = sc_kv[hidx // GROUP][g * WINDOW:(g + 1) * WINDOW, :] + bias_s[hidx]
            if n == 0:
                sc = sc + first
            sink = sink_ref[hidx] * LOG2E
            m = jnp.maximum(jnp.max(sc, axis=-1, keepdims=True), sink)
            ps.append(jnp.exp2(sc - m).astype(BF16))
            sink_terms.append(jnp.exp2(sink - m))
        vals = vbuf[r0:r0 + 2 * WINDOW, :]
        pv_kv = [_dot(jnp.concatenate(ps[kv * GROUP:(kv + 1) * GROUP], axis=0), vals) for kv in range(KV_HEADS)]
        outs = []
        for hidx in range(nheads):
            g = hidx % GROUP
            pv_h = pv_kv[hidx // GROUP][g * WINDOW:(g + 1) * WINDOW, :]
            outs.append(pv_h[:, 0:nkv] / (pv_h[:, nkv:2 * nkv] + sink_terms[hidx]))
        for g in range(GROUP):
            og = jnp.where(lo, outs[g], outs[GROUP + g])
            mixbuf[r0:r0 + WINDOW, aw + g * nkv:aw + (g + 1) * nkv] = og.astype(BF16)

    kbuf[0:WINDOW, :] = kbuf[ts:ts + WINDOW, :]
    vbuf[0:WINDOW, :] = vbuf[ts:ts + WINDOW, :]

    out = _dot(mixbuf[...], wout_ref[...])
    y_ref[...] = x + _rms(out, gpost_ref[...])


def _mix_ab_prompt(x, gpre, gpost, win, cw, sinks, wout):
    b, seq, d = x.shape
    ts = MIX_TILE
    assert seq % ts == 0 and ts % WINDOW == 0, (seq, ts)
    aw = cw.shape[1]
    nq = GROUP * KV_HEADS * HEAD_DIM
    nkv = KV_HEADS * HEAD_DIM
    tile = pl.BlockSpec((None, ts, d), lambda i, j: (i, j, 0))
    return pl.pallas_call(
        _mix_ab_prompt_kernel,
        grid=(b, seq // ts),
        in_specs=[tile, _resident(gpre.shape), _resident(gpost.shape), _resident(win.shape),
                  _resident(cw.shape), pl.BlockSpec(memory_space=pltpu.SMEM), _resident(wout.shape)],
        out_specs=[tile,
                   pl.BlockSpec((None, 2, aw), lambda i, j: (i, 0, 0)),
                   pl.BlockSpec((None, WINDOW, nkv), lambda i, j: (i, 0, 0)),
                   pl.BlockSpec((None, WINDOW, nkv), lambda i, j: (i, 0, 0))],
        out_shape=[jax.ShapeDtypeStruct((b, seq, d), F32),
                   jax.ShapeDtypeStruct((b, 2, aw), F32),
                   jax.ShapeDtypeStruct((b, WINDOW, nkv), F32),
                   jax.ShapeDtypeStruct((b, WINDOW, nkv), F32)],
        scratch_shapes=[pltpu.VMEM((ts + SUBLANES, aw), F32),
                        pltpu.VMEM((ts // WINDOW, KV_HEADS * GROUP * WINDOW, nkv), BF16),
                        pltpu.VMEM((ts + WINDOW, nkv), BF16),
                        pltpu.VMEM((ts + WINDOW, 2 * nkv), BF16),
                        pltpu.VMEM((ts, d), BF16),
                        pltpu.VMEM((KV_HEADS * GROUP, WINDOW, 2 * WINDOW), F32)],
        compiler_params=_params(("arbitrary", "arbitrary")),
        name="mix_ab_prompt",
    )(x, gpre, gpost, win, cw, sinks.reshape(-1), wout)


def _ffn_prompt_kernel(x_ref, gpre_ref, gpost_ref, wup_ref, cw_ref, wdown_ref,
                       y_ref, st_ref, abuf, carry, gbuf):
    s = pl.program_id(1)
    rows = x_ref.shape[0]
    ff = cw_ref.shape[1]

    @pl.when(s == 0)
    def _():
        carry[...] = jnp.zeros(carry.shape, F32)

    x = x_ref[...]
    h = _rms(x, gpre_ref[...]).astype(BF16)
    for c in range(ff // FFN_CHUNK):
        c0 = c * FFN_CHUNK
        buf = abuf.at[c % 2]
        a = _dot(h, wup_ref[:, c0:c0 + FFN_CHUNK])
        b = _dot(h, wup_ref[:, ff + c0:ff + c0 + FFN_CHUNK])
        buf[0:SUBLANES, :] = carry[:, c0:c0 + FFN_CHUNK]
        buf[SUBLANES:SUBLANES + rows, :] = a
        cw = cw_ref[:, c0:c0 + FFN_CHUNK]
        conv = (cw[2:3, :] * a + cw[1:2, :] * buf[SUBLANES - 1:SUBLANES - 1 + rows, :]
                + cw[0:1, :] * buf[SUBLANES - 2:SUBLANES - 2 + rows, :])
        tail = a[rows - 2:rows, :]
        st_ref[:, c0:c0 + FFN_CHUNK] = tail
        carry[SUBLANES - 2:SUBLANES, c0:c0 + FFN_CHUNK] = tail
        gbuf[:, c0:c0 + FFN_CHUNK] = (_gelu_tanh(conv) * b).astype(BF16)
    out = _dot(gbuf[...], wdown_ref[...])
    y_ref[...] = x + _rms(out, gpost_ref[...])


def _ffn_prompt(layer, x, gpre, gpost, wup, cw, wdown):
    b, seq, d = x.shape
    ts = FFN_TILE
    ff = cw.shape[1]
    assert seq % ts == 0 and ff % FFN_CHUNK == 0, (seq, ts, ff)
    tile = pl.BlockSpec((None, ts, d), lambda i, j: (i, j, 0))
    layer_block = lambda w: pl.BlockSpec((None,) + w.shape[1:], lambda i, j: (layer, 0, 0),
                                         pipeline_mode=pl.Buffered(1))
    return pl.pallas_call(
        _ffn_prompt_kernel,
        grid=(b, seq // ts),
        in_specs=[tile, _resident(gpre.shape), _resident(gpost.shape), layer_block(wup),
                  _resident(cw.shape), layer_block(wdown)],
        out_specs=[tile, pl.BlockSpec((None, 2, ff), lambda i, j: (i, 0, 0))],
        out_shape=[jax.ShapeDtypeStruct((b, seq, d), F32), jax.ShapeDtypeStruct((b, 2, ff), F32)],
        scratch_shapes=[pltpu.VMEM((2, ts + SUBLANES, FFN_CHUNK), F32), pltpu.VMEM((SUBLANES, ff), F32),
                        pltpu.VMEM((ts, ff), BF16)],
        compiler_params=_params(("arbitrary", "arbitrary")),
        name="ffn_prompt",
    )(x, gpre, gpost, wup, cw, wdown)


def _segment_conv_taps(a, t, seg, prev0_ref, prev1_ref, abuf, fix1, fix2):
    rows, c = a.shape
    nseq = rows // seg
    am1, am2 = [], []
    for k in range(c // LANES):
        sl = slice(k * LANES, (k + 1) * LANES)
        abuf[k, SUBLANES:SUBLANES + rows, :] = a[:, sl]
        fix1[k, pl.ds(0, nseq, stride=seg), :] = prev1_ref[:, sl]
        fix2[k, pl.ds(0, nseq, stride=seg), :] = prev0_ref[:, sl]
        fix2[k, pl.ds(1, nseq, stride=seg), :] = prev1_ref[:, sl]
        am1.append(jnp.where(t >= 1, abuf[k, SUBLANES - 1:SUBLANES - 1 + rows, :], fix1[k]))
        am2.append(jnp.where(t >= 2, abuf[k, SUBLANES - 2:SUBLANES - 2 + rows, :], fix2[k]))
    return jnp.concatenate(am1, axis=1), jnp.concatenate(am2, axis=1)


def _segment_tails(abuf, nseq, seg):
    tail = lambda back: jnp.concatenate(
        [abuf[k, pl.ds(SUBLANES + seg - back, nseq, stride=seg), :] for k in range(abuf.shape[0])], axis=1)
    return tail(2), tail(1)


def _ffn_sample_kernel(seg, x_ref, prev0_ref, prev1_ref, gpre_ref, gpost_ref, wa_ref, wb_ref, cw_ref, wdown_ref,
                       y_ref, st0_ref, st1_ref, hbuf, abuf, fix1, fix2, acc):
    c = pl.program_id(0)
    rows = x_ref.shape[0]

    @pl.when(c == 0)
    def _():
        hbuf[...] = _rms(x_ref[...], gpre_ref[...]).astype(BF16)
        acc[...] = jnp.zeros(acc.shape, F32)
        abuf[:, 0:SUBLANES, :] = jnp.zeros((abuf.shape[0], SUBLANES, LANES), F32)
        fix1[...] = jnp.zeros(fix1.shape, F32)
        fix2[...] = jnp.zeros(fix2.shape, F32)

    h = hbuf[...]
    t = lax.broadcasted_iota(jnp.int32, (rows, 1), 0) % seg
    a = _dot(h, wa_ref[...])
    b = _dot(h, wb_ref[...])
    am1, am2 = _segment_conv_taps(a, t, seg, prev0_ref, prev1_ref, abuf, fix1, fix2)
    st0_ref[...], st1_ref[...] = _segment_tails(abuf, rows // seg, seg)
    cw = cw_ref[...]
    conv = cw[2:3, :] * a + cw[1:2, :] * am1 + cw[0:1, :] * am2
    acc[...] += _dot((_gelu_tanh(conv) * b).astype(BF16), wdown_ref[...])

    @pl.when(c == pl.num_programs(0) - 1)
    def _():
        y_ref[...] = x_ref[...] + _rms(acc[...], gpost_ref[...])


def _ffn_sample(layer, x, prev, seg, gpre, gpost, wup, cw, wdown):
    rows, d = x.shape
    nseq = rows // seg
    ff = cw.shape[1]
    cf = FFN_CHUNK
    nc = ff // cf
    cols = lambda r: pl.BlockSpec((r, cf), lambda c: (0, c))
    y, st0, st1 = pl.pallas_call(
        functools.partial(_ffn_sample_kernel, seg),
        grid=(nc,),
        in_specs=[_resident(x.shape), cols(nseq), cols(nseq), _resident(gpre.shape), _resident(gpost.shape),
                  pl.BlockSpec((None, d, cf), lambda c: (layer, 0, c)),
                  pl.BlockSpec((None, d, cf), lambda c: (layer, 0, c + nc)), cols(3),
                  pl.BlockSpec((None, cf, d), lambda c: (layer, c, 0))],
        out_specs=[pl.BlockSpec((rows, d), lambda c: (0, 0)), cols(nseq), cols(nseq)],
        out_shape=[jax.ShapeDtypeStruct((rows, d), F32), jax.ShapeDtypeStruct((nseq, ff), F32),
                   jax.ShapeDtypeStruct((nseq, ff), F32)],
        scratch_shapes=[pltpu.VMEM((rows, d), BF16), pltpu.VMEM((cf // LANES, rows + SUBLANES, LANES), F32),
                        pltpu.VMEM((cf // LANES, rows, LANES), F32), pltpu.VMEM((cf // LANES, rows, LANES), F32),
                        pltpu.VMEM((rows, d), F32)],
        compiler_params=_params(("arbitrary",)),
        name="ffn_sample",
    )(x, prev[:, 0], prev[:, 1], gpre, gpost, wup, wup, cw, wdown)
    return y, jnp.stack([st0, st1], axis=1)


LOG2E = 1.4426950408889634


def _lower_bound_terms(logits_ref, layer):
    lg = logits_ref[...]
    e = jnp.exp(lg - jnp.max(lg, axis=0, keepdims=True))
    pr = e / jnp.sum(e, axis=0, keepdims=True)
    cum = pr[0:1, :]
    for r in range(1, layer + 1):
        cum = cum + pr[r:r + 1, :]
    lb = jnp.clip(cum - pr[0:1, :], 0.0, 1.0)
    return jnp.log(lb) * LOG2E, jnp.log1p(-lb) * LOG2E, 1.0 - lb


def _forget_gate(f, lb_terms):
    log2_lb, log2_1m_lb, one_m_lb = lb_terms
    f2 = f * LOG2E
    e = jnp.exp2(jnp.minimum(f2, -f2))
    w = 1.0 + e
    b = log2_1m_lb + (jnp.minimum(f2, 0.0) - jnp.log2(w))
    d = log2_lb - b
    log2_ft = jnp.maximum(log2_lb, b) + jnp.log2(1.0 + jnp.exp2(jnp.minimum(d, -d)))
    return log2_ft, one_m_lb * (jnp.where(f2 >= 0.0, e, 1.0) / w)


def _split_bf16(x):
    hi = x.astype(BF16)
    return hi, (x - hi.astype(F32)).astype(BF16)


def _hgrn_gate_out(x, h, ob, mixbuf, win_ref, ng_ref, wout_ref, gpost_ref):
    dkt = C_HEADS * C_DK
    g = _dot(h, win_ref[:, 3 * dkt:4 * dkt])
    ng = ng_ref[...]
    for hh in range(C_HEADS):
        sl = slice(hh * C_DK, (hh + 1) * C_DK)
        mixbuf[:, sl] = (_rms(ob[:, sl], ng) * (g[:, sl] * _sigmoid(g[:, sl]))).astype(BF16)
    out = _dot(mixbuf[...], wout_ref[...])
    return x + _rms(out, gpost_ref[...])


def _hgrn_exact_tile(q, k_in, log2_ft, vb, st, ob, sq, sk, slg):
    ts = sq.shape[0]
    sq[...] = q
    sk[...] = k_in
    slg[...] = log2_ft
    ob[...] = jnp.zeros(ob.shape, F32)
    grp = 2 * SUBLANES
    row = lax.broadcasted_iota(jnp.int32, (grp, 1), 0)

    def body(t, carry):
        t0 = pl.multiple_of(lax.shift_left(lax.shift_right_logical(t, 4), 4), grp)
        sel = row == lax.bitwise_and(t, grp - 1)
        for hh in range(C_HEADS):
            sl = slice(hh * C_DK, (hh + 1) * C_DK)
            rows_of = lambda ref: jnp.where(sel, ref[pl.ds(t0, grp), sl].astype(F32), 0.0)
            pick = lambda ref: rows_of(ref).astype(BF16)
            log2_f = jnp.sum(rows_of(slg), axis=0, keepdims=True)
            s_new = st[hh] * jnp.exp2(log2_f) + _dot_tn(pick(vb), pick(sk))
            st[hh] = s_new
            ob[pl.ds(t0, grp), sl] += _dot_nt(pick(sq), s_new.astype(BF16))
        return carry

    lax.fori_loop(0, ts, body, 0)


def _hgrn_prompt_kernel(layer, x_ref, gpre_ref, gpost_ref, win_ref, lbl_ref, ng_ref, wout_ref,
                        y_ref, sout_ref, st, st0, qtb, ktb, qib, klb, eglb, vb, ob, mixbuf, sq, sk, slg):
    s = pl.program_id(1)
    ts = x_ref.shape[0]
    dkt = C_HEADS * C_DK
    ck = HGRN_CHUNK

    @pl.when(s == 0)
    def _():
        st[...] = jnp.zeros(st.shape, F32)

    x = x_ref[...]
    h = _rms(x, gpre_ref[...]).astype(BF16)
    f = _dot(h, win_ref[:, dkt:2 * dkt])
    log2_ft, k_in = _forget_gate(f, _lower_bound_terms(lbl_ref, layer))
    q = _dot(h, win_ref[:, 0:dkt])
    vb[...] = _dot(h, win_ref[:, 2 * dkt:3 * dkt]).astype(BF16)

    ri = lax.broadcasted_iota(jnp.int32, (ck, ck), 0)
    ci = lax.broadcasted_iota(jnp.int32, (ck, ck), 1)
    causal = ri >= ci
    tri = jnp.where(causal, 1.0, 0.0).astype(BF16)
    mid = ck // 2 - 1
    nck = ts // ck

    spread = jnp.zeros((1, dkt), F32)
    for c in range(nck):
        rows = slice(c * ck, (c + 1) * ck)
        hi, lo = _split_bf16(log2_ft[rows, :])
        gcum = _dot(tri, hi) + _dot(tri, lo)
        gr = gcum[mid:mid + 1, :]
        gl = gcum[ck - 1:ck, :]
        spread = jnp.maximum(spread, jnp.maximum(gcum[0:1, :] - gr, gr - gl))
        qc = q[rows, :]
        kc = k_in[rows, :]
        qtb[rows, :] = (qc * jnp.exp2(gcum - gr)).astype(BF16)
        ktb[rows, :] = (kc * jnp.exp2(gr - gcum)).astype(BF16)
        qib[rows, :] = (qc * jnp.exp2(gcum)).astype(BF16)
        klb[rows, :] = (kc * jnp.exp2(gl - gcum)).astype(BF16)
        eglb[c] = jnp.exp2(gl)

    st0[...] = st[...]
    for c in range(nck):
        rows = slice(c * ck, (c + 1) * ck)
        for hh in range(C_HEADS):
            sl = slice(hh * C_DK, (hh + 1) * C_DK)
            vh = vb[rows, sl]
            att = jnp.where(causal, _dot_nt(qtb[rows, sl], ktb[rows, sl]), 0.0).astype(BF16)
            st_h = st[hh]
            ob[rows, sl] = _dot(jnp.concatenate([qib[rows, sl], att], axis=1),
                                jnp.concatenate([st_h.astype(BF16).T, vh], axis=0))
            st[hh] = eglb[c, :, sl] * st_h + _dot_tn(vh, klb[rows, sl])

    y_ref[...] = _hgrn_gate_out(x, h, ob, mixbuf, win_ref, ng_ref, wout_ref, gpost_ref)

    @pl.when(jnp.max(spread) > DECAY_LOG2_LIMIT)
    def _():
        st[...] = st0[...]
        _hgrn_exact_tile(q, k_in, log2_ft, vb, st, ob, sq, sk, slg)
        y_ref[...] = _hgrn_gate_out(x, h, ob, mixbuf, win_ref, ng_ref, wout_ref, gpost_ref)

    @pl.when(s == pl.num_programs(1) - 1)
    def _():
        for hh in range(C_HEADS):
            sout_ref[hh] = st[hh].T


def _hgrn_prompt(layer, x, gpre, gpost, win, lbl, ng, wout):
    b, seq, d = x.shape
    ts = HGRN_TILE
    assert seq % ts == 0 and ts % HGRN_CHUNK == 0, (seq, ts)
    dkt = C_HEADS * C_DK
    tile = pl.BlockSpec((None, ts, d), lambda i, j: (i, j, 0))
    return pl.pallas_call(
        functools.partial(_hgrn_prompt_kernel, layer),
        grid=(b, seq // ts),
        in_specs=[tile, _resident(gpre.shape), _resident(gpost.shape), _resident(win.shape),
                  _resident(lbl.shape), _resident(ng.shape), _resident(wout.shape)],
        out_specs=[tile, pl.BlockSpec((None, C_HEADS, C_DK, C_DK), lambda i, j: (i, 0, 0, 0))],
        out_shape=[jax.ShapeDtypeStruct((b, seq, d), F32),
                   jax.ShapeDtypeStruct((b, C_HEADS, C_DK, C_DK), F32)],
        scratch_shapes=[pltpu.VMEM((C_HEADS, C_DK, C_DK), F32), pltpu.VMEM((C_HEADS, C_DK, C_DK), F32),
                        pltpu.VMEM((ts, dkt), BF16), pltpu.VMEM((ts, dkt), BF16),
                        pltpu.VMEM((ts, dkt), BF16), pltpu.VMEM((ts, dkt), BF16),
                        pltpu.VMEM((ts // HGRN_CHUNK, 1, dkt), F32),
                        pltpu.VMEM((ts, dkt), BF16), pltpu.VMEM((ts, dkt), F32),
                        pltpu.VMEM((ts, dkt), BF16),
                        pltpu.VMEM((ts, dkt), F32), pltpu.VMEM((ts, dkt), F32), pltpu.VMEM((ts, dkt), F32)],
        compiler_params=_params(("arbitrary", "arbitrary")),
        name="hgrn_prompt",
    )(x, gpre, gpost, win, lbl, ng, wout)


def _mix_ab_sample_kernel(seg, x_ref, prev0_ref, prev1_ref, kc_ref, vc_ref, gpre_ref, gpost_ref, win_ref,
                          cw_ref, sink_ref, wout_ref,
                          y_ref, conv0_ref, conv1_ref, kout_ref, vout_ref,
                          ubuf, fix1, fix2, qbuf, knew_s, vnew_s, yabuf, ybbuf, kall, vall, bias_s, sink_s):
    i = pl.program_id(0)
    rows = x_ref.shape[0]
    aw = cw_ref.shape[1]
    nq = GROUP * KV_HEADS * HEAD_DIM
    nkv = KV_HEADS * HEAD_DIM
    pair = 2 * seg
    qrows = KV_HEADS * GROUP * pair
    tk = WINDOW + pair

    @pl.when(i == 0)
    def _():
        x = x_ref[...]
        h = _rms(x, gpre_ref[...]).astype(BF16)
        t = lax.broadcasted_iota(jnp.int32, (rows, 1), 0) % seg
        u = _dot(h, win_ref[:, 2 * aw:3 * aw]) * _dot(h, win_ref[:, 0:aw])
        ubuf[:, 0:SUBLANES, :] = jnp.zeros((ubuf.shape[0], SUBLANES, LANES), F32)
        fix1[...] = jnp.zeros(fix1.shape, F32)
        fix2[...] = jnp.zeros(fix2.shape, F32)
        um1, um2 = _segment_conv_taps(u, t, seg, prev0_ref, prev1_ref, ubuf, fix1, fix2)
        conv0_ref[...], conv1_ref[...] = _segment_tails(ubuf, rows // seg, seg)
        cw = cw_ref[...]
        conv = cw[2:3, :] * u + cw[1:2, :] * um1 + cw[0:1, :] * um2
        yabuf[...] = (_dot(h, win_ref[:, aw:2 * aw]) * conv).astype(BF16)
        q0 = 3 * aw
        qbuf[...] = _dot(h, win_ref[:, q0:q0 + nq]) * (HEAD_DIM ** -0.5)
        kv_new = _dot(h, win_ref[:, q0 + nq:q0 + nq + 2 * nkv])
        knew_s[...] = kv_new[:, 0:nkv]
        vnew_s[...] = kv_new[:, nkv:2 * nkv]

        r = lax.broadcasted_iota(jnp.int32, (qrows, tk), 0)
        j = lax.broadcasted_iota(jnp.int32, (qrows, tk), 1)
        dist = r % seg + WINDOW - j
        valid = (dist >= 0) & (dist < WINDOW) & (j < WINDOW + seg)
        rc = lax.broadcasted_iota(jnp.int32, (qrows, 1), 0)
        head = (rc // (GROUP * pair)) * GROUP + (rc // pair) % GROUP
        slope = jnp.zeros((qrows, 1), F32)
        sink = jnp.zeros((qrows, 1), F32)
        for hidx in range(KV_HEADS * GROUP):
            slope = jnp.where(head == hidx, _head_slope(hidx), slope)
            sink = jnp.where(head == hidx, sink_ref[:, hidx:hidx + 1], sink)
        bias_s[...] = jnp.where(valid, -slope * dist.astype(F32), NEG)
        sink_s[...] = sink

    nblk = kc_ref.shape[0]
    lane = lax.broadcasted_iota(jnp.int32, (1, nkv), 1)
    lo = lane < HEAD_DIM
    rq = lax.broadcasted_iota(jnp.int32, (qrows, 1), 0)
    first_seq = (rq // seg) % 2 == 0

    def body(p, carry):
        r0 = pl.multiple_of((i * nblk + 2 * p) * seg, pair)
        q8 = qbuf[pl.ds(r0, pair), :]
        qg = jnp.concatenate([q8[:, g * nkv:(g + 1) * nkv] for g in range(GROUP)], axis=0)
        qs = jnp.concatenate([jnp.where(lo, qg, 0.0), jnp.where(lo, 0.0, qg)], axis=0).astype(BF16)
        knew = knew_s[pl.ds(r0, pair), :]
        vnew = vnew_s[pl.ds(r0, pair), :]
        sc, ov = [], []
        for a in range(2):
            own_first = (lambda z: z) if a == 0 else (lambda z: jnp.concatenate([z[seg:], z[:seg]], axis=0))
            kall[a, 0:WINDOW, :] = kc_ref[2 * p + a]
            kall[a, WINDOW:tk, :] = own_first(knew)
            vall[a, 0:WINDOW, :] = vc_ref[2 * p + a]
            vall[a, WINDOW:tk, :] = own_first(vnew)
            kout_ref[2 * p + a] = kall[a, seg:seg + WINDOW, :]
            vout_ref[2 * p + a] = vall[a, seg:seg + WINDOW, :]
            sc.append(_dot_nt(qs, kall[a].astype(BF16)))
        s2 = jnp.where(first_seq, sc[0], sc[1]) + bias_s[...]
        sink = sink_s[...]
        m = jnp.maximum(jnp.max(s2, axis=-1, keepdims=True), sink)
        pr = jnp.exp(s2 - m)
        den = jnp.sum(pr, axis=-1, keepdims=True) + jnp.exp(sink - m)
        prb = pr.astype(BF16)
        for a in range(2):
            ov.append(_dot(prb, vall[a].astype(BF16)))
        o = jnp.where(first_seq, ov[0], ov[1]) / den
        og = jnp.where(lo, o[0:GROUP * pair, :], o[GROUP * pair:, :])
        for g in range(GROUP):
            ybbuf[pl.ds(r0, pair), g * nkv:(g + 1) * nkv] = og[g * pair:(g + 1) * pair, :]
        return carry

    lax.fori_loop(0, nblk // 2, body, 0)

    @pl.when(i == pl.num_programs(0) - 1)
    def _():
        out = _dot(yabuf[...], wout_ref[0:aw, :]) + _dot(ybbuf[...].astype(BF16), wout_ref[aw:, :])
        y_ref[...] = x_ref[...] + _rms(out, gpost_ref[...])


def _mix_ab_sample(x, conv_prev, kcache, vcache, seg, gpre, gpost, win, cw, sinks, wout):
    rows, d = x.shape
    nseq = rows // seg
    aw = cw.shape[1]
    nq = GROUP * KV_HEADS * HEAD_DIM
    nkv = KV_HEADS * HEAD_DIM
    sb = SAMPLE_SEQ_BLOCK
    pair = 2 * seg
    qrows = KV_HEADS * GROUP * pair
    tk = WINDOW + pair
    full = lambda shape: pl.BlockSpec(shape, lambda i: (0,) * len(shape))
    cache = pl.BlockSpec((sb, WINDOW, nkv), lambda i: (i, 0, 0))
    prev0, prev1 = conv_prev[:, 0], conv_prev[:, 1]
    y, conv0, conv1, k_new, v_new = pl.pallas_call(
        functools.partial(_mix_ab_sample_kernel, seg),
        grid=(nseq // sb,),
        in_specs=[_resident(x.shape), _resident(prev0.shape), _resident(prev1.shape), cache, cache,
                  _resident(gpre.shape), _resident(gpost.shape), _resident(win.shape), _resident(cw.shape),
                  _resident(sinks.shape), _resident(wout.shape)],
        out_specs=[full((rows, d)), full((nseq, aw)), full((nseq, aw)), cache, cache],
        out_shape=[jax.ShapeDtypeStruct((rows, d), F32), jax.ShapeDtypeStruct((nseq, aw), F32),
                   jax.ShapeDtypeStruct((nseq, aw), F32), jax.ShapeDtypeStruct(kcache.shape, F32),
                   jax.ShapeDtypeStruct(vcache.shape, F32)],
        scratch_shapes=[pltpu.VMEM((aw // LANES, rows + SUBLANES, LANES), F32),
                        pltpu.VMEM((aw // LANES, rows, LANES), F32),
                        pltpu.VMEM((aw // LANES, rows, LANES), F32),
                        pltpu.VMEM((rows, nq), F32),
                        pltpu.VMEM((rows, nkv), F32),
                        pltpu.VMEM((rows, nkv), F32),
                        pltpu.VMEM((rows, aw), BF16),
                        pltpu.VMEM((rows, nq), F32),
                        pltpu.VMEM((2, tk, nkv), F32),
                        pltpu.VMEM((2, tk, nkv), F32),
                        pltpu.VMEM((qrows, tk), F32),
                        pltpu.VMEM((qrows, 1), F32)],
        compiler_params=_params(("arbitrary",)),
        name="mix_ab_sample",
    )(x, prev0, prev1, kcache, vcache, gpre, gpost, win, cw, sinks, wout)
    return y, jnp.stack([conv0, conv1], axis=1), k_new, v_new


def _hgrn_sample_kernel(layer, seg, x_ref, s0_ref, gpre_ref, gpost_ref, win_ref, lbl_ref, ng_ref, wout_ref,
                        y_ref, s1_ref, qb, kb, vb, gb, ob, mixbuf):
    i = pl.program_id(0)
    rows = x_ref.shape[0]
    dkt = C_HEADS * C_DK
    pair = 2 * seg

    @pl.when(i == 0)
    def _():
        x = x_ref[...]
        h = _rms(x, gpre_ref[...]).astype(BF16)
        f = _dot(h, win_ref[:, dkt:2 * dkt])
        log2_ft, kb[...] = _forget_gate(f, _lower_bound_terms(lbl_ref, layer))
        hi, lo = _split_bf16(log2_ft)
        ri = lax.broadcasted_iota(jnp.int32, (rows, rows), 0)
        ci = lax.broadcasted_iota(jnp.int32, (rows, rows), 1)
        tri = jnp.where((ri // seg == ci // seg) & (ri >= ci), 1.0, 0.0).astype(BF16)
        gb[...] = _dot(tri, hi) + _dot(tri, lo)
        qb[...] = _dot(h, win_ref[:, 0:dkt])
        vb[...] = _dot(h, win_ref[:, 2 * dkt:3 * dkt])

    nblk = s0_ref.shape[0]
    rr = lax.broadcasted_iota(jnp.int32, (pair, 1), 0)
    first_seq = rr < seg
    second_seq = rr >= seg
    ri = lax.broadcasted_iota(jnp.int32, (pair, pair), 0)
    ci = lax.broadcasted_iota(jnp.int32, (pair, pair), 1)
    key_s = [(ri // seg == ci // seg) & (ci % seg == sk) & (ri % seg >= sk) for sk in range(seg)]

    def body(p, carry):
        r0 = pl.multiple_of((i * nblk + 2 * p) * seg, pair)
        for hh in range(C_HEADS):
            sl = slice(hh * C_DK, (hh + 1) * C_DK)
            gh = gb[pl.ds(r0, pair), sl]
            g_row = lambda t: jnp.where(first_seq, gh[t:t + 1, :], gh[seg + t:seg + t + 1, :])
            gl = g_row(seg - 1)
            qh = qb[pl.ds(r0, pair), sl]
            kh = kb[pl.ds(r0, pair), sl]
            vh = vb[pl.ds(r0, pair), sl].astype(BF16)
            q_rel = [(qh * jnp.exp2(gh - g_row(sk))).astype(BF16) for sk in range(seg - 1)] + [qh.astype(BF16)]
            rel = _dot_nt(jnp.concatenate(q_rel, axis=0), kh.astype(BF16))
            att = jnp.where(key_s[0], rel[0:pair, :], 0.0)
            for sk in range(1, seg):
                att = att + jnp.where(key_s[sk], rel[sk * pair:(sk + 1) * pair, :], 0.0)
            att = att.astype(BF16)
            qi = (qh * jnp.exp2(gh)).astype(BF16)
            kl = kh * jnp.exp2(gl - gh)
            o_prev = []
            for a in range(2):
                s_a = s0_ref[2 * p + a, hh]
                o_prev.append(_dot(qi, s_a.astype(BF16)))
                kl_a = jnp.where(first_seq if a == 0 else second_seq, kl, 0.0).astype(BF16)
                e = jnp.exp2(gh[(a + 1) * seg - 1:(a + 1) * seg, :])
                e1 = e.astype(BF16).astype(F32)
                e2 = (e - e1).astype(BF16).astype(F32)
                e3 = (e - e1 - e2).astype(BF16).astype(F32)
                dec_l = jnp.where(rr == 0, e1, jnp.where(rr == 1, e2, jnp.where(rr == 2, e3, 0.0)))
                dec = _dot_tn(dec_l.astype(BF16), jnp.ones((pair, C_DK), BF16))
                s1_ref[2 * p + a, hh] = dec * s_a + _dot_tn(kl_a, vh)
            ob[pl.ds(r0, pair), sl] = _dot(att, vh) + jnp.where(first_seq, o_prev[0], o_prev[1])
        return carry

    lax.fori_loop(0, nblk // 2, body, 0)

    @pl.when(i == pl.num_programs(0) - 1)
    def _():
        x = x_ref[...]
        h = _rms(x, gpre_ref[...]).astype(BF16)
        y_ref[...] = _hgrn_gate_out(x, h, ob, mixbuf, win_ref, ng_ref, wout_ref, gpost_ref)


def _hgrn_sample(layer, x, s0, seg, gpre, gpost, win, lbl, ng, wout):
    rows, d = x.shape
    assert 2 * seg == SUBLANES, "two sequences per iteration must fill one f32 vreg of sublanes"
    nseq = rows // seg
    dkt = C_HEADS * C_DK
    sb = SAMPLE_SEQ_BLOCK
    full = lambda shape: pl.BlockSpec(shape, lambda i: (0,) * len(shape))
    state = pl.BlockSpec((sb, C_HEADS, C_DK, C_DK), lambda i: (i, 0, 0, 0))
    return pl.pallas_call(
        functools.partial(_hgrn_sample_kernel, layer, seg),
        grid=(nseq // sb,),
        in_specs=[_resident(x.shape), state, _resident(gpre.shape), _resident(gpost.shape), _resident(win.shape),
                  _resident(lbl.shape), _resident(ng.shape), _resident(wout.shape)],
        out_specs=[full((rows, d)), state],
        out_shape=[jax.ShapeDtypeStruct((rows, d), F32), jax.ShapeDtypeStruct(s0.shape, F32)],
        scratch_shapes=[pltpu.VMEM((rows, dkt), F32), pltpu.VMEM((rows, dkt), F32),
                        pltpu.VMEM((rows, dkt), F32), pltpu.VMEM((rows, dkt), F32),
                        pltpu.VMEM((rows, dkt), F32), pltpu.VMEM((rows, dkt), BF16)],
        compiler_params=_params(("arbitrary",)),
        name="hgrn_sample",
    )(x, s0, gpre, gpost, win, lbl, ng, wout)


def _head_permutation():
    cols = []
    for g in range(GROUP):
        for kv in range(KV_HEADS):
            hidx = kv * GROUP + g
            cols.extend(range(hidx * HEAD_DIM, (hidx + 1) * HEAD_DIM))
    return jnp.asarray(cols, dtype=jnp.int32)


def kernel(x_prompt, x_sample, state_conv_a, cache_swa_k, cache_swa_v, state_hgrn, state_ffn_conv, norm_mix_pre, norm_mix_post, norm_ffn_pre, norm_ffn_post, w_in_ab, conv_a_w, attn_sinks, w_out_ab, w_in_c, hgrn_lb_logits, hgrn_norm_g, w_out_c, w_ffn_up, ffn_conv_w, w_ffn_down):
    bp, seq, d = x_prompt.shape
    ns, seg, _ = x_sample.shape
    aw = conv_a_w.shape[-1]
    nq = GROUP * KV_HEADS * HEAD_DIM
    nkv = KV_HEADS * HEAD_DIM

    perm = _head_permutation()
    q0 = 3 * aw
    win_ab = w_in_ab[0]
    win_ab = jnp.concatenate([win_ab[:, :q0], win_ab[:, q0:q0 + nq][:, perm], win_ab[:, q0 + nq:]], axis=1).astype(BF16)
    wout_ab = w_out_ab[0]
    wout_ab = jnp.concatenate([wout_ab[:aw], wout_ab[aw:][perm]], axis=0).astype(BF16)
    win_c = w_in_c[0].astype(BF16)
    wout_c = w_out_c[0].astype(BF16)
    wup = w_ffn_up.astype(BF16)
    wdown = w_ffn_down.astype(BF16)
    row = lambda a: a.reshape(1, -1)
    sinks = row(attn_sinks[0])
    ng = row(hgrn_norm_g[0])

    x1, conv_p, k_p, v_p = _mix_ab_prompt(x_prompt, row(norm_mix_pre[0]), row(norm_mix_post[0]), win_ab,
                                          conv_a_w[0], sinks, wout_ab)
    x2, ffn_p0 = _ffn_prompt(0, x1, row(norm_ffn_pre[0]), row(norm_ffn_post[0]), wup, ffn_conv_w[0], wdown)
    x3, hgrn_p = _hgrn_prompt(1, x2, row(norm_mix_pre[1]), row(norm_mix_post[1]), win_c, hgrn_lb_logits,
                              ng, wout_c)
    y_prompt, ffn_p1 = _ffn_prompt(1, x3, row(norm_ffn_pre[1]), row(norm_ffn_post[1]), wup, ffn_conv_w[1], wdown)

    xs = x_sample.reshape(ns * seg, d)
    kc = cache_swa_k[0].reshape(ns, WINDOW, nkv)
    vc = cache_swa_v[0].reshape(ns, WINDOW, nkv)
    s1, conv_s, k_s, v_s = _mix_ab_sample(xs, state_conv_a[0], kc, vc, seg, row(norm_mix_pre[0]),
                                          row(norm_mix_post[0]), win_ab, conv_a_w[0], sinks, wout_ab)
    s2, ffn_s0 = _ffn_sample(0, s1, state_ffn_conv[0], seg, row(norm_ffn_pre[0]), row(norm_ffn_post[0]),
                             wup, ffn_conv_w[0], wdown)
    s3, hgrn_s = _hgrn_sample(1, s2, state_hgrn[0], seg, row(norm_mix_pre[1]), row(norm_mix_post[1]),
                              win_c, hgrn_lb_logits, ng, wout_c)
    y_sample, ffn_s1 = _ffn_sample(1, s3, state_ffn_conv[1], seg, row(norm_ffn_pre[1]), row(norm_ffn_post[1]),
                                   wup, ffn_conv_w[1], wdown)

    kvshape = (KV_HEADS, HEAD_DIM)
    return (y_prompt, y_sample.reshape(ns, seg, d),
            conv_p[None], conv_s[None],
            k_p.reshape(1, bp, WINDOW, *kvshape), k_s.reshape(1, ns, WINDOW, *kvshape),
            v_p.reshape(1, bp, WINDOW, *kvshape), v_s.reshape(1, ns, WINDOW, *kvshape),
            hgrn_p[None], hgrn_s[None],
            jnp.stack([ffn_p0, ffn_p1]), jnp.stack([ffn_s0, ffn_s1]))
```

```python
import functools
import math

import jax
import jax.numpy as jnp
from jax import lax
from jax.experimental import pallas as pl
from jax.experimental.pallas import tpu as pltpu

F32 = jnp.float32
BF16 = jnp.bfloat16

EPS = 1e-6
WINDOW = 128
KV_HEADS = 2
GROUP = 4
HEAD_DIM = 64
C_HEADS = 8
C_DK = 128
NEG = -1e30
SUBLANES = 8
LANES = 128
MIX_TILE = 1024
HGRN_TILE = 512
FFN_TILE = 1024
HGRN_CHUNK = 64
DECAY_LOG2_LIMIT = 100.0
FFN_CHUNK = 256
SAMPLE_SEQ_BLOCK = 8
CAST_ROWS = 256
VMEM_LIMIT_V7X = 56 * 2**20


def _dot(a, b):
    return jnp.dot(a, b, preferred_element_type=F32)


def _dot_nt(a, b):
    return lax.dot_general(a, b, (((1,), (1,)), ((), ())), preferred_element_type=F32)


def _dot_tn(a, b):
    return lax.dot_general(a, b, (((0,), (0,)), ((), ())), preferred_element_type=F32)


def _rms(x, g):
    return x * lax.rsqrt(jnp.mean(x * x, axis=-1, keepdims=True) + EPS) * g


def _gelu_tanh(x):
    c = math.sqrt(2.0 / math.pi)
    return x * (0.5 * (1.0 + jnp.tanh(c * (x + 0.044715 * (x * x * x)))))


def _sigmoid(x):
    return 1.0 / (1.0 + jnp.exp(-x))


def _resident(shape):
    nd = len(shape)
    return pl.BlockSpec(shape, lambda *_: (0,) * nd, pipeline_mode=pl.Buffered(1))


def _params(sem):
    return pltpu.CompilerParams(dimension_semantics=sem, vmem_limit_bytes=VMEM_LIMIT_V7X)


def _head_slope(h):
    return 2.0 ** (-8.0 * (h + 1) / (KV_HEADS * GROUP))


def _mix_ab_prompt_kernel(x_ref, gpre_ref, gpost_ref, win_ref, cw_ref, sink_ref, wout_ref,
                          y_ref, conv_ref, kout_ref, vout_ref,
                          ubuf, qbuf, kbuf, vbuf, mixbuf, bias_s):
    s = pl.program_id(1)
    ts = x_ref.shape[0]
    aw = cw_ref.shape[1]
    nq = GROUP * KV_HEADS * HEAD_DIM
    nkv = KV_HEADS * HEAD_DIM

    @pl.when(s == 0)
    def _():
        ubuf[0:SUBLANES, :] = jnp.zeros((SUBLANES, aw), F32)
        kbuf[0:WINDOW, :] = jnp.zeros((WINDOW, nkv), BF16)
        vbuf[0:WINDOW, 0:nkv] = jnp.zeros((WINDOW, nkv), BF16)
        vbuf[:, nkv:2 * nkv] = jnp.ones((ts + WINDOW, nkv), BF16)

    x = x_ref[...]
    h = _rms(x, gpre_ref[...]).astype(BF16)

    u = _dot(h, win_ref[:, 2 * aw:3 * aw]) * _dot(h, win_ref[:, 0:aw])
    ubuf[SUBLANES:SUBLANES + ts, :] = u
    cw = cw_ref[...]
    conv = (cw[2:3, :] * u + cw[1:2, :] * ubuf[SUBLANES - 1:SUBLANES - 1 + ts, :]
            + cw[0:1, :] * ubuf[SUBLANES - 2:SUBLANES - 2 + ts, :])
    mixbuf[:, 0:aw] = (_dot(h, win_ref[:, aw:2 * aw]) * conv).astype(BF16)
    tail = ubuf[ts + SUBLANES - 2:ts + SUBLANES, :]
    conv_ref[...] = tail
    ubuf[SUBLANES - 2:SUBLANES, :] = tail

    q0 = 3 * aw
    q = _dot(h, win_ref[:, q0:q0 + nq]) * (HEAD_DIM ** -0.5 * LOG2E)
    lane = lax.broadcasted_iota(jnp.int32, (1, nkv), 1)
    lo = lane < HEAD_DIM
    nblk = ts // WINDOW
    for n in range(nblk):
        for g in range(GROUP):
            qg = q[n * WINDOW:(n + 1) * WINDOW, g * nkv:(g + 1) * nkv]
            qbuf[n, g * WINDOW:(g + 1) * WINDOW, :] = jnp.where(lo, qg, 0.0).astype(BF16)
            qbuf[n, (GROUP + g) * WINDOW:(GROUP + g + 1) * WINDOW, :] = jnp.where(lo, 0.0, qg).astype(BF16)
    wkv = win_ref[:, q0 + nq:q0 + nq + 2 * nkv]
    kv_new = jnp.concatenate([_dot(h[0:ts // 2, :], wkv), _dot(h[ts // 2:, :], wkv)], axis=0)
    k = kv_new[:, 0:nkv]
    v = kv_new[:, nkv:2 * nkv]
    kbuf[WINDOW:WINDOW + ts, :] = k.astype(BF16)
    vbuf[WINDOW:WINDOW + ts, 0:nkv] = v.astype(BF16)
    kout_ref[...] = k[ts - WINDOW:, :]
    vout_ref[...] = v[ts - WINDOW:, :]

    qi = lax.broadcasted_iota(jnp.int32, (WINDOW, 2 * WINDOW), 0)
    kj = lax.broadcasted_iota(jnp.int32, (WINDOW, 2 * WINDOW), 1)
    dist = qi + WINDOW - kj
    valid = (dist >= 0) & (dist < WINDOW)
    negdist = jnp.where(valid, -dist.astype(F32), 0.0)
    maskadd = jnp.where(valid, 0.0, NEG)
    nheads = KV_HEADS * GROUP
    for hidx in range(nheads):
        bias_s[hidx] = (_head_slope(hidx) * LOG2E) * negdist + maskadd
    first = jnp.where(kj < WINDOW, jnp.where(s == 0, NEG, 0.0), 0.0)

    for n in range(nblk):
        r0 = n * WINDOW
        half = GROUP * WINDOW
        keys = kbuf[r0:r0 + 2 * WINDOW, :]
        sc_kv = [_dot_nt(qbuf[n, kv * half:(kv + 1) * half, :], keys) for kv in range(KV_HEADS)]
        ps, sink_terms = [], []
        for hidx in range(nheads):
            g = hidx % GROUP
            sc = sc_kv[hidx // GROUP][g * WINDOW:(g + 1) * WINDOW, :] + bias_s[hidx]
            if n == 0:
                sc = sc + first
            sink = sink_ref[hidx] * LOG2E
            m = jnp.maximum(jnp.max(sc, axis=-1, keepdims=True), sink)
            ps.append(jnp.exp2(sc - m).astype(BF16))
            sink_terms.append(jnp.exp2(sink - m))
        vals = vbuf[r0:r0 + 2 * WINDOW, :]
        pv_kv = [_dot(jnp.concatenate(ps[kv * GROUP:(kv + 1) * GROUP], axis=0), vals) for kv in range(KV_HEADS)]
        outs = []
        for hidx in range(nheads):
            g = hidx % GROUP
            pv_h = pv_kv[hidx // GROUP][g * WINDOW:(g + 1) * WINDOW, :]
            outs.append(pv_h[:, 0:nkv] / (pv_h[:, nkv:2 * nkv] + sink_terms[hidx]))
        for g in range(GROUP):
            og = jnp.where(lo, outs[g], outs[GROUP + g])
            mixbuf[r0:r0 + WINDOW, aw + g * nkv:aw + (g + 1) * nkv] = og.astype(BF16)

    kbuf[0:WINDOW, :] = kbuf[ts:ts + WINDOW, :]
    vbuf[0:WINDOW, :] = vbuf[ts:ts + WINDOW, :]

    out = _dot(mixbuf[...], wout_ref[...])
    y_ref[...] = x + _rms(out, gpost_ref[...])


def _mix_ab_prompt(x, gpre, gpost, win, cw, sinks, wout):
    b, seq, d = x.shape
    ts = MIX_TILE
    assert seq % ts == 0 and ts % WINDOW == 0, (seq, ts)
    aw = cw.shape[1]
    nq = GROUP * KV_HEADS * HEAD_DIM
    nkv = KV_HEADS * HEAD_DIM
    tile = pl.BlockSpec((None, ts, d), lambda i, j: (i, j, 0))
    return pl.pallas_call(
        _mix_ab_prompt_kernel,
        grid=(b, seq // ts),
        in_specs=[tile, _resident(gpre.shape), _resident(gpost.shape), _resident(win.shape),
                  _resident(cw.shape), pl.BlockSpec(memory_space=pltpu.SMEM), _resident(wout.shape)],
        out_specs=[tile,
                   pl.BlockSpec((None, 2, aw), lambda i, j: (i, 0, 0)),
                   pl.BlockSpec((None, WINDOW, nkv), lambda i, j: (i, 0, 0)),
                   pl.BlockSpec((None, WINDOW, nkv), lambda i, j: (i, 0, 0))],
        out_shape=[jax.ShapeDtypeStruct((b, seq, d), F32),
                   jax.ShapeDtypeStruct((b, 2, aw), F32),
                   jax.ShapeDtypeStruct((b, WINDOW, nkv), F32),
                   jax.ShapeDtypeStruct((b, WINDOW, nkv), F32)],
        scratch_shapes=[pltpu.VMEM((ts + SUBLANES, aw), F32),
                        pltpu.VMEM((ts // WINDOW, KV_HEADS * GROUP * WINDOW, nkv), BF16),
                        pltpu.VMEM((ts + WINDOW, nkv), BF16),
                        pltpu.VMEM((ts + WINDOW, 2 * nkv), BF16),
                        pltpu.VMEM((ts, d), BF16),
                        pltpu.VMEM((KV_HEADS * GROUP, WINDOW, 2 * WINDOW), F32)],
        compiler_params=_params(("arbitrary", "arbitrary")),
        name="mix_ab_prompt",
    )(x, gpre, gpost, win, cw, sinks.reshape(-1), wout)


def _ffn_prompt_kernel(x_ref, gpre_ref, gpost_ref, wup_ref, cw_ref, wdown_ref,
                       y_ref, st_ref, abuf, carry, gbuf):
    s = pl.program_id(1)
    rows = x_ref.shape[0]
    ff = cw_ref.shape[1]

    @pl.when(s == 0)
    def _():
        carry[...] = jnp.zeros(carry.shape, F32)

    x = x_ref[...]
    h = _rms(x, gpre_ref[...]).astype(BF16)
    for c in range(ff // FFN_CHUNK):
        c0 = c * FFN_CHUNK
        buf = abuf.at[c % 2]
        a = _dot(h, wup_ref[:, c0:c0 + FFN_CHUNK])
        b = _dot(h, wup_ref[:, ff + c0:ff + c0 + FFN_CHUNK])
        buf[0:SUBLANES, :] = carry[:, c0:c0 + FFN_CHUNK]
        buf[SUBLANES:SUBLANES + rows, :] = a
        cw = cw_ref[:, c0:c0 + FFN_CHUNK]
        conv = (cw[2:3, :] * a + cw[1:2, :] * buf[SUBLANES - 1:SUBLANES - 1 + rows, :]
                + cw[0:1, :] * buf[SUBLANES - 2:SUBLANES - 2 + rows, :])
        tail = a[rows - 2:rows, :]
        st_ref[:, c0:c0 + FFN_CHUNK] = tail
        carry[SUBLANES - 2:SUBLANES, c0:c0 + FFN_CHUNK] = tail
        gbuf[:, c0:c0 + FFN_CHUNK] = (_gelu_tanh(conv) * b).astype(BF16)
    out = _dot(gbuf[...], wdown_ref[...])
    y_ref[...] = x + _rms(out, gpost_ref[...])


def _ffn_prompt(layer, x, gpre, gpost, wup, cw, wdown):
    b, seq, d = x.shape
    ts = FFN_TILE
    ff = cw.shape[1]
    assert seq % ts == 0 and ff % FFN_CHUNK == 0, (seq, ts, ff)
    tile = pl.BlockSpec((None, ts, d), lambda i, j: (i, j, 0))
    layer_block = lambda w: pl.BlockSpec((None,) + w.shape[1:], lambda i, j: (layer, 0, 0),
                                         pipeline_mode=pl.Buffered(1))
    return pl.pallas_call(
        _ffn_prompt_kernel,
        grid=(b, seq // ts),
        in_specs=[tile, _resident(gpre.shape), _resident(gpost.shape), layer_block(wup),
                  _resident(cw.shape), layer_block(wdown)],
        out_specs=[tile, pl.BlockSpec((None, 2, ff), lambda i, j: (i, 0, 0))],
        out_shape=[jax.ShapeDtypeStruct((b, seq, d), F32), jax.ShapeDtypeStruct((b, 2, ff), F32)],
        scratch_shapes=[pltpu.VMEM((2, ts + SUBLANES, FFN_CHUNK), F32), pltpu.VMEM((SUBLANES, ff), F32),
                        pltpu.VMEM((ts, ff), BF16)],
        compiler_params=_params(("arbitrary", "arbitrary")),
        name="ffn_prompt",
    )(x, gpre, gpost, wup, cw, wdown)


def _segment_conv_taps(a, t, seg, prev0_ref, prev1_ref, abuf, fix1, fix2):
    rows, c = a.shape
    nseq = rows // seg
    am1, am2 = [], []
    for k in range(c // LANES):
        sl = slice(k * LANES, (k + 1) * LANES)
        abuf[k, SUBLANES:SUBLANES + rows, :] = a[:, sl]
        fix1[k, pl.ds(0, nseq, stride=seg), :] = prev1_ref[:, sl]
        fix2[k, pl.ds(0, nseq, stride=seg), :] = prev0_ref[:, sl]
        fix2[k, pl.ds(1, nseq, stride=seg), :] = prev1_ref[:, sl]
        am1.append(jnp.where(t >= 1, abuf[k, SUBLANES - 1:SUBLANES - 1 + rows, :], fix1[k]))
        am2.append(jnp.where(t >= 2, abuf[k, SUBLANES - 2:SUBLANES - 2 + rows, :], fix2[k]))
    return jnp.concatenate(am1, axis=1), jnp.concatenate(am2, axis=1)


def _segment_tails(abuf, nseq, seg):
    tail = lambda back: jnp.concatenate(
        [abuf[k, pl.ds(SUBLANES + seg - back, nseq, stride=seg), :] for k in range(abuf.shape[0])], axis=1)
    return tail(2), tail(1)


def _ffn_sample_kernel(seg, x_ref, prev0_ref, prev1_ref, gpre_ref, gpost_ref, wa_ref, wb_ref, cw_ref, wdown_ref,
                       y_ref, st0_ref, st1_ref, hbuf, abuf, fix1, fix2, acc):
    c = pl.program_id(0)
    rows = x_ref.shape[0]

    @pl.when(c == 0)
    def _():
        hbuf[...] = _rms(x_ref[...], gpre_ref[...]).astype(BF16)
        acc[...] = jnp.zeros(acc.shape, F32)
        abuf[:, 0:SUBLANES, :] = jnp.zeros((abuf.shape[0], SUBLANES, LANES), F32)
        fix1[...] = jnp.zeros(fix1.shape, F32)
        fix2[...] = jnp.zeros(fix2.shape, F32)

    h = hbuf[...]
    t = lax.broadcasted_iota(jnp.int32, (rows, 1), 0) % seg
    a = _dot(h, wa_ref[...])
    b = _dot(h, wb_ref[...])
    am1, am2 = _segment_conv_taps(a, t, seg, prev0_ref, prev1_ref, abuf, fix1, fix2)
    st0_ref[...], st1_ref[...] = _segment_tails(abuf, rows // seg, seg)
    cw = cw_ref[...]
    conv = cw[2:3, :] * a + cw[1:2, :] * am1 + cw[0:1, :] * am2
    acc[...] += _dot((_gelu_tanh(conv) * b).astype(BF16), wdown_ref[...])

    @pl.when(c == pl.num_programs(0) - 1)
    def _():
        y_ref[...] = x_ref[...] + _rms(acc[...], gpost_ref[...])


def _ffn_sample(layer, x, prev, seg, gpre, gpost, wup, cw, wdown):
    rows, d = x.shape
    nseq = rows // seg
    ff = cw.shape[1]
    cf = FFN_CHUNK
    nc = ff // cf
    cols = lambda r: pl.BlockSpec((r, cf), lambda c: (0, c))
    y, st0, st1 = pl.pallas_call(
        functools.partial(_ffn_sample_kernel, seg),
        grid=(nc,),
        in_specs=[_resident(x.shape), cols(nseq), cols(nseq), _resident(gpre.shape), _resident(gpost.shape),
                  pl.BlockSpec((None, d, cf), lambda c: (layer, 0, c)),
                  pl.BlockSpec((None, d, cf), lambda c: (layer, 0, c + nc)), cols(3),
                  pl.BlockSpec((None, cf, d), lambda c: (layer, c, 0))],
        out_specs=[pl.BlockSpec((rows, d), lambda c: (0, 0)), cols(nseq), cols(nseq)],
        out_shape=[jax.ShapeDtypeStruct((rows, d), F32), jax.ShapeDtypeStruct((nseq, ff), F32),
                   jax.ShapeDtypeStruct((nseq, ff), F32)],
        scratch_shapes=[pltpu.VMEM((rows, d), BF16), pltpu.VMEM((cf // LANES, rows + SUBLANES, LANES), F32),
                        pltpu.VMEM((cf // LANES, rows, LANES), F32), pltpu.VMEM((cf // LANES, rows, LANES), F32),
                        pltpu.VMEM((rows, d), F32)],
        compiler_params=_params(("arbitrary",)),
        name="ffn_sample",
    )(x, prev[:, 0], prev[:, 1], gpre, gpost, wup, wup, cw, wdown)
    return y, jnp.stack([st0, st1], axis=1)


LOG2E = 1.4426950408889634


def _lower_bound_terms(logits_ref, layer):
    lg = logits_ref[...]
    e = jnp.exp(lg - jnp.max(lg, axis=0, keepdims=True))
    pr = e / jnp.sum(e, axis=0, keepdims=True)
    cum = pr[0:1, :]
    for r in range(1, layer + 1):
        cum = cum + pr[r:r + 1, :]
    lb = jnp.clip(cum - pr[0:1, :], 0.0, 1.0)
    return jnp.log(lb) * LOG2E, jnp.log1p(-lb) * LOG2E, 1.0 - lb


def _forget_gate(f, lb_terms):
    log2_lb, log2_1m_lb, one_m_lb = lb_terms
    f2 = f * LOG2E
    e = jnp.exp2(jnp.minimum(f2, -f2))
    w = 1.0 + e
    b = log2_1m_lb + (jnp.minimum(f2, 0.0) - jnp.log2(w))
    d = log2_lb - b
    log2_ft = jnp.maximum(log2_lb, b) + jnp.log2(1.0 + jnp.exp2(jnp.minimum(d, -d)))
    return log2_ft, one_m_lb * (jnp.where(f2 >= 0.0, e, 1.0) / w)


def _split_bf16(x):
    hi = x.astype(BF16)
    return hi, (x - hi.astype(F32)).astype(BF16)


def _hgrn_gate_out(x, h, ob, mixbuf, win_ref, ng_ref, wout_ref, gpost_ref):
    dkt = C_HEADS * C_DK
    g = _dot(h, win_ref[:, 3 * dkt:4 * dkt])
    ng = ng_ref[...]
    for hh in range(C_HEADS):
        sl = slice(hh * C_DK, (hh + 1) * C_DK)
        mixbuf[:, sl] = (_rms(ob[:, sl], ng) * (g[:, sl] * _sigmoid(g[:, sl]))).astype(BF16)
    out = _dot(mixbuf[...], wout_ref[...])
    return x + _rms(out, gpost_ref[...])


def _hgrn_exact_tile(q, k_in, log2_ft, vb, st, ob, sq, sk, slg):
    ts = sq.shape[0]
    sq[...] = q
    sk[...] = k_in
    slg[...] = log2_ft
    ob[...] = jnp.zeros(ob.shape, F32)
    grp = 2 * SUBLANES
    row = lax.broadcasted_iota(jnp.int32, (grp, 1), 0)

    def body(t, carry):
        t0 = pl.multiple_of(lax.shift_left(lax.shift_right_logical(t, 4), 4), grp)
        sel = row == lax.bitwise_and(t, grp - 1)
        for hh in range(C_HEADS):
            sl = slice(hh * C_DK, (hh + 1) * C_DK)
            rows_of = lambda ref: jnp.where(sel, ref[pl.ds(t0, grp), sl].astype(F32), 0.0)
            pick = lambda ref: rows_of(ref).astype(BF16)
            log2_f = jnp.sum(rows_of(slg), axis=0, keepdims=True)
            s_new = st[hh] * jnp.exp2(log2_f) + _dot_tn(pick(vb), pick(sk))
            st[hh] = s_new
            ob[pl.ds(t0, grp), sl] += _dot_nt(pick(sq), s_new.astype(BF16))
        return carry

    lax.fori_loop(0, ts, body, 0)


def _hgrn_prompt_kernel(layer, x_ref, gpre_ref, gpost_ref, win_ref, lbl_ref, ng_ref, wout_ref,
                        y_ref, sout_ref, st, st0, qtb, ktb, qib, klb, eglb, vb, ob, mixbuf, sq, sk, slg):
    s = pl.program_id(1)
    ts = x_ref.shape[0]
    dkt = C_HEADS * C_DK
    ck = HGRN_CHUNK

    @pl.when(s == 0)
    def _():
        st[...] = jnp.zeros(st.shape, F32)

    x = x_ref[...]
    h = _rms(x, gpre_ref[...]).astype(BF16)
    f = _dot(h, win_ref[:, dkt:2 * dkt])
    log2_ft, k_in = _forget_gate(f, _lower_bound_terms(lbl_ref, layer))
    q = _dot(h, win_ref[:, 0:dkt])
    vb[...] = _dot(h, win_ref[:, 2 * dkt:3 * dkt]).astype(BF16)

    ri = lax.broadcasted_iota(jnp.int32, (ck, ck), 0)
    ci = lax.broadcasted_iota(jnp.int32, (ck, ck), 1)
    causal = ri >= ci
    tri = jnp.where(causal, 1.0, 0.0).astype(BF16)
    mid = ck // 2 - 1
    nck = ts // ck

    spread = jnp.zeros((1, dkt), F32)
    for c in range(nck):
        rows = slice(c * ck, (c + 1) * ck)
        hi, lo = _split_bf16(log2_ft[rows, :])
        gcum = _dot(tri, hi) + _dot(tri, lo)
        gr = gcum[mid:mid + 1, :]
        gl = gcum[ck - 1:ck, :]
        spread = jnp.maximum(spread, jnp.maximum(gcum[0:1, :] - gr, gr - gl))
        qc = q[rows, :]
        kc = k_in[rows, :]
        qtb[rows, :] = (qc * jnp.exp2(gcum - gr)).astype(BF16)
        ktb[rows, :] = (kc * jnp.exp2(gr - gcum)).astype(BF16)
        qib[rows, :] = (qc * jnp.exp2(gcum)).astype(BF16)
        klb[rows, :] = (kc * jnp.exp2(gl - gcum)).astype(BF16)
        eglb[c] = jnp.exp2(gl)

    st0[...] = st[...]
    for c in range(nck):
        rows = slice(c * ck, (c + 1) * ck)
        for hh in range(C_HEADS):
            sl = slice(hh * C_DK, (hh + 1) * C_DK)
            vh = vb[rows, sl]
            att = jnp.where(causal, _dot_nt(qtb[rows, sl], ktb[rows, sl]), 0.0).astype(BF16)
            st_h = st[hh]
            ob[rows, sl] = _dot(jnp.concatenate([qib[rows, sl], att], axis=1),
                                jnp.concatenate([st_h.astype(BF16).T, vh], axis=0))
            st[hh] = eglb[c, :, sl] * st_h + _dot_tn(vh, klb[rows, sl])

    y_ref[...] = _hgrn_gate_out(x, h, ob, mixbuf, win_ref, ng_ref, wout_ref, gpost_ref)

    @pl.when(jnp.max(spread) > DECAY_LOG2_LIMIT)
    def _():
        st[...] = st0[...]
        _hgrn_exact_tile(q, k_in, log2_ft, vb, st, ob, sq, sk, slg)
        y_ref[...] = _hgrn_gate_out(x, h, ob, mixbuf, win_ref, ng_ref, wout_ref, gpost_ref)

    @pl.when(s == pl.num_programs(1) - 1)
    def _():
        for hh in range(C_HEADS):
            sout_ref[hh] = st[hh].T


def _hgrn_prompt(layer, x, gpre, gpost, win, lbl, ng, wout):
    b, seq, d = x.shape
    ts = HGRN_TILE
    assert seq % ts == 0 and ts % HGRN_CHUNK == 0, (seq, ts)
    dkt = C_HEADS * C_DK
    tile = pl.BlockSpec((None, ts, d), lambda i, j: (i, j, 0))
    return pl.pallas_call(
        functools.partial(_hgrn_prompt_kernel, layer),
        grid=(b, seq // ts),
        in_specs=[tile, _resident(gpre.shape), _resident(gpost.shape), _resident(win.shape),
                  _resident(lbl.shape), _resident(ng.shape), _resident(wout.shape)],
        out_specs=[tile, pl.BlockSpec((None, C_HEADS, C_DK, C_DK), lambda i, j: (i, 0, 0, 0))],
        out_shape=[jax.ShapeDtypeStruct((b, seq, d), F32),
                   jax.ShapeDtypeStruct((b, C_HEADS, C_DK, C_DK), F32)],
        scratch_shapes=[pltpu.VMEM((C_HEADS, C_DK, C_DK), F32), pltpu.VMEM((C_HEADS, C_DK, C_DK), F32),
                        pltpu.VMEM((ts, dkt), BF16), pltpu.VMEM((ts, dkt), BF16),
                        pltpu.VMEM((ts, dkt), BF16), pltpu.VMEM((ts, dkt), BF16),
                        pltpu.VMEM((ts // HGRN_CHUNK, 1, dkt), F32),
                        pltpu.VMEM((ts, dkt), BF16), pltpu.VMEM((ts, dkt), F32),
                        pltpu.VMEM((ts, dkt), BF16),
                        pltpu.VMEM((ts, dkt), F32), pltpu.VMEM((ts, dkt), F32), pltpu.VMEM((ts, dkt), F32)],
        compiler_params=_params(("arbitrary", "arbitrary")),
        name="hgrn_prompt",
    )(x, gpre, gpost, win, lbl, ng, wout)


def _mix_ab_sample_kernel(seg, x_ref, prev0_ref, prev1_ref, kc_ref, vc_ref, gpre_ref, gpost_ref, win_ref,
                          cw_ref, sink_ref, wout_ref,
                          y_ref, conv0_ref, conv1_ref, kout_ref, vout_ref,
                          ubuf, fix1, fix2, qbuf, knew_s, vnew_s, yabuf, ybbuf, kall, vall, bias_s, sink_s):
    i = pl.program_id(0)
    rows = x_ref.shape[0]
    aw = cw_ref.shape[1]
    nq = GROUP * KV_HEADS * HEAD_DIM
    nkv = KV_HEADS * HEAD_DIM
    pair = 2 * seg
    qrows = KV_HEADS * GROUP * pair
    tk = WINDOW + pair

    @pl.when(i == 0)
    def _():
        x = x_ref[...]
        h = _rms(x, gpre_ref[...]).astype(BF16)
        t = lax.broadcasted_iota(jnp.int32, (rows, 1), 0) % seg
        u = _dot(h, win_ref[:, 2 * aw:3 * aw]) * _dot(h, win_ref[:, 0:aw])
        ubuf[:, 0:SUBLANES, :] = jnp.zeros((ubuf.shape[0], SUBLANES, LANES), F32)
        fix1[...] = jnp.zeros(fix1.shape, F32)
        fix2[...] = jnp.zeros(fix2.shape, F32)
        um1, um2 = _segment_conv_taps(u, t, seg, prev0_ref, prev1_ref, ubuf, fix1, fix2)
        conv0_ref[...], conv1_ref[...] = _segment_tails(ubuf, rows // seg, seg)
        cw = cw_ref[...]
        conv = cw[2:3, :] * u + cw[1:2, :] * um1 + cw[0:1, :] * um2
        yabuf[...] = (_dot(h, win_ref[:, aw:2 * aw]) * conv).astype(BF16)
        q0 = 3 * aw
        qbuf[...] = _dot(h, win_ref[:, q0:q0 + nq]) * (HEAD_DIM ** -0.5)
        kv_new = _dot(h, win_ref[:, q0 + nq:q0 + nq + 2 * nkv])
        knew_s[...] = kv_new[:, 0:nkv]
        vnew_s[...] = kv_new[:, nkv:2 * nkv]

        r = lax.broadcasted_iota(jnp.int32, (qrows, tk), 0)
        j = lax.broadcasted_iota(jnp.int32, (qrows, tk), 1)
        dist = r % seg + WINDOW - j
        valid = (dist >= 0) & (dist < WINDOW) & (j < WINDOW + seg)
        rc = lax.broadcasted_iota(jnp.int32, (qrows, 1), 0)
        head = (rc // (GROUP * pair)) * GROUP + (rc // pair) % GROUP
        slope = jnp.zeros((qrows, 1), F32)
        sink = jnp.zeros((qrows, 1), F32)
        for hidx in range(KV_HEADS * GROUP):
            slope = jnp.where(head == hidx, _head_slope(hidx), slope)
            sink = jnp.where(head == hidx, sink_ref[:, hidx:hidx + 1], sink)
        bias_s[...] = jnp.where(valid, -slope * dist.astype(F32), NEG)
        sink_s[...] = sink

    nblk = kc_ref.shape[0]
    lane = lax.broadcasted_iota(jnp.int32, (1, nkv), 1)
    lo = lane < HEAD_DIM
    rq = lax.broadcasted_iota(jnp.int32, (qrows, 1), 0)
    first_seq = (rq // seg) % 2 == 0

    def body(p, carry):
        r0 = pl.multiple_of((i * nblk + 2 * p) * seg, pair)
        q8 = qbuf[pl.ds(r0, pair), :]
        qg = jnp.concatenate([q8[:, g * nkv:(g + 1) * nkv] for g in range(GROUP)], axis=0)
        qs = jnp.concatenate([jnp.where(lo, qg, 0.0), jnp.where(lo, 0.0, qg)], axis=0).astype(BF16)
        knew = knew_s[pl.ds(r0, pair), :]
        vnew = vnew_s[pl.ds(r0, pair), :]
        sc, ov = [], []
        for a in range(2):
            own_first = (lambda z: z) if a == 0 else (lambda z: jnp.concatenate([z[seg:], z[:seg]], axis=0))
            kall[a, 0:WINDOW, :] = kc_ref[2 * p + a]
            kall[a, WINDOW:tk, :] = own_first(knew)
            vall[a, 0:WINDOW, :] = vc_ref[2 * p + a]
            vall[a, WINDOW:tk, :] = own_first(vnew)
            kout_ref[2 * p + a] = kall[a, seg:seg + WINDOW, :]
            vout_ref[2 * p + a] = vall[a, seg:seg + WINDOW, :]
            sc.append(_dot_nt(qs, kall[a].astype(BF16)))
        s2 = jnp.where(first_seq, sc[0], sc[1]) + bias_s[...]
        sink = sink_s[...]
        m = jnp.maximum(jnp.max(s2, axis=-1, keepdims=True), sink)
        pr = jnp.exp(s2 - m)
        den = jnp.sum(pr, axis=-1, keepdims=True) + jnp.exp(sink - m)
        prb = pr.astype(BF16)
        for a in range(2):
            ov.append(_dot(prb, vall[a].astype(BF16)))
        o = jnp.where(first_seq, ov[0], ov[1]) / den
        og = jnp.where(lo, o[0:GROUP * pair, :], o[GROUP * pair:, :])
        for g in range(GROUP):
            ybbuf[pl.ds(r0, pair), g * nkv:(g + 1) * nkv] = og[g * pair:(g + 1) * pair, :]
        return carry

    lax.fori_loop(0, nblk // 2, body, 0, unroll=True)

    @pl.when(i == pl.num_programs(0) - 1)
    def _():
        out = _dot(yabuf[...], wout_ref[0:aw, :]) + _dot(ybbuf[...].astype(BF16), wout_ref[aw:, :])
        y_ref[...] = x_ref[...] + _rms(out, gpost_ref[...])


def _mix_ab_sample(x, conv_prev, kcache, vcache, seg, gpre, gpost, win, cw, sinks, wout):
    rows, d = x.shape
    nseq = rows // seg
    aw = cw.shape[1]
    nq = GROUP * KV_HEADS * HEAD_DIM
    nkv = KV_HEADS * HEAD_DIM
    sb = SAMPLE_SEQ_BLOCK
    pair = 2 * seg
    qrows = KV_HEADS * GROUP * pair
    tk = WINDOW + pair
    full = lambda shape: pl.BlockSpec(shape, lambda i: (0,) * len(shape))
    cache = pl.BlockSpec((sb, WINDOW, nkv), lambda i: (i, 0, 0))
    prev0, prev1 = conv_prev[:, 0], conv_prev[:, 1]
    y, conv0, conv1, k_new, v_new = pl.pallas_call(
        functools.partial(_mix_ab_sample_kernel, seg),
        grid=(nseq // sb,),
        in_specs=[_resident(x.shape), _resident(prev0.shape), _resident(prev1.shape), cache, cache,
                  _resident(gpre.shape), _resident(gpost.shape), _resident(win.shape), _resident(cw.shape),
                  _resident(sinks.shape), _resident(wout.shape)],
        out_specs=[full((rows, d)), full((nseq, aw)), full((nseq, aw)), cache, cache],
        out_shape=[jax.ShapeDtypeStruct((rows, d), F32), jax.ShapeDtypeStruct((nseq, aw), F32),
                   jax.ShapeDtypeStruct((nseq, aw), F32), jax.ShapeDtypeStruct(kcache.shape, F32),
                   jax.ShapeDtypeStruct(vcache.shape, F32)],
        scratch_shapes=[pltpu.VMEM((aw // LANES, rows + SUBLANES, LANES), F32),
                        pltpu.VMEM((aw // LANES, rows, LANES), F32),
                        pltpu.VMEM((aw // LANES, rows, LANES), F32),
                        pltpu.VMEM((rows, nq), F32),
                        pltpu.VMEM((rows, nkv), F32),
                        pltpu.VMEM((rows, nkv), F32),
                        pltpu.VMEM((rows, aw), BF16),
                        pltpu.VMEM((rows, nq), F32),
                        pltpu.VMEM((2, tk, nkv), F32),
                        pltpu.VMEM((2, tk, nkv), F32),
                        pltpu.VMEM((qrows, tk), F32),
                        pltpu.VMEM((qrows, 1), F32)],
        compiler_params=_params(("arbitrary",)),
        name="mix_ab_sample",
    )(x, prev0, prev1, kcache, vcache, gpre, gpost, win, cw, sinks, wout)
    return y, jnp.stack([conv0, conv1], axis=1), k_new, v_new


def _hgrn_sample_kernel(layer, seg, x_ref, s0_ref, gpre_ref, gpost_ref, win_ref, lbl_ref, ng_ref, wout_ref,
                        y_ref, s1_ref, qb, kb, vb, gb, ob, mixbuf):
    i = pl.program_id(0)
    rows = x_ref.shape[0]
    dkt = C_HEADS * C_DK
    pair = 2 * seg

    @pl.when(i == 0)
    def _():
        x = x_ref[...]
        h = _rms(x, gpre_ref[...]).astype(BF16)
        f = _dot(h, win_ref[:, dkt:2 * dkt])
        log2_ft, kb[...] = _forget_gate(f, _lower_bound_terms(lbl_ref, layer))
        hi, lo = _split_bf16(log2_ft)
        ri = lax.broadcasted_iota(jnp.int32, (rows, rows), 0)
        ci = lax.broadcasted_iota(jnp.int32, (rows, rows), 1)
        tri = jnp.where((ri // seg == ci // seg) & (ri >= ci), 1.0, 0.0).astype(BF16)
        gb[...] = _dot(tri, hi) + _dot(tri, lo)
        qb[...] = _dot(h, win_ref[:, 0:dkt])
        vb[...] = _dot(h, win_ref[:, 2 * dkt:3 * dkt])

    nblk = s0_ref.shape[0]
    rr = lax.broadcasted_iota(jnp.int32, (pair, 1), 0)
    first_seq = rr < seg
    second_seq = rr >= seg
    ri = lax.broadcasted_iota(jnp.int32, (pair, pair), 0)
    ci = lax.broadcasted_iota(jnp.int32, (pair, pair), 1)
    key_s = [(ri // seg == ci // seg) & (ci % seg == sk) & (ri % seg >= sk) for sk in range(seg)]

    def body(p, carry):
        r0 = pl.multiple_of((i * nblk + 2 * p) * seg, pair)
        for hh in range(C_HEADS):
            sl = slice(hh * C_DK, (hh + 1) * C_DK)
            gh = gb[pl.ds(r0, pair), sl]
            g_row = lambda t: jnp.where(first_seq, gh[t:t + 1, :], gh[seg + t:seg + t + 1, :])
            gl = g_row(seg - 1)
            qh = qb[pl.ds(r0, pair), sl]
            kh = kb[pl.ds(r0, pair), sl]
            vh = vb[pl.ds(r0, pair), sl].astype(BF16)
            q_rel = [(qh * jnp.exp2(gh - g_row(sk))).astype(BF16) for sk in range(seg - 1)] + [qh.astype(BF16)]
            rel = _dot_nt(jnp.concatenate(q_rel, axis=0), kh.astype(BF16))
            att = jnp.where(key_s[0], rel[0:pair, :], 0.0)
            for sk in range(1, seg):
                att = att + jnp.where(key_s[sk], rel[sk * pair:(sk + 1) * pair, :], 0.0)
            att = att.astype(BF16)
            qi = (qh * jnp.exp2(gh)).astype(BF16)
            kl = kh * jnp.exp2(gl - gh)
            o_prev = []
            for a in range(2):
                s_a = s0_ref[2 * p + a, hh]
                o_prev.append(_dot(qi, s_a.astype(BF16)))
                kl_a = jnp.where(first_seq if a == 0 else second_seq, kl, 0.0).astype(BF16)
                e = jnp.exp2(gh[(a + 1) * seg - 1:(a + 1) * seg, :])
                e1 = e.astype(BF16).astype(F32)
                e2 = (e - e1).astype(BF16).astype(F32)
                e3 = (e - e1 - e2).astype(BF16).astype(F32)
                dec_l = jnp.where(rr == 0, e1, jnp.where(rr == 1, e2, jnp.where(rr == 2, e3, 0.0)))
                dec = _dot_tn(dec_l.astype(BF16), jnp.ones((pair, C_DK), BF16))
                s1_ref[2 * p + a, hh] = dec * s_a + _dot_tn(kl_a, vh)
            ob[pl.ds(r0, pair), sl] = _dot(att, vh) + jnp.where(first_seq, o_prev[0], o_prev[1])
        return carry

    lax.fori_loop(0, nblk // 2, body, 0, unroll=True)

    @pl.when(i == pl.num_programs(0) - 1)
    def _():
        x = x_ref[...]
        h = _rms(x, gpre_ref[...]).astype(BF16)
        y_ref[...] = _hgrn_gate_out(x, h, ob, mixbuf, win_ref, ng_ref, wout_ref, gpost_ref)


def _hgrn_sample(layer, x, s0, seg, gpre, gpost, win, lbl, ng, wout):
    rows, d = x.shape
    assert 2 * seg == SUBLANES, "two sequences per iteration must fill one f32 vreg of sublanes"
    nseq = rows // seg
    dkt = C_HEADS * C_DK
    sb = SAMPLE_SEQ_BLOCK
    full = lambda shape: pl.BlockSpec(shape, lambda i: (0,) * len(shape))
    state = pl.BlockSpec((sb, C_HEADS, C_DK, C_DK), lambda i: (i, 0, 0, 0))
    return pl.pallas_call(
        functools.partial(_hgrn_sample_kernel, layer, seg),
        grid=(nseq // sb,),
        in_specs=[_resident(x.shape), state, _resident(gpre.shape), _resident(gpost.shape), _resident(win.shape),
                  _resident(lbl.shape), _resident(ng.shape), _resident(wout.shape)],
        out_specs=[full((rows, d)), state],
        out_shape=[jax.ShapeDtypeStruct((rows, d), F32), jax.ShapeDtypeStruct(s0.shape, F32)],
        scratch_shapes=[pltpu.VMEM((rows, dkt), F32), pltpu.VMEM((rows, dkt), F32),
                        pltpu.VMEM((rows, dkt), F32), pltpu.VMEM((rows, dkt), F32),
                        pltpu.VMEM((rows, dkt), F32), pltpu.VMEM((rows, dkt), BF16)],
        compiler_params=_params(("arbitrary",)),
        name="hgrn_sample",
    )(x, s0, gpre, gpost, win, lbl, ng, wout)


def _cast_kernel(w_ref, o_ref):
    o_ref[...] = w_ref[...].astype(BF16)


def _to_bf16(w, row_block):
    lead = w.shape[:-2]
    rows, cols = w.shape[-2:]
    w2 = w.reshape(-1, cols)
    total = w2.shape[0]
    assert total % row_block == 0, (w.shape, row_block)
    spec = pl.BlockSpec((row_block, cols), lambda i: (i, 0))
    out = pl.pallas_call(
        _cast_kernel,
        grid=(total // row_block,),
        in_specs=[spec],
        out_specs=spec,
        out_shape=jax.ShapeDtypeStruct(w2.shape, BF16),
        compiler_params=_params(("arbitrary",)),
        name="cast_bf16",
    )(w2)
    return out.reshape(*lead, rows, cols)


def kernel(x_prompt, x_sample, state_conv_a, cache_swa_k, cache_swa_v, state_hgrn, state_ffn_conv, norm_mix_pre, norm_mix_post, norm_ffn_pre, norm_ffn_post, w_in_ab, conv_a_w, attn_sinks, w_out_ab, w_in_c, hgrn_lb_logits, hgrn_norm_g, w_out_c, w_ffn_up, ffn_conv_w, w_ffn_down):
    bp, seq, d = x_prompt.shape
    ns, seg, _ = x_sample.shape
    aw = conv_a_w.shape[-1]
    nq = GROUP * KV_HEADS * HEAD_DIM
    nkv = KV_HEADS * HEAD_DIM

    q0 = 3 * aw
    win_ab = w_in_ab[0].astype(BF16)
    wq = win_ab[:, q0:q0 + nq].reshape(d, KV_HEADS, GROUP, HEAD_DIM).swapaxes(1, 2).reshape(d, nq)
    win_ab = jnp.concatenate([win_ab[:, :q0], wq, win_ab[:, q0 + nq:]], axis=1)
    wout_ab = w_out_ab[0].astype(BF16)
    wo = wout_ab[aw:].reshape(KV_HEADS, GROUP, HEAD_DIM, d).swapaxes(0, 1).reshape(nq, d)
    wout_ab = jnp.concatenate([wout_ab[:aw], wo], axis=0)
    win_c = _to_bf16(w_in_c[0], CAST_ROWS)
    wout_c = w_out_c[0].astype(BF16)
    wup = _to_bf16(w_ffn_up, CAST_ROWS)
    wdown = _to_bf16(w_ffn_down, 2 * CAST_ROWS)
    row = lambda a: a.reshape(1, -1)
    sinks = row(attn_sinks[0])
    ng = row(hgrn_norm_g[0])

    x1, conv_p, k_p, v_p = _mix_ab_prompt(x_prompt, row(norm_mix_pre[0]), row(norm_mix_post[0]), win_ab,
                                          conv_a_w[0], sinks, wout_ab)
    x2, ffn_p0 = _ffn_prompt(0, x1, row(norm_ffn_pre[0]), row(norm_ffn_post[0]), wup, ffn_conv_w[0], wdown)
    x3, hgrn_p = _hgrn_prompt(1, x2, row(norm_mix_pre[1]), row(norm_mix_post[1]), win_c, hgrn_lb_logits,
                              ng, wout_c)
    y_prompt, ffn_p1 = _ffn_prompt(1, x3, row(norm_ffn_pre[1]), row(norm_ffn_post[1]), wup, ffn_conv_w[1], wdown)

    xs = x_sample.reshape(ns * seg, d)
    kc = cache_swa_k[0].reshape(ns, WINDOW, nkv)
    vc = cache_swa_v[0].reshape(ns, WINDOW, nkv)
    s1, conv_s, k_s, v_s = _mix_ab_sample(xs, state_conv_a[0], kc, vc, seg, row(norm_mix_pre[0]),
                                          row(norm_mix_post[0]), win_ab, conv_a_w[0], sinks, wout_ab)
    s2, ffn_s0 = _ffn_sample(0, s1, state_ffn_conv[0], seg, row(norm_ffn_pre[0]), row(norm_ffn_post[0]),
                             wup, ffn_conv_w[0], wdown)
    s3, hgrn_s = _hgrn_sample(1, s2, state_hgrn[0], seg, row(norm_mix_pre[1]), row(norm_mix_post[1]),
                              win_c, hgrn_lb_logits, ng, wout_c)
    y_sample, ffn_s1 = _ffn_sample(1, s3, state_ffn_conv[1], seg, row(norm_ffn_pre[1]), row(norm_ffn_post[1]),
                                   wup, ffn_conv_w[1], wdown)

    kvshape = (KV_HEADS, HEAD_DIM)
    return (y_prompt, y_sample.reshape(ns, seg, d),
            conv_p[None], conv_s[None],
            k_p.reshape(1, bp, WINDOW, *kvshape), k_s.reshape(1, ns, WINDOW, *kvshape),
            v_p.reshape(1, bp, WINDOW, *kvshape), v_s.reshape(1, ns, WINDOW, *kvshape),
            hgrn_p[None], hgrn_s[None],
            jnp.stack([ffn_p0, ffn_p1]), jnp.stack([ffn_s0, ffn_s1]))
```

```python
import functools
import math

import jax
import jax.numpy as jnp
from jax import lax
from jax.experimental import pallas as pl
from jax.experimental.pallas import tpu as pltpu

F32 = jnp.float32
BF16 = jnp.bfloat16

EPS = 1e-6
WINDOW = 128
KV_HEADS = 2
GROUP = 4
HEAD_DIM = 64
C_HEADS = 8
C_DK = 128
NEG = -1e30
SUBLANES = 8
LANES = 128
MIX_TILE = 1024
HGRN_TILE = 512
FFN_TILE = 1024
HGRN_CHUNK = 64
DECAY_LOG2_LIMIT = 100.0
FFN_CHUNK = 256
SAMPLE_SEQ_BLOCK = 8
VMEM_LIMIT_V7X = 56 * 2**20


def _dot(a, b):
    return jnp.dot(a, b, preferred_element_type=F32)


def _dot_nt(a, b):
    return lax.dot_general(a, b, (((1,), (1,)), ((), ())), preferred_element_type=F32)


def _dot_tn(a, b):
    return lax.dot_general(a, b, (((0,), (0,)), ((), ())), preferred_element_type=F32)


def _rms(x, g):
    return x * lax.rsqrt(jnp.mean(x * x, axis=-1, keepdims=True) + EPS) * g


def _gelu_tanh(x):
    c = math.sqrt(2.0 / math.pi)
    return x * (0.5 * (1.0 + jnp.tanh(c * (x + 0.044715 * (x * x * x)))))


def _sigmoid(x):
    return 1.0 / (1.0 + jnp.exp(-x))


def _resident(shape):
    nd = len(shape)
    return pl.BlockSpec(shape, lambda *_: (0,) * nd, pipeline_mode=pl.Buffered(1))


def _params(sem):
    return pltpu.CompilerParams(dimension_semantics=sem, vmem_limit_bytes=VMEM_LIMIT_V7X)


def _head_slope(h):
    return 2.0 ** (-8.0 * (h + 1) / (KV_HEADS * GROUP))


def _mix_ab_prompt_kernel(x_ref, gpre_ref, gpost_ref, win_ref, cw_ref, sink_ref, wout_ref,
                          y_ref, conv_ref, kout_ref, vout_ref,
                          ubuf, qbuf, kbuf, vbuf, mixbuf, bias_s):
    s = pl.program_id(1)
    ts = x_ref.shape[0]
    aw = cw_ref.shape[1]
    nq = GROUP * KV_HEADS * HEAD_DIM
    nkv = KV_HEADS * HEAD_DIM

    @pl.when(s == 0)
    def _():
        ubuf[0:SUBLANES, :] = jnp.zeros((SUBLANES, aw), F32)
        kbuf[0:WINDOW, :] = jnp.zeros((WINDOW, nkv), BF16)
        vbuf[0:WINDOW, 0:nkv] = jnp.zeros((WINDOW, nkv), BF16)
        vbuf[:, nkv:2 * nkv] = jnp.ones((ts + WINDOW, nkv), BF16)

    x = x_ref[...]
    h = _rms(x, gpre_ref[...]).astype(BF16)

    u = _dot(h, win_ref[:, 2 * aw:3 * aw]) * _dot(h, win_ref[:, 0:aw])
    ubuf[SUBLANES:SUBLANES + ts, :] = u
    cw = cw_ref[...]
    conv = (cw[2:3, :] * u + cw[1:2, :] * ubuf[SUBLANES - 1:SUBLANES - 1 + ts, :]
            + cw[0:1, :] * ubuf[SUBLANES - 2:SUBLANES - 2 + ts, :])
    mixbuf[:, 0:aw] = (_dot(h, win_ref[:, aw:2 * aw]) * conv).astype(BF16)
    tail = ubuf[ts + SUBLANES - 2:ts + SUBLANES, :]
    conv_ref[...] = tail
    ubuf[SUBLANES - 2:SUBLANES, :] = tail

    q0 = 3 * aw
    q = _dot(h, win_ref[:, q0:q0 + nq]) * (HEAD_DIM ** -0.5 * LOG2E)
    lane = lax.broadcasted_iota(jnp.int32, (1, nkv), 1)
    lo = lane < HEAD_DIM
    nblk = ts // WINDOW
    for n in range(nblk):
        for g in range(GROUP):
            qg = q[n * WINDOW:(n + 1) * WINDOW, g * nkv:(g + 1) * nkv]
            qbuf[n, g * WINDOW:(g + 1) * WINDOW, :] = jnp.where(lo, qg, 0.0).astype(BF16)
            qbuf[n, (GROUP + g) * WINDOW:(GROUP + g + 1) * WINDOW, :] = jnp.where(lo, 0.0, qg).astype(BF16)
    wkv = win_ref[:, q0 + nq:q0 + nq + 2 * nkv]
    kv_new = jnp.concatenate([_dot(h[0:ts // 2, :], wkv), _dot(h[ts // 2:, :], wkv)], axis=0)
    k = kv_new[:, 0:nkv]
    v = kv_new[:, nkv:2 * nkv]
    kbuf[WINDOW:WINDOW + ts, :] = k.astype(BF16)
    vbuf[WINDOW:WINDOW + ts, 0:nkv] = v.astype(BF16)
    kout_ref[...] = k[ts - WINDOW:, :]
    vout_ref[...] = v[ts - WINDOW:, :]

    qi = lax.broadcasted_iota(jnp.int32, (WINDOW, 2 * WINDOW), 0)
    kj = lax.broadcasted_iota(jnp.int32, (WINDOW, 2 * WINDOW), 1)
    dist = qi + WINDOW - kj
    valid = (dist >= 0) & (dist < WINDOW)
    negdist = jnp.where(valid, -dist.astype(F32), 0.0)
    maskadd = jnp.where(valid, 0.0, NEG)
    nheads = KV_HEADS * GROUP
    for hidx in range(nheads):
        bias_s[hidx] = (_head_slope(hidx) * LOG2E) * negdist + maskadd
    first = jnp.where(kj < WINDOW, jnp.where(s == 0, NEG, 0.0), 0.0)

    for n in range(nblk):
        r0 = n * WINDOW
        half = GROUP * WINDOW
        keys = kbuf[r0:r0 + 2 * WINDOW, :]
        sc_kv = [_dot_nt(qbuf[n, kv * half:(kv + 1) * half, :], keys) for kv in range(KV_HEADS)]
        ps, sink_terms = [], []
        for hidx in range(nheads):
            g = hidx % GROUP
            sc = sc_kv[hidx // GROUP][g * WINDOW:(g + 1) * WINDOW, :] + bias_s[hidx]
            if n == 0:
                sc = sc + first
            sink = sink_ref[hidx] * LOG2E
            m = jnp.maximum(jnp.max(sc, axis=-1, keepdims=True), sink)
            ps.append(jnp.exp2(sc - m).astype(BF16))
            sink_terms.append(jnp.exp2(sink - m))
        vals = vbuf[r0:r0 + 2 * WINDOW, :]
        pv_kv = [_dot(jnp.concatenate(ps[kv * GROUP:(kv + 1) * GROUP], axis=0), vals) for kv in range(KV_HEADS)]
        outs = []
        for hidx in range(nheads):
            g = hidx % GROUP
            pv_h = pv_kv[hidx // GROUP][g * WINDOW:(g + 1) * WINDOW, :]
            outs.append(pv_h[:, 0:nkv] / (pv_h[:, nkv:2 * nkv] + sink_terms[hidx]))
        for g in range(GROUP):
            og = jnp.where(lo, outs[g], outs[GROUP + g])
            mixbuf[r0:r0 + WINDOW, aw + g * nkv:aw + (g + 1) * nkv] = og.astype(BF16)

    kbuf[0:WINDOW, :] = kbuf[ts:ts + WINDOW, :]
    vbuf[0:WINDOW, :] = vbuf[ts:ts + WINDOW, :]

    out = _dot(mixbuf[...], wout_ref[...])
    y_ref[...] = x + _rms(out, gpost_ref[...])


def _mix_ab_prompt(x, gpre, gpost, win, cw, sinks, wout):
    b, seq, d = x.shape
    ts = MIX_TILE
    assert seq % ts == 0 and ts % WINDOW == 0, (seq, ts)
    aw = cw.shape[1]
    nq = GROUP * KV_HEADS * HEAD_DIM
    nkv = KV_HEADS * HEAD_DIM
    tile = pl.BlockSpec((None, ts, d), lambda i, j: (i, j, 0))
    return pl.pallas_call(
        _mix_ab_prompt_kernel,
        grid=(b, seq // ts),
        in_specs=[tile, _resident(gpre.shape), _resident(gpost.shape), _resident(win.shape),
                  _resident(cw.shape), pl.BlockSpec(memory_space=pltpu.SMEM), _resident(wout.shape)],
        out_specs=[tile,
                   pl.BlockSpec((None, 2, aw), lambda i, j: (i, 0, 0)),
                   pl.BlockSpec((None, WINDOW, nkv), lambda i, j: (i, 0, 0)),
                   pl.BlockSpec((None, WINDOW, nkv), lambda i, j: (i, 0, 0))],
        out_shape=[jax.ShapeDtypeStruct((b, seq, d), F32),
                   jax.ShapeDtypeStruct((b, 2, aw), F32),
                   jax.ShapeDtypeStruct((b, WINDOW, nkv), F32),
                   jax.ShapeDtypeStruct((b, WINDOW, nkv), F32)],
        scratch_shapes=[pltpu.VMEM((ts + SUBLANES, aw), F32),
                        pltpu.VMEM((ts // WINDOW, KV_HEADS * GROUP * WINDOW, nkv), BF16),
                        pltpu.VMEM((ts + WINDOW, nkv), BF16),
                        pltpu.VMEM((ts + WINDOW, 2 * nkv), BF16),
                        pltpu.VMEM((ts, d), BF16),
                        pltpu.VMEM((KV_HEADS * GROUP, WINDOW, 2 * WINDOW), F32)],
        compiler_params=_params(("arbitrary", "arbitrary")),
        name="mix_ab_prompt",
    )(x, gpre, gpost, win, cw, sinks.reshape(-1), wout)


def _ffn_prompt_kernel(x_ref, gpre_ref, gpost_ref, wup_ref, cw_ref, wdown_ref,
                       y_ref, st_ref, abuf, carry, gbuf):
    s = pl.program_id(1)
    rows = x_ref.shape[0]
    ff = cw_ref.shape[1]

    @pl.when(s == 0)
    def _():
        carry[...] = jnp.zeros(carry.shape, F32)

    x = x_ref[...]
    h = _rms(x, gpre_ref[...]).astype(BF16)
    for c in range(ff // FFN_CHUNK):
        c0 = c * FFN_CHUNK
        buf = abuf.at[c % 2]
        a = _dot(h, wup_ref[:, c0:c0 + FFN_CHUNK])
        b = _dot(h, wup_ref[:, ff + c0:ff + c0 + FFN_CHUNK])
        buf[0:SUBLANES, :] = carry[:, c0:c0 + FFN_CHUNK]
        buf[SUBLANES:SUBLANES + rows, :] = a
        cw = cw_ref[:, c0:c0 + FFN_CHUNK]
        conv = (cw[2:3, :] * a + cw[1:2, :] * buf[SUBLANES - 1:SUBLANES - 1 + rows, :]
                + cw[0:1, :] * buf[SUBLANES - 2:SUBLANES - 2 + rows, :])
        tail = a[rows - 2:rows, :]
        st_ref[:, c0:c0 + FFN_CHUNK] = tail
        carry[SUBLANES - 2:SUBLANES, c0:c0 + FFN_CHUNK] = tail
        gbuf[:, c0:c0 + FFN_CHUNK] = (_gelu_tanh(conv) * b).astype(BF16)
    out = _dot(gbuf[...], wdown_ref[...])
    y_ref[...] = x + _rms(out, gpost_ref[...])


def _ffn_prompt(layer, x, gpre, gpost, wup, cw, wdown):
    b, seq, d = x.shape
    ts = FFN_TILE
    ff = cw.shape[1]
    assert seq % ts == 0 and ff % FFN_CHUNK == 0, (seq, ts, ff)
    tile = pl.BlockSpec((None, ts, d), lambda i, j: (i, j, 0))
    layer_block = lambda w: pl.BlockSpec((None,) + w.shape[1:], lambda i, j: (layer, 0, 0),
                                         pipeline_mode=pl.Buffered(1))
    return pl.pallas_call(
        _ffn_prompt_kernel,
        grid=(b, seq // ts),
        in_specs=[tile, _resident(gpre.shape), _resident(gpost.shape), layer_block(wup),
                  _resident(cw.shape), layer_block(wdown)],
        out_specs=[tile, pl.BlockSpec((None, 2, ff), lambda i, j: (i, 0, 0))],
        out_shape=[jax.ShapeDtypeStruct((b, seq, d), F32), jax.ShapeDtypeStruct((b, 2, ff), F32)],
        scratch_shapes=[pltpu.VMEM((2, ts + SUBLANES, FFN_CHUNK), F32), pltpu.VMEM((SUBLANES, ff), F32),
                        pltpu.VMEM((ts, ff), BF16)],
        compiler_params=_params(("arbitrary", "arbitrary")),
        name="ffn_prompt",
    )(x, gpre, gpost, wup, cw, wdown)


def _segment_conv_taps(a, t, seg, prev0_ref, prev1_ref, abuf, fix1, fix2):
    rows, c = a.shape
    nseq = rows // seg
    am1, am2 = [], []
    for k in range(c // LANES):
        sl = slice(k * LANES, (k + 1) * LANES)
        abuf[k, SUBLANES:SUBLANES + rows, :] = a[:, sl]
        fix1[k, pl.ds(0, nseq, stride=seg), :] = prev1_ref[:, sl]
        fix2[k, pl.ds(0, nseq, stride=seg), :] = prev0_ref[:, sl]
        fix2[k, pl.ds(1, nseq, stride=seg), :] = prev1_ref[:, sl]
        am1.append(jnp.where(t >= 1, abuf[k, SUBLANES - 1:SUBLANES - 1 + rows, :], fix1[k]))
        am2.append(jnp.where(t >= 2, abuf[k, SUBLANES - 2:SUBLANES - 2 + rows, :], fix2[k]))
    return jnp.concatenate(am1, axis=1), jnp.concatenate(am2, axis=1)


def _segment_tails(abuf, nseq, seg):
    tail = lambda back: jnp.concatenate(
        [abuf[k, pl.ds(SUBLANES + seg - back, nseq, stride=seg), :] for k in range(abuf.shape[0])], axis=1)
    return tail(2), tail(1)


def _ffn_sample_kernel(seg, x_ref, prev0_ref, prev1_ref, gpre_ref, gpost_ref, wa_ref, wb_ref, cw_ref, wdown_ref,
                       y_ref, st0_ref, st1_ref, hbuf, abuf, fix1, fix2, acc):
    c = pl.program_id(0)
    rows = x_ref.shape[0]

    @pl.when(c == 0)
    def _():
        hbuf[...] = _rms(x_ref[...], gpre_ref[...]).astype(BF16)
        acc[...] = jnp.zeros(acc.shape, F32)
        abuf[:, 0:SUBLANES, :] = jnp.zeros((abuf.shape[0], SUBLANES, LANES), F32)
        fix1[...] = jnp.zeros(fix1.shape, F32)
        fix2[...] = jnp.zeros(fix2.shape, F32)

    h = hbuf[...]
    t = lax.broadcasted_iota(jnp.int32, (rows, 1), 0) % seg
    a = _dot(h, wa_ref[...])
    b = _dot(h, wb_ref[...])
    am1, am2 = _segment_conv_taps(a, t, seg, prev0_ref, prev1_ref, abuf, fix1, fix2)
    st0_ref[...], st1_ref[...] = _segment_tails(abuf, rows // seg, seg)
    cw = cw_ref[...]
    conv = cw[2:3, :] * a + cw[1:2, :] * am1 + cw[0:1, :] * am2
    acc[...] += _dot((_gelu_tanh(conv) * b).astype(BF16), wdown_ref[...])

    @pl.when(c == pl.num_programs(0) - 1)
    def _():
        y_ref[...] = x_ref[...] + _rms(acc[...], gpost_ref[...])


def _ffn_sample(layer, x, prev, seg, gpre, gpost, wup, cw, wdown):
    rows, d = x.shape
    nseq = rows // seg
    ff = cw.shape[1]
    cf = FFN_CHUNK
    nc = ff // cf
    cols = lambda r: pl.BlockSpec((r, cf), lambda c: (0, c))
    y, st0, st1 = pl.pallas_call(
        functools.partial(_ffn_sample_kernel, seg),
        grid=(nc,),
        in_specs=[_resident(x.shape), cols(nseq), cols(nseq), _resident(gpre.shape), _resident(gpost.shape),
                  pl.BlockSpec((None, d, cf), lambda c: (layer, 0, c)),
                  pl.BlockSpec((None, d, cf), lambda c: (layer, 0, c + nc)), cols(3),
                  pl.BlockSpec((None, cf, d), lambda c: (layer, c, 0))],
        out_specs=[pl.BlockSpec((rows, d), lambda c: (0, 0)), cols(nseq), cols(nseq)],
        out_shape=[jax.ShapeDtypeStruct((rows, d), F32), jax.ShapeDtypeStruct((nseq, ff), F32),
                   jax.ShapeDtypeStruct((nseq, ff), F32)],
        scratch_shapes=[pltpu.VMEM((rows, d), BF16), pltpu.VMEM((cf // LANES, rows + SUBLANES, LANES), F32),
                        pltpu.VMEM((cf // LANES, rows, LANES), F32), pltpu.VMEM((cf // LANES, rows, LANES), F32),
                        pltpu.VMEM((rows, d), F32)],
        compiler_params=_params(("arbitrary",)),
        name="ffn_sample",
    )(x, prev[:, 0], prev[:, 1], gpre, gpost, wup, wup, cw, wdown)
    return y, jnp.stack([st0, st1], axis=1)


LOG2E = 1.4426950408889634


def _lower_bound_terms(logits_ref, layer):
    lg = logits_ref[...]
    e = jnp.exp(lg - jnp.max(lg, axis=0, keepdims=True))
    pr = e / jnp.sum(e, axis=0, keepdims=True)
    cum = pr[0:1, :]
    for r in range(1, layer + 1):
        cum = cum + pr[r:r + 1, :]
    lb = jnp.clip(cum - pr[0:1, :], 0.0, 1.0)
    return jnp.log(lb) * LOG2E, jnp.log1p(-lb) * LOG2E, 1.0 - lb


def _forget_gate(f, lb_terms):
    log2_lb, log2_1m_lb, one_m_lb = lb_terms
    f2 = f * LOG2E
    e = jnp.exp2(jnp.minimum(f2, -f2))
    w = 1.0 + e
    b = log2_1m_lb + (jnp.minimum(f2, 0.0) - jnp.log2(w))
    d = log2_lb - b
    log2_ft = jnp.maximum(log2_lb, b) + jnp.log2(1.0 + jnp.exp2(jnp.minimum(d, -d)))
    return log2_ft, one_m_lb * (jnp.where(f2 >= 0.0, e, 1.0) / w)


def _split_bf16(x):
    hi = x.astype(BF16)
    return hi, (x - hi.astype(F32)).astype(BF16)


def _hgrn_gate_out(x, h, ob, mixbuf, win_ref, ng_ref, wout_ref, gpost_ref):
    dkt = C_HEADS * C_DK
    g = _dot(h, win_ref[:, 3 * dkt:4 * dkt])
    ng = ng_ref[...]
    for hh in range(C_HEADS):
        sl = slice(hh * C_DK, (hh + 1) * C_DK)
        mixbuf[:, sl] = (_rms(ob[:, sl], ng) * (g[:, sl] * _sigmoid(g[:, sl]))).astype(BF16)
    out = _dot(mixbuf[...], wout_ref[...])
    return x + _rms(out, gpost_ref[...])


def _hgrn_exact_tile(q, k_in, log2_ft, vb, st, ob, sq, sk, slg):
    ts = sq.shape[0]
    sq[...] = q
    sk[...] = k_in
    slg[...] = log2_ft
    ob[...] = jnp.zeros(ob.shape, F32)
    grp = 2 * SUBLANES
    row = lax.broadcasted_iota(jnp.int32, (grp, 1), 0)

    def body(t, carry):
        t0 = pl.multiple_of(lax.shift_left(lax.shift_right_logical(t, 4), 4), grp)
        sel = row == lax.bitwise_and(t, grp - 1)
        for hh in range(C_HEADS):
            sl = slice(hh * C_DK, (hh + 1) * C_DK)
            rows_of = lambda ref: jnp.where(sel, ref[pl.ds(t0, grp), sl].astype(F32), 0.0)
            pick = lambda ref: rows_of(ref).astype(BF16)
            log2_f = jnp.sum(rows_of(slg), axis=0, keepdims=True)
            s_new = st[hh] * jnp.exp2(log2_f) + _dot_tn(pick(vb), pick(sk))
            st[hh] = s_new
            ob[pl.ds(t0, grp), sl] += _dot_nt(pick(sq), s_new.astype(BF16))
        return carry

    lax.fori_loop(0, ts, body, 0)


def _hgrn_prompt_kernel(layer, x_ref, gpre_ref, gpost_ref, win_ref, lbl_ref, ng_ref, wout_ref,
                        y_ref, sout_ref, st, st0, qtb, ktb, qib, klb, eglb, vb, ob, mixbuf, sq, sk, slg):
    s = pl.program_id(1)
    ts = x_ref.shape[0]
    dkt = C_HEADS * C_DK
    ck = HGRN_CHUNK

    @pl.when(s == 0)
    def _():
        st[...] = jnp.zeros(st.shape, F32)

    x = x_ref[...]
    h = _rms(x, gpre_ref[...]).astype(BF16)
    f = _dot(h, win_ref[:, dkt:2 * dkt])
    log2_ft, k_in = _forget_gate(f, _lower_bound_terms(lbl_ref, layer))
    q = _dot(h, win_ref[:, 0:dkt])
    vb[...] = _dot(h, win_ref[:, 2 * dkt:3 * dkt]).astype(BF16)

    ri = lax.broadcasted_iota(jnp.int32, (ck, ck), 0)
    ci = lax.broadcasted_iota(jnp.int32, (ck, ck), 1)
    causal = ri >= ci
    tri = jnp.where(causal, 1.0, 0.0).astype(BF16)
    mid = ck // 2 - 1
    nck = ts // ck

    spread = jnp.zeros((1, dkt), F32)
    for c in range(nck):
        rows = slice(c * ck, (c + 1) * ck)
        hi, lo = _split_bf16(log2_ft[rows, :])
        gcum = _dot(tri, hi) + _dot(tri, lo)
        gr = gcum[mid:mid + 1, :]
        gl = gcum[ck - 1:ck, :]
        spread = jnp.maximum(spread, jnp.maximum(gcum[0:1, :] - gr, gr - gl))
        qc = q[rows, :]
        kc = k_in[rows, :]
        qtb[rows, :] = (qc * jnp.exp2(gcum - gr)).astype(BF16)
        ktb[rows, :] = (kc * jnp.exp2(gr - gcum)).astype(BF16)
        qib[rows, :] = (qc * jnp.exp2(gcum)).astype(BF16)
        klb[rows, :] = (kc * jnp.exp2(gl - gcum)).astype(BF16)
        eglb[c] = jnp.exp2(gl)

    st0[...] = st[...]
    for c in range(nck):
        rows = slice(c * ck, (c + 1) * ck)
        for hh in range(C_HEADS):
            sl = slice(hh * C_DK, (hh + 1) * C_DK)
            vh = vb[rows, sl]
            att = jnp.where(causal, _dot_nt(qtb[rows, sl], ktb[rows, sl]), 0.0).astype(BF16)
            st_h = st[hh]
            ob[rows, sl] = _dot(jnp.concatenate([qib[rows, sl], att], axis=1),
                                jnp.concatenate([st_h.astype(BF16).T, vh], axis=0))
            st[hh] = eglb[c, :, sl] * st_h + _dot_tn(vh, klb[rows, sl])

    y_ref[...] = _hgrn_gate_out(x, h, ob, mixbuf, win_ref, ng_ref, wout_ref, gpost_ref)

    @pl.when(jnp.max(spread) > DECAY_LOG2_LIMIT)
    def _():
        st[...] = st0[...]
        _hgrn_exact_tile(q, k_in, log2_ft, vb, st, ob, sq, sk, slg)
        y_ref[...] = _hgrn_gate_out(x, h, ob, mixbuf, win_ref, ng_ref, wout_ref, gpost_ref)

    @pl.when(s == pl.num_programs(1) - 1)
    def _():
        for hh in range(C_HEADS):
            sout_ref[hh] = st[hh].T


def _hgrn_prompt(layer, x, gpre, gpost, win, lbl, ng, wout):
    b, seq, d = x.shape
    ts = HGRN_TILE
    assert seq % ts == 0 and ts % HGRN_CHUNK == 0, (seq, ts)
    dkt = C_HEADS * C_DK
    tile = pl.BlockSpec((None, ts, d), lambda i, j: (i, j, 0))
    return pl.pallas_call(
        functools.partial(_hgrn_prompt_kernel, layer),
        grid=(b, seq // ts),
        in_specs=[tile, _resident(gpre.shape), _resident(gpost.shape), _resident(win.shape),
                  _resident(lbl.shape), _resident(ng.shape), _resident(wout.shape)],
        out_specs=[tile, pl.BlockSpec((None, C_HEADS, C_DK, C_DK), lambda i, j: (i, 0, 0, 0))],
        out_shape=[jax.ShapeDtypeStruct((b, seq, d), F32),
                   jax.ShapeDtypeStruct((b, C_HEADS, C_DK, C_DK), F32)],
        scratch_shapes=[pltpu.VMEM((C_HEADS, C_DK, C_DK), F32), pltpu.VMEM((C_HEADS, C_DK, C_DK), F32),
                        pltpu.VMEM((ts, dkt), BF16), pltpu.VMEM((ts, dkt), BF16),
                        pltpu.VMEM((ts, dkt), BF16), pltpu.VMEM((ts, dkt), BF16),
                        pltpu.VMEM((ts // HGRN_CHUNK, 1, dkt), F32),
                        pltpu.VMEM((ts, dkt), BF16), pltpu.VMEM((ts, dkt), F32),
                        pltpu.VMEM((ts, dkt), BF16),
                        pltpu.VMEM((ts, dkt), F32), pltpu.VMEM((ts, dkt), F32), pltpu.VMEM((ts, dkt), F32)],
        compiler_params=_params(("arbitrary", "arbitrary")),
        name="hgrn_prompt",
    )(x, gpre, gpost, win, lbl, ng, wout)


def _mix_ab_sample_kernel(seg, x_ref, prev0_ref, prev1_ref, kc_ref, vc_ref, gpre_ref, gpost_ref, win_ref,
                          cw_ref, sink_ref, wout_ref,
                          y_ref, conv0_ref, conv1_ref, kout_ref, vout_ref,
                          ubuf, fix1, fix2, qbuf, knew_s, vnew_s, yabuf, ybbuf, kall, vall, bias_s, sink_s):
    i = pl.program_id(0)
    rows = x_ref.shape[0]
    aw = cw_ref.shape[1]
    nq = GROUP * KV_HEADS * HEAD_DIM
    nkv = KV_HEADS * HEAD_DIM
    pair = 2 * seg
    qrows = KV_HEADS * GROUP * pair
    tk = WINDOW + pair

    @pl.when(i == 0)
    def _():
        x = x_ref[...]
        h = _rms(x, gpre_ref[...]).astype(BF16)
        t = lax.broadcasted_iota(jnp.int32, (rows, 1), 0) % seg
        u = _dot(h, win_ref[:, 2 * aw:3 * aw]) * _dot(h, win_ref[:, 0:aw])
        ubuf[:, 0:SUBLANES, :] = jnp.zeros((ubuf.shape[0], SUBLANES, LANES), F32)
        fix1[...] = jnp.zeros(fix1.shape, F32)
        fix2[...] = jnp.zeros(fix2.shape, F32)
        um1, um2 = _segment_conv_taps(u, t, seg, prev0_ref, prev1_ref, ubuf, fix1, fix2)
        conv0_ref[...], conv1_ref[...] = _segment_tails(ubuf, rows // seg, seg)
        cw = cw_ref[...]
        conv = cw[2:3, :] * u + cw[1:2, :] * um1 + cw[0:1, :] * um2
        yabuf[...] = (_dot(h, win_ref[:, aw:2 * aw]) * conv).astype(BF16)
        q0 = 3 * aw
        qbuf[...] = _dot(h, win_ref[:, q0:q0 + nq]) * (HEAD_DIM ** -0.5)
        kv_new = _dot(h, win_ref[:, q0 + nq:q0 + nq + 2 * nkv])
        knew_s[...] = kv_new[:, 0:nkv]
        vnew_s[...] = kv_new[:, nkv:2 * nkv]

        r = lax.broadcasted_iota(jnp.int32, (qrows, tk), 0)
        j = lax.broadcasted_iota(jnp.int32, (qrows, tk), 1)
        dist = r % seg + WINDOW - j
        valid = (dist >= 0) & (dist < WINDOW) & (j < WINDOW + seg)
        rc = lax.broadcasted_iota(jnp.int32, (qrows, 1), 0)
        head = (rc // (GROUP * pair)) * GROUP + (rc // pair) % GROUP
        slope = jnp.zeros((qrows, 1), F32)
        sink = jnp.zeros((qrows, 1), F32)
        for hidx in range(KV_HEADS * GROUP):
            slope = jnp.where(head == hidx, _head_slope(hidx), slope)
            sink = jnp.where(head == hidx, sink_ref[:, hidx:hidx + 1], sink)
        bias_s[...] = jnp.where(valid, -slope * dist.astype(F32), NEG)
        sink_s[...] = sink

    nblk = kc_ref.shape[0]
    lane = lax.broadcasted_iota(jnp.int32, (1, nkv), 1)
    lo = lane < HEAD_DIM
    rq = lax.broadcasted_iota(jnp.int32, (qrows, 1), 0)
    first_seq = (rq // seg) % 2 == 0

    def body(p, carry):
        r0 = pl.multiple_of((i * nblk + 2 * p) * seg, pair)
        q8 = qbuf[pl.ds(r0, pair), :]
        qg = jnp.concatenate([q8[:, g * nkv:(g + 1) * nkv] for g in range(GROUP)], axis=0)
        qs = jnp.concatenate([jnp.where(lo, qg, 0.0), jnp.where(lo, 0.0, qg)], axis=0).astype(BF16)
        knew = knew_s[pl.ds(r0, pair), :]
        vnew = vnew_s[pl.ds(r0, pair), :]
        sc, ov = [], []
        for a in range(2):
            own_first = (lambda z: z) if a == 0 else (lambda z: jnp.concatenate([z[seg:], z[:seg]], axis=0))
            kall[a, 0:WINDOW, :] = kc_ref[2 * p + a]
            kall[a, WINDOW:tk, :] = own_first(knew)
            vall[a, 0:WINDOW, :] = vc_ref[2 * p + a]
            vall[a, WINDOW:tk, :] = own_first(vnew)
            kout_ref[2 * p + a] = kall[a, seg:seg + WINDOW, :]
            vout_ref[2 * p + a] = vall[a, seg:seg + WINDOW, :]
            sc.append(_dot_nt(qs, kall[a].astype(BF16)))
        s2 = jnp.where(first_seq, sc[0], sc[1]) + bias_s[...]
        sink = sink_s[...]
        m = jnp.maximum(jnp.max(s2, axis=-1, keepdims=True), sink)
        pr = jnp.exp(s2 - m)
        den = jnp.sum(pr, axis=-1, keepdims=True) + jnp.exp(sink - m)
        prb = pr.astype(BF16)
        for a in range(2):
            ov.append(_dot(prb, vall[a].astype(BF16)))
        o = jnp.where(first_seq, ov[0], ov[1]) / den
        og = jnp.where(lo, o[0:GROUP * pair, :], o[GROUP * pair:, :])
        for g in range(GROUP):
            ybbuf[pl.ds(r0, pair), g * nkv:(g + 1) * nkv] = og[g * pair:(g + 1) * pair, :]
        return carry

    lax.fori_loop(0, nblk // 2, body, 0, unroll=True)

    @pl.when(i == pl.num_programs(0) - 1)
    def _():
        out = _dot(yabuf[...], wout_ref[0:aw, :]) + _dot(ybbuf[...].astype(BF16), wout_ref[aw:, :])
        y_ref[...] = x_ref[...] + _rms(out, gpost_ref[...])


def _mix_ab_sample(x, conv_prev, kcache, vcache, seg, gpre, gpost, win, cw, sinks, wout):
    rows, d = x.shape
    nseq = rows // seg
    aw = cw.shape[1]
    nq = GROUP * KV_HEADS * HEAD_DIM
    nkv = KV_HEADS * HEAD_DIM
    sb = SAMPLE_SEQ_BLOCK
    pair = 2 * seg
    qrows = KV_HEADS * GROUP * pair
    tk = WINDOW + pair
    full = lambda shape: pl.BlockSpec(shape, lambda i: (0,) * len(shape))
    cache = pl.BlockSpec((sb, WINDOW, nkv), lambda i: (i, 0, 0))
    prev0, prev1 = conv_prev[:, 0], conv_prev[:, 1]
    y, conv0, conv1, k_new, v_new = pl.pallas_call(
        functools.partial(_mix_ab_sample_kernel, seg),
        grid=(nseq // sb,),
        in_specs=[_resident(x.shape), _resident(prev0.shape), _resident(prev1.shape), cache, cache,
                  _resident(gpre.shape), _resident(gpost.shape), _resident(win.shape), _resident(cw.shape),
                  _resident(sinks.shape), _resident(wout.shape)],
        out_specs=[full((rows, d)), full((nseq, aw)), full((nseq, aw)), cache, cache],
        out_shape=[jax.ShapeDtypeStruct((rows, d), F32), jax.ShapeDtypeStruct((nseq, aw), F32),
                   jax.ShapeDtypeStruct((nseq, aw), F32), jax.ShapeDtypeStruct(kcache.shape, F32),
                   jax.ShapeDtypeStruct(vcache.shape, F32)],
        scratch_shapes=[pltpu.VMEM((aw // LANES, rows + SUBLANES, LANES), F32),
                        pltpu.VMEM((aw // LANES, rows, LANES), F32),
                        pltpu.VMEM((aw // LANES, rows, LANES), F32),
                        pltpu.VMEM((rows, nq), F32),
                        pltpu.VMEM((rows, nkv), F32),
                        pltpu.VMEM((rows, nkv), F32),
                        pltpu.VMEM((rows, aw), BF16),
                        pltpu.VMEM((rows, nq), F32),
                        pltpu.VMEM((2, tk, nkv), F32),
                        pltpu.VMEM((2, tk, nkv), F32),
                        pltpu.VMEM((qrows, tk), F32),
                        pltpu.VMEM((qrows, 1), F32)],
        compiler_params=_params(("arbitrary",)),
        name="mix_ab_sample",
    )(x, prev0, prev1, kcache, vcache, gpre, gpost, win, cw, sinks, wout)
    return y, jnp.stack([conv0, conv1], axis=1), k_new, v_new


def _hgrn_sample_kernel(layer, seg, x_ref, s0_ref, gpre_ref, gpost_ref, win_ref, lbl_ref, ng_ref, wout_ref,
                        y_ref, s1_ref, qb, kb, vb, gb, ob, mixbuf):
    i = pl.program_id(0)
    rows = x_ref.shape[0]
    dkt = C_HEADS * C_DK
    pair = 2 * seg

    @pl.when(i == 0)
    def _():
        x = x_ref[...]
        h = _rms(x, gpre_ref[...]).astype(BF16)
        f = _dot(h, win_ref[:, dkt:2 * dkt])
        log2_ft, kb[...] = _forget_gate(f, _lower_bound_terms(lbl_ref, layer))
        hi, lo = _split_bf16(log2_ft)
        ri = lax.broadcasted_iota(jnp.int32, (rows, rows), 0)
        ci = lax.broadcasted_iota(jnp.int32, (rows, rows), 1)
        tri = jnp.where((ri // seg == ci // seg) & (ri >= ci), 1.0, 0.0).astype(BF16)
        gb[...] = _dot(tri, hi) + _dot(tri, lo)
        qb[...] = _dot(h, win_ref[:, 0:dkt])
        vb[...] = _dot(h, win_ref[:, 2 * dkt:3 * dkt])

    nblk = s0_ref.shape[0]
    rr = lax.broadcasted_iota(jnp.int32, (pair, 1), 0)
    first_seq = rr < seg
    second_seq = rr >= seg
    ri = lax.broadcasted_iota(jnp.int32, (pair, pair), 0)
    ci = lax.broadcasted_iota(jnp.int32, (pair, pair), 1)
    key_s = [(ri // seg == ci // seg) & (ci % seg == sk) & (ri % seg >= sk) for sk in range(seg)]

    def body(p, carry):
        r0 = pl.multiple_of((i * nblk + 2 * p) * seg, pair)
        for hh in range(C_HEADS):
            sl = slice(hh * C_DK, (hh + 1) * C_DK)
            gh = gb[pl.ds(r0, pair), sl]
            g_row = lambda t: jnp.where(first_seq, gh[t:t + 1, :], gh[seg + t:seg + t + 1, :])
            gl = g_row(seg - 1)
            qh = qb[pl.ds(r0, pair), sl]
            kh = kb[pl.ds(r0, pair), sl]
            vh = vb[pl.ds(r0, pair), sl].astype(BF16)
            q_rel = [(qh * jnp.exp2(gh - g_row(sk))).astype(BF16) for sk in range(seg - 1)] + [qh.astype(BF16)]
            rel = _dot_nt(jnp.concatenate(q_rel, axis=0), kh.astype(BF16))
            att = jnp.where(key_s[0], rel[0:pair, :], 0.0)
            for sk in range(1, seg):
                att = att + jnp.where(key_s[sk], rel[sk * pair:(sk + 1) * pair, :], 0.0)
            att = att.astype(BF16)
            qi = (qh * jnp.exp2(gh)).astype(BF16)
            kl = kh * jnp.exp2(gl - gh)
            o_prev = []
            for a in range(2):
                s_a = s0_ref[2 * p + a, hh]
                o_prev.append(_dot(qi, s_a.astype(BF16)))
                kl_a = jnp.where(first_seq if a == 0 else second_seq, kl, 0.0).astype(BF16)
                e = jnp.exp2(gh[(a + 1) * seg - 1:(a + 1) * seg, :])
                e1 = e.astype(BF16).astype(F32)
                e2 = (e - e1).astype(BF16).astype(F32)
                e3 = (e - e1 - e2).astype(BF16).astype(F32)
                dec_l = jnp.where(rr == 0, e1, jnp.where(rr == 1, e2, jnp.where(rr == 2, e3, 0.0)))
                dec = _dot_tn(dec_l.astype(BF16), jnp.ones((pair, C_DK), BF16))
                s1_ref[2 * p + a, hh] = dec * s_a + _dot_tn(kl_a, vh)
            ob[pl.ds(r0, pair), sl] = _dot(att, vh) + jnp.where(first_seq, o_prev[0], o_prev[1])
        return carry

    lax.fori_loop(0, nblk // 2, body, 0, unroll=True)

    @pl.when(i == pl.num_programs(0) - 1)
    def _():
        x = x_ref[...]
        h = _rms(x, gpre_ref[...]).astype(BF16)
        y_ref[...] = _hgrn_gate_out(x, h, ob, mixbuf, win_ref, ng_ref, wout_ref, gpost_ref)


def _hgrn_sample(layer, x, s0, seg, gpre, gpost, win, lbl, ng, wout):
    rows, d = x.shape
    assert 2 * seg == SUBLANES, "two sequences per iteration must fill one f32 vreg of sublanes"
    nseq = rows // seg
    dkt = C_HEADS * C_DK
    sb = SAMPLE_SEQ_BLOCK
    full = lambda shape: pl.BlockSpec(shape, lambda i: (0,) * len(shape))
    state = pl.BlockSpec((sb, C_HEADS, C_DK, C_DK), lambda i: (i, 0, 0, 0))
    return pl.pallas_call(
        functools.partial(_hgrn_sample_kernel, layer, seg),
        grid=(nseq // sb,),
        in_specs=[_resident(x.shape), state, _resident(gpre.shape), _resident(gpost.shape), _resident(win.shape),
                  _resident(lbl.shape), _resident(ng.shape), _resident(wout.shape)],
        out_specs=[full((rows, d)), state],
        out_shape=[jax.ShapeDtypeStruct((rows, d), F32), jax.ShapeDtypeStruct(s0.shape, F32)],
        scratch_shapes=[pltpu.VMEM((rows, dkt), F32), pltpu.VMEM((rows, dkt), F32),
                        pltpu.VMEM((rows, dkt), F32), pltpu.VMEM((rows, dkt), F32),
                        pltpu.VMEM((rows, dkt), F32), pltpu.VMEM((rows, dkt), BF16)],
        compiler_params=_params(("arbitrary",)),
        name="hgrn_sample",
    )(x, s0, gpre, gpost, win, lbl, ng, wout)


def kernel(x_prompt, x_sample, state_conv_a, cache_swa_k, cache_swa_v, state_hgrn, state_ffn_conv, norm_mix_pre, norm_mix_post, norm_ffn_pre, norm_ffn_post, w_in_ab, conv_a_w, attn_sinks, w_out_ab, w_in_c, hgrn_lb_logits, hgrn_norm_g, w_out_c, w_ffn_up, ffn_conv_w, w_ffn_down):
    bp, seq, d = x_prompt.shape
    ns, seg, _ = x_sample.shape
    aw = conv_a_w.shape[-1]
    nq = GROUP * KV_HEADS * HEAD_DIM
    nkv = KV_HEADS * HEAD_DIM

    q0 = 3 * aw
    win_ab = w_in_ab[0].astype(BF16)
    wq = win_ab[:, q0:q0 + nq].reshape(d, KV_HEADS, GROUP, HEAD_DIM).swapaxes(1, 2).reshape(d, nq)
    win_ab = jnp.concatenate([win_ab[:, :q0], wq, win_ab[:, q0 + nq:]], axis=1)
    wout_ab = w_out_ab[0].astype(BF16)
    wo = wout_ab[aw:].reshape(KV_HEADS, GROUP, HEAD_DIM, d).swapaxes(0, 1).reshape(nq, d)
    wout_ab = jnp.concatenate([wout_ab[:aw], wo], axis=0)
    win_c = w_in_c[0].astype(BF16)
    wout_c = w_out_c[0].astype(BF16)
    wup = w_ffn_up.astype(BF16)
    wdown = w_ffn_down.astype(BF16)
    row = lambda a: a.reshape(1, -1)
    sinks = row(attn_sinks[0])
    ng = row(hgrn_norm_g[0])

    x1, conv_p, k_p, v_p = _mix_ab_prompt(x_prompt, row(norm_mix_pre[0]), row(norm_mix_post[0]), win_ab,
                                          conv_a_w[0], sinks, wout_ab)
    x2, ffn_p0 = _ffn_prompt(0, x1, row(norm_ffn_pre[0]), row(norm_ffn_post[0]), wup, ffn_conv_w[0], wdown)
    x3, hgrn_p = _hgrn_prompt(1, x2, row(norm_mix_pre[1]), row(norm_mix_post[1]), win_c, hgrn_lb_logits,
                              ng, wout_c)
    y_prompt, ffn_p1 = _ffn_prompt(1, x3, row(norm_ffn_pre[1]), row(norm_ffn_post[1]), wup, ffn_conv_w[1], wdown)

    xs = x_sample.reshape(ns * seg, d)
    kc = cache_swa_k[0].reshape(ns, WINDOW, nkv)
    vc = cache_swa_v[0].reshape(ns, WINDOW, nkv)
    s1, conv_s, k_s, v_s = _mix_ab_sample(xs, state_conv_a[0], kc, vc, seg, row(norm_mix_pre[0]),
                                          row(norm_mix_post[0]), win_ab, conv_a_w[0], sinks, wout_ab)
    s2, ffn_s0 = _ffn_sample(0, s1, state_ffn_conv[0], seg, row(norm_ffn_pre[0]), row(norm_ffn_post[0]),
                             wup, ffn_conv_w[0], wdown)
    s3, hgrn_s = _hgrn_sample(1, s2, state_hgrn[0], seg, row(norm_mix_pre[1]), row(norm_mix_post[1]),
                              win_c, hgrn_lb_logits, ng, wout_c)
    y_sample, ffn_s1 = _ffn_sample(1, s3, state_ffn_conv[1], seg, row(norm_ffn_pre[1]), row(norm_ffn_post[1]),
                                   wup, ffn_conv_w[1], wdown)

    kvshape = (KV_HEADS, HEAD_DIM)
    return (y_prompt, y_sample.reshape(ns, seg, d),
            conv_p[None], conv_s[None],
            k_p.reshape(1, bp, WINDOW, *kvshape), k_s.reshape(1, ns, WINDOW, *kvshape),
            v_p.reshape(1, bp, WINDOW, *kvshape), v_s.reshape(1, ns, WINDOW, *kvshape),
            hgrn_p[None], hgrn_s[None],
            jnp.stack([ffn_p0, ffn_p1]), jnp.stack([ffn_s0, ffn_s1]))
```

```python
import functools
import math

import jax
import jax.numpy as jnp
from jax import lax
from jax.experimental import pallas as pl
from jax.experimental.pallas import tpu as pltpu

F32 = jnp.float32
BF16 = jnp.bfloat16

EPS = 1e-6
LOG2E = 1.4426950408889634
WINDOW = 128
KV_HEADS = 2
GROUP = 4
HEAD_DIM = 64
C_HEADS = 8
C_DK = 128
NEG = -1e30
SUBLANES = 8
LANES = 128
MIX_TILE = 1024
HGRN_TILE = 512
FFN_TILE = 1024
HGRN_CHUNK = 64
DECAY_LOG2_LIMIT = 100.0
FFN_CHUNK = 256
SAMPLE_SEQ_BLOCK = 8
VMEM_LIMIT_V7X = 56 * 2**20


def _dot(a, b):
    return jnp.dot(a, b, preferred_element_type=F32)


def _dot_nt(a, b):
    return lax.dot_general(a, b, (((1,), (1,)), ((), ())), preferred_element_type=F32)


def _dot_tn(a, b):
    return lax.dot_general(a, b, (((0,), (0,)), ((), ())), preferred_element_type=F32)


def _rms(x, g):
    return x * lax.rsqrt(jnp.mean(x * x, axis=-1, keepdims=True) + EPS) * g


def _gelu_tanh(x):
    c = math.sqrt(2.0 / math.pi)
    return x * (0.5 * (1.0 + jnp.tanh(c * (x + 0.044715 * (x * x * x)))))


def _sigmoid(x):
    return 1.0 / (1.0 + jnp.exp(-x))


def _resident(shape):
    nd = len(shape)
    return pl.BlockSpec(shape, lambda *_: (0,) * nd, pipeline_mode=pl.Buffered(1))


def _params(sem):
    return pltpu.CompilerParams(dimension_semantics=sem, vmem_limit_bytes=VMEM_LIMIT_V7X)


def _head_slope(h):
    return 2.0 ** (-8.0 * (h + 1) / (KV_HEADS * GROUP))


def _mix_ab_prompt_kernel(x_ref, gpre_ref, gpost_ref, win_ref, cw_ref, sink_ref, wout_ref,
                          y_ref, conv_ref, kout_ref, vout_ref,
                          ubuf, qbuf, kbuf, vbuf, mixbuf, bias_s):
    s = pl.program_id(1)
    ts = x_ref.shape[0]
    aw = cw_ref.shape[1]
    nq = GROUP * KV_HEADS * HEAD_DIM
    nkv = KV_HEADS * HEAD_DIM

    @pl.when(s == 0)
    def _():
        ubuf[0:SUBLANES, :] = jnp.zeros((SUBLANES, aw), F32)
        kbuf[0:WINDOW, :] = jnp.zeros((WINDOW, nkv), BF16)
        vbuf[0:WINDOW, 0:nkv] = jnp.zeros((WINDOW, nkv), BF16)
        vbuf[:, nkv:2 * nkv] = jnp.ones((ts + WINDOW, nkv), BF16)

    x = x_ref[...]
    h = _rms(x, gpre_ref[...]).astype(BF16)

    u = _dot(h, win_ref[:, 2 * aw:3 * aw]) * _dot(h, win_ref[:, 0:aw])
    ubuf[SUBLANES:SUBLANES + ts, :] = u
    cw = cw_ref[...]
    conv = (cw[2:3, :] * u + cw[1:2, :] * ubuf[SUBLANES - 1:SUBLANES - 1 + ts, :]
            + cw[0:1, :] * ubuf[SUBLANES - 2:SUBLANES - 2 + ts, :])
    mixbuf[:, 0:aw] = (_dot(h, win_ref[:, aw:2 * aw]) * conv).astype(BF16)
    tail = ubuf[ts + SUBLANES - 2:ts + SUBLANES, :]
    conv_ref[...] = tail
    ubuf[SUBLANES - 2:SUBLANES, :] = tail

    q0 = 3 * aw
    q = _dot(h, win_ref[:, q0:q0 + nq]) * (HEAD_DIM ** -0.5 * LOG2E)
    lane = lax.broadcasted_iota(jnp.int32, (1, nkv), 1)
    lo = lane < HEAD_DIM
    nblk = ts // WINDOW
    for n in range(nblk):
        for g in range(GROUP):
            qg = q[n * WINDOW:(n + 1) * WINDOW, g * nkv:(g + 1) * nkv]
            qbuf[n, g * WINDOW:(g + 1) * WINDOW, :] = jnp.where(lo, qg, 0.0).astype(BF16)
            qbuf[n, (GROUP + g) * WINDOW:(GROUP + g + 1) * WINDOW, :] = jnp.where(lo, 0.0, qg).astype(BF16)
    wkv = win_ref[:, q0 + nq:q0 + nq + 2 * nkv]
    kv_new = jnp.concatenate([_dot(h[0:ts // 2, :], wkv), _dot(h[ts // 2:, :], wkv)], axis=0)
    k = kv_new[:, 0:nkv]
    v = kv_new[:, nkv:2 * nkv]
    kbuf[WINDOW:WINDOW + ts, :] = k.astype(BF16)
    vbuf[WINDOW:WINDOW + ts, 0:nkv] = v.astype(BF16)
    kout_ref[...] = k[ts - WINDOW:, :]
    vout_ref[...] = v[ts - WINDOW:, :]

    qi = lax.broadcasted_iota(jnp.int32, (WINDOW, 2 * WINDOW), 0)
    kj = lax.broadcasted_iota(jnp.int32, (WINDOW, 2 * WINDOW), 1)
    dist = qi + WINDOW - kj
    valid = (dist >= 0) & (dist < WINDOW)
    negdist = jnp.where(valid, -dist.astype(F32), 0.0)
    maskadd = jnp.where(valid, 0.0, NEG)
    nheads = KV_HEADS * GROUP
    for hidx in range(nheads):
        bias_s[hidx] = (_head_slope(hidx) * LOG2E) * negdist + maskadd
    first = jnp.where(kj < WINDOW, jnp.where(s == 0, NEG, 0.0), 0.0)

    for n in range(nblk):
        r0 = n * WINDOW
        half = GROUP * WINDOW
        keys = kbuf[r0:r0 + 2 * WINDOW, :]
        sc_kv = [_dot_nt(qbuf[n, kv * half:(kv + 1) * half, :], keys) for kv in range(KV_HEADS)]
        ps, sink_terms = [], []
        for hidx in range(nheads):
            g = hidx % GROUP
            sc = sc_kv[hidx // GROUP][g * WINDOW:(g + 1) * WINDOW, :] + bias_s[hidx]
            if n == 0:
                sc = sc + first
            sink = sink_ref[hidx] * LOG2E
            m = jnp.maximum(jnp.max(sc, axis=-1, keepdims=True), sink)
            ps.append(jnp.exp2(sc - m).astype(BF16))
            sink_terms.append(jnp.exp2(sink - m))
        vals = vbuf[r0:r0 + 2 * WINDOW, :]
        pv_kv = [_dot(jnp.concatenate(ps[kv * GROUP:(kv + 1) * GROUP], axis=0), vals) for kv in range(KV_HEADS)]
        outs = []
        for hidx in range(nheads):
            g = hidx % GROUP
            pv_h = pv_kv[hidx // GROUP][g * WINDOW:(g + 1) * WINDOW, :]
            outs.append(pv_h[:, 0:nkv] / (pv_h[:, nkv:2 * nkv] + sink_terms[hidx]))
        for g in range(GROUP):
            og = jnp.where(lo, outs[g], outs[GROUP + g])
            mixbuf[r0:r0 + WINDOW, aw + g * nkv:aw + (g + 1) * nkv] = og.astype(BF16)

    kbuf[0:WINDOW, :] = kbuf[ts:ts + WINDOW, :]
    vbuf[0:WINDOW, :] = vbuf[ts:ts + WINDOW, :]

    out = _dot(mixbuf[...], wout_ref[...])
    y_ref[...] = x + _rms(out, gpost_ref[...])


def _mix_ab_prompt(x, gpre, gpost, win, cw, sinks, wout):
    b, seq, d = x.shape
    ts = MIX_TILE
    assert seq % ts == 0 and ts % WINDOW == 0, (seq, ts)
    aw = cw.shape[1]
    nq = GROUP * KV_HEADS * HEAD_DIM
    nkv = KV_HEADS * HEAD_DIM
    tile = pl.BlockSpec((None, ts, d), lambda i, j: (i, j, 0))
    return pl.pallas_call(
        _mix_ab_prompt_kernel,
        grid=(b, seq // ts),
        in_specs=[tile, _resident(gpre.shape), _resident(gpost.shape), _resident(win.shape),
                  _resident(cw.shape), pl.BlockSpec(memory_space=pltpu.SMEM), _resident(wout.shape)],
        out_specs=[tile,
                   pl.BlockSpec((None, 2, aw), lambda i, j: (i, 0, 0)),
                   pl.BlockSpec((None, WINDOW, nkv), lambda i, j: (i, 0, 0)),
                   pl.BlockSpec((None, WINDOW, nkv), lambda i, j: (i, 0, 0))],
        out_shape=[jax.ShapeDtypeStruct((b, seq, d), F32),
                   jax.ShapeDtypeStruct((b, 2, aw), F32),
                   jax.ShapeDtypeStruct((b, WINDOW, nkv), F32),
                   jax.ShapeDtypeStruct((b, WINDOW, nkv), F32)],
        scratch_shapes=[pltpu.VMEM((ts + SUBLANES, aw), F32),
                        pltpu.VMEM((ts // WINDOW, KV_HEADS * GROUP * WINDOW, nkv), BF16),
                        pltpu.VMEM((ts + WINDOW, nkv), BF16),
                        pltpu.VMEM((ts + WINDOW, 2 * nkv), BF16),
                        pltpu.VMEM((ts, d), BF16),
                        pltpu.VMEM((KV_HEADS * GROUP, WINDOW, 2 * WINDOW), F32)],
        compiler_params=_params(("arbitrary", "arbitrary")),
        name="mix_ab_prompt",
    )(x, gpre, gpost, win, cw, sinks.reshape(-1), wout)


def _ffn_prompt_kernel(x_ref, gpre_ref, gpost_ref, wup_ref, cw_ref, wdown_ref,
                       y_ref, st_ref, abuf, carry, gbuf):
    s = pl.program_id(1)
    rows = x_ref.shape[0]
    ff = cw_ref.shape[1]

    @pl.when(s == 0)
    def _():
        carry[...] = jnp.zeros(carry.shape, F32)

    x = x_ref[...]
    h = _rms(x, gpre_ref[...]).astype(BF16)
    for c in range(ff // FFN_CHUNK):
        c0 = c * FFN_CHUNK
        buf = abuf.at[c % 2]
        a = _dot(h, wup_ref[:, c0:c0 + FFN_CHUNK])
        b = _dot(h, wup_ref[:, ff + c0:ff + c0 + FFN_CHUNK])
        buf[0:SUBLANES, :] = carry[:, c0:c0 + FFN_CHUNK]
        buf[SUBLANES:SUBLANES + rows, :] = a
        cw = cw_ref[:, c0:c0 + FFN_CHUNK]
        conv = (cw[2:3, :] * a + cw[1:2, :] * buf[SUBLANES - 1:SUBLANES - 1 + rows, :]
                + cw[0:1, :] * buf[SUBLANES - 2:SUBLANES - 2 + rows, :])
        tail = a[rows - 2:rows, :]
        st_ref[:, c0:c0 + FFN_CHUNK] = tail
        carry[SUBLANES - 2:SUBLANES, c0:c0 + FFN_CHUNK] = tail
        gbuf[:, c0:c0 + FFN_CHUNK] = (_gelu_tanh(conv) * b).astype(BF16)
    out = _dot(gbuf[...], wdown_ref[...])
    y_ref[...] = x + _rms(out, gpost_ref[...])


def _ffn_prompt(layer, x, gpre, gpost, wup, cw, wdown):
    b, seq, d = x.shape
    ts = FFN_TILE
    ff = cw.shape[1]
    assert seq % ts == 0 and ff % FFN_CHUNK == 0, (seq, ts, ff)
    tile = pl.BlockSpec((None, ts, d), lambda i, j: (i, j, 0))
    layer_block = lambda w: pl.BlockSpec((None,) + w.shape[1:], lambda i, j: (layer, 0, 0),
                                         pipeline_mode=pl.Buffered(1))
    return pl.pallas_call(
        _ffn_prompt_kernel,
        grid=(b, seq // ts),
        in_specs=[tile, _resident(gpre.shape), _resident(gpost.shape), layer_block(wup),
                  _resident(cw.shape), layer_block(wdown)],
        out_specs=[tile, pl.BlockSpec((None, 2, ff), lambda i, j: (i, 0, 0))],
        out_shape=[jax.ShapeDtypeStruct((b, seq, d), F32), jax.ShapeDtypeStruct((b, 2, ff), F32)],
        scratch_shapes=[pltpu.VMEM((2, ts + SUBLANES, FFN_CHUNK), F32), pltpu.VMEM((SUBLANES, ff), F32),
                        pltpu.VMEM((ts, ff), BF16)],
        compiler_params=_params(("arbitrary", "arbitrary")),
        name="ffn_prompt",
    )(x, gpre, gpost, wup, cw, wdown)


def _segment_conv_taps(a, t, seg, prev0_ref, prev1_ref, abuf, fix1, fix2):
    rows, c = a.shape
    nseq = rows // seg
    am1, am2 = [], []
    for k in range(c // LANES):
        sl = slice(k * LANES, (k + 1) * LANES)
        abuf[k, SUBLANES:SUBLANES + rows, :] = a[:, sl]
        fix1[k, pl.ds(0, nseq, stride=seg), :] = prev1_ref[:, sl]
        fix2[k, pl.ds(0, nseq, stride=seg), :] = prev0_ref[:, sl]
        fix2[k, pl.ds(1, nseq, stride=seg), :] = prev1_ref[:, sl]
        am1.append(jnp.where(t >= 1, abuf[k, SUBLANES - 1:SUBLANES - 1 + rows, :], fix1[k]))
        am2.append(jnp.where(t >= 2, abuf[k, SUBLANES - 2:SUBLANES - 2 + rows, :], fix2[k]))
    return jnp.concatenate(am1, axis=1), jnp.concatenate(am2, axis=1)


def _segment_tails(abuf, nseq, seg):
    tail = lambda back: jnp.concatenate(
        [abuf[k, pl.ds(SUBLANES + seg - back, nseq, stride=seg), :] for k in range(abuf.shape[0])], axis=1)
    return tail(2), tail(1)


def _ffn_sample_kernel(seg, x_ref, prev0_ref, prev1_ref, gpre_ref, gpost_ref, wa_ref, wb_ref, cw_ref, wdown_ref,
                       y_ref, st0_ref, st1_ref, hbuf, abuf, fix1, fix2, acc):
    c = pl.program_id(0)
    rows = x_ref.shape[0]

    @pl.when(c == 0)
    def _():
        hbuf[...] = _rms(x_ref[...], gpre_ref[...]).astype(BF16)
        acc[...] = jnp.zeros(acc.shape, F32)
        abuf[:, 0:SUBLANES, :] = jnp.zeros((abuf.shape[0], SUBLANES, LANES), F32)
        fix1[...] = jnp.zeros(fix1.shape, F32)
        fix2[...] = jnp.zeros(fix2.shape, F32)

    h = hbuf[...]
    t = lax.broadcasted_iota(jnp.int32, (rows, 1), 0) % seg
    a = _dot(h, wa_ref[...])
    b = _dot(h, wb_ref[...])
    am1, am2 = _segment_conv_taps(a, t, seg, prev0_ref, prev1_ref, abuf, fix1, fix2)
    st0_ref[...], st1_ref[...] = _segment_tails(abuf, rows // seg, seg)
    cw = cw_ref[...]
    conv = cw[2:3, :] * a + cw[1:2, :] * am1 + cw[0:1, :] * am2
    acc[...] += _dot((_gelu_tanh(conv) * b).astype(BF16), wdown_ref[...])

    @pl.when(c == pl.num_programs(0) - 1)
    def _():
        y_ref[...] = x_ref[...] + _rms(acc[...], gpost_ref[...])


def _ffn_sample(layer, x, prev, seg, gpre, gpost, wup, cw, wdown):
    rows, d = x.shape
    nseq = rows // seg
    ff = cw.shape[1]
    cf = FFN_CHUNK
    nc = ff // cf
    cols = lambda r: pl.BlockSpec((r, cf), lambda c: (0, c))
    y, st0, st1 = pl.pallas_call(
        functools.partial(_ffn_sample_kernel, seg),
        grid=(nc,),
        in_specs=[_resident(x.shape), cols(nseq), cols(nseq), _resident(gpre.shape), _resident(gpost.shape),
                  pl.BlockSpec((None, d, cf), lambda c: (layer, 0, c)),
                  pl.BlockSpec((None, d, cf), lambda c: (layer, 0, c + nc)), cols(3),
                  pl.BlockSpec((None, cf, d), lambda c: (layer, c, 0))],
        out_specs=[pl.BlockSpec((rows, d), lambda c: (0, 0)), cols(nseq), cols(nseq)],
        out_shape=[jax.ShapeDtypeStruct((rows, d), F32), jax.ShapeDtypeStruct((nseq, ff), F32),
                   jax.ShapeDtypeStruct((nseq, ff), F32)],
        scratch_shapes=[pltpu.VMEM((rows, d), BF16), pltpu.VMEM((cf // LANES, rows + SUBLANES, LANES), F32),
                        pltpu.VMEM((cf // LANES, rows, LANES), F32), pltpu.VMEM((cf // LANES, rows, LANES), F32),
                        pltpu.VMEM((rows, d), F32)],
        compiler_params=_params(("arbitrary",)),
        name="ffn_sample",
    )(x, prev[:, 0], prev[:, 1], gpre, gpost, wup, wup, cw, wdown)
    return y, jnp.stack([st0, st1], axis=1)


def _lower_bound_terms(logits_ref, layer):
    lg = logits_ref[...]
    e = jnp.exp(lg - jnp.max(lg, axis=0, keepdims=True))
    pr = e / jnp.sum(e, axis=0, keepdims=True)
    cum = pr[0:1, :]
    for r in range(1, layer + 1):
        cum = cum + pr[r:r + 1, :]
    lb = jnp.clip(cum - pr[0:1, :], 0.0, 1.0)
    return jnp.log(lb) * LOG2E, jnp.log1p(-lb) * LOG2E, 1.0 - lb


def _forget_gate(f, lb_terms):
    log2_lb, log2_1m_lb, one_m_lb = lb_terms
    f2 = f * LOG2E
    e = jnp.exp2(jnp.minimum(f2, -f2))
    w = 1.0 + e
    b = log2_1m_lb + (jnp.minimum(f2, 0.0) - jnp.log2(w))
    d = log2_lb - b
    log2_ft = jnp.maximum(log2_lb, b) + jnp.log2(1.0 + jnp.exp2(jnp.minimum(d, -d)))
    return log2_ft, one_m_lb * (jnp.where(f2 >= 0.0, e, 1.0) / w)


def _split_bf16(x):
    hi = x.astype(BF16)
    return hi, (x - hi.astype(F32)).astype(BF16)


def _hgrn_gate_out(x, h, ob, mixbuf, win_ref, ng_ref, wout_ref, gpost_ref):
    dkt = C_HEADS * C_DK
    g = _dot(h, win_ref[:, 3 * dkt:4 * dkt])
    ng = ng_ref[...]
    for hh in range(C_HEADS):
        sl = slice(hh * C_DK, (hh + 1) * C_DK)
        mixbuf[:, sl] = (_rms(ob[:, sl], ng) * (g[:, sl] * _sigmoid(g[:, sl]))).astype(BF16)
    out = _dot(mixbuf[...], wout_ref[...])
    return x + _rms(out, gpost_ref[...])


def _hgrn_exact_tile(q, k_in, log2_ft, vb, st, ob, sq, sk, slg):
    ts = sq.shape[0]
    sq[...] = q
    sk[...] = k_in
    slg[...] = log2_ft
    ob[...] = jnp.zeros(ob.shape, F32)
    grp = 2 * SUBLANES
    row = lax.broadcasted_iota(jnp.int32, (grp, 1), 0)

    def body(t, carry):
        t0 = pl.multiple_of(lax.bitwise_and(t, -grp), grp)
        sel = row == lax.bitwise_and(t, grp - 1)
        for hh in range(C_HEADS):
            sl = slice(hh * C_DK, (hh + 1) * C_DK)
            rows_of = lambda ref: jnp.where(sel, ref[pl.ds(t0, grp), sl].astype(F32), 0.0)
            pick = lambda ref: rows_of(ref).astype(BF16)
            log2_f = jnp.sum(rows_of(slg), axis=0, keepdims=True)
            s_new = st[hh] * jnp.exp2(log2_f) + _dot_tn(pick(vb), pick(sk))
            st[hh] = s_new
            ob[pl.ds(t0, grp), sl] += _dot_nt(pick(sq), s_new.astype(BF16))
        return carry

    lax.fori_loop(0, ts, body, 0)


def _hgrn_prompt_kernel(layer, x_ref, gpre_ref, gpost_ref, win_ref, lbl_ref, ng_ref, wout_ref,
                        y_ref, sout_ref, st, st0, qtb, ktb, qib, klb, eglb, vb, ob, mixbuf, sq, sk, slg):
    s = pl.program_id(1)
    ts = x_ref.shape[0]
    dkt = C_HEADS * C_DK
    ck = HGRN_CHUNK

    @pl.when(s == 0)
    def _():
        st[...] = jnp.zeros(st.shape, F32)

    x = x_ref[...]
    h = _rms(x, gpre_ref[...]).astype(BF16)
    f = _dot(h, win_ref[:, dkt:2 * dkt])
    log2_ft, k_in = _forget_gate(f, _lower_bound_terms(lbl_ref, layer))
    q = _dot(h, win_ref[:, 0:dkt])
    vb[...] = _dot(h, win_ref[:, 2 * dkt:3 * dkt]).astype(BF16)

    ri = lax.broadcasted_iota(jnp.int32, (ck, ck), 0)
    ci = lax.broadcasted_iota(jnp.int32, (ck, ck), 1)
    causal = ri >= ci
    tri = jnp.where(causal, 1.0, 0.0).astype(BF16)
    mid = ck // 2 - 1
    nck = ts // ck

    spread = jnp.zeros((1, dkt), F32)
    for c in range(nck):
        rows = slice(c * ck, (c + 1) * ck)
        hi, lo = _split_bf16(log2_ft[rows, :])
        gcum = _dot(tri, hi) + _dot(tri, lo)
        gr = gcum[mid:mid + 1, :]
        gl = gcum[ck - 1:ck, :]
        spread = jnp.maximum(spread, jnp.maximum(gcum[0:1, :] - gr, gr - gl))
        qc = q[rows, :]
        kc = k_in[rows, :]
        qtb[rows, :] = (qc * jnp.exp2(gcum - gr)).astype(BF16)
        ktb[rows, :] = (kc * jnp.exp2(gr - gcum)).astype(BF16)
        qib[rows, :] = (qc * jnp.exp2(gcum)).astype(BF16)
        klb[rows, :] = (kc * jnp.exp2(gl - gcum)).astype(BF16)
        eglb[c] = jnp.exp2(gl)

    st0[...] = st[...]
    for c in range(nck):
        rows = slice(c * ck, (c + 1) * ck)
        for hh in range(C_HEADS):
            sl = slice(hh * C_DK, (hh + 1) * C_DK)
            vh = vb[rows, sl]
            att = jnp.where(causal, _dot_nt(qtb[rows, sl], ktb[rows, sl]), 0.0).astype(BF16)
            st_h = st[hh]
            ob[rows, sl] = _dot(jnp.concatenate([qib[rows, sl], att], axis=1),
                                jnp.concatenate([st_h.astype(BF16).T, vh], axis=0))
            st[hh] = eglb[c, :, sl] * st_h + _dot_tn(vh, klb[rows, sl])

    y_ref[...] = _hgrn_gate_out(x, h, ob, mixbuf, win_ref, ng_ref, wout_ref, gpost_ref)

    @pl.when(jnp.max(spread) > DECAY_LOG2_LIMIT)
    def _():
        st[...] = st0[...]
        _hgrn_exact_tile(q, k_in, log2_ft, vb, st, ob, sq, sk, slg)
        y_ref[...] = _hgrn_gate_out(x, h, ob, mixbuf, win_ref, ng_ref, wout_ref, gpost_ref)

    @pl.when(s == pl.num_programs(1) - 1)
    def _():
        for hh in range(C_HEADS):
            sout_ref[hh] = st[hh].T


def _hgrn_prompt(layer, x, gpre, gpost, win, lbl, ng, wout):
    b, seq, d = x.shape
    ts = HGRN_TILE
    assert seq % ts == 0 and ts % HGRN_CHUNK == 0, (seq, ts)
    dkt = C_HEADS * C_DK
    tile = pl.BlockSpec((None, ts, d), lambda i, j: (i, j, 0))
    return pl.pallas_call(
        functools.partial(_hgrn_prompt_kernel, layer),
        grid=(b, seq // ts),
        in_specs=[tile, _resident(gpre.shape), _resident(gpost.shape), _resident(win.shape),
                  _resident(lbl.shape), _resident(ng.shape), _resident(wout.shape)],
        out_specs=[tile, pl.BlockSpec((None, C_HEADS, C_DK, C_DK), lambda i, j: (i, 0, 0, 0))],
        out_shape=[jax.ShapeDtypeStruct((b, seq, d), F32),
                   jax.ShapeDtypeStruct((b, C_HEADS, C_DK, C_DK), F32)],
        scratch_shapes=[pltpu.VMEM((C_HEADS, C_DK, C_DK), F32), pltpu.VMEM((C_HEADS, C_DK, C_DK), F32),
                        pltpu.VMEM((ts, dkt), BF16), pltpu.VMEM((ts, dkt), BF16),
                        pltpu.VMEM((ts, dkt), BF16), pltpu.VMEM((ts, dkt), BF16),
                        pltpu.VMEM((ts // HGRN_CHUNK, 1, dkt), F32),
                        pltpu.VMEM((ts, dkt), BF16), pltpu.VMEM((ts, dkt), F32),
                        pltpu.VMEM((ts, dkt), BF16),
                        pltpu.VMEM((ts, dkt), F32), pltpu.VMEM((ts, dkt), F32), pltpu.VMEM((ts, dkt), F32)],
        compiler_params=_params(("arbitrary", "arbitrary")),
        name="hgrn_prompt",
    )(x, gpre, gpost, win, lbl, ng, wout)


def _mix_ab_sample_kernel(seg, x_ref, prev0_ref, prev1_ref, kc_ref, vc_ref, gpre_ref, gpost_ref, win_ref,
                          cw_ref, sink_ref, wout_ref,
                          y_ref, conv0_ref, conv1_ref, kout_ref, vout_ref,
                          ubuf, fix1, fix2, qbuf, knew_s, vnew_s, yabuf, ybbuf, kall, vall, bias_s, sink_s):
    i = pl.program_id(0)
    rows = x_ref.shape[0]
    aw = cw_ref.shape[1]
    nq = GROUP * KV_HEADS * HEAD_DIM
    nkv = KV_HEADS * HEAD_DIM
    pair = 2 * seg
    qrows = KV_HEADS * GROUP * pair
    tk = WINDOW + pair

    @pl.when(i == 0)
    def _():
        x = x_ref[...]
        h = _rms(x, gpre_ref[...]).astype(BF16)
        t = lax.broadcasted_iota(jnp.int32, (rows, 1), 0) % seg
        u = _dot(h, win_ref[:, 2 * aw:3 * aw]) * _dot(h, win_ref[:, 0:aw])
        ubuf[:, 0:SUBLANES, :] = jnp.zeros((ubuf.shape[0], SUBLANES, LANES), F32)
        fix1[...] = jnp.zeros(fix1.shape, F32)
        fix2[...] = jnp.zeros(fix2.shape, F32)
        um1, um2 = _segment_conv_taps(u, t, seg, prev0_ref, prev1_ref, ubuf, fix1, fix2)
        conv0_ref[...], conv1_ref[...] = _segment_tails(ubuf, rows // seg, seg)
        cw = cw_ref[...]
        conv = cw[2:3, :] * u + cw[1:2, :] * um1 + cw[0:1, :] * um2
        yabuf[...] = (_dot(h, win_ref[:, aw:2 * aw]) * conv).astype(BF16)
        q0 = 3 * aw
        qbuf[...] = _dot(h, win_ref[:, q0:q0 + nq]) * (HEAD_DIM ** -0.5)
        kv_new = _dot(h, win_ref[:, q0 + nq:q0 + nq + 2 * nkv])
        knew_s[...] = kv_new[:, 0:nkv]
        vnew_s[...] = kv_new[:, nkv:2 * nkv]

        r = lax.broadcasted_iota(jnp.int32, (qrows, tk), 0)
        j = lax.broadcasted_iota(jnp.int32, (qrows, tk), 1)
        dist = r % seg + WINDOW - j
        valid = (dist >= 0) & (dist < WINDOW) & (j < WINDOW + seg)
        rc = lax.broadcasted_iota(jnp.int32, (qrows, 1), 0)
        head = (rc // (GROUP * pair)) * GROUP + (rc // pair) % GROUP
        slope = jnp.zeros((qrows, 1), F32)
        sink = jnp.zeros((qrows, 1), F32)
        for hidx in range(KV_HEADS * GROUP):
            slope = jnp.where(head == hidx, _head_slope(hidx), slope)
            sink = jnp.where(head == hidx, sink_ref[:, hidx:hidx + 1], sink)
        bias_s[...] = jnp.where(valid, -slope * dist.astype(F32), NEG)
        sink_s[...] = sink

    nblk = kc_ref.shape[0]
    lane = lax.broadcasted_iota(jnp.int32, (1, nkv), 1)
    lo = lane < HEAD_DIM
    rq = lax.broadcasted_iota(jnp.int32, (qrows, 1), 0)
    first_seq = (rq // seg) % 2 == 0

    def body(p, carry):
        r0 = pl.multiple_of((i * nblk + 2 * p) * seg, pair)
        q8 = qbuf[pl.ds(r0, pair), :]
        qg = jnp.concatenate([q8[:, g * nkv:(g + 1) * nkv] for g in range(GROUP)], axis=0)
        qs = jnp.concatenate([jnp.where(lo, qg, 0.0), jnp.where(lo, 0.0, qg)], axis=0).astype(BF16)
        knew = knew_s[pl.ds(r0, pair), :]
        vnew = vnew_s[pl.ds(r0, pair), :]
        sc, ov = [], []
        for a in range(2):
            own_first = (lambda z: z) if a == 0 else (lambda z: jnp.concatenate([z[seg:], z[:seg]], axis=0))
            kall[a, 0:WINDOW, :] = kc_ref[2 * p + a]
            kall[a, WINDOW:tk, :] = own_first(knew)
            vall[a, 0:WINDOW, :] = vc_ref[2 * p + a]
            vall[a, WINDOW:tk, :] = own_first(vnew)
            kout_ref[2 * p + a] = kall[a, seg:seg + WINDOW, :]
            vout_ref[2 * p + a] = vall[a, seg:seg + WINDOW, :]
            sc.append(_dot_nt(qs, kall[a].astype(BF16)))
        s2 = jnp.where(first_seq, sc[0], sc[1]) + bias_s[...]
        sink = sink_s[...]
        m = jnp.maximum(jnp.max(s2, axis=-1, keepdims=True), sink)
        pr = jnp.exp(s2 - m)
        den = jnp.sum(pr, axis=-1, keepdims=True) + jnp.exp(sink - m)
        prb = pr.astype(BF16)
        for a in range(2):
            ov.append(_dot(prb, vall[a].astype(BF16)))
        o = jnp.where(first_seq, ov[0], ov[1]) / den
        og = jnp.where(lo, o[0:GROUP * pair, :], o[GROUP * pair:, :])
        for g in range(GROUP):
            ybbuf[pl.ds(r0, pair), g * nkv:(g + 1) * nkv] = og[g * pair:(g + 1) * pair, :]
        return carry

    lax.fori_loop(0, nblk // 2, body, 0, unroll=True)

    @pl.when(i == pl.num_programs(0) - 1)
    def _():
        out = _dot(yabuf[...], wout_ref[0:aw, :]) + _dot(ybbuf[...].astype(BF16), wout_ref[aw:, :])
        y_ref[...] = x_ref[...] + _rms(out, gpost_ref[...])


def _mix_ab_sample(x, conv_prev, kcache, vcache, seg, gpre, gpost, win, cw, sinks, wout):
    rows, d = x.shape
    nseq = rows // seg
    aw = cw.shape[1]
    nq = GROUP * KV_HEADS * HEAD_DIM
    nkv = KV_HEADS * HEAD_DIM
    sb = SAMPLE_SEQ_BLOCK
    pair = 2 * seg
    qrows = KV_HEADS * GROUP * pair
    tk = WINDOW + pair
    full = lambda shape: pl.BlockSpec(shape, lambda i: (0,) * len(shape))
    cache = pl.BlockSpec((sb, WINDOW, nkv), lambda i: (i, 0, 0))
    prev0, prev1 = conv_prev[:, 0], conv_prev[:, 1]
    y, conv0, conv1, k_new, v_new = pl.pallas_call(
        functools.partial(_mix_ab_sample_kernel, seg),
        grid=(nseq // sb,),
        in_specs=[_resident(x.shape), _resident(prev0.shape), _resident(prev1.shape), cache, cache,
                  _resident(gpre.shape), _resident(gpost.shape), _resident(win.shape), _resident(cw.shape),
                  _resident(sinks.shape), _resident(wout.shape)],
        out_specs=[full((rows, d)), full((nseq, aw)), full((nseq, aw)), cache, cache],
        out_shape=[jax.ShapeDtypeStruct((rows, d), F32), jax.ShapeDtypeStruct((nseq, aw), F32),
                   jax.ShapeDtypeStruct((nseq, aw), F32), jax.ShapeDtypeStruct(kcache.shape, F32),
                   jax.ShapeDtypeStruct(vcache.shape, F32)],
        scratch_shapes=[pltpu.VMEM((aw // LANES, rows + SUBLANES, LANES), F32),
                        pltpu.VMEM((aw // LANES, rows, LANES), F32),
                        pltpu.VMEM((aw // LANES, rows, LANES), F32),
                        pltpu.VMEM((rows, nq), F32),
                        pltpu.VMEM((rows, nkv), F32),
                        pltpu.VMEM((rows, nkv), F32),
                        pltpu.VMEM((rows, aw), BF16),
                        pltpu.VMEM((rows, nq), F32),
                        pltpu.VMEM((2, tk, nkv), F32),
                        pltpu.VMEM((2, tk, nkv), F32),
                        pltpu.VMEM((qrows, tk), F32),
                        pltpu.VMEM((qrows, 1), F32)],
        compiler_params=_params(("arbitrary",)),
        name="mix_ab_sample",
    )(x, prev0, prev1, kcache, vcache, gpre, gpost, win, cw, sinks, wout)
    return y, jnp.stack([conv0, conv1], axis=1), k_new, v_new


def _hgrn_sample_kernel(layer, seg, x_ref, s0_ref, gpre_ref, gpost_ref, win_ref, lbl_ref, ng_ref, wout_ref,
                        y_ref, s1_ref, qb, kb, vb, gb, ob, mixbuf):
    i = pl.program_id(0)
    rows = x_ref.shape[0]
    dkt = C_HEADS * C_DK
    pair = 2 * seg

    @pl.when(i == 0)
    def _():
        x = x_ref[...]
        h = _rms(x, gpre_ref[...]).astype(BF16)
        f = _dot(h, win_ref[:, dkt:2 * dkt])
        log2_ft, kb[...] = _forget_gate(f, _lower_bound_terms(lbl_ref, layer))
        hi, lo = _split_bf16(log2_ft)
        ri = lax.broadcasted_iota(jnp.int32, (rows, rows), 0)
        ci = lax.broadcasted_iota(jnp.int32, (rows, rows), 1)
        tri = jnp.where((ri // seg == ci // seg) & (ri >= ci), 1.0, 0.0).astype(BF16)
        gb[...] = _dot(tri, hi) + _dot(tri, lo)
        qb[...] = _dot(h, win_ref[:, 0:dkt])
        vb[...] = _dot(h, win_ref[:, 2 * dkt:3 * dkt])

    nblk = s0_ref.shape[0]
    rr = lax.broadcasted_iota(jnp.int32, (pair, 1), 0)
    first_seq = rr < seg
    second_seq = rr >= seg
    ri = lax.broadcasted_iota(jnp.int32, (pair, pair), 0)
    ci = lax.broadcasted_iota(jnp.int32, (pair, pair), 1)
    key_s = [(ri // seg == ci // seg) & (ci % seg == sk) & (ri % seg >= sk) for sk in range(seg)]

    def body(p, carry):
        r0 = pl.multiple_of((i * nblk + 2 * p) * seg, pair)
        for hh in range(C_HEADS):
            sl = slice(hh * C_DK, (hh + 1) * C_DK)
            gh = gb[pl.ds(r0, pair), sl]
            g_row = lambda t: jnp.where(first_seq, gh[t:t + 1, :], gh[seg + t:seg + t + 1, :])
            gl = g_row(seg - 1)
            qh = qb[pl.ds(r0, pair), sl]
            kh = kb[pl.ds(r0, pair), sl]
            vh = vb[pl.ds(r0, pair), sl].astype(BF16)
            q_rel = [(qh * jnp.exp2(gh - g_row(sk))).astype(BF16) for sk in range(seg - 1)] + [qh.astype(BF16)]
            rel = _dot_nt(jnp.concatenate(q_rel, axis=0), kh.astype(BF16))
            att = jnp.where(key_s[0], rel[0:pair, :], 0.0)
            for sk in range(1, seg):
                att = att + jnp.where(key_s[sk], rel[sk * pair:(sk + 1) * pair, :], 0.0)
            att = att.astype(BF16)
            qi = (qh * jnp.exp2(gh)).astype(BF16)
            kl = kh * jnp.exp2(gl - gh)
            o_prev = []
            for a in range(2):
                s_a = s0_ref[2 * p + a, hh]
                o_prev.append(_dot(qi, s_a.astype(BF16)))
                kl_a = jnp.where(first_seq if a == 0 else second_seq, kl, 0.0).astype(BF16)
                e = jnp.exp2(gh[(a + 1) * seg - 1:(a + 1) * seg, :])
                e1 = e.astype(BF16).astype(F32)
                e2 = (e - e1).astype(BF16).astype(F32)
                e3 = (e - e1 - e2).astype(BF16).astype(F32)
                dec_l = jnp.where(rr == 0, e1, jnp.where(rr == 1, e2, jnp.where(rr == 2, e3, 0.0)))
                dec = _dot_tn(dec_l.astype(BF16), jnp.ones((pair, C_DK), BF16))
                s1_ref[2 * p + a, hh] = dec * s_a + _dot_tn(kl_a, vh)
            ob[pl.ds(r0, pair), sl] = _dot(att, vh) + jnp.where(first_seq, o_prev[0], o_prev[1])
        return carry

    lax.fori_loop(0, nblk // 2, body, 0, unroll=True)

    @pl.when(i == pl.num_programs(0) - 1)
    def _():
        x = x_ref[...]
        h = _rms(x, gpre_ref[...]).astype(BF16)
        y_ref[...] = _hgrn_gate_out(x, h, ob, mixbuf, win_ref, ng_ref, wout_ref, gpost_ref)


def _hgrn_sample(layer, x, s0, seg, gpre, gpost, win, lbl, ng, wout):
    rows, d = x.shape
    assert 2 * seg == SUBLANES, "two sequences per iteration must fill one f32 vreg of sublanes"
    nseq = rows // seg
    dkt = C_HEADS * C_DK
    sb = SAMPLE_SEQ_BLOCK
    full = lambda shape: pl.BlockSpec(shape, lambda i: (0,) * len(shape))
    state = pl.BlockSpec((sb, C_HEADS, C_DK, C_DK), lambda i: (i, 0, 0, 0))
    return pl.pallas_call(
        functools.partial(_hgrn_sample_kernel, layer, seg),
        grid=(nseq // sb,),
        in_specs=[_resident(x.shape), state, _resident(gpre.shape), _resident(gpost.shape), _resident(win.shape),
                  _resident(lbl.shape), _resident(ng.shape), _resident(wout.shape)],
        out_specs=[full((rows, d)), state],
        out_shape=[jax.ShapeDtypeStruct((rows, d), F32), jax.ShapeDtypeStruct(s0.shape, F32)],
        scratch_shapes=[pltpu.VMEM((rows, dkt), F32), pltpu.VMEM((rows, dkt), F32),
                        pltpu.VMEM((rows, dkt), F32), pltpu.VMEM((rows, dkt), F32),
                        pltpu.VMEM((rows, dkt), F32), pltpu.VMEM((rows, dkt), BF16)],
        compiler_params=_params(("arbitrary",)),
        name="hgrn_sample",
    )(x, s0, gpre, gpost, win, lbl, ng, wout)


def kernel(x_prompt, x_sample, state_conv_a, cache_swa_k, cache_swa_v, state_hgrn, state_ffn_conv, norm_mix_pre, norm_mix_post, norm_ffn_pre, norm_ffn_post, w_in_ab, conv_a_w, attn_sinks, w_out_ab, w_in_c, hgrn_lb_logits, hgrn_norm_g, w_out_c, w_ffn_up, ffn_conv_w, w_ffn_down):
    bp, seq, d = x_prompt.shape
    ns, seg, _ = x_sample.shape
    aw = conv_a_w.shape[-1]
    nq = GROUP * KV_HEADS * HEAD_DIM
    nkv = KV_HEADS * HEAD_DIM

    q0 = 3 * aw
    win_ab = w_in_ab[0].astype(BF16)
    wq = win_ab[:, q0:q0 + nq].reshape(d, KV_HEADS, GROUP, HEAD_DIM).swapaxes(1, 2).reshape(d, nq)
    win_ab = jnp.concatenate([win_ab[:, :q0], wq, win_ab[:, q0 + nq:]], axis=1)
    wout_ab = w_out_ab[0].astype(BF16)
    wo = wout_ab[aw:].reshape(KV_HEADS, GROUP, HEAD_DIM, d).swapaxes(0, 1).reshape(nq, d)
    wout_ab = jnp.concatenate([wout_ab[:aw], wo], axis=0)
    win_c = w_in_c[0].astype(BF16)
    wout_c = w_out_c[0].astype(BF16)
    wup = w_ffn_up.astype(BF16)
    wdown = w_ffn_down.astype(BF16)
    row = lambda a: a.reshape(1, -1)
    sinks = row(attn_sinks[0])
    ng = row(hgrn_norm_g[0])

    xs = x_sample.reshape(ns * seg, d)
    kc = cache_swa_k[0].reshape(ns, WINDOW, nkv)
    vc = cache_swa_v[0].reshape(ns, WINDOW, nkv)
    s1, conv_s, k_s, v_s = _mix_ab_sample(xs, state_conv_a[0], kc, vc, seg, row(norm_mix_pre[0]),
                                          row(norm_mix_post[0]), win_ab, conv_a_w[0], sinks, wout_ab)
    s2, ffn_s0 = _ffn_sample(0, s1, state_ffn_conv[0], seg, row(norm_ffn_pre[0]), row(norm_ffn_post[0]),
                             wup, ffn_conv_w[0], wdown)
    s3, hgrn_s = _hgrn_sample(1, s2, state_hgrn[0], seg, row(norm_mix_pre[1]), row(norm_mix_post[1]),
                              win_c, hgrn_lb_logits, ng, wout_c)
    y_sample, ffn_s1 = _ffn_sample(1, s3, state_ffn_conv[1], seg, row(norm_ffn_pre[1]), row(norm_ffn_post[1]),
                                   wup, ffn_conv_w[1], wdown)

    x1, conv_p, k_p, v_p = _mix_ab_prompt(x_prompt, row(norm_mix_pre[0]), row(norm_mix_post[0]), win_ab,
                                          conv_a_w[0], sinks, wout_ab)
    x2, ffn_p0 = _ffn_prompt(0, x1, row(norm_ffn_pre[0]), row(norm_ffn_post[0]), wup, ffn_conv_w[0], wdown)
    x3, hgrn_p = _hgrn_prompt(1, x2, row(norm_mix_pre[1]), row(norm_mix_post[1]), win_c, hgrn_lb_logits,
                              ng, wout_c)
    y_prompt, ffn_p1 = _ffn_prompt(1, x3, row(norm_ffn_pre[1]), row(norm_ffn_post[1]), wup, ffn_conv_w[1], wdown)

    kvshape = (KV_HEADS, HEAD_DIM)
    return (y_prompt, y_sample.reshape(ns, seg, d),
            conv_p[None], conv_s[None],
            k_p.reshape(1, bp, WINDOW, *kvshape), k_s.reshape(1, ns, WINDOW, *kvshape),
            v_p.reshape(1, bp, WINDOW, *kvshape), v_s.reshape(1, ns, WINDOW, *kvshape),
            hgrn_p[None], hgrn_s[None],
            jnp.stack([ffn_p0, ffn_p1]), jnp.stack([ffn_s0, ffn_s1]))
```

```python
import functools
import math

import jax
import jax.numpy as jnp
from jax import lax
from jax.experimental import pallas as pl
from jax.experimental.pallas import tpu as pltpu

F32 = jnp.float32
BF16 = jnp.bfloat16

EPS = 1e-6
LOG2E = 1.4426950408889634
WINDOW = 128
KV_HEADS = 2
GROUP = 4
HEAD_DIM = 64
C_HEADS = 8
C_DK = 128
NEG = -1e30
SUBLANES = 8
LANES = 128
MIX_TILE = 1024
HGRN_TILE = 512
FFN_TILE = 1024
HGRN_CHUNK = 64
DECAY_LOG2_LIMIT = 100.0
FFN_CHUNK = 256
SAMPLE_FFN_CHUNK = 1408
SAMPLE_SEQ_BLOCK = 8
VMEM_LIMIT_V7X = 56 * 2**20


def _dot(a, b):
    return jnp.dot(a, b, preferred_element_type=F32)


def _dot_nt(a, b):
    return lax.dot_general(a, b, (((1,), (1,)), ((), ())), preferred_element_type=F32)


def _dot_tn(a, b):
    return lax.dot_general(a, b, (((0,), (0,)), ((), ())), preferred_element_type=F32)


def _rms(x, g):
    return x * lax.rsqrt(jnp.mean(x * x, axis=-1, keepdims=True) + EPS) * g


def _gelu_tanh(x):
    c = math.sqrt(2.0 / math.pi)
    return x * (0.5 * (1.0 + jnp.tanh(c * (x + 0.044715 * (x * x * x)))))


def _sigmoid(x):
    return 1.0 / (1.0 + jnp.exp(-x))


def _resident(shape):
    nd = len(shape)
    return pl.BlockSpec(shape, lambda *_: (0,) * nd, pipeline_mode=pl.Buffered(1))


def _params(sem):
    return pltpu.CompilerParams(dimension_semantics=sem, vmem_limit_bytes=VMEM_LIMIT_V7X)


def _head_slope(h):
    return 2.0 ** (-8.0 * (h + 1) / (KV_HEADS * GROUP))


def _mix_ab_prompt_kernel(x_ref, gpre_ref, gpost_ref, win_ref, cw_ref, sink_ref, wout_ref,
                          y_ref, conv_ref, kout_ref, vout_ref,
                          ubuf, qbuf, kbuf, vbuf, mixbuf, bias_s):
    s = pl.program_id(1)
    ts = x_ref.shape[0]
    aw = cw_ref.shape[1]
    nq = GROUP * KV_HEADS * HEAD_DIM
    nkv = KV_HEADS * HEAD_DIM

    @pl.when(s == 0)
    def _():
        ubuf[0:SUBLANES, :] = jnp.zeros((SUBLANES, aw), F32)
        kbuf[0:WINDOW, :] = jnp.zeros((WINDOW, nkv), BF16)
        vbuf[0:WINDOW, 0:nkv] = jnp.zeros((WINDOW, nkv), BF16)
        vbuf[:, nkv:2 * nkv] = jnp.ones((ts + WINDOW, nkv), BF16)

    x = x_ref[...]
    h = _rms(x, gpre_ref[...]).astype(BF16)

    u = _dot(h, win_ref[:, 2 * aw:3 * aw]) * _dot(h, win_ref[:, 0:aw])
    ubuf[SUBLANES:SUBLANES + ts, :] = u
    cw = cw_ref[...]
    conv = (cw[2:3, :] * u + cw[1:2, :] * ubuf[SUBLANES - 1:SUBLANES - 1 + ts, :]
            + cw[0:1, :] * ubuf[SUBLANES - 2:SUBLANES - 2 + ts, :])
    mixbuf[:, 0:aw] = (_dot(h, win_ref[:, aw:2 * aw]) * conv).astype(BF16)
    tail = ubuf[ts + SUBLANES - 2:ts + SUBLANES, :]
    conv_ref[...] = tail
    ubuf[SUBLANES - 2:SUBLANES, :] = tail

    q0 = 3 * aw
    q = _dot(h, win_ref[:, q0:q0 + nq]) * (HEAD_DIM ** -0.5 * LOG2E)
    lane = lax.broadcasted_iota(jnp.int32, (1, nkv), 1)
    lo = lane < HEAD_DIM
    nblk = ts // WINDOW
    for n in range(nblk):
        for g in range(GROUP):
            qg = q[n * WINDOW:(n + 1) * WINDOW, g * nkv:(g + 1) * nkv]
            qbuf[n, g * WINDOW:(g + 1) * WINDOW, :] = jnp.where(lo, qg, 0.0).astype(BF16)
            qbuf[n, (GROUP + g) * WINDOW:(GROUP + g + 1) * WINDOW, :] = jnp.where(lo, 0.0, qg).astype(BF16)
    wkv = win_ref[:, q0 + nq:q0 + nq + 2 * nkv]
    kv_new = jnp.concatenate([_dot(h[0:ts // 2, :], wkv), _dot(h[ts // 2:, :], wkv)], axis=0)
    k = kv_new[:, 0:nkv]
    v = kv_new[:, nkv:2 * nkv]
    kbuf[WINDOW:WINDOW + ts, :] = k.astype(BF16)
    vbuf[WINDOW:WINDOW + ts, 0:nkv] = v.astype(BF16)
    kout_ref[...] = k[ts - WINDOW:, :]
    vout_ref[...] = v[ts - WINDOW:, :]

    qi = lax.broadcasted_iota(jnp.int32, (WINDOW, 2 * WINDOW), 0)
    kj = lax.broadcasted_iota(jnp.int32, (WINDOW, 2 * WINDOW), 1)
    dist = qi + WINDOW - kj
    valid = (dist >= 0) & (dist < WINDOW)
    negdist = jnp.where(valid, -dist.astype(F32), 0.0)
    maskadd = jnp.where(valid, 0.0, NEG)
    nheads = KV_HEADS * GROUP
    for hidx in range(nheads):
        bias_s[hidx] = (_head_slope(hidx) * LOG2E) * negdist + maskadd
    first = jnp.where(kj < WINDOW, jnp.where(s == 0, NEG, 0.0), 0.0)

    for n in range(nblk):
        r0 = n * WINDOW
        half = GROUP * WINDOW
        keys = kbuf[r0:r0 + 2 * WINDOW, :]
        sc_kv = [_dot_nt(qbuf[n, kv * half:(kv + 1) * half, :], keys) for kv in range(KV_HEADS)]
        ps, sink_terms = [], []
        for hidx in range(nheads):
            g = hidx % GROUP
            sc = sc_kv[hidx // GROUP][g * WINDOW:(g + 1) * WINDOW, :] + bias_s[hidx]
            if n == 0:
                sc = sc + first
            sink = sink_ref[hidx] * LOG2E
            m = jnp.maximum(jnp.max(sc, axis=-1, keepdims=True), sink)
            ps.append(jnp.exp2(sc - m).astype(BF16))
            sink_terms.append(jnp.exp2(sink - m))
        vals = vbuf[r0:r0 + 2 * WINDOW, :]
        pv_kv = [_dot(jnp.concatenate(ps[kv * GROUP:(kv + 1) * GROUP], axis=0), vals) for kv in range(KV_HEADS)]
        outs = []
        for hidx in range(nheads):
            g = hidx % GROUP
            pv_h = pv_kv[hidx // GROUP][g * WINDOW:(g + 1) * WINDOW, :]
            outs.append(pv_h[:, 0:nkv] / (pv_h[:, nkv:2 * nkv] + sink_terms[hidx]))
        for g in range(GROUP):
            og = jnp.where(lo, outs[g], outs[GROUP + g])
            mixbuf[r0:r0 + WINDOW, aw + g * nkv:aw + (g + 1) * nkv] = og.astype(BF16)

    kbuf[0:WINDOW, :] = kbuf[ts:ts + WINDOW, :]
    vbuf[0:WINDOW, :] = vbuf[ts:ts + WINDOW, :]

    out = _dot(mixbuf[...], wout_ref[...])
    y_ref[...] = x + _rms(out, gpost_ref[...])


def _mix_ab_prompt(x, gpre, gpost, win, cw, sinks, wout):
    b, seq, d = x.shape
    ts = MIX_TILE
    assert seq % ts == 0 and ts % WINDOW == 0, (seq, ts)
    aw = cw.shape[1]
    nq = GROUP * KV_HEADS * HEAD_DIM
    nkv = KV_HEADS * HEAD_DIM
    tile = pl.BlockSpec((None, ts, d), lambda i, j: (i, j, 0))
    return pl.pallas_call(
        _mix_ab_prompt_kernel,
        grid=(b, seq // ts),
        in_specs=[tile, _resident(gpre.shape), _resident(gpost.shape), _resident(win.shape),
                  _resident(cw.shape), pl.BlockSpec(memory_space=pltpu.SMEM), _resident(wout.shape)],
        out_specs=[tile,
                   pl.BlockSpec((None, 2, aw), lambda i, j: (i, 0, 0)),
                   pl.BlockSpec((None, WINDOW, nkv), lambda i, j: (i, 0, 0)),
                   pl.BlockSpec((None, WINDOW, nkv), lambda i, j: (i, 0, 0))],
        out_shape=[jax.ShapeDtypeStruct((b, seq, d), F32),
                   jax.ShapeDtypeStruct((b, 2, aw), F32),
                   jax.ShapeDtypeStruct((b, WINDOW, nkv), F32),
                   jax.ShapeDtypeStruct((b, WINDOW, nkv), F32)],
        scratch_shapes=[pltpu.VMEM((ts + SUBLANES, aw), F32),
                        pltpu.VMEM((ts // WINDOW, KV_HEADS * GROUP * WINDOW, nkv), BF16),
                        pltpu.VMEM((ts + WINDOW, nkv), BF16),
                        pltpu.VMEM((ts + WINDOW, 2 * nkv), BF16),
                        pltpu.VMEM((ts, d), BF16),
                        pltpu.VMEM((KV_HEADS * GROUP, WINDOW, 2 * WINDOW), F32)],
        compiler_params=_params(("arbitrary", "arbitrary")),
        name="mix_ab_prompt",
    )(x, gpre, gpost, win, cw, sinks.reshape(-1), wout)


def _ffn_prompt_kernel(x_ref, gpre_ref, gpost_ref, wup_ref, cw_ref, wdown_ref,
                       y_ref, st_ref, abuf, carry, gbuf):
    s = pl.program_id(1)
    rows = x_ref.shape[0]
    ff = cw_ref.shape[1]

    @pl.when(s == 0)
    def _():
        carry[...] = jnp.zeros(carry.shape, F32)

    x = x_ref[...]
    h = _rms(x, gpre_ref[...]).astype(BF16)
    for c in range(ff // FFN_CHUNK):
        c0 = c * FFN_CHUNK
        buf = abuf.at[c % 2]
        a = _dot(h, wup_ref[:, c0:c0 + FFN_CHUNK])
        b = _dot(h, wup_ref[:, ff + c0:ff + c0 + FFN_CHUNK])
        buf[0:SUBLANES, :] = carry[:, c0:c0 + FFN_CHUNK]
        buf[SUBLANES:SUBLANES + rows, :] = a
        cw = cw_ref[:, c0:c0 + FFN_CHUNK]
        conv = (cw[2:3, :] * a + cw[1:2, :] * buf[SUBLANES - 1:SUBLANES - 1 + rows, :]
                + cw[0:1, :] * buf[SUBLANES - 2:SUBLANES - 2 + rows, :])
        tail = a[rows - 2:rows, :]
        st_ref[:, c0:c0 + FFN_CHUNK] = tail
        carry[SUBLANES - 2:SUBLANES, c0:c0 + FFN_CHUNK] = tail
        gbuf[:, c0:c0 + FFN_CHUNK] = (_gelu_tanh(conv) * b).astype(BF16)
    out = _dot(gbuf[...], wdown_ref[...])
    y_ref[...] = x + _rms(out, gpost_ref[...])


def _ffn_prompt(layer, x, gpre, gpost, wup, cw, wdown):
    b, seq, d = x.shape
    ts = FFN_TILE
    ff = cw.shape[1]
    assert seq % ts == 0 and ff % FFN_CHUNK == 0, (seq, ts, ff)
    tile = pl.BlockSpec((None, ts, d), lambda i, j: (i, j, 0))
    layer_block = lambda w: pl.BlockSpec((None,) + w.shape[1:], lambda i, j: (layer, 0, 0),
                                         pipeline_mode=pl.Buffered(1))
    return pl.pallas_call(
        _ffn_prompt_kernel,
        grid=(b, seq // ts),
        in_specs=[tile, _resident(gpre.shape), _resident(gpost.shape), layer_block(wup),
                  _resident(cw.shape), layer_block(wdown)],
        out_specs=[tile, pl.BlockSpec((None, 2, ff), lambda i, j: (i, 0, 0))],
        out_shape=[jax.ShapeDtypeStruct((b, seq, d), F32), jax.ShapeDtypeStruct((b, 2, ff), F32)],
        scratch_shapes=[pltpu.VMEM((2, ts + SUBLANES, FFN_CHUNK), F32), pltpu.VMEM((SUBLANES, ff), F32),
                        pltpu.VMEM((ts, ff), BF16)],
        compiler_params=_params(("arbitrary", "arbitrary")),
        name="ffn_prompt",
    )(x, gpre, gpost, wup, cw, wdown)


def _segment_conv_taps(a, t, seg, prev0_ref, prev1_ref, abuf, fix1, fix2):
    rows, c = a.shape
    nseq = rows // seg
    am1, am2 = [], []
    for k in range(c // LANES):
        sl = slice(k * LANES, (k + 1) * LANES)
        abuf[k, SUBLANES:SUBLANES + rows, :] = a[:, sl]
        fix1[k, pl.ds(0, nseq, stride=seg), :] = prev1_ref[:, sl]
        fix2[k, pl.ds(0, nseq, stride=seg), :] = prev0_ref[:, sl]
        fix2[k, pl.ds(1, nseq, stride=seg), :] = prev1_ref[:, sl]
        am1.append(jnp.where(t >= 1, abuf[k, SUBLANES - 1:SUBLANES - 1 + rows, :], fix1[k]))
        am2.append(jnp.where(t >= 2, abuf[k, SUBLANES - 2:SUBLANES - 2 + rows, :], fix2[k]))
    return jnp.concatenate(am1, axis=1), jnp.concatenate(am2, axis=1)


def _segment_tails(abuf, nseq, seg):
    tail = lambda back: jnp.concatenate(
        [abuf[k, pl.ds(SUBLANES + seg - back, nseq, stride=seg), :] for k in range(abuf.shape[0])], axis=1)
    return tail(2), tail(1)


def _ffn_sample_kernel(seg, x_ref, prev0_ref, prev1_ref, gpre_ref, gpost_ref, wa_ref, wb_ref, cw_ref, wdown_ref,
                       y_ref, st0_ref, st1_ref, hbuf, abuf, fix1, fix2, acc):
    c = pl.program_id(0)
    rows = x_ref.shape[0]

    @pl.when(c == 0)
    def _():
        hbuf[...] = _rms(x_ref[...], gpre_ref[...]).astype(BF16)
        acc[...] = jnp.zeros(acc.shape, F32)
        abuf[:, 0:SUBLANES, :] = jnp.zeros((abuf.shape[0], SUBLANES, LANES), F32)
        fix1[...] = jnp.zeros(fix1.shape, F32)
        fix2[...] = jnp.zeros(fix2.shape, F32)

    h = hbuf[...]
    t = lax.broadcasted_iota(jnp.int32, (rows, 1), 0) % seg
    a = _dot(h, wa_ref[...])
    b = _dot(h, wb_ref[...])
    am1, am2 = _segment_conv_taps(a, t, seg, prev0_ref, prev1_ref, abuf, fix1, fix2)
    st0_ref[...], st1_ref[...] = _segment_tails(abuf, rows // seg, seg)
    cw = cw_ref[...]
    conv = cw[2:3, :] * a + cw[1:2, :] * am1 + cw[0:1, :] * am2
    acc[...] += _dot((_gelu_tanh(conv) * b).astype(BF16), wdown_ref[...])

    @pl.when(c == pl.num_programs(0) - 1)
    def _():
        y_ref[...] = x_ref[...] + _rms(acc[...], gpost_ref[...])


def _ffn_sample(layer, x, prev, seg, gpre, gpost, wup, cw, wdown):
    rows, d = x.shape
    nseq = rows // seg
    ff = cw.shape[1]
    cf = SAMPLE_FFN_CHUNK
    assert ff % cf == 0 and cf % LANES == 0, (ff, cf)
    nc = ff // cf
    cols = lambda r: pl.BlockSpec((r, cf), lambda c: (0, c))
    y, st0, st1 = pl.pallas_call(
        functools.partial(_ffn_sample_kernel, seg),
        grid=(nc,),
        in_specs=[_resident(x.shape), cols(nseq), cols(nseq), _resident(gpre.shape), _resident(gpost.shape),
                  pl.BlockSpec((None, d, cf), lambda c: (layer, 0, c)),
                  pl.BlockSpec((None, d, cf), lambda c: (layer, 0, c + nc)), cols(3),
                  pl.BlockSpec((None, cf, d), lambda c: (layer, c, 0))],
        out_specs=[pl.BlockSpec((rows, d), lambda c: (0, 0)), cols(nseq), cols(nseq)],
        out_shape=[jax.ShapeDtypeStruct((rows, d), F32), jax.ShapeDtypeStruct((nseq, ff), F32),
                   jax.ShapeDtypeStruct((nseq, ff), F32)],
        scratch_shapes=[pltpu.VMEM((rows, d), BF16), pltpu.VMEM((cf // LANES, rows + SUBLANES, LANES), F32),
                        pltpu.VMEM((cf // LANES, rows, LANES), F32), pltpu.VMEM((cf // LANES, rows, LANES), F32),
                        pltpu.VMEM((rows, d), F32)],
        compiler_params=_params(("arbitrary",)),
        name="ffn_sample",
    )(x, prev[:, 0], prev[:, 1], gpre, gpost, wup, wup, cw, wdown)
    return y, jnp.stack([st0, st1], axis=1)


def _lower_bound_terms(logits_ref, layer):
    lg = logits_ref[...]
    e = jnp.exp(lg - jnp.max(lg, axis=0, keepdims=True))
    pr = e / jnp.sum(e, axis=0, keepdims=True)
    cum = pr[0:1, :]
    for r in range(1, layer + 1):
        cum = cum + pr[r:r + 1, :]
    lb = jnp.clip(cum - pr[0:1, :], 0.0, 1.0)
    return jnp.log(lb) * LOG2E, jnp.log1p(-lb) * LOG2E, 1.0 - lb


def _forget_gate(f, lb_terms):
    log2_lb, log2_1m_lb, one_m_lb = lb_terms
    f2 = f * LOG2E
    e = jnp.exp2(jnp.minimum(f2, -f2))
    w = 1.0 + e
    b = log2_1m_lb + (jnp.minimum(f2, 0.0) - jnp.log2(w))
    d = log2_lb - b
    log2_ft = jnp.maximum(log2_lb, b) + jnp.log2(1.0 + jnp.exp2(jnp.minimum(d, -d)))
    return log2_ft, one_m_lb * (jnp.where(f2 >= 0.0, e, 1.0) / w)


def _split_bf16(x):
    hi = x.astype(BF16)
    return hi, (x - hi.astype(F32)).astype(BF16)


def _hgrn_gate_out(x, h, ob, mixbuf, win_ref, ng_ref, wout_ref, gpost_ref):
    dkt = C_HEADS * C_DK
    g = _dot(h, win_ref[:, 3 * dkt:4 * dkt])
    ng = ng_ref[...]
    for hh in range(C_HEADS):
        sl = slice(hh * C_DK, (hh + 1) * C_DK)
        mixbuf[:, sl] = (_rms(ob[:, sl], ng) * (g[:, sl] * _sigmoid(g[:, sl]))).astype(BF16)
    out = _dot(mixbuf[...], wout_ref[...])
    return x + _rms(out, gpost_ref[...])


def _hgrn_exact_tile(q, k_in, log2_ft, vb, st, ob, sq, sk, slg):
    ts = sq.shape[0]
    sq[...] = q
    sk[...] = k_in
    slg[...] = log2_ft
    ob[...] = jnp.zeros(ob.shape, F32)
    grp = 2 * SUBLANES
    row = lax.broadcasted_iota(jnp.int32, (grp, 1), 0)

    def body(t, carry):
        t0 = pl.multiple_of(lax.bitwise_and(t, -grp), grp)
        sel = row == lax.bitwise_and(t, grp - 1)
        for hh in range(C_HEADS):
            sl = slice(hh * C_DK, (hh + 1) * C_DK)
            rows_of = lambda ref: jnp.where(sel, ref[pl.ds(t0, grp), sl].astype(F32), 0.0)
            pick = lambda ref: rows_of(ref).astype(BF16)
            log2_f = jnp.sum(rows_of(slg), axis=0, keepdims=True)
            s_new = st[hh] * jnp.exp2(log2_f) + _dot_tn(pick(vb), pick(sk))
            st[hh] = s_new
            ob[pl.ds(t0, grp), sl] += _dot_nt(pick(sq), s_new.astype(BF16))
        return carry

    lax.fori_loop(0, ts, body, 0)


def _hgrn_prompt_kernel(layer, x_ref, gpre_ref, gpost_ref, win_ref, lbl_ref, ng_ref, wout_ref,
                        y_ref, sout_ref, st, st0, qtb, ktb, qib, klb, eglb, vb, ob, mixbuf, sq, sk, slg):
    s = pl.program_id(1)
    ts = x_ref.shape[0]
    dkt = C_HEADS * C_DK
    ck = HGRN_CHUNK

    @pl.when(s == 0)
    def _():
        st[...] = jnp.zeros(st.shape, F32)

    x = x_ref[...]
    h = _rms(x, gpre_ref[...]).astype(BF16)
    f = _dot(h, win_ref[:, dkt:2 * dkt])
    log2_ft, k_in = _forget_gate(f, _lower_bound_terms(lbl_ref, layer))
    q = _dot(h, win_ref[:, 0:dkt])
    vb[...] = _dot(h, win_ref[:, 2 * dkt:3 * dkt]).astype(BF16)

    ri = lax.broadcasted_iota(jnp.int32, (ck, ck), 0)
    ci = lax.broadcasted_iota(jnp.int32, (ck, ck), 1)
    causal = ri >= ci
    tri = jnp.where(causal, 1.0, 0.0).astype(BF16)
    mid = ck // 2 - 1
    nck = ts // ck

    spread = jnp.zeros((1, dkt), F32)
    for c in range(nck):
        rows = slice(c * ck, (c + 1) * ck)
        hi, lo = _split_bf16(log2_ft[rows, :])
        gcum = _dot(tri, hi) + _dot(tri, lo)
        gr = gcum[mid:mid + 1, :]
        gl = gcum[ck - 1:ck, :]
        spread = jnp.maximum(spread, jnp.maximum(gcum[0:1, :] - gr, gr - gl))
        qc = q[rows, :]
        kc = k_in[rows, :]
        qtb[rows, :] = (qc * jnp.exp2(gcum - gr)).astype(BF16)
        ktb[rows, :] = (kc * jnp.exp2(gr - gcum)).astype(BF16)
        qib[rows, :] = (qc * jnp.exp2(gcum)).astype(BF16)
        klb[rows, :] = (kc * jnp.exp2(gl - gcum)).astype(BF16)
        eglb[c] = jnp.exp2(gl)

    st0[...] = st[...]
    for c in range(nck):
        rows = slice(c * ck, (c + 1) * ck)
        for hh in range(C_HEADS):
            sl = slice(hh * C_DK, (hh + 1) * C_DK)
            vh = vb[rows, sl]
            att = jnp.where(causal, _dot_nt(qtb[rows, sl], ktb[rows, sl]), 0.0).astype(BF16)
            st_h = st[hh]
            ob[rows, sl] = _dot(jnp.concatenate([qib[rows, sl], att], axis=1),
                                jnp.concatenate([st_h.astype(BF16).T, vh], axis=0))
            st[hh] = eglb[c, :, sl] * st_h + _dot_tn(vh, klb[rows, sl])

    y_ref[...] = _hgrn_gate_out(x, h, ob, mixbuf, win_ref, ng_ref, wout_ref, gpost_ref)

    @pl.when(jnp.max(spread) > DECAY_LOG2_LIMIT)
    def _():
        st[...] = st0[...]
        _hgrn_exact_tile(q, k_in, log2_ft, vb, st, ob, sq, sk, slg)
        y_ref[...] = _hgrn_gate_out(x, h, ob, mixbuf, win_ref, ng_ref, wout_ref, gpost_ref)

    @pl.when(s == pl.num_programs(1) - 1)
    def _():
        for hh in range(C_HEADS):
            sout_ref[hh] = st[hh].T


def _hgrn_prompt(layer, x, gpre, gpost, win, lbl, ng, wout):
    b, seq, d = x.shape
    ts = HGRN_TILE
    assert seq % ts == 0 and ts % HGRN_CHUNK == 0, (seq, ts)
    dkt = C_HEADS * C_DK
    tile = pl.BlockSpec((None, ts, d), lambda i, j: (i, j, 0))
    return pl.pallas_call(
        functools.partial(_hgrn_prompt_kernel, layer),
        grid=(b, seq // ts),
        in_specs=[tile, _resident(gpre.shape), _resident(gpost.shape), _resident(win.shape),
                  _resident(lbl.shape), _resident(ng.shape), _resident(wout.shape)],
        out_specs=[tile, pl.BlockSpec((None, C_HEADS, C_DK, C_DK), lambda i, j: (i, 0, 0, 0))],
        out_shape=[jax.ShapeDtypeStruct((b, seq, d), F32),
                   jax.ShapeDtypeStruct((b, C_HEADS, C_DK, C_DK), F32)],
        scratch_shapes=[pltpu.VMEM((C_HEADS, C_DK, C_DK), F32), pltpu.VMEM((C_HEADS, C_DK, C_DK), F32),
                        pltpu.VMEM((ts, dkt), BF16), pltpu.VMEM((ts, dkt), BF16),
                        pltpu.VMEM((ts, dkt), BF16), pltpu.VMEM((ts, dkt), BF16),
                        pltpu.VMEM((ts // HGRN_CHUNK, 1, dkt), F32),
                        pltpu.VMEM((ts, dkt), BF16), pltpu.VMEM((ts, dkt), F32),
                        pltpu.VMEM((ts, dkt), BF16),
                        pltpu.VMEM((ts, dkt), F32), pltpu.VMEM((ts, dkt), F32), pltpu.VMEM((ts, dkt), F32)],
        compiler_params=_params(("arbitrary", "arbitrary")),
        name="hgrn_prompt",
    )(x, gpre, gpost, win, lbl, ng, wout)


def _mix_ab_sample_kernel(seg, x_ref, prev0_ref, prev1_ref, kc_ref, vc_ref, gpre_ref, gpost_ref, win_ref,
                          cw_ref, sink_ref, wout_ref,
                          y_ref, conv0_ref, conv1_ref, kout_ref, vout_ref,
                          ubuf, fix1, fix2, qbuf, knew_s, vnew_s, yabuf, ybbuf, kall, vall, bias_s, sink_s):
    i = pl.program_id(0)
    rows = x_ref.shape[0]
    aw = cw_ref.shape[1]
    nq = GROUP * KV_HEADS * HEAD_DIM
    nkv = KV_HEADS * HEAD_DIM
    pair = 2 * seg
    qrows = KV_HEADS * GROUP * pair
    tk = WINDOW + pair

    @pl.when(i == 0)
    def _():
        x = x_ref[...]
        h = _rms(x, gpre_ref[...]).astype(BF16)
        t = lax.broadcasted_iota(jnp.int32, (rows, 1), 0) % seg
        u = _dot(h, win_ref[:, 2 * aw:3 * aw]) * _dot(h, win_ref[:, 0:aw])
        ubuf[:, 0:SUBLANES, :] = jnp.zeros((ubuf.shape[0], SUBLANES, LANES), F32)
        fix1[...] = jnp.zeros(fix1.shape, F32)
        fix2[...] = jnp.zeros(fix2.shape, F32)
        um1, um2 = _segment_conv_taps(u, t, seg, prev0_ref, prev1_ref, ubuf, fix1, fix2)
        conv0_ref[...], conv1_ref[...] = _segment_tails(ubuf, rows // seg, seg)
        cw = cw_ref[...]
        conv = cw[2:3, :] * u + cw[1:2, :] * um1 + cw[0:1, :] * um2
        yabuf[...] = (_dot(h, win_ref[:, aw:2 * aw]) * conv).astype(BF16)
        q0 = 3 * aw
        qbuf[...] = _dot(h, win_ref[:, q0:q0 + nq]) * (HEAD_DIM ** -0.5)
        kv_new = _dot(h, win_ref[:, q0 + nq:q0 + nq + 2 * nkv])
        knew_s[...] = kv_new[:, 0:nkv]
        vnew_s[...] = kv_new[:, nkv:2 * nkv]

        r = lax.broadcasted_iota(jnp.int32, (qrows, tk), 0)
        j = lax.broadcasted_iota(jnp.int32, (qrows, tk), 1)
        dist = r % seg + WINDOW - j
        valid = (dist >= 0) & (dist < WINDOW) & (j < WINDOW + seg)
        rc = lax.broadcasted_iota(jnp.int32, (qrows, 1), 0)
        head = (rc // (GROUP * pair)) * GROUP + (rc // pair) % GROUP
        slope = jnp.zeros((qrows, 1), F32)
        sink = jnp.zeros((qrows, 1), F32)
        for hidx in range(KV_HEADS * GROUP):
            slope = jnp.where(head == hidx, _head_slope(hidx), slope)
            sink = jnp.where(head == hidx, sink_ref[:, hidx:hidx + 1], sink)
        bias_s[...] = jnp.where(valid, -slope * dist.astype(F32), NEG)
        sink_s[...] = sink

    nblk = kc_ref.shape[0]
    lane = lax.broadcasted_iota(jnp.int32, (1, nkv), 1)
    lo = lane < HEAD_DIM
    rq = lax.broadcasted_iota(jnp.int32, (qrows, 1), 0)
    first_seq = (rq // seg) % 2 == 0

    def body(p, carry):
        r0 = pl.multiple_of((i * nblk + 2 * p) * seg, pair)
        q8 = qbuf[pl.ds(r0, pair), :]
        qg = jnp.concatenate([q8[:, g * nkv:(g + 1) * nkv] for g in range(GROUP)], axis=0)
        qs = jnp.concatenate([jnp.where(lo, qg, 0.0), jnp.where(lo, 0.0, qg)], axis=0).astype(BF16)
        knew = knew_s[pl.ds(r0, pair), :]
        vnew = vnew_s[pl.ds(r0, pair), :]
        sc, ov = [], []
        for a in range(2):
            own_first = (lambda z: z) if a == 0 else (lambda z: jnp.concatenate([z[seg:], z[:seg]], axis=0))
            kall[a, 0:WINDOW, :] = kc_ref[2 * p + a]
            kall[a, WINDOW:tk, :] = own_first(knew)
            vall[a, 0:WINDOW, :] = vc_ref[2 * p + a]
            vall[a, WINDOW:tk, :] = own_first(vnew)
            kout_ref[2 * p + a] = kall[a, seg:seg + WINDOW, :]
            vout_ref[2 * p + a] = vall[a, seg:seg + WINDOW, :]
            sc.append(_dot_nt(qs, kall[a].astype(BF16)))
        s2 = jnp.where(first_seq, sc[0], sc[1]) + bias_s[...]
        sink = sink_s[...]
        m = jnp.maximum(jnp.max(s2, axis=-1, keepdims=True), sink)
        pr = jnp.exp(s2 - m)
        den = jnp.sum(pr, axis=-1, keepdims=True) + jnp.exp(sink - m)
        prb = pr.astype(BF16)
        for a in range(2):
            ov.append(_dot(prb, vall[a].astype(BF16)))
        o = jnp.where(first_seq, ov[0], ov[1]) / den
        og = jnp.where(lo, o[0:GROUP * pair, :], o[GROUP * pair:, :])
        for g in range(GROUP):
            ybbuf[pl.ds(r0, pair), g * nkv:(g + 1) * nkv] = og[g * pair:(g + 1) * pair, :]
        return carry

    lax.fori_loop(0, nblk // 2, body, 0, unroll=True)

    @pl.when(i == pl.num_programs(0) - 1)
    def _():
        out = _dot(yabuf[...], wout_ref[0:aw, :]) + _dot(ybbuf[...].astype(BF16), wout_ref[aw:, :])
        y_ref[...] = x_ref[...] + _rms(out, gpost_ref[...])


def _mix_ab_sample(x, conv_prev, kcache, vcache, seg, gpre, gpost, win, cw, sinks, wout):
    rows, d = x.shape
    nseq = rows // seg
    aw = cw.shape[1]
    nq = GROUP * KV_HEADS * HEAD_DIM
    nkv = KV_HEADS * HEAD_DIM
    sb = SAMPLE_SEQ_BLOCK
    pair = 2 * seg
    qrows = KV_HEADS * GROUP * pair
    tk = WINDOW + pair
    full = lambda shape: pl.BlockSpec(shape, lambda i: (0,) * len(shape))
    cache = pl.BlockSpec((sb, WINDOW, nkv), lambda i: (i, 0, 0))
    prev0, prev1 = conv_prev[:, 0], conv_prev[:, 1]
    y, conv0, conv1, k_new, v_new = pl.pallas_call(
        functools.partial(_mix_ab_sample_kernel, seg),
        grid=(nseq // sb,),
        in_specs=[_resident(x.shape), _resident(prev0.shape), _resident(prev1.shape), cache, cache,
                  _resident(gpre.shape), _resident(gpost.shape), _resident(win.shape), _resident(cw.shape),
                  _resident(sinks.shape), _resident(wout.shape)],
        out_specs=[full((rows, d)), full((nseq, aw)), full((nseq, aw)), cache, cache],
        out_shape=[jax.ShapeDtypeStruct((rows, d), F32), jax.ShapeDtypeStruct((nseq, aw), F32),
                   jax.ShapeDtypeStruct((nseq, aw), F32), jax.ShapeDtypeStruct(kcache.shape, F32),
                   jax.ShapeDtypeStruct(vcache.shape, F32)],
        scratch_shapes=[pltpu.VMEM((aw // LANES, rows + SUBLANES, LANES), F32),
                        pltpu.VMEM((aw // LANES, rows, LANES), F32),
                        pltpu.VMEM((aw // LANES, rows, LANES), F32),
                        pltpu.VMEM((rows, nq), F32),
                        pltpu.VMEM((rows, nkv), F32),
                        pltpu.VMEM((rows, nkv), F32),
                        pltpu.VMEM((rows, aw), BF16),
                        pltpu.VMEM((rows, nq), F32),
                        pltpu.VMEM((2, tk, nkv), F32),
                        pltpu.VMEM((2, tk, nkv), F32),
                        pltpu.VMEM((qrows, tk), F32),
                        pltpu.VMEM((qrows, 1), F32)],
        compiler_params=_params(("arbitrary",)),
        name="mix_ab_sample",
    )(x, prev0, prev1, kcache, vcache, gpre, gpost, win, cw, sinks, wout)
    return y, jnp.stack([conv0, conv1], axis=1), k_new, v_new


def _hgrn_sample_kernel(layer, seg, x_ref, s0_ref, gpre_ref, gpost_ref, win_ref, lbl_ref, ng_ref, wout_ref,
                        y_ref, s1_ref, qb, kb, vb, gb, ob, mixbuf):
    i = pl.program_id(0)
    rows = x_ref.shape[0]
    dkt = C_HEADS * C_DK
    pair = 2 * seg

    @pl.when(i == 0)
    def _():
        x = x_ref[...]
        h = _rms(x, gpre_ref[...]).astype(BF16)
        f = _dot(h, win_ref[:, dkt:2 * dkt])
        log2_ft, kb[...] = _forget_gate(f, _lower_bound_terms(lbl_ref, layer))
        hi, lo = _split_bf16(log2_ft)
        ri = lax.broadcasted_iota(jnp.int32, (rows, rows), 0)
        ci = lax.broadcasted_iota(jnp.int32, (rows, rows), 1)
        tri = jnp.where((ri // seg == ci // seg) & (ri >= ci), 1.0, 0.0).astype(BF16)
        gb[...] = _dot(tri, hi) + _dot(tri, lo)
        qb[...] = _dot(h, win_ref[:, 0:dkt])
        vb[...] = _dot(h, win_ref[:, 2 * dkt:3 * dkt])

    nblk = s0_ref.shape[0]
    rr = lax.broadcasted_iota(jnp.int32, (pair, 1), 0)
    first_seq = rr < seg
    second_seq = rr >= seg
    ri = lax.broadcasted_iota(jnp.int32, (pair, pair), 0)
    ci = lax.broadcasted_iota(jnp.int32, (pair, pair), 1)
    key_s = [(ri // seg == ci // seg) & (ci % seg == sk) & (ri % seg >= sk) for sk in range(seg)]

    def body(p, carry):
        r0 = pl.multiple_of((i * nblk + 2 * p) * seg, pair)
        for hh in range(C_HEADS):
            sl = slice(hh * C_DK, (hh + 1) * C_DK)
            gh = gb[pl.ds(r0, pair), sl]
            g_row = lambda t: jnp.where(first_seq, gh[t:t + 1, :], gh[seg + t:seg + t + 1, :])
            gl = g_row(seg - 1)
            qh = qb[pl.ds(r0, pair), sl]
            kh = kb[pl.ds(r0, pair), sl]
            vh = vb[pl.ds(r0, pair), sl].astype(BF16)
            q_rel = [(qh * jnp.exp2(gh - g_row(sk))).astype(BF16) for sk in range(seg - 1)] + [qh.astype(BF16)]
            rel = _dot_nt(jnp.concatenate(q_rel, axis=0), kh.astype(BF16))
            att = jnp.where(key_s[0], rel[0:pair, :], 0.0)
            for sk in range(1, seg):
                att = att + jnp.where(key_s[sk], rel[sk * pair:(sk + 1) * pair, :], 0.0)
            att = att.astype(BF16)
            qi = (qh * jnp.exp2(gh)).astype(BF16)
            kl = kh * jnp.exp2(gl - gh)
            o_prev = []
            for a in range(2):
                s_a = s0_ref[2 * p + a, hh]
                o_prev.append(_dot(qi, s_a.astype(BF16)))
                kl_a = jnp.where(first_seq if a == 0 else second_seq, kl, 0.0).astype(BF16)
                e = jnp.exp2(gh[(a + 1) * seg - 1:(a + 1) * seg, :])
                e1 = e.astype(BF16).astype(F32)
                e2 = (e - e1).astype(BF16).astype(F32)
                e3 = (e - e1 - e2).astype(BF16).astype(F32)
                dec_l = jnp.where(rr == 0, e1, jnp.where(rr == 1, e2, jnp.where(rr == 2, e3, 0.0)))
                dec = _dot_tn(dec_l.astype(BF16), jnp.ones((pair, C_DK), BF16))
                s1_ref[2 * p + a, hh] = dec * s_a + _dot_tn(kl_a, vh)
            ob[pl.ds(r0, pair), sl] = _dot(att, vh) + jnp.where(first_seq, o_prev[0], o_prev[1])
        return carry

    lax.fori_loop(0, nblk // 2, body, 0, unroll=True)

    @pl.when(i == pl.num_programs(0) - 1)
    def _():
        x = x_ref[...]
        h = _rms(x, gpre_ref[...]).astype(BF16)
        y_ref[...] = _hgrn_gate_out(x, h, ob, mixbuf, win_ref, ng_ref, wout_ref, gpost_ref)


def _hgrn_sample(layer, x, s0, seg, gpre, gpost, win, lbl, ng, wout):
    rows, d = x.shape
    assert 2 * seg == SUBLANES, "two sequences per iteration must fill one f32 vreg of sublanes"
    nseq = rows // seg
    dkt = C_HEADS * C_DK
    sb = SAMPLE_SEQ_BLOCK
    full = lambda shape: pl.BlockSpec(shape, lambda i: (0,) * len(shape))
    state = pl.BlockSpec((sb, C_HEADS, C_DK, C_DK), lambda i: (i, 0, 0, 0))
    return pl.pallas_call(
        functools.partial(_hgrn_sample_kernel, layer, seg),
        grid=(nseq // sb,),
        in_specs=[_resident(x.shape), state, _resident(gpre.shape), _resident(gpost.shape), _resident(win.shape),
                  _resident(lbl.shape), _resident(ng.shape), _resident(wout.shape)],
        out_specs=[full((rows, d)), state],
        out_shape=[jax.ShapeDtypeStruct((rows, d), F32), jax.ShapeDtypeStruct(s0.shape, F32)],
        scratch_shapes=[pltpu.VMEM((rows, dkt), F32), pltpu.VMEM((rows, dkt), F32),
                        pltpu.VMEM((rows, dkt), F32), pltpu.VMEM((rows, dkt), F32),
                        pltpu.VMEM((rows, dkt), F32), pltpu.VMEM((rows, dkt), BF16)],
        compiler_params=_params(("arbitrary",)),
        name="hgrn_sample",
    )(x, s0, gpre, gpost, win, lbl, ng, wout)


def kernel(x_prompt, x_sample, state_conv_a, cache_swa_k, cache_swa_v, state_hgrn, state_ffn_conv, norm_mix_pre, norm_mix_post, norm_ffn_pre, norm_ffn_post, w_in_ab, conv_a_w, attn_sinks, w_out_ab, w_in_c, hgrn_lb_logits, hgrn_norm_g, w_out_c, w_ffn_up, ffn_conv_w, w_ffn_down):
    bp, seq, d = x_prompt.shape
    ns, seg, _ = x_sample.shape
    aw = conv_a_w.shape[-1]
    nq = GROUP * KV_HEADS * HEAD_DIM
    nkv = KV_HEADS * HEAD_DIM

    q0 = 3 * aw
    win_ab = w_in_ab[0].astype(BF16)
    wq = win_ab[:, q0:q0 + nq].reshape(d, KV_HEADS, GROUP, HEAD_DIM).swapaxes(1, 2).reshape(d, nq)
    win_ab = jnp.concatenate([win_ab[:, :q0], wq, win_ab[:, q0 + nq:]], axis=1)
    wout_ab = w_out_ab[0].astype(BF16)
    wo = wout_ab[aw:].reshape(KV_HEADS, GROUP, HEAD_DIM, d).swapaxes(0, 1).reshape(nq, d)
    wout_ab = jnp.concatenate([wout_ab[:aw], wo], axis=0)
    win_c = w_in_c[0].astype(BF16)
    wout_c = w_out_c[0].astype(BF16)
    wup = w_ffn_up.astype(BF16)
    wdown = w_ffn_down.astype(BF16)
    row = lambda a: a.reshape(1, -1)
    sinks = row(attn_sinks[0])
    ng = row(hgrn_norm_g[0])

    xs = x_sample.reshape(ns * seg, d)
    kc = cache_swa_k[0].reshape(ns, WINDOW, nkv)
    vc = cache_swa_v[0].reshape(ns, WINDOW, nkv)
    s1, conv_s, k_s, v_s = _mix_ab_sample(xs, state_conv_a[0], kc, vc, seg, row(norm_mix_pre[0]),
                                          row(norm_mix_post[0]), win_ab, conv_a_w[0], sinks, wout_ab)
    s2, ffn_s0 = _ffn_sample(0, s1, state_ffn_conv[0], seg, row(norm_ffn_pre[0]), row(norm_ffn_post[0]),
                             wup, ffn_conv_w[0], wdown)
    s3, hgrn_s = _hgrn_sample(1, s2, state_hgrn[0], seg, row(norm_mix_pre[1]), row(norm_mix_post[1]),
                              win_c, hgrn_lb_logits, ng, wout_c)
    y_sample, ffn_s1 = _ffn_sample(1, s3, state_ffn_conv[1], seg, row(norm_ffn_pre[1]), row(norm_ffn_post[1]),
                                   wup, ffn_conv_w[1], wdown)

    x1, conv_p, k_p, v_p = _mix_ab_prompt(x_prompt, row(norm_mix_pre[0]), row(norm_mix_post[0]), win_ab,
                                          conv_a_w[0], sinks, wout_ab)
    x2, ffn_p0 = _ffn_prompt(0, x1, row(norm_ffn_pre[0]), row(norm_ffn_post[0]), wup, ffn_conv_w[0], wdown)
    x3, hgrn_p = _hgrn_prompt(1, x2, row(norm_mix_pre[1]), row(norm_mix_post[1]), win_c, hgrn_lb_logits,
                              ng, wout_c)
    y_prompt, ffn_p1 = _ffn_prompt(1, x3, row(norm_ffn_pre[1]), row(norm_ffn_post[1]), wup, ffn_conv_w[1], wdown)

    kvshape = (KV_HEADS, HEAD_DIM)
    return (y_prompt, y_sample.reshape(ns, seg, d),
            conv_p[None], conv_s[None],
            k_p.reshape(1, bp, WINDOW, *kvshape), k_s.reshape(1, ns, WINDOW, *kvshape),
            v_p.reshape(1, bp, WINDOW, *kvshape), v_s.reshape(1, ns, WINDOW, *kvshape),
            hgrn_p[None], hgrn_s[None],
            jnp.stack([ffn_p0, ffn_p1]), jnp.stack([ffn_s0, ffn_s1]))
```

```python
import functools
import math

import jax
import jax.numpy as jnp
from jax import lax
from jax.experimental import pallas as pl
from jax.experimental.pallas import tpu as pltpu

F32 = jnp.float32
BF16 = jnp.bfloat16

EPS = 1e-6
LOG2E = 1.4426950408889634
WINDOW = 128
KV_HEADS = 2
GROUP = 4
HEAD_DIM = 64
C_HEADS = 8
C_DK = 128
NEG = -1e30
SUBLANES = 8
LANES = 128
MIX_TILE = 1024
HGRN_TILE = 512
FFN_TILE = 1024
HGRN_CHUNK = 64
DECAY_LOG2_LIMIT = 100.0
FFN_CHUNK = 256
SAMPLE_SEQ_BLOCK = 8
VMEM_LIMIT_V7X = 56 * 2**20


def _dot(a, b):
    return jnp.dot(a, b, preferred_element_type=F32)


def _dot_nt(a, b):
    return lax.dot_general(a, b, (((1,), (1,)), ((), ())), preferred_element_type=F32)


def _dot_tn(a, b):
    return lax.dot_general(a, b, (((0,), (0,)), ((), ())), preferred_element_type=F32)


def _rms(x, g):
    return x * lax.rsqrt(jnp.mean(x * x, axis=-1, keepdims=True) + EPS) * g


def _gelu_tanh(x):
    c = math.sqrt(2.0 / math.pi)
    return x * (0.5 * (1.0 + jnp.tanh(c * (x + 0.044715 * (x * x * x)))))


def _sigmoid(x):
    return 1.0 / (1.0 + jnp.exp(-x))


def _resident(shape):
    nd = len(shape)
    return pl.BlockSpec(shape, lambda *_: (0,) * nd, pipeline_mode=pl.Buffered(1))


def _params(sem):
    return pltpu.CompilerParams(dimension_semantics=sem, vmem_limit_bytes=VMEM_LIMIT_V7X)


def _head_slope(h):
    return 2.0 ** (-8.0 * (h + 1) / (KV_HEADS * GROUP))


def _mix_ab_prompt_kernel(x_ref, gpre_ref, gpost_ref, win_ref, cw_ref, sink_ref, wout_ref,
                          y_ref, conv_ref, kout_ref, vout_ref,
                          ubuf, qbuf, kbuf, vbuf, mixbuf, bias_s):
    s = pl.program_id(1)
    ts = x_ref.shape[0]
    aw = cw_ref.shape[1]
    nq = GROUP * KV_HEADS * HEAD_DIM
    nkv = KV_HEADS * HEAD_DIM

    @pl.when(s == 0)
    def _():
        ubuf[0:SUBLANES, :] = jnp.zeros((SUBLANES, aw), F32)
        kbuf[0:WINDOW, :] = jnp.zeros((WINDOW, nkv), BF16)
        vbuf[0:WINDOW, 0:nkv] = jnp.zeros((WINDOW, nkv), BF16)
        vbuf[:, nkv:2 * nkv] = jnp.ones((ts + WINDOW, nkv), BF16)

    x = x_ref[...]
    h = _rms(x, gpre_ref[...]).astype(BF16)

    u = _dot(h, win_ref[:, 2 * aw:3 * aw]) * _dot(h, win_ref[:, 0:aw])
    ubuf[SUBLANES:SUBLANES + ts, :] = u
    cw = cw_ref[...]
    conv = (cw[2:3, :] * u + cw[1:2, :] * ubuf[SUBLANES - 1:SUBLANES - 1 + ts, :]
            + cw[0:1, :] * ubuf[SUBLANES - 2:SUBLANES - 2 + ts, :])
    mixbuf[:, 0:aw] = (_dot(h, win_ref[:, aw:2 * aw]) * conv).astype(BF16)
    tail = ubuf[ts + SUBLANES - 2:ts + SUBLANES, :]
    conv_ref[...] = tail
    ubuf[SUBLANES - 2:SUBLANES, :] = tail

    q0 = 3 * aw
    q = _dot(h, win_ref[:, q0:q0 + nq]) * (HEAD_DIM ** -0.5 * LOG2E)
    lane = lax.broadcasted_iota(jnp.int32, (1, nkv), 1)
    lo = lane < HEAD_DIM
    nblk = ts // WINDOW
    for n in range(nblk):
        for g in range(GROUP):
            qg = q[n * WINDOW:(n + 1) * WINDOW, g * nkv:(g + 1) * nkv]
            qbuf[n, g * WINDOW:(g + 1) * WINDOW, :] = jnp.where(lo, qg, 0.0).astype(BF16)
            qbuf[n, (GROUP + g) * WINDOW:(GROUP + g + 1) * WINDOW, :] = jnp.where(lo, 0.0, qg).astype(BF16)
    wkv = win_ref[:, q0 + nq:q0 + nq + 2 * nkv]
    kv_new = jnp.concatenate([_dot(h[0:ts // 2, :], wkv), _dot(h[ts // 2:, :], wkv)], axis=0)
    k = kv_new[:, 0:nkv]
    v = kv_new[:, nkv:2 * nkv]
    kbuf[WINDOW:WINDOW + ts, :] = k.astype(BF16)
    vbuf[WINDOW:WINDOW + ts, 0:nkv] = v.astype(BF16)
    kout_ref[...] = k[ts - WINDOW:, :]
    vout_ref[...] = v[ts - WINDOW:, :]

    qi = lax.broadcasted_iota(jnp.int32, (WINDOW, 2 * WINDOW), 0)
    kj = lax.broadcasted_iota(jnp.int32, (WINDOW, 2 * WINDOW), 1)
    dist = qi + WINDOW - kj
    valid = (dist >= 0) & (dist < WINDOW)
    negdist = jnp.where(valid, -dist.astype(F32), 0.0)
    maskadd = jnp.where(valid, 0.0, NEG)
    nheads = KV_HEADS * GROUP
    for hidx in range(nheads):
        bias_s[hidx] = (_head_slope(hidx) * LOG2E) * negdist + maskadd
    first = jnp.where(kj < WINDOW, jnp.where(s == 0, NEG, 0.0), 0.0)

    for n in range(nblk):
        r0 = n * WINDOW
        half = GROUP * WINDOW
        keys = kbuf[r0:r0 + 2 * WINDOW, :]
        sc_kv = [_dot_nt(qbuf[n, kv * half:(kv + 1) * half, :], keys) for kv in range(KV_HEADS)]
        ps, sink_terms = [], []
        for hidx in range(nheads):
            g = hidx % GROUP
            sc = sc_kv[hidx // GROUP][g * WINDOW:(g + 1) * WINDOW, :] + bias_s[hidx]
            if n == 0:
                sc = sc + first
            sink = sink_ref[hidx] * LOG2E
            m = jnp.maximum(jnp.max(sc, axis=-1, keepdims=True), sink)
            ps.append(jnp.exp2(sc - m).astype(BF16))
            sink_terms.append(jnp.exp2(sink - m))
        vals = vbuf[r0:r0 + 2 * WINDOW, :]
        pv_kv = [_dot(jnp.concatenate(ps[kv * GROUP:(kv + 1) * GROUP], axis=0), vals) for kv in range(KV_HEADS)]
        outs = []
        for hidx in range(nheads):
            g = hidx % GROUP
            pv_h = pv_kv[hidx // GROUP][g * WINDOW:(g + 1) * WINDOW, :]
            outs.append(pv_h[:, 0:nkv] / (pv_h[:, nkv:2 * nkv] + sink_terms[hidx]))
        for g in range(GROUP):
            og = jnp.where(lo, outs[g], outs[GROUP + g])
            mixbuf[r0:r0 + WINDOW, aw + g * nkv:aw + (g + 1) * nkv] = og.astype(BF16)

    kbuf[0:WINDOW, :] = kbuf[ts:ts + WINDOW, :]
    vbuf[0:WINDOW, :] = vbuf[ts:ts + WINDOW, :]

    out = _dot(mixbuf[...], wout_ref[...])
    y_ref[...] = x + _rms(out, gpost_ref[...])


def _mix_ab_prompt(x, gpre, gpost, win, cw, sinks, wout):
    b, seq, d = x.shape
    ts = MIX_TILE
    assert seq % ts == 0 and ts % WINDOW == 0, (seq, ts)
    aw = cw.shape[1]
    nkv = KV_HEADS * HEAD_DIM
    tile = pl.BlockSpec((None, ts, d), lambda i, j: (i, j, 0))
    return pl.pallas_call(
        _mix_ab_prompt_kernel,
        grid=(b, seq // ts),
        in_specs=[tile, _resident(gpre.shape), _resident(gpost.shape), _resident(win.shape),
                  _resident(cw.shape), pl.BlockSpec(memory_space=pltpu.SMEM), _resident(wout.shape)],
        out_specs=[tile,
                   pl.BlockSpec((None, 2, aw), lambda i, j: (i, 0, 0)),
                   pl.BlockSpec((None, WINDOW, nkv), lambda i, j: (i, 0, 0)),
                   pl.BlockSpec((None, WINDOW, nkv), lambda i, j: (i, 0, 0))],
        out_shape=[jax.ShapeDtypeStruct((b, seq, d), F32),
                   jax.ShapeDtypeStruct((b, 2, aw), F32),
                   jax.ShapeDtypeStruct((b, WINDOW, nkv), F32),
                   jax.ShapeDtypeStruct((b, WINDOW, nkv), F32)],
        scratch_shapes=[pltpu.VMEM((ts + SUBLANES, aw), F32),
                        pltpu.VMEM((ts // WINDOW, KV_HEADS * GROUP * WINDOW, nkv), BF16),
                        pltpu.VMEM((ts + WINDOW, nkv), BF16),
                        pltpu.VMEM((ts + WINDOW, 2 * nkv), BF16),
                        pltpu.VMEM((ts, d), BF16),
                        pltpu.VMEM((KV_HEADS * GROUP, WINDOW, 2 * WINDOW), F32)],
        compiler_params=_params(("arbitrary", "arbitrary")),
        name="mix_ab_prompt",
    )(x, gpre, gpost, win, cw, sinks.reshape(-1), wout)


def _ffn_prompt_kernel(x_ref, gpre_ref, gpost_ref, wup_ref, cw_ref, wdown_ref,
                       y_ref, st_ref, abuf, carry, gbuf):
    s = pl.program_id(1)
    rows = x_ref.shape[0]
    ff = cw_ref.shape[1]

    @pl.when(s == 0)
    def _():
        carry[...] = jnp.zeros(carry.shape, F32)

    x = x_ref[...]
    h = _rms(x, gpre_ref[...]).astype(BF16)
    for c in range(ff // FFN_CHUNK):
        c0 = c * FFN_CHUNK
        buf = abuf.at[c % 2]
        a = _dot(h, wup_ref[:, c0:c0 + FFN_CHUNK])
        b = _dot(h, wup_ref[:, ff + c0:ff + c0 + FFN_CHUNK])
        buf[0:SUBLANES, :] = carry[:, c0:c0 + FFN_CHUNK]
        buf[SUBLANES:SUBLANES + rows, :] = a
        cw = cw_ref[:, c0:c0 + FFN_CHUNK]
        conv = (cw[2:3, :] * a + cw[1:2, :] * buf[SUBLANES - 1:SUBLANES - 1 + rows, :]
                + cw[0:1, :] * buf[SUBLANES - 2:SUBLANES - 2 + rows, :])
        tail = a[rows - 2:rows, :]
        st_ref[:, c0:c0 + FFN_CHUNK] = tail
        carry[SUBLANES - 2:SUBLANES, c0:c0 + FFN_CHUNK] = tail
        gbuf[:, c0:c0 + FFN_CHUNK] = (_gelu_tanh(conv) * b).astype(BF16)
    out = _dot(gbuf[...], wdown_ref[...])
    y_ref[...] = x + _rms(out, gpost_ref[...])


def _ffn_prompt(layer, x, gpre, gpost, wup, cw, wdown):
    b, seq, d = x.shape
    ts = FFN_TILE
    ff = cw.shape[1]
    assert seq % ts == 0 and ff % FFN_CHUNK == 0, (seq, ts, ff)
    tile = pl.BlockSpec((None, ts, d), lambda i, j: (i, j, 0))
    layer_block = lambda w: pl.BlockSpec((None,) + w.shape[1:], lambda i, j: (layer, 0, 0),
                                         pipeline_mode=pl.Buffered(1))
    return pl.pallas_call(
        _ffn_prompt_kernel,
        grid=(b, seq // ts),
        in_specs=[tile, _resident(gpre.shape), _resident(gpost.shape), layer_block(wup),
                  _resident(cw.shape), layer_block(wdown)],
        out_specs=[tile, pl.BlockSpec((None, 2, ff), lambda i, j: (i, 0, 0))],
        out_shape=[jax.ShapeDtypeStruct((b, seq, d), F32), jax.ShapeDtypeStruct((b, 2, ff), F32)],
        scratch_shapes=[pltpu.VMEM((2, ts + SUBLANES, FFN_CHUNK), F32), pltpu.VMEM((SUBLANES, ff), F32),
                        pltpu.VMEM((ts, ff), BF16)],
        compiler_params=_params(("arbitrary", "arbitrary")),
        name="ffn_prompt",
    )(x, gpre, gpost, wup, cw, wdown)


def _segment_conv_taps(a, t, seg, prev0_ref, prev1_ref, abuf, fix1, fix2):
    rows, c = a.shape
    nseq = rows // seg
    am1, am2 = [], []
    for k in range(c // LANES):
        sl = slice(k * LANES, (k + 1) * LANES)
        abuf[k, SUBLANES:SUBLANES + rows, :] = a[:, sl]
        fix1[k, pl.ds(0, nseq, stride=seg), :] = prev1_ref[:, sl]
        fix2[k, pl.ds(0, nseq, stride=seg), :] = prev0_ref[:, sl]
        fix2[k, pl.ds(1, nseq, stride=seg), :] = prev1_ref[:, sl]
        am1.append(jnp.where(t >= 1, abuf[k, SUBLANES - 1:SUBLANES - 1 + rows, :], fix1[k]))
        am2.append(jnp.where(t >= 2, abuf[k, SUBLANES - 2:SUBLANES - 2 + rows, :], fix2[k]))
    return jnp.concatenate(am1, axis=1), jnp.concatenate(am2, axis=1)


def _segment_tails(abuf, nseq, seg):
    tail = lambda back: jnp.concatenate(
        [abuf[k, pl.ds(SUBLANES + seg - back, nseq, stride=seg), :] for k in range(abuf.shape[0])], axis=1)
    return tail(2), tail(1)


def _ffn_sample_kernel(seg, x_ref, prev0_ref, prev1_ref, gpre_ref, gpost_ref, wup_ref, cw_ref, wdown_ref,
                       y_ref, st0_ref, st1_ref, abuf, fix1, fix2, gbuf):
    rows = x_ref.shape[0]
    ff = cw_ref.shape[1]
    abuf[:, 0:SUBLANES, :] = jnp.zeros((abuf.shape[0], SUBLANES, LANES), F32)
    fix1[...] = jnp.zeros(fix1.shape, F32)
    fix2[...] = jnp.zeros(fix2.shape, F32)
    x = x_ref[...]
    h = _rms(x, gpre_ref[...]).astype(BF16)
    t = lax.broadcasted_iota(jnp.int32, (rows, 1), 0) % seg
    for c in range(ff // FFN_CHUNK):
        cs = slice(c * FFN_CHUNK, (c + 1) * FFN_CHUNK)
        a = _dot(h, wup_ref[:, cs])
        b = _dot(h, wup_ref[:, ff + c * FFN_CHUNK:ff + (c + 1) * FFN_CHUNK])
        am1, am2 = _segment_conv_taps(a, t, seg, prev0_ref.at[:, cs], prev1_ref.at[:, cs], abuf, fix1, fix2)
        st0_ref[:, cs], st1_ref[:, cs] = _segment_tails(abuf, rows // seg, seg)
        cw = cw_ref[:, cs]
        conv = cw[2:3, :] * a + cw[1:2, :] * am1 + cw[0:1, :] * am2
        gbuf[:, cs] = (_gelu_tanh(conv) * b).astype(BF16)
    out = _dot(gbuf[...], wdown_ref[...])
    y_ref[...] = x + _rms(out, gpost_ref[...])


def _ffn_sample(layer, x, prev, seg, gpre, gpost, wup, cw, wdown):
    rows, d = x.shape
    nseq = rows // seg
    ff = cw.shape[1]
    assert ff % FFN_CHUNK == 0 and FFN_CHUNK % LANES == 0, ff
    full = lambda shape: pl.BlockSpec(shape, lambda i: (0,) * len(shape))
    layer_block = lambda w: pl.BlockSpec((None,) + w.shape[1:], lambda i: (layer, 0, 0),
                                         pipeline_mode=pl.Buffered(1))
    nlb = FFN_CHUNK // LANES
    y, st0, st1 = pl.pallas_call(
        functools.partial(_ffn_sample_kernel, seg),
        grid=(1,),
        in_specs=[_resident(x.shape), _resident((nseq, ff)), _resident((nseq, ff)), _resident(gpre.shape),
                  _resident(gpost.shape), layer_block(wup), _resident(cw.shape), layer_block(wdown)],
        out_specs=[full((rows, d)), full((nseq, ff)), full((nseq, ff))],
        out_shape=[jax.ShapeDtypeStruct((rows, d), F32), jax.ShapeDtypeStruct((nseq, ff), F32),
                   jax.ShapeDtypeStruct((nseq, ff), F32)],
        scratch_shapes=[pltpu.VMEM((nlb, rows + SUBLANES, LANES), F32), pltpu.VMEM((nlb, rows, LANES), F32),
                        pltpu.VMEM((nlb, rows, LANES), F32), pltpu.VMEM((rows, ff), BF16)],
        compiler_params=_params(("arbitrary",)),
        name="ffn_sample",
    )(x, prev[:, 0], prev[:, 1], gpre, gpost, wup, cw, wdown)
    return y, jnp.stack([st0, st1], axis=1)


def _lower_bound_terms(logits_ref, layer):
    lg = logits_ref[...]
    e = jnp.exp(lg - jnp.max(lg, axis=0, keepdims=True))
    pr = e / jnp.sum(e, axis=0, keepdims=True)
    cum = pr[0:1, :]
    for r in range(1, layer + 1):
        cum = cum + pr[r:r + 1, :]
    lb = jnp.clip(cum - pr[0:1, :], 0.0, 1.0)
    return jnp.log(lb) * LOG2E, jnp.log1p(-lb) * LOG2E, 1.0 - lb


def _forget_gate(f, lb_terms):
    log2_lb, log2_1m_lb, one_m_lb = lb_terms
    f2 = f * LOG2E
    e = jnp.exp2(jnp.minimum(f2, -f2))
    w = 1.0 + e
    b = log2_1m_lb + (jnp.minimum(f2, 0.0) - jnp.log2(w))
    d = log2_lb - b
    log2_ft = jnp.maximum(log2_lb, b) + jnp.log2(1.0 + jnp.exp2(jnp.minimum(d, -d)))
    return log2_ft, one_m_lb * (jnp.where(f2 >= 0.0, e, 1.0) / w)


def _split_bf16(x):
    hi = x.astype(BF16)
    return hi, (x - hi.astype(F32)).astype(BF16)


def _hgrn_gate_out(x, h, ob, mixbuf, win_ref, ng_ref, wout_ref, gpost_ref):
    dkt = C_HEADS * C_DK
    g = _dot(h, win_ref[:, 3 * dkt:4 * dkt])
    ng = ng_ref[...]
    for hh in range(C_HEADS):
        sl = slice(hh * C_DK, (hh + 1) * C_DK)
        mixbuf[:, sl] = (_rms(ob[:, sl], ng) * (g[:, sl] * _sigmoid(g[:, sl]))).astype(BF16)
    out = _dot(mixbuf[...], wout_ref[...])
    return x + _rms(out, gpost_ref[...])


def _hgrn_exact_tile(q, k_in, log2_ft, vb, st, ob, sq, sk, slg):
    ts = sq.shape[0]
    sq[...] = q
    sk[...] = k_in
    slg[...] = log2_ft
    ob[...] = jnp.zeros(ob.shape, F32)
    grp = 2 * SUBLANES
    row = lax.broadcasted_iota(jnp.int32, (grp, 1), 0)

    def body(t, carry):
        t0 = pl.multiple_of(lax.bitwise_and(t, -grp), grp)
        sel = row == lax.bitwise_and(t, grp - 1)
        for hh in range(C_HEADS):
            sl = slice(hh * C_DK, (hh + 1) * C_DK)
            rows_of = lambda ref: jnp.where(sel, ref[pl.ds(t0, grp), sl].astype(F32), 0.0)
            pick = lambda ref: rows_of(ref).astype(BF16)
            log2_f = jnp.sum(rows_of(slg), axis=0, keepdims=True)
            s_new = st[hh] * jnp.exp2(log2_f) + _dot_tn(pick(vb), pick(sk))
            st[hh] = s_new
            ob[pl.ds(t0, grp), sl] += _dot_nt(pick(sq), s_new.astype(BF16))
        return carry

    lax.fori_loop(0, ts, body, 0)


def _hgrn_prompt_kernel(layer, x_ref, gpre_ref, gpost_ref, win_ref, lbl_ref, ng_ref, wout_ref,
                        y_ref, sout_ref, st, st0, qtb, ktb, qib, klb, eglb, vb, ob, mixbuf, sq, sk, slg):
    s = pl.program_id(1)
    ts = x_ref.shape[0]
    dkt = C_HEADS * C_DK
    ck = HGRN_CHUNK

    @pl.when(s == 0)
    def _():
        st[...] = jnp.zeros(st.shape, F32)

    x = x_ref[...]
    h = _rms(x, gpre_ref[...]).astype(BF16)
    f = _dot(h, win_ref[:, dkt:2 * dkt])
    log2_ft, k_in = _forget_gate(f, _lower_bound_terms(lbl_ref, layer))
    q = _dot(h, win_ref[:, 0:dkt])
    vb[...] = _dot(h, win_ref[:, 2 * dkt:3 * dkt]).astype(BF16)

    ri = lax.broadcasted_iota(jnp.int32, (ck, ck), 0)
    ci = lax.broadcasted_iota(jnp.int32, (ck, ck), 1)
    causal = ri >= ci
    tri = jnp.where(causal, 1.0, 0.0).astype(BF16)
    mid = ck // 2 - 1
    nck = ts // ck

    spread = jnp.zeros((1, dkt), F32)
    for c in range(nck):
        rows = slice(c * ck, (c + 1) * ck)
        hi, lo = _split_bf16(log2_ft[rows, :])
        gcum = _dot(tri, hi) + _dot(tri, lo)
        gr = gcum[mid:mid + 1, :]
        gl = gcum[ck - 1:ck, :]
        spread = jnp.maximum(spread, jnp.maximum(gcum[0:1, :] - gr, gr - gl))
        qc = q[rows, :]
        kc = k_in[rows, :]
        qtb[rows, :] = (qc * jnp.exp2(gcum - gr)).astype(BF16)
        ktb[rows, :] = (kc * jnp.exp2(gr - gcum)).astype(BF16)
        qib[rows, :] = (qc * jnp.exp2(gcum)).astype(BF16)
        klb[rows, :] = (kc * jnp.exp2(gl - gcum)).astype(BF16)
        eglb[c] = jnp.exp2(gl)

    st0[...] = st[...]
    for c in range(nck):
        rows = slice(c * ck, (c + 1) * ck)
        for hh in range(C_HEADS):
            sl = slice(hh * C_DK, (hh + 1) * C_DK)
            vh = vb[rows, sl]
            att = jnp.where(causal, _dot_nt(qtb[rows, sl], ktb[rows, sl]), 0.0).astype(BF16)
            st_h = st[hh]
            ob[rows, sl] = _dot(jnp.concatenate([qib[rows, sl], att], axis=1),
                                jnp.concatenate([st_h.astype(BF16).T, vh], axis=0))
            st[hh] = eglb[c, :, sl] * st_h + _dot_tn(vh, klb[rows, sl])

    y_ref[...] = _hgrn_gate_out(x, h, ob, mixbuf, win_ref, ng_ref, wout_ref, gpost_ref)

    @pl.when(jnp.max(spread) > DECAY_LOG2_LIMIT)
    def _():
        st[...] = st0[...]
        _hgrn_exact_tile(q, k_in, log2_ft, vb, st, ob, sq, sk, slg)
        y_ref[...] = _hgrn_gate_out(x, h, ob, mixbuf, win_ref, ng_ref, wout_ref, gpost_ref)

    @pl.when(s == pl.num_programs(1) - 1)
    def _():
        for hh in range(C_HEADS):
            sout_ref[hh] = st[hh].T


def _hgrn_prompt(layer, x, gpre, gpost, win, lbl, ng, wout):
    b, seq, d = x.shape
    ts = HGRN_TILE
    assert seq % ts == 0 and ts % HGRN_CHUNK == 0, (seq, ts)
    dkt = C_HEADS * C_DK
    tile = pl.BlockSpec((None, ts, d), lambda i, j: (i, j, 0))
    return pl.pallas_call(
        functools.partial(_hgrn_prompt_kernel, layer),
        grid=(b, seq // ts),
        in_specs=[tile, _resident(gpre.shape), _resident(gpost.shape), _resident(win.shape),
                  _resident(lbl.shape), _resident(ng.shape), _resident(wout.shape)],
        out_specs=[tile, pl.BlockSpec((None, C_HEADS, C_DK, C_DK), lambda i, j: (i, 0, 0, 0))],
        out_shape=[jax.ShapeDtypeStruct((b, seq, d), F32),
                   jax.ShapeDtypeStruct((b, C_HEADS, C_DK, C_DK), F32)],
        scratch_shapes=[pltpu.VMEM((C_HEADS, C_DK, C_DK), F32), pltpu.VMEM((C_HEADS, C_DK, C_DK), F32),
                        pltpu.VMEM((ts, dkt), BF16), pltpu.VMEM((ts, dkt), BF16),
                        pltpu.VMEM((ts, dkt), BF16), pltpu.VMEM((ts, dkt), BF16),
                        pltpu.VMEM((ts // HGRN_CHUNK, 1, dkt), F32),
                        pltpu.VMEM((ts, dkt), BF16), pltpu.VMEM((ts, dkt), F32),
                        pltpu.VMEM((ts, dkt), BF16),
                        pltpu.VMEM((ts, dkt), F32), pltpu.VMEM((ts, dkt), F32), pltpu.VMEM((ts, dkt), F32)],
        compiler_params=_params(("arbitrary", "arbitrary")),
        name="hgrn_prompt",
    )(x, gpre, gpost, win, lbl, ng, wout)


def _mix_ab_sample_kernel(seg, x_ref, prev0_ref, prev1_ref, kc_ref, vc_ref, gpre_ref, gpost_ref, win_ref,
                          cw_ref, sink_ref, wout_ref,
                          y_ref, conv0_ref, conv1_ref, kout_ref, vout_ref,
                          ubuf, fix1, fix2, qbuf, knew_s, vnew_s, yabuf, ybbuf, kall, vall, bias_s, sink_s):
    i = pl.program_id(0)
    rows = x_ref.shape[0]
    aw = cw_ref.shape[1]
    nq = GROUP * KV_HEADS * HEAD_DIM
    nkv = KV_HEADS * HEAD_DIM
    pair = 2 * seg
    qrows = KV_HEADS * GROUP * pair
    tk = WINDOW + pair

    @pl.when(i == 0)
    def _():
        x = x_ref[...]
        h = _rms(x, gpre_ref[...]).astype(BF16)
        t = lax.broadcasted_iota(jnp.int32, (rows, 1), 0) % seg
        u = _dot(h, win_ref[:, 2 * aw:3 * aw]) * _dot(h, win_ref[:, 0:aw])
        ubuf[:, 0:SUBLANES, :] = jnp.zeros((ubuf.shape[0], SUBLANES, LANES), F32)
        fix1[...] = jnp.zeros(fix1.shape, F32)
        fix2[...] = jnp.zeros(fix2.shape, F32)
        um1, um2 = _segment_conv_taps(u, t, seg, prev0_ref, prev1_ref, ubuf, fix1, fix2)
        conv0_ref[...], conv1_ref[...] = _segment_tails(ubuf, rows // seg, seg)
        cw = cw_ref[...]
        conv = cw[2:3, :] * u + cw[1:2, :] * um1 + cw[0:1, :] * um2
        yabuf[...] = (_dot(h, win_ref[:, aw:2 * aw]) * conv).astype(BF16)
        q0 = 3 * aw
        qbuf[...] = _dot(h, win_ref[:, q0:q0 + nq]) * (HEAD_DIM ** -0.5)
        kv_new = _dot(h, win_ref[:, q0 + nq:q0 + nq + 2 * nkv])
        knew_s[...] = kv_new[:, 0:nkv]
        vnew_s[...] = kv_new[:, nkv:2 * nkv]

        r = lax.broadcasted_iota(jnp.int32, (qrows, tk), 0)
        j = lax.broadcasted_iota(jnp.int32, (qrows, tk), 1)
        dist = r % seg + WINDOW - j
        valid = (dist >= 0) & (dist < WINDOW) & (j < WINDOW + seg)
        rc = lax.broadcasted_iota(jnp.int32, (qrows, 1), 0)
        head = (rc // (GROUP * pair)) * GROUP + (rc // pair) % GROUP
        slope = jnp.zeros((qrows, 1), F32)
        sink = jnp.zeros((qrows, 1), F32)
        for hidx in range(KV_HEADS * GROUP):
            slope = jnp.where(head == hidx, _head_slope(hidx), slope)
            sink = jnp.where(head == hidx, sink_ref[:, hidx:hidx + 1], sink)
        bias_s[...] = jnp.where(valid, -slope * dist.astype(F32), NEG)
        sink_s[...] = sink

    nblk = kc_ref.shape[0]
    lane = lax.broadcasted_iota(jnp.int32, (1, nkv), 1)
    lo = lane < HEAD_DIM
    rq = lax.broadcasted_iota(jnp.int32, (qrows, 1), 0)
    first_seq = (rq // seg) % 2 == 0

    def body(p, carry):
        r0 = pl.multiple_of((i * nblk + 2 * p) * seg, pair)
        q8 = qbuf[pl.ds(r0, pair), :]
        qg = jnp.concatenate([q8[:, g * nkv:(g + 1) * nkv] for g in range(GROUP)], axis=0)
        qs = jnp.concatenate([jnp.where(lo, qg, 0.0), jnp.where(lo, 0.0, qg)], axis=0).astype(BF16)
        knew = knew_s[pl.ds(r0, pair), :]
        vnew = vnew_s[pl.ds(r0, pair), :]
        sc, ov = [], []
        for a in range(2):
            own_first = (lambda z: z) if a == 0 else (lambda z: jnp.concatenate([z[seg:], z[:seg]], axis=0))
            kall[a, 0:WINDOW, :] = kc_ref[2 * p + a]
            kall[a, WINDOW:tk, :] = own_first(knew)
            vall[a, 0:WINDOW, :] = vc_ref[2 * p + a]
            vall[a, WINDOW:tk, :] = own_first(vnew)
            kout_ref[2 * p + a] = kall[a, seg:seg + WINDOW, :]
            vout_ref[2 * p + a] = vall[a, seg:seg + WINDOW, :]
            sc.append(_dot_nt(qs, kall[a].astype(BF16)))
        s2 = jnp.where(first_seq, sc[0], sc[1]) + bias_s[...]
        sink = sink_s[...]
        m = jnp.maximum(jnp.max(s2, axis=-1, keepdims=True), sink)
        pr = jnp.exp(s2 - m)
        den = jnp.sum(pr, axis=-1, keepdims=True) + jnp.exp(sink - m)
        prb = pr.astype(BF16)
        for a in range(2):
            ov.append(_dot(prb, vall[a].astype(BF16)))
        o = jnp.where(first_seq, ov[0], ov[1]) / den
        og = jnp.where(lo, o[0:GROUP * pair, :], o[GROUP * pair:, :])
        for g in range(GROUP):
            ybbuf[pl.ds(r0, pair), g * nkv:(g + 1) * nkv] = og[g * pair:(g + 1) * pair, :]
        return carry

    lax.fori_loop(0, nblk // 2, body, 0, unroll=True)

    @pl.when(i == pl.num_programs(0) - 1)
    def _():
        out = _dot(yabuf[...], wout_ref[0:aw, :]) + _dot(ybbuf[...].astype(BF16), wout_ref[aw:, :])
        y_ref[...] = x_ref[...] + _rms(out, gpost_ref[...])


def _mix_ab_sample(x, conv_prev, kcache, vcache, seg, gpre, gpost, win, cw, sinks, wout):
    rows, d = x.shape
    nseq = rows // seg
    aw = cw.shape[1]
    nq = GROUP * KV_HEADS * HEAD_DIM
    nkv = KV_HEADS * HEAD_DIM
    sb = SAMPLE_SEQ_BLOCK
    pair = 2 * seg
    qrows = KV_HEADS * GROUP * pair
    tk = WINDOW + pair
    full = lambda shape: pl.BlockSpec(shape, lambda i: (0,) * len(shape))
    cache = pl.BlockSpec((sb, WINDOW, nkv), lambda i: (i, 0, 0))
    prev0, prev1 = conv_prev[:, 0], conv_prev[:, 1]
    y, conv0, conv1, k_new, v_new = pl.pallas_call(
        functools.partial(_mix_ab_sample_kernel, seg),
        grid=(nseq // sb,),
        in_specs=[_resident(x.shape), _resident(prev0.shape), _resident(prev1.shape), cache, cache,
                  _resident(gpre.shape), _resident(gpost.shape), _resident(win.shape), _resident(cw.shape),
                  _resident(sinks.shape), _resident(wout.shape)],
        out_specs=[full((rows, d)), full((nseq, aw)), full((nseq, aw)), cache, cache],
        out_shape=[jax.ShapeDtypeStruct((rows, d), F32), jax.ShapeDtypeStruct((nseq, aw), F32),
                   jax.ShapeDtypeStruct((nseq, aw), F32), jax.ShapeDtypeStruct(kcache.shape, F32),
                   jax.ShapeDtypeStruct(vcache.shape, F32)],
        scratch_shapes=[pltpu.VMEM((aw // LANES, rows + SUBLANES, LANES), F32),
                        pltpu.VMEM((aw // LANES, rows, LANES), F32),
                        pltpu.VMEM((aw // LANES, rows, LANES), F32),
                        pltpu.VMEM((rows, nq), F32),
                        pltpu.VMEM((rows, nkv), F32),
                        pltpu.VMEM((rows, nkv), F32),
                        pltpu.VMEM((rows, aw), BF16),
                        pltpu.VMEM((rows, nq), F32),
                        pltpu.VMEM((2, tk, nkv), F32),
                        pltpu.VMEM((2, tk, nkv), F32),
                        pltpu.VMEM((qrows, tk), F32),
                        pltpu.VMEM((qrows, 1), F32)],
        compiler_params=_params(("arbitrary",)),
        name="mix_ab_sample",
    )(x, prev0, prev1, kcache, vcache, gpre, gpost, win, cw, sinks, wout)
    return y, jnp.stack([conv0, conv1], axis=1), k_new, v_new


def _hgrn_sample_kernel(layer, seg, x_ref, s0_ref, gpre_ref, gpost_ref, win_ref, lbl_ref, ng_ref, wout_ref,
                        y_ref, s1_ref, qb, kb, vb, gb, ob, mixbuf):
    i = pl.program_id(0)
    rows = x_ref.shape[0]
    dkt = C_HEADS * C_DK
    pair = 2 * seg

    @pl.when(i == 0)
    def _():
        x = x_ref[...]
        h = _rms(x, gpre_ref[...]).astype(BF16)
        f = _dot(h, win_ref[:, dkt:2 * dkt])
        log2_ft, kb[...] = _forget_gate(f, _lower_bound_terms(lbl_ref, layer))
        hi, lo = _split_bf16(log2_ft)
        ri = lax.broadcasted_iota(jnp.int32, (rows, rows), 0)
        ci = lax.broadcasted_iota(jnp.int32, (rows, rows), 1)
        tri = jnp.where((ri // seg == ci // seg) & (ri >= ci), 1.0, 0.0).astype(BF16)
        gb[...] = _dot(tri, hi) + _dot(tri, lo)
        qb[...] = _dot(h, win_ref[:, 0:dkt])
        vb[...] = _dot(h, win_ref[:, 2 * dkt:3 * dkt])

    nblk = s0_ref.shape[0]
    rr = lax.broadcasted_iota(jnp.int32, (pair, 1), 0)
    first_seq = rr < seg
    second_seq = rr >= seg
    ri = lax.broadcasted_iota(jnp.int32, (pair, pair), 0)
    ci = lax.broadcasted_iota(jnp.int32, (pair, pair), 1)
    key_s = [(ri // seg == ci // seg) & (ci % seg == sk) & (ri % seg >= sk) for sk in range(seg)]

    def body(p, carry):
        r0 = pl.multiple_of((i * nblk + 2 * p) * seg, pair)
        for hh in range(C_HEADS):
            sl = slice(hh * C_DK, (hh + 1) * C_DK)
            gh = gb[pl.ds(r0, pair), sl]
            g_row = lambda t: jnp.where(first_seq, gh[t:t + 1, :], gh[seg + t:seg + t + 1, :])
            gl = g_row(seg - 1)
            qh = qb[pl.ds(r0, pair), sl]
            kh = kb[pl.ds(r0, pair), sl]
            vh = vb[pl.ds(r0, pair), sl].astype(BF16)
            q_rel = [(qh * jnp.exp2(gh - g_row(sk))).astype(BF16) for sk in range(seg - 1)] + [qh.astype(BF16)]
            rel = _dot_nt(jnp.concatenate(q_rel, axis=0), kh.astype(BF16))
            att = jnp.where(key_s[0], rel[0:pair, :], 0.0)
            for sk in range(1, seg):
                att = att + jnp.where(key_s[sk], rel[sk * pair:(sk + 1) * pair, :], 0.0)
            att = att.astype(BF16)
            qi = (qh * jnp.exp2(gh)).astype(BF16)
            kl = kh * jnp.exp2(gl - gh)
            o_prev = []
            for a in range(2):
                s_a = s0_ref[2 * p + a, hh]
                o_prev.append(_dot(qi, s_a.astype(BF16)))
                kl_a = jnp.where(first_seq if a == 0 else second_seq, kl, 0.0).astype(BF16)
                e = jnp.exp2(gh[(a + 1) * seg - 1:(a + 1) * seg, :])
                e1 = e.astype(BF16).astype(F32)
                e2 = (e - e1).astype(BF16).astype(F32)
                e3 = (e - e1 - e2).astype(BF16).astype(F32)
                dec_l = jnp.where(rr == 0, e1, jnp.where(rr == 1, e2, jnp.where(rr == 2, e3, 0.0)))
                dec = _dot_tn(dec_l.astype(BF16), jnp.ones((pair, C_DK), BF16))
                s1_ref[2 * p + a, hh] = dec * s_a + _dot_tn(kl_a, vh)
            ob[pl.ds(r0, pair), sl] = _dot(att, vh) + jnp.where(first_seq, o_prev[0], o_prev[1])
        return carry

    lax.fori_loop(0, nblk // 2, body, 0, unroll=True)

    @pl.when(i == pl.num_programs(0) - 1)
    def _():
        x = x_ref[...]
        h = _rms(x, gpre_ref[...]).astype(BF16)
        y_ref[...] = _hgrn_gate_out(x, h, ob, mixbuf, win_ref, ng_ref, wout_ref, gpost_ref)


def _hgrn_sample(layer, x, s0, seg, gpre, gpost, win, lbl, ng, wout):
    rows, d = x.shape
    assert 2 * seg == SUBLANES, "two sequences per iteration must fill one f32 vreg of sublanes"
    nseq = rows // seg
    dkt = C_HEADS * C_DK
    sb = SAMPLE_SEQ_BLOCK
    full = lambda shape: pl.BlockSpec(shape, lambda i: (0,) * len(shape))
    state = pl.BlockSpec((sb, C_HEADS, C_DK, C_DK), lambda i: (i, 0, 0, 0))
    return pl.pallas_call(
        functools.partial(_hgrn_sample_kernel, layer, seg),
        grid=(nseq // sb,),
        in_specs=[_resident(x.shape), state, _resident(gpre.shape), _resident(gpost.shape), _resident(win.shape),
                  _resident(lbl.shape), _resident(ng.shape), _resident(wout.shape)],
        out_specs=[full((rows, d)), state],
        out_shape=[jax.ShapeDtypeStruct((rows, d), F32), jax.ShapeDtypeStruct(s0.shape, F32)],
        scratch_shapes=[pltpu.VMEM((rows, dkt), F32), pltpu.VMEM((rows, dkt), F32),
                        pltpu.VMEM((rows, dkt), F32), pltpu.VMEM((rows, dkt), F32),
                        pltpu.VMEM((rows, dkt), F32), pltpu.VMEM((rows, dkt), BF16)],
        compiler_params=_params(("arbitrary",)),
        name="hgrn_sample",
    )(x, s0, gpre, gpost, win, lbl, ng, wout)


def kernel(x_prompt, x_sample, state_conv_a, cache_swa_k, cache_swa_v, state_hgrn, state_ffn_conv, norm_mix_pre, norm_mix_post, norm_ffn_pre, norm_ffn_post, w_in_ab, conv_a_w, attn_sinks, w_out_ab, w_in_c, hgrn_lb_logits, hgrn_norm_g, w_out_c, w_ffn_up, ffn_conv_w, w_ffn_down):
    bp, _, d = x_prompt.shape
    ns, seg, _ = x_sample.shape
    aw = conv_a_w.shape[-1]
    nq = GROUP * KV_HEADS * HEAD_DIM
    nkv = KV_HEADS * HEAD_DIM

    q0 = 3 * aw
    win_ab = w_in_ab[0].astype(BF16)
    wq = win_ab[:, q0:q0 + nq].reshape(d, KV_HEADS, GROUP, HEAD_DIM).swapaxes(1, 2).reshape(d, nq)
    win_ab = jnp.concatenate([win_ab[:, :q0], wq, win_ab[:, q0 + nq:]], axis=1)
    wout_ab = w_out_ab[0].astype(BF16)
    wo = wout_ab[aw:].reshape(KV_HEADS, GROUP, HEAD_DIM, d).swapaxes(0, 1).reshape(nq, d)
    wout_ab = jnp.concatenate([wout_ab[:aw], wo], axis=0)
    win_c = w_in_c[0].astype(BF16)
    wout_c = w_out_c[0].astype(BF16)
    wup = w_ffn_up.astype(BF16)
    wdown = w_ffn_down.astype(BF16)
    row = lambda a: a.reshape(1, -1)
    sinks = row(attn_sinks[0])
    ng = row(hgrn_norm_g[0])

    xs = x_sample.reshape(ns * seg, d)
    kc = cache_swa_k[0].reshape(ns, WINDOW, nkv)
    vc = cache_swa_v[0].reshape(ns, WINDOW, nkv)
    s1, conv_s, k_s, v_s = _mix_ab_sample(xs, state_conv_a[0], kc, vc, seg, row(norm_mix_pre[0]),
                                          row(norm_mix_post[0]), win_ab, conv_a_w[0], sinks, wout_ab)
    s2, ffn_s0 = _ffn_sample(0, s1, state_ffn_conv[0], seg, row(norm_ffn_pre[0]), row(norm_ffn_post[0]),
                             wup, ffn_conv_w[0], wdown)
    s3, hgrn_s = _hgrn_sample(1, s2, state_hgrn[0], seg, row(norm_mix_pre[1]), row(norm_mix_post[1]),
                              win_c, hgrn_lb_logits, ng, wout_c)
    y_sample, ffn_s1 = _ffn_sample(1, s3, state_ffn_conv[1], seg, row(norm_ffn_pre[1]), row(norm_ffn_post[1]),
                                   wup, ffn_conv_w[1], wdown)

    x1, conv_p, k_p, v_p = _mix_ab_prompt(x_prompt, row(norm_mix_pre[0]), row(norm_mix_post[0]), win_ab,
                                          conv_a_w[0], sinks, wout_ab)
    x2, ffn_p0 = _ffn_prompt(0, x1, row(norm_ffn_pre[0]), row(norm_ffn_post[0]), wup, ffn_conv_w[0], wdown)
    x3, hgrn_p = _hgrn_prompt(1, x2, row(norm_mix_pre[1]), row(norm_mix_post[1]), win_c, hgrn_lb_logits,
                              ng, wout_c)
    y_prompt, ffn_p1 = _ffn_prompt(1, x3, row(norm_ffn_pre[1]), row(norm_ffn_post[1]), wup, ffn_conv_w[1], wdown)

    kvshape = (KV_HEADS, HEAD_DIM)
    return (y_prompt, y_sample.reshape(ns, seg, d),
            conv_p[None], conv_s[None],
            k_p.reshape(1, bp, WINDOW, *kvshape), k_s.reshape(1, ns, WINDOW, *kvshape),
            v_p.reshape(1, bp, WINDOW, *kvshape), v_s.reshape(1, ns, WINDOW, *kvshape),
            hgrn_p[None], hgrn_s[None],
            jnp.stack([ffn_p0, ffn_p1]), jnp.stack([ffn_s0, ffn_s1]))
```

```python
import functools
import math

import jax
import jax.numpy as jnp
from jax import lax
from jax.experimental import pallas as pl
from jax.experimental.pallas import tpu as pltpu

F32 = jnp.float32
BF16 = jnp.bfloat16

EPS = 1e-6
LOG2E = 1.4426950408889634
WINDOW = 128
KV_HEADS = 2
GROUP = 4
HEAD_DIM = 64
C_HEADS = 8
C_DK = 128
NEG = -1e30
SUBLANES = 8
LANES = 128
MIX_TILE = 1024
HGRN_TILE = 512
FFN_TILE = 1024
HGRN_CHUNK = 64
DECAY_LOG2_LIMIT = 100.0
FFN_CHUNK = 256
SAMPLE_SEQ_BLOCK = 8
VMEM_LIMIT_V7X = 56 * 2**20


def _dot(a, b):
    return jnp.dot(a, b, preferred_element_type=F32)


def _dot_nt(a, b):
    return lax.dot_general(a, b, (((1,), (1,)), ((), ())), preferred_element_type=F32)


def _dot_tn(a, b):
    return lax.dot_general(a, b, (((0,), (0,)), ((), ())), preferred_element_type=F32)


def _rms(x, g):
    return x * lax.rsqrt(jnp.mean(x * x, axis=-1, keepdims=True) + EPS) * g


def _gelu_tanh(x):
    c = math.sqrt(2.0 / math.pi)
    return x * (0.5 * (1.0 + jnp.tanh(c * (x + 0.044715 * (x * x * x)))))


def _sigmoid(x):
    return 1.0 / (1.0 + jnp.exp(-x))


def _resident(shape):
    nd = len(shape)
    return pl.BlockSpec(shape, lambda *_: (0,) * nd, pipeline_mode=pl.Buffered(1))


def _params(sem):
    return pltpu.CompilerParams(dimension_semantics=sem, vmem_limit_bytes=VMEM_LIMIT_V7X)


def _head_slope(h):
    return 2.0 ** (-8.0 * (h + 1) / (KV_HEADS * GROUP))


def _mix_ab_prompt_kernel(x_ref, gpre_ref, gpost_ref, win_ref, cw_ref, sink_ref, wout_ref,
                          y_ref, conv_ref, kout_ref, vout_ref,
                          ubuf, qbuf, kbuf, vbuf, mixbuf, bias_s):
    s = pl.program_id(1)
    ts = x_ref.shape[0]
    aw = cw_ref.shape[1]
    nq = GROUP * KV_HEADS * HEAD_DIM
    nkv = KV_HEADS * HEAD_DIM

    @pl.when(s == 0)
    def _():
        ubuf[0:SUBLANES, :] = jnp.zeros((SUBLANES, aw), F32)
        kbuf[0:WINDOW, :] = jnp.zeros((WINDOW, nkv), BF16)
        vbuf[0:WINDOW, 0:nkv] = jnp.zeros((WINDOW, nkv), BF16)
        vbuf[:, nkv:2 * nkv] = jnp.ones((ts + WINDOW, nkv), BF16)

    x = x_ref[...]
    h = _rms(x, gpre_ref[...]).astype(BF16)

    u = _dot(h, win_ref[:, 2 * aw:3 * aw]) * _dot(h, win_ref[:, 0:aw])
    ubuf[SUBLANES:SUBLANES + ts, :] = u
    cw = cw_ref[...]
    conv = (cw[2:3, :] * u + cw[1:2, :] * ubuf[SUBLANES - 1:SUBLANES - 1 + ts, :]
            + cw[0:1, :] * ubuf[SUBLANES - 2:SUBLANES - 2 + ts, :])
    mixbuf[:, 0:aw] = (_dot(h, win_ref[:, aw:2 * aw]) * conv).astype(BF16)
    tail = ubuf[ts + SUBLANES - 2:ts + SUBLANES, :]
    conv_ref[...] = tail
    ubuf[SUBLANES - 2:SUBLANES, :] = tail

    q0 = 3 * aw
    q = _dot(h, win_ref[:, q0:q0 + nq]) * (HEAD_DIM ** -0.5 * LOG2E)
    lane = lax.broadcasted_iota(jnp.int32, (1, nkv), 1)
    lo = lane < HEAD_DIM
    nblk = ts // WINDOW
    for n in range(nblk):
        for g in range(GROUP):
            qg = q[n * WINDOW:(n + 1) * WINDOW, g * nkv:(g + 1) * nkv]
            qbuf[n, g * WINDOW:(g + 1) * WINDOW, :] = jnp.where(lo, qg, 0.0).astype(BF16)
            qbuf[n, (GROUP + g) * WINDOW:(GROUP + g + 1) * WINDOW, :] = jnp.where(lo, 0.0, qg).astype(BF16)
    wkv = win_ref[:, q0 + nq:q0 + nq + 2 * nkv]
    kv_new = jnp.concatenate([_dot(h[0:ts // 2, :], wkv), _dot(h[ts // 2:, :], wkv)], axis=0)
    k = kv_new[:, 0:nkv]
    v = kv_new[:, nkv:2 * nkv]
    kbuf[WINDOW:WINDOW + ts, :] = k.astype(BF16)
    vbuf[WINDOW:WINDOW + ts, 0:nkv] = v.astype(BF16)
    kout_ref[...] = k[ts - WINDOW:, :]
    vout_ref[...] = v[ts - WINDOW:, :]

    qi = lax.broadcasted_iota(jnp.int32, (WINDOW, 2 * WINDOW), 0)
    kj = lax.broadcasted_iota(jnp.int32, (WINDOW, 2 * WINDOW), 1)
    dist = qi + WINDOW - kj
    valid = (dist >= 0) & (dist < WINDOW)
    negdist = jnp.where(valid, -dist.astype(F32), 0.0)
    maskadd = jnp.where(valid, 0.0, NEG)
    nheads = KV_HEADS * GROUP
    for hidx in range(nheads):
        bias_s[hidx] = (_head_slope(hidx) * LOG2E) * negdist + maskadd
    first = jnp.where(kj < WINDOW, jnp.where(s == 0, NEG, 0.0), 0.0)

    for n in range(nblk):
        r0 = n * WINDOW
        half = GROUP * WINDOW
        keys = kbuf[r0:r0 + 2 * WINDOW, :]
        sc_kv = [_dot_nt(qbuf[n, kv * half:(kv + 1) * half, :], keys) for kv in range(KV_HEADS)]
        ps, sink_terms = [], []
        for hidx in range(nheads):
            g = hidx % GROUP
            sc = sc_kv[hidx // GROUP][g * WINDOW:(g + 1) * WINDOW, :] + bias_s[hidx]
            if n == 0:
                sc = sc + first
            sink = sink_ref[hidx] * LOG2E
            m = jnp.maximum(jnp.max(sc, axis=-1, keepdims=True), sink)
            ps.append(jnp.exp2(sc - m).astype(BF16))
            sink_terms.append(jnp.exp2(sink - m))
        vals = vbuf[r0:r0 + 2 * WINDOW, :]
        pv_kv = [_dot(jnp.concatenate(ps[kv * GROUP:(kv + 1) * GROUP], axis=0), vals) for kv in range(KV_HEADS)]
        outs = []
        for hidx in range(nheads):
            g = hidx % GROUP
            pv_h = pv_kv[hidx // GROUP][g * WINDOW:(g + 1) * WINDOW, :]
            outs.append(pv_h[:, 0:nkv] / (pv_h[:, nkv:2 * nkv] + sink_terms[hidx]))
        for g in range(GROUP):
            og = jnp.where(lo, outs[g], outs[GROUP + g])
            mixbuf[r0:r0 + WINDOW, aw + g * nkv:aw + (g + 1) * nkv] = og.astype(BF16)

    kbuf[0:WINDOW, :] = kbuf[ts:ts + WINDOW, :]
    vbuf[0:WINDOW, :] = vbuf[ts:ts + WINDOW, :]

    out = _dot(mixbuf[...], wout_ref[...])
    y_ref[...] = x + _rms(out, gpost_ref[...])


def _mix_ab_prompt(x, gpre, gpost, win, cw, sinks, wout):
    b, seq, d = x.shape
    ts = MIX_TILE
    assert seq % ts == 0 and ts % WINDOW == 0, (seq, ts)
    aw = cw.shape[1]
    nkv = KV_HEADS * HEAD_DIM
    tile = pl.BlockSpec((None, ts, d), lambda i, j: (i, j, 0))
    return pl.pallas_call(
        _mix_ab_prompt_kernel,
        grid=(b, seq // ts),
        in_specs=[tile, _resident(gpre.shape), _resident(gpost.shape), _resident(win.shape),
                  _resident(cw.shape), pl.BlockSpec(memory_space=pltpu.SMEM), _resident(wout.shape)],
        out_specs=[tile,
                   pl.BlockSpec((None, 2, aw), lambda i, j: (i, 0, 0)),
                   pl.BlockSpec((None, WINDOW, nkv), lambda i, j: (i, 0, 0)),
                   pl.BlockSpec((None, WINDOW, nkv), lambda i, j: (i, 0, 0))],
        out_shape=[jax.ShapeDtypeStruct((b, seq, d), F32),
                   jax.ShapeDtypeStruct((b, 2, aw), F32),
                   jax.ShapeDtypeStruct((b, WINDOW, nkv), F32),
                   jax.ShapeDtypeStruct((b, WINDOW, nkv), F32)],
        scratch_shapes=[pltpu.VMEM((ts + SUBLANES, aw), F32),
                        pltpu.VMEM((ts // WINDOW, KV_HEADS * GROUP * WINDOW, nkv), BF16),
                        pltpu.VMEM((ts + WINDOW, nkv), BF16),
                        pltpu.VMEM((ts + WINDOW, 2 * nkv), BF16),
                        pltpu.VMEM((ts, d), BF16),
                        pltpu.VMEM((KV_HEADS * GROUP, WINDOW, 2 * WINDOW), F32)],
        compiler_params=_params(("arbitrary", "arbitrary")),
        name="mix_ab_prompt",
    )(x, gpre, gpost, win, cw, sinks.reshape(-1), wout)


def _ffn_prompt_kernel(x_ref, gpre_ref, gpost_ref, wup_ref, cw_ref, wdown_ref,
                       y_ref, st_ref, abuf, carry, gbuf):
    s = pl.program_id(1)
    rows = x_ref.shape[0]
    ff = cw_ref.shape[1]

    @pl.when(s == 0)
    def _():
        carry[...] = jnp.zeros(carry.shape, F32)

    x = x_ref[...]
    h = _rms(x, gpre_ref[...]).astype(BF16)
    for c in range(ff // FFN_CHUNK):
        c0 = c * FFN_CHUNK
        buf = abuf.at[c % 2]
        a = _dot(h, wup_ref[:, c0:c0 + FFN_CHUNK])
        b = _dot(h, wup_ref[:, ff + c0:ff + c0 + FFN_CHUNK])
        buf[0:SUBLANES, :] = carry[:, c0:c0 + FFN_CHUNK]
        buf[SUBLANES:SUBLANES + rows, :] = a
        cw = cw_ref[:, c0:c0 + FFN_CHUNK]
        conv = (cw[2:3, :] * a + cw[1:2, :] * buf[SUBLANES - 1:SUBLANES - 1 + rows, :]
                + cw[0:1, :] * buf[SUBLANES - 2:SUBLANES - 2 + rows, :])
        tail = a[rows - 2:rows, :]
        st_ref[:, c0:c0 + FFN_CHUNK] = tail
        carry[SUBLANES - 2:SUBLANES, c0:c0 + FFN_CHUNK] = tail
        gbuf[:, c0:c0 + FFN_CHUNK] = (_gelu_tanh(conv) * b).astype(BF16)
    out = _dot(gbuf[...], wdown_ref[...])
    y_ref[...] = x + _rms(out, gpost_ref[...])


def _ffn_prompt(layer, x, gpre, gpost, wup, cw, wdown):
    b, seq, d = x.shape
    ts = FFN_TILE
    ff = cw.shape[1]
    assert seq % ts == 0 and ff % FFN_CHUNK == 0, (seq, ts, ff)
    tile = pl.BlockSpec((None, ts, d), lambda i, j: (i, j, 0))
    layer_block = lambda w: pl.BlockSpec((None,) + w.shape[1:], lambda i, j: (layer, 0, 0),
                                         pipeline_mode=pl.Buffered(1))
    return pl.pallas_call(
        _ffn_prompt_kernel,
        grid=(b, seq // ts),
        in_specs=[tile, _resident(gpre.shape), _resident(gpost.shape), layer_block(wup),
                  _resident(cw.shape), layer_block(wdown)],
        out_specs=[tile, pl.BlockSpec((None, 2, ff), lambda i, j: (i, 0, 0))],
        out_shape=[jax.ShapeDtypeStruct((b, seq, d), F32), jax.ShapeDtypeStruct((b, 2, ff), F32)],
        scratch_shapes=[pltpu.VMEM((2, ts + SUBLANES, FFN_CHUNK), F32), pltpu.VMEM((SUBLANES, ff), F32),
                        pltpu.VMEM((ts, ff), BF16)],
        compiler_params=_params(("arbitrary", "arbitrary")),
        name="ffn_prompt",
    )(x, gpre, gpost, wup, cw, wdown)


def _segment_conv_taps(a, t, seg, prev0_ref, prev1_ref, abuf, fix1, fix2):
    rows, c = a.shape
    nseq = rows // seg
    am1, am2 = [], []
    for k in range(c // LANES):
        sl = slice(k * LANES, (k + 1) * LANES)
        abuf[k, SUBLANES:SUBLANES + rows, :] = a[:, sl]
        fix1[k, pl.ds(0, nseq, stride=seg), :] = prev1_ref[:, sl]
        fix2[k, pl.ds(0, nseq, stride=seg), :] = prev0_ref[:, sl]
        fix2[k, pl.ds(1, nseq, stride=seg), :] = prev1_ref[:, sl]
        am1.append(jnp.where(t >= 1, abuf[k, SUBLANES - 1:SUBLANES - 1 + rows, :], fix1[k]))
        am2.append(jnp.where(t >= 2, abuf[k, SUBLANES - 2:SUBLANES - 2 + rows, :], fix2[k]))
    return jnp.concatenate(am1, axis=1), jnp.concatenate(am2, axis=1)


def _segment_tails(abuf, nseq, seg):
    tail = lambda back: jnp.concatenate(
        [abuf[k, pl.ds(SUBLANES + seg - back, nseq, stride=seg), :] for k in range(abuf.shape[0])], axis=1)
    return tail(2), tail(1)


def _ffn_sample_kernel(seg, layer, x_ref, prev0_ref, prev1_ref, gpre_ref, gpost_ref, wup_hbm, cw_ref, wdown_hbm,
                       y_ref, st0_ref, st1_ref, abuf, fix1, fix2, gbuf, wup_ref, wdown_ref, sems):
    rows = x_ref.shape[0]
    ff = cw_ref.shape[1]
    up_copy = pltpu.make_async_copy(wup_hbm.at[layer], wup_ref, sems.at[0])
    down_copy = pltpu.make_async_copy(wdown_hbm.at[layer], wdown_ref, sems.at[1])
    up_copy.start()
    down_copy.start()
    abuf[:, 0:SUBLANES, :] = jnp.zeros((abuf.shape[0], SUBLANES, LANES), F32)
    fix1[...] = jnp.zeros(fix1.shape, F32)
    fix2[...] = jnp.zeros(fix2.shape, F32)
    x = x_ref[...]
    h = _rms(x, gpre_ref[...]).astype(BF16)
    t = lax.broadcasted_iota(jnp.int32, (rows, 1), 0) % seg
    up_copy.wait()
    for c in range(ff // FFN_CHUNK):
        cs = slice(c * FFN_CHUNK, (c + 1) * FFN_CHUNK)
        a = _dot(h, wup_ref[:, cs])
        b = _dot(h, wup_ref[:, ff + c * FFN_CHUNK:ff + (c + 1) * FFN_CHUNK])
        am1, am2 = _segment_conv_taps(a, t, seg, prev0_ref.at[:, cs], prev1_ref.at[:, cs], abuf, fix1, fix2)
        st0_ref[:, cs], st1_ref[:, cs] = _segment_tails(abuf, rows // seg, seg)
        cw = cw_ref[:, cs]
        conv = cw[2:3, :] * a + cw[1:2, :] * am1 + cw[0:1, :] * am2
        gbuf[:, cs] = (_gelu_tanh(conv) * b).astype(BF16)
    down_copy.wait()
    out = _dot(gbuf[...], wdown_ref[...])
    y_ref[...] = x + _rms(out, gpost_ref[...])


def _ffn_sample(layer, x, prev, seg, gpre, gpost, wup, cw, wdown):
    rows, d = x.shape
    nseq = rows // seg
    ff = cw.shape[1]
    assert ff % FFN_CHUNK == 0 and FFN_CHUNK % LANES == 0, ff
    full = lambda shape: pl.BlockSpec(shape, lambda i: (0,) * len(shape))
    in_hbm = pl.BlockSpec(memory_space=pl.ANY)
    nlb = FFN_CHUNK // LANES
    y, st0, st1 = pl.pallas_call(
        functools.partial(_ffn_sample_kernel, seg, layer),
        grid=(1,),
        in_specs=[_resident(x.shape), _resident((nseq, ff)), _resident((nseq, ff)), _resident(gpre.shape),
                  _resident(gpost.shape), in_hbm, _resident(cw.shape), in_hbm],
        out_specs=[full((rows, d)), full((nseq, ff)), full((nseq, ff))],
        out_shape=[jax.ShapeDtypeStruct((rows, d), F32), jax.ShapeDtypeStruct((nseq, ff), F32),
                   jax.ShapeDtypeStruct((nseq, ff), F32)],
        scratch_shapes=[pltpu.VMEM((nlb, rows + SUBLANES, LANES), F32), pltpu.VMEM((nlb, rows, LANES), F32),
                        pltpu.VMEM((nlb, rows, LANES), F32), pltpu.VMEM((rows, ff), BF16),
                        pltpu.VMEM(wup.shape[1:], BF16), pltpu.VMEM(wdown.shape[1:], BF16),
                        pltpu.SemaphoreType.DMA((2,))],
        compiler_params=_params(("arbitrary",)),
        name="ffn_sample",
    )(x, prev[:, 0], prev[:, 1], gpre, gpost, wup, cw, wdown)
    return y, jnp.stack([st0, st1], axis=1)


def _lower_bound_terms(logits_ref, layer):
    lg = logits_ref[...]
    e = jnp.exp(lg - jnp.max(lg, axis=0, keepdims=True))
    pr = e / jnp.sum(e, axis=0, keepdims=True)
    cum = pr[0:1, :]
    for r in range(1, layer + 1):
        cum = cum + pr[r:r + 1, :]
    lb = jnp.clip(cum - pr[0:1, :], 0.0, 1.0)
    return jnp.log(lb) * LOG2E, jnp.log1p(-lb) * LOG2E, 1.0 - lb


def _forget_gate(f, lb_terms):
    log2_lb, log2_1m_lb, one_m_lb = lb_terms
    f2 = f * LOG2E
    e = jnp.exp2(jnp.minimum(f2, -f2))
    w = 1.0 + e
    b = log2_1m_lb + (jnp.minimum(f2, 0.0) - jnp.log2(w))
    d = log2_lb - b
    log2_ft = jnp.maximum(log2_lb, b) + jnp.log2(1.0 + jnp.exp2(jnp.minimum(d, -d)))
    return log2_ft, one_m_lb * (jnp.where(f2 >= 0.0, e, 1.0) / w)


def _split_bf16(x):
    hi = x.astype(BF16)
    return hi, (x - hi.astype(F32)).astype(BF16)


def _hgrn_gate_out(x, h, ob, mixbuf, win_ref, ng_ref, wout_ref, gpost_ref):
    dkt = C_HEADS * C_DK
    g = _dot(h, win_ref[:, 3 * dkt:4 * dkt])
    ng = ng_ref[...]
    for hh in range(C_HEADS):
        sl = slice(hh * C_DK, (hh + 1) * C_DK)
        mixbuf[:, sl] = (_rms(ob[:, sl], ng) * (g[:, sl] * _sigmoid(g[:, sl]))).astype(BF16)
    out = _dot(mixbuf[...], wout_ref[...])
    return x + _rms(out, gpost_ref[...])


def _hgrn_exact_tile(q, k_in, log2_ft, vb, st, ob, sq, sk, slg):
    ts = sq.shape[0]
    sq[...] = q
    sk[...] = k_in
    slg[...] = log2_ft
    ob[...] = jnp.zeros(ob.shape, F32)
    grp = 2 * SUBLANES
    row = lax.broadcasted_iota(jnp.int32, (grp, 1), 0)

    def body(t, carry):
        t0 = pl.multiple_of(lax.bitwise_and(t, -grp), grp)
        sel = row == lax.bitwise_and(t, grp - 1)
        for hh in range(C_HEADS):
            sl = slice(hh * C_DK, (hh + 1) * C_DK)
            rows_of = lambda ref: jnp.where(sel, ref[pl.ds(t0, grp), sl].astype(F32), 0.0)
            pick = lambda ref: rows_of(ref).astype(BF16)
            log2_f = jnp.sum(rows_of(slg), axis=0, keepdims=True)
            s_new = st[hh] * jnp.exp2(log2_f) + _dot_tn(pick(vb), pick(sk))
            st[hh] = s_new
            ob[pl.ds(t0, grp), sl] += _dot_nt(pick(sq), s_new.astype(BF16))
        return carry

    lax.fori_loop(0, ts, body, 0)


def _hgrn_prompt_kernel(layer, x_ref, gpre_ref, gpost_ref, win_ref, lbl_ref, ng_ref, wout_ref,
                        y_ref, sout_ref, st, st0, qtb, ktb, qib, klb, eglb, vb, ob, mixbuf, sq, sk, slg):
    s = pl.program_id(1)
    ts = x_ref.shape[0]
    dkt = C_HEADS * C_DK
    ck = HGRN_CHUNK

    @pl.when(s == 0)
    def _():
        st[...] = jnp.zeros(st.shape, F32)

    x = x_ref[...]
    h = _rms(x, gpre_ref[...]).astype(BF16)
    f = _dot(h, win_ref[:, dkt:2 * dkt])
    log2_ft, k_in = _forget_gate(f, _lower_bound_terms(lbl_ref, layer))
    q = _dot(h, win_ref[:, 0:dkt])
    vb[...] = _dot(h, win_ref[:, 2 * dkt:3 * dkt]).astype(BF16)

    ri = lax.broadcasted_iota(jnp.int32, (ck, ck), 0)
    ci = lax.broadcasted_iota(jnp.int32, (ck, ck), 1)
    causal = ri >= ci
    tri = jnp.where(causal, 1.0, 0.0).astype(BF16)
    mid = ck // 2 - 1
    nck = ts // ck

    spread = jnp.zeros((1, dkt), F32)
    for c in range(nck):
        rows = slice(c * ck, (c + 1) * ck)
        hi, lo = _split_bf16(log2_ft[rows, :])
        gcum = _dot(tri, hi) + _dot(tri, lo)
        gr = gcum[mid:mid + 1, :]
        gl = gcum[ck - 1:ck, :]
        spread = jnp.maximum(spread, jnp.maximum(gcum[0:1, :] - gr, gr - gl))
        qc = q[rows, :]
        kc = k_in[rows, :]
        qtb[rows, :] = (qc * jnp.exp2(gcum - gr)).astype(BF16)
        ktb[rows, :] = (kc * jnp.exp2(gr - gcum)).astype(BF16)
        qib[rows, :] = (qc * jnp.exp2(gcum)).astype(BF16)
        klb[rows, :] = (kc * jnp.exp2(gl - gcum)).astype(BF16)
        eglb[c] = jnp.exp2(gl)

    st0[...] = st[...]
    for c in range(nck):
        rows = slice(c * ck, (c + 1) * ck)
        for hh in range(C_HEADS):
            sl = slice(hh * C_DK, (hh + 1) * C_DK)
            vh = vb[rows, sl]
            att = jnp.where(causal, _dot_nt(qtb[rows, sl], ktb[rows, sl]), 0.0).astype(BF16)
            st_h = st[hh]
            ob[rows, sl] = _dot(jnp.concatenate([qib[rows, sl], att], axis=1),
                                jnp.concatenate([st_h.astype(BF16).T, vh], axis=0))
            st[hh] = eglb[c, :, sl] * st_h + _dot_tn(vh, klb[rows, sl])

    y_ref[...] = _hgrn_gate_out(x, h, ob, mixbuf, win_ref, ng_ref, wout_ref, gpost_ref)

    @pl.when(jnp.max(spread) > DECAY_LOG2_LIMIT)
    def _():
        st[...] = st0[...]
        _hgrn_exact_tile(q, k_in, log2_ft, vb, st, ob, sq, sk, slg)
        y_ref[...] = _hgrn_gate_out(x, h, ob, mixbuf, win_ref, ng_ref, wout_ref, gpost_ref)

    @pl.when(s == pl.num_programs(1) - 1)
    def _():
        for hh in range(C_HEADS):
            sout_ref[hh] = st[hh].T


def _hgrn_prompt(layer, x, gpre, gpost, win, lbl, ng, wout):
    b, seq, d = x.shape
    ts = HGRN_TILE
    assert seq % ts == 0 and ts % HGRN_CHUNK == 0, (seq, ts)
    dkt = C_HEADS * C_DK
    tile = pl.BlockSpec((None, ts, d), lambda i, j: (i, j, 0))
    return pl.pallas_call(
        functools.partial(_hgrn_prompt_kernel, layer),
        grid=(b, seq // ts),
        in_specs=[tile, _resident(gpre.shape), _resident(gpost.shape), _resident(win.shape),
                  _resident(lbl.shape), _resident(ng.shape), _resident(wout.shape)],
        out_specs=[tile, pl.BlockSpec((None, C_HEADS, C_DK, C_DK), lambda i, j: (i, 0, 0, 0))],
        out_shape=[jax.ShapeDtypeStruct((b, seq, d), F32),
                   jax.ShapeDtypeStruct((b, C_HEADS, C_DK, C_DK), F32)],
        scratch_shapes=[pltpu.VMEM((C_HEADS, C_DK, C_DK), F32), pltpu.VMEM((C_HEADS, C_DK, C_DK), F32),
                        pltpu.VMEM((ts, dkt), BF16), pltpu.VMEM((ts, dkt), BF16),
                        pltpu.VMEM((ts, dkt), BF16), pltpu.VMEM((ts, dkt), BF16),
                        pltpu.VMEM((ts // HGRN_CHUNK, 1, dkt), F32),
                        pltpu.VMEM((ts, dkt), BF16), pltpu.VMEM((ts, dkt), F32),
                        pltpu.VMEM((ts, dkt), BF16),
                        pltpu.VMEM((ts, dkt), F32), pltpu.VMEM((ts, dkt), F32), pltpu.VMEM((ts, dkt), F32)],
        compiler_params=_params(("arbitrary", "arbitrary")),
        name="hgrn_prompt",
    )(x, gpre, gpost, win, lbl, ng, wout)


def _mix_ab_sample_kernel(seg, x_ref, prev0_ref, prev1_ref, kc_ref, vc_ref, gpre_ref, gpost_ref, win_ref,
                          cw_ref, sink_ref, wout_ref,
                          y_ref, conv0_ref, conv1_ref, kout_ref, vout_ref,
                          ubuf, fix1, fix2, qbuf, knew_s, vnew_s, yabuf, ybbuf, kall, vall, bias_s, sink_s):
    i = pl.program_id(0)
    rows = x_ref.shape[0]
    aw = cw_ref.shape[1]
    nq = GROUP * KV_HEADS * HEAD_DIM
    nkv = KV_HEADS * HEAD_DIM
    pair = 2 * seg
    qrows = KV_HEADS * GROUP * pair
    tk = WINDOW + pair

    @pl.when(i == 0)
    def _():
        x = x_ref[...]
        h = _rms(x, gpre_ref[...]).astype(BF16)
        t = lax.broadcasted_iota(jnp.int32, (rows, 1), 0) % seg
        u = _dot(h, win_ref[:, 2 * aw:3 * aw]) * _dot(h, win_ref[:, 0:aw])
        ubuf[:, 0:SUBLANES, :] = jnp.zeros((ubuf.shape[0], SUBLANES, LANES), F32)
        fix1[...] = jnp.zeros(fix1.shape, F32)
        fix2[...] = jnp.zeros(fix2.shape, F32)
        um1, um2 = _segment_conv_taps(u, t, seg, prev0_ref, prev1_ref, ubuf, fix1, fix2)
        conv0_ref[...], conv1_ref[...] = _segment_tails(ubuf, rows // seg, seg)
        cw = cw_ref[...]
        conv = cw[2:3, :] * u + cw[1:2, :] * um1 + cw[0:1, :] * um2
        yabuf[...] = (_dot(h, win_ref[:, aw:2 * aw]) * conv).astype(BF16)
        q0 = 3 * aw
        qbuf[...] = _dot(h, win_ref[:, q0:q0 + nq]) * (HEAD_DIM ** -0.5)
        kv_new = _dot(h, win_ref[:, q0 + nq:q0 + nq + 2 * nkv])
        knew_s[...] = kv_new[:, 0:nkv]
        vnew_s[...] = kv_new[:, nkv:2 * nkv]

        r = lax.broadcasted_iota(jnp.int32, (qrows, tk), 0)
        j = lax.broadcasted_iota(jnp.int32, (qrows, tk), 1)
        dist = r % seg + WINDOW - j
        valid = (dist >= 0) & (dist < WINDOW) & (j < WINDOW + seg)
        rc = lax.broadcasted_iota(jnp.int32, (qrows, 1), 0)
        head = (rc // (GROUP * pair)) * GROUP + (rc // pair) % GROUP
        slope = jnp.zeros((qrows, 1), F32)
        sink = jnp.zeros((qrows, 1), F32)
        for hidx in range(KV_HEADS * GROUP):
            slope = jnp.where(head == hidx, _head_slope(hidx), slope)
            sink = jnp.where(head == hidx, sink_ref[:, hidx:hidx + 1], sink)
        bias_s[...] = jnp.where(valid, -slope * dist.astype(F32), NEG)
        sink_s[...] = sink

    nblk = kc_ref.shape[0]
    lane = lax.broadcasted_iota(jnp.int32, (1, nkv), 1)
    lo = lane < HEAD_DIM
    rq = lax.broadcasted_iota(jnp.int32, (qrows, 1), 0)
    first_seq = (rq // seg) % 2 == 0

    def body(p, carry):
        r0 = pl.multiple_of((i * nblk + 2 * p) * seg, pair)
        q8 = qbuf[pl.ds(r0, pair), :]
        qg = jnp.concatenate([q8[:, g * nkv:(g + 1) * nkv] for g in range(GROUP)], axis=0)
        qs = jnp.concatenate([jnp.where(lo, qg, 0.0), jnp.where(lo, 0.0, qg)], axis=0).astype(BF16)
        knew = knew_s[pl.ds(r0, pair), :]
        vnew = vnew_s[pl.ds(r0, pair), :]
        sc, ov = [], []
        for a in range(2):
            own_first = (lambda z: z) if a == 0 else (lambda z: jnp.concatenate([z[seg:], z[:seg]], axis=0))
            kall[a, 0:WINDOW, :] = kc_ref[2 * p + a]
            kall[a, WINDOW:tk, :] = own_first(knew)
            vall[a, 0:WINDOW, :] = vc_ref[2 * p + a]
            vall[a, WINDOW:tk, :] = own_first(vnew)
            kout_ref[2 * p + a] = kall[a, seg:seg + WINDOW, :]
            vout_ref[2 * p + a] = vall[a, seg:seg + WINDOW, :]
            sc.append(_dot_nt(qs, kall[a].astype(BF16)))
        s2 = jnp.where(first_seq, sc[0], sc[1]) + bias_s[...]
        sink = sink_s[...]
        m = jnp.maximum(jnp.max(s2, axis=-1, keepdims=True), sink)
        pr = jnp.exp(s2 - m)
        den = jnp.sum(pr, axis=-1, keepdims=True) + jnp.exp(sink - m)
        prb = pr.astype(BF16)
        for a in range(2):
            ov.append(_dot(prb, vall[a].astype(BF16)))
        o = jnp.where(first_seq, ov[0], ov[1]) / den
        og = jnp.where(lo, o[0:GROUP * pair, :], o[GROUP * pair:, :])
        for g in range(GROUP):
            ybbuf[pl.ds(r0, pair), g * nkv:(g + 1) * nkv] = og[g * pair:(g + 1) * pair, :]
        return carry

    lax.fori_loop(0, nblk // 2, body, 0, unroll=True)

    @pl.when(i == pl.num_programs(0) - 1)
    def _():
        out = _dot(yabuf[...], wout_ref[0:aw, :]) + _dot(ybbuf[...].astype(BF16), wout_ref[aw:, :])
        y_ref[...] = x_ref[...] + _rms(out, gpost_ref[...])


def _mix_ab_sample(x, conv_prev, kcache, vcache, seg, gpre, gpost, win, cw, sinks, wout):
    rows, d = x.shape
    nseq = rows // seg
    aw = cw.shape[1]
    nq = GROUP * KV_HEADS * HEAD_DIM
    nkv = KV_HEADS * HEAD_DIM
    sb = SAMPLE_SEQ_BLOCK
    pair = 2 * seg
    qrows = KV_HEADS * GROUP * pair
    tk = WINDOW + pair
    full = lambda shape: pl.BlockSpec(shape, lambda i: (0,) * len(shape))
    cache = pl.BlockSpec((sb, WINDOW, nkv), lambda i: (i, 0, 0))
    prev0, prev1 = conv_prev[:, 0], conv_prev[:, 1]
    y, conv0, conv1, k_new, v_new = pl.pallas_call(
        functools.partial(_mix_ab_sample_kernel, seg),
        grid=(nseq // sb,),
        in_specs=[_resident(x.shape), _resident(prev0.shape), _resident(prev1.shape), cache, cache,
                  _resident(gpre.shape), _resident(gpost.shape), _resident(win.shape), _resident(cw.shape),
                  _resident(sinks.shape), _resident(wout.shape)],
        out_specs=[full((rows, d)), full((nseq, aw)), full((nseq, aw)), cache, cache],
        out_shape=[jax.ShapeDtypeStruct((rows, d), F32), jax.ShapeDtypeStruct((nseq, aw), F32),
                   jax.ShapeDtypeStruct((nseq, aw), F32), jax.ShapeDtypeStruct(kcache.shape, F32),
                   jax.ShapeDtypeStruct(vcache.shape, F32)],
        scratch_shapes=[pltpu.VMEM((aw // LANES, rows + SUBLANES, LANES), F32),
                        pltpu.VMEM((aw // LANES, rows, LANES), F32),
                        pltpu.VMEM((aw // LANES, rows, LANES), F32),
                        pltpu.VMEM((rows, nq), F32),
                        pltpu.VMEM((rows, nkv), F32),
                        pltpu.VMEM((rows, nkv), F32),
                        pltpu.VMEM((rows, aw), BF16),
                        pltpu.VMEM((rows, nq), F32),
                        pltpu.VMEM((2, tk, nkv), F32),
                        pltpu.VMEM((2, tk, nkv), F32),
                        pltpu.VMEM((qrows, tk), F32),
                        pltpu.VMEM((qrows, 1), F32)],
        compiler_params=_params(("arbitrary",)),
        name="mix_ab_sample",
    )(x, prev0, prev1, kcache, vcache, gpre, gpost, win, cw, sinks, wout)
    return y, jnp.stack([conv0, conv1], axis=1), k_new, v_new


def _hgrn_sample_kernel(layer, seg, x_ref, s0_ref, gpre_ref, gpost_ref, win_ref, lbl_ref, ng_ref, wout_ref,
                        y_ref, s1_ref, qb, kb, vb, gb, ob, mixbuf):
    i = pl.program_id(0)
    rows = x_ref.shape[0]
    dkt = C_HEADS * C_DK
    pair = 2 * seg

    @pl.when(i == 0)
    def _():
        x = x_ref[...]
        h = _rms(x, gpre_ref[...]).astype(BF16)
        f = _dot(h, win_ref[:, dkt:2 * dkt])
        log2_ft, kb[...] = _forget_gate(f, _lower_bound_terms(lbl_ref, layer))
        hi, lo = _split_bf16(log2_ft)
        ri = lax.broadcasted_iota(jnp.int32, (rows, rows), 0)
        ci = lax.broadcasted_iota(jnp.int32, (rows, rows), 1)
        tri = jnp.where((ri // seg == ci // seg) & (ri >= ci), 1.0, 0.0).astype(BF16)
        gb[...] = _dot(tri, hi) + _dot(tri, lo)
        qb[...] = _dot(h, win_ref[:, 0:dkt])
        vb[...] = _dot(h, win_ref[:, 2 * dkt:3 * dkt])

    nblk = s0_ref.shape[0]
    rr = lax.broadcasted_iota(jnp.int32, (pair, 1), 0)
    first_seq = rr < seg
    second_seq = rr >= seg
    ri = lax.broadcasted_iota(jnp.int32, (pair, pair), 0)
    ci = lax.broadcasted_iota(jnp.int32, (pair, pair), 1)
    key_s = [(ri // seg == ci // seg) & (ci % seg == sk) & (ri % seg >= sk) for sk in range(seg)]

    def body(p, carry):
        r0 = pl.multiple_of((i * nblk + 2 * p) * seg, pair)
        for hh in range(C_HEADS):
            sl = slice(hh * C_DK, (hh + 1) * C_DK)
            gh = gb[pl.ds(r0, pair), sl]
            g_row = lambda t: jnp.where(first_seq, gh[t:t + 1, :], gh[seg + t:seg + t + 1, :])
            gl = g_row(seg - 1)
            qh = qb[pl.ds(r0, pair), sl]
            kh = kb[pl.ds(r0, pair), sl]
            vh = vb[pl.ds(r0, pair), sl].astype(BF16)
            q_rel = [(qh * jnp.exp2(gh - g_row(sk))).astype(BF16) for sk in range(seg - 1)] + [qh.astype(BF16)]
            rel = _dot_nt(jnp.concatenate(q_rel, axis=0), kh.astype(BF16))
            att = jnp.where(key_s[0], rel[0:pair, :], 0.0)
            for sk in range(1, seg):
                att = att + jnp.where(key_s[sk], rel[sk * pair:(sk + 1) * pair, :], 0.0)
            att = att.astype(BF16)
            qi = (qh * jnp.exp2(gh)).astype(BF16)
            kl = kh * jnp.exp2(gl - gh)
            o_prev = []
            for a in range(2):
                s_a = s0_ref[2 * p + a, hh]
                o_prev.append(_dot(qi, s_a.astype(BF16)))
                kl_a = jnp.where(first_seq if a == 0 else second_seq, kl, 0.0).astype(BF16)
                e = jnp.exp2(gh[(a + 1) * seg - 1:(a + 1) * seg, :])
                e1 = e.astype(BF16).astype(F32)
                e2 = (e - e1).astype(BF16).astype(F32)
                e3 = (e - e1 - e2).astype(BF16).astype(F32)
                dec_l = jnp.where(rr == 0, e1, jnp.where(rr == 1, e2, jnp.where(rr == 2, e3, 0.0)))
                dec = _dot_tn(dec_l.astype(BF16), jnp.ones((pair, C_DK), BF16))
                s1_ref[2 * p + a, hh] = dec * s_a + _dot_tn(kl_a, vh)
            ob[pl.ds(r0, pair), sl] = _dot(att, vh) + jnp.where(first_seq, o_prev[0], o_prev[1])
        return carry

    lax.fori_loop(0, nblk // 2, body, 0, unroll=True)

    @pl.when(i == pl.num_programs(0) - 1)
    def _():
        x = x_ref[...]
        h = _rms(x, gpre_ref[...]).astype(BF16)
        y_ref[...] = _hgrn_gate_out(x, h, ob, mixbuf, win_ref, ng_ref, wout_ref, gpost_ref)


def _hgrn_sample(layer, x, s0, seg, gpre, gpost, win, lbl, ng, wout):
    rows, d = x.shape
    assert 2 * seg == SUBLANES, "two sequences per iteration must fill one f32 vreg of sublanes"
    nseq = rows // seg
    dkt = C_HEADS * C_DK
    sb = SAMPLE_SEQ_BLOCK
    full = lambda shape: pl.BlockSpec(shape, lambda i: (0,) * len(shape))
    state = pl.BlockSpec((sb, C_HEADS, C_DK, C_DK), lambda i: (i, 0, 0, 0))
    return pl.pallas_call(
        functools.partial(_hgrn_sample_kernel, layer, seg),
        grid=(nseq // sb,),
        in_specs=[_resident(x.shape), state, _resident(gpre.shape), _resident(gpost.shape), _resident(win.shape),
                  _resident(lbl.shape), _resident(ng.shape), _resident(wout.shape)],
        out_specs=[full((rows, d)), state],
        out_shape=[jax.ShapeDtypeStruct((rows, d), F32), jax.ShapeDtypeStruct(s0.shape, F32)],
        scratch_shapes=[pltpu.VMEM((rows, dkt), F32), pltpu.VMEM((rows, dkt), F32),
                        pltpu.VMEM((rows, dkt), F32), pltpu.VMEM((rows, dkt), F32),
                        pltpu.VMEM((rows, dkt), F32), pltpu.VMEM((rows, dkt), BF16)],
        compiler_params=_params(("arbitrary",)),
        name="hgrn_sample",
    )(x, s0, gpre, gpost, win, lbl, ng, wout)


def kernel(x_prompt, x_sample, state_conv_a, cache_swa_k, cache_swa_v, state_hgrn, state_ffn_conv, norm_mix_pre, norm_mix_post, norm_ffn_pre, norm_ffn_post, w_in_ab, conv_a_w, attn_sinks, w_out_ab, w_in_c, hgrn_lb_logits, hgrn_norm_g, w_out_c, w_ffn_up, ffn_conv_w, w_ffn_down):
    bp, _, d = x_prompt.shape
    ns, seg, _ = x_sample.shape
    aw = conv_a_w.shape[-1]
    nq = GROUP * KV_HEADS * HEAD_DIM
    nkv = KV_HEADS * HEAD_DIM

    q0 = 3 * aw
    win_ab = w_in_ab[0].astype(BF16)
    wq = win_ab[:, q0:q0 + nq].reshape(d, KV_HEADS, GROUP, HEAD_DIM).swapaxes(1, 2).reshape(d, nq)
    win_ab = jnp.concatenate([win_ab[:, :q0], wq, win_ab[:, q0 + nq:]], axis=1)
    wout_ab = w_out_ab[0].astype(BF16)
    wo = wout_ab[aw:].reshape(KV_HEADS, GROUP, HEAD_DIM, d).swapaxes(0, 1).reshape(nq, d)
    wout_ab = jnp.concatenate([wout_ab[:aw], wo], axis=0)
    win_c = w_in_c[0].astype(BF16)
    wout_c = w_out_c[0].astype(BF16)
    wup = w_ffn_up.astype(BF16)
    wdown = w_ffn_down.astype(BF16)
    row = lambda a: a.reshape(1, -1)
    sinks = row(attn_sinks[0])
    ng = row(hgrn_norm_g[0])

    xs = x_sample.reshape(ns * seg, d)
    kc = cache_swa_k[0].reshape(ns, WINDOW, nkv)
    vc = cache_swa_v[0].reshape(ns, WINDOW, nkv)
    s1, conv_s, k_s, v_s = _mix_ab_sample(xs, state_conv_a[0], kc, vc, seg, row(norm_mix_pre[0]),
                                          row(norm_mix_post[0]), win_ab, conv_a_w[0], sinks, wout_ab)
    s2, ffn_s0 = _ffn_sample(0, s1, state_ffn_conv[0], seg, row(norm_ffn_pre[0]), row(norm_ffn_post[0]),
                             wup, ffn_conv_w[0], wdown)
    s3, hgrn_s = _hgrn_sample(1, s2, state_hgrn[0], seg, row(norm_mix_pre[1]), row(norm_mix_post[1]),
                              win_c, hgrn_lb_logits, ng, wout_c)
    y_sample, ffn_s1 = _ffn_sample(1, s3, state_ffn_conv[1], seg, row(norm_ffn_pre[1]), row(norm_ffn_post[1]),
                                   wup, ffn_conv_w[1], wdown)

    x1, conv_p, k_p, v_p = _mix_ab_prompt(x_prompt, row(norm_mix_pre[0]), row(norm_mix_post[0]), win_ab,
                                          conv_a_w[0], sinks, wout_ab)
    x2, ffn_p0 = _ffn_prompt(0, x1, row(norm_ffn_pre[0]), row(norm_ffn_post[0]), wup, ffn_conv_w[0], wdown)
    x3, hgrn_p = _hgrn_prompt(1, x2, row(norm_mix_pre[1]), row(norm_mix_post[1]), win_c, hgrn_lb_logits,
                              ng, wout_c)
    y_prompt, ffn_p1 = _ffn_prompt(1, x3, row(norm_ffn_pre[1]), row(norm_ffn_post[1]), wup, ffn_conv_w[1], wdown)

    kvshape = (KV_HEADS, HEAD_DIM)
    return (y_prompt, y_sample.reshape(ns, seg, d),
            conv_p[None], conv_s[None],
            k_p.reshape(1, bp, WINDOW, *kvshape), k_s.reshape(1, ns, WINDOW, *kvshape),
            v_p.reshape(1, bp, WINDOW, *kvshape), v_s.reshape(1, ns, WINDOW, *kvshape),
            hgrn_p[None], hgrn_s[None],
            jnp.stack([ffn_p0, ffn_p1]), jnp.stack([ffn_s0, ffn_s1]))
```

```python
import functools
import math

import jax
import jax.numpy as jnp
from jax import lax
from jax.experimental import pallas as pl
from jax.experimental.pallas import tpu as pltpu

F32 = jnp.float32
BF16 = jnp.bfloat16

EPS = 1e-6
LOG2E = 1.4426950408889634
WINDOW = 128
KV_HEADS = 2
GROUP = 4
HEAD_DIM = 64
C_HEADS = 8
C_DK = 128
NEG = -1e30
SUBLANES = 8
LANES = 128
MIX_TILE = 1024
HGRN_TILE = 512
FFN_TILE = 1024
HGRN_CHUNK = 64
DECAY_LOG2_LIMIT = 100.0
FFN_CHUNK = 256
SAMPLE_SEQ_BLOCK = 8
VMEM_LIMIT_V7X = 56 * 2**20


def _dot(a, b):
    return jnp.dot(a, b, preferred_element_type=F32)


def _dot_nt(a, b):
    return lax.dot_general(a, b, (((1,), (1,)), ((), ())), preferred_element_type=F32)


def _dot_tn(a, b):
    return lax.dot_general(a, b, (((0,), (0,)), ((), ())), preferred_element_type=F32)


def _rms(x, g):
    return x * lax.rsqrt(jnp.mean(x * x, axis=-1, keepdims=True) + EPS) * g


def _gelu_tanh(x):
    c = math.sqrt(2.0 / math.pi)
    return x * (0.5 * (1.0 + jnp.tanh(c * (x + 0.044715 * (x * x * x)))))


def _sigmoid(x):
    return 1.0 / (1.0 + jnp.exp(-x))


def _resident(shape):
    nd = len(shape)
    return pl.BlockSpec(shape, lambda *_: (0,) * nd, pipeline_mode=pl.Buffered(1))


def _params(sem):
    return pltpu.CompilerParams(dimension_semantics=sem, vmem_limit_bytes=VMEM_LIMIT_V7X)


def _head_slope(h):
    return 2.0 ** (-8.0 * (h + 1) / (KV_HEADS * GROUP))


def _mix_ab_prompt_kernel(x_ref, gpre_ref, gpost_ref, win_ref, cw_ref, sink_ref, wout_ref,
                          y_ref, conv_ref, kout_ref, vout_ref,
                          ubuf, qbuf, kbuf, vbuf, mixbuf, bias_s):
    s = pl.program_id(1)
    ts = x_ref.shape[0]
    aw = cw_ref.shape[1]
    nq = GROUP * KV_HEADS * HEAD_DIM
    nkv = KV_HEADS * HEAD_DIM

    @pl.when(s == 0)
    def _():
        ubuf[0:SUBLANES, :] = jnp.zeros((SUBLANES, aw), F32)
        kbuf[0:WINDOW, :] = jnp.zeros((WINDOW, nkv), BF16)
        vbuf[0:WINDOW, 0:nkv] = jnp.zeros((WINDOW, nkv), BF16)
        vbuf[:, nkv:2 * nkv] = jnp.ones((ts + WINDOW, nkv), BF16)

    x = x_ref[...]
    h = _rms(x, gpre_ref[...]).astype(BF16)

    u = _dot(h, win_ref[:, 2 * aw:3 * aw]) * _dot(h, win_ref[:, 0:aw])
    ubuf[SUBLANES:SUBLANES + ts, :] = u
    cw = cw_ref[...]
    conv = (cw[2:3, :] * u + cw[1:2, :] * ubuf[SUBLANES - 1:SUBLANES - 1 + ts, :]
            + cw[0:1, :] * ubuf[SUBLANES - 2:SUBLANES - 2 + ts, :])
    mixbuf[:, 0:aw] = (_dot(h, win_ref[:, aw:2 * aw]) * conv).astype(BF16)
    tail = ubuf[ts + SUBLANES - 2:ts + SUBLANES, :]
    conv_ref[...] = tail
    ubuf[SUBLANES - 2:SUBLANES, :] = tail

    q0 = 3 * aw
    q = _dot(h, win_ref[:, q0:q0 + nq]) * (HEAD_DIM ** -0.5 * LOG2E)
    lane = lax.broadcasted_iota(jnp.int32, (1, nkv), 1)
    lo = lane < HEAD_DIM
    nblk = ts // WINDOW
    for n in range(nblk):
        for g in range(GROUP):
            qg = q[n * WINDOW:(n + 1) * WINDOW, g * nkv:(g + 1) * nkv]
            qbuf[n, g * WINDOW:(g + 1) * WINDOW, :] = jnp.where(lo, qg, 0.0).astype(BF16)
            qbuf[n, (GROUP + g) * WINDOW:(GROUP + g + 1) * WINDOW, :] = jnp.where(lo, 0.0, qg).astype(BF16)
    wkv = win_ref[:, q0 + nq:q0 + nq + 2 * nkv]
    kv_new = jnp.concatenate([_dot(h[0:ts // 2, :], wkv), _dot(h[ts // 2:, :], wkv)], axis=0)
    k = kv_new[:, 0:nkv]
    v = kv_new[:, nkv:2 * nkv]
    kbuf[WINDOW:WINDOW + ts, :] = k.astype(BF16)
    vbuf[WINDOW:WINDOW + ts, 0:nkv] = v.astype(BF16)
    kout_ref[...] = k[ts - WINDOW:, :]
    vout_ref[...] = v[ts - WINDOW:, :]

    qi = lax.broadcasted_iota(jnp.int32, (WINDOW, 2 * WINDOW), 0)
    kj = lax.broadcasted_iota(jnp.int32, (WINDOW, 2 * WINDOW), 1)
    dist = qi + WINDOW - kj
    valid = (dist >= 0) & (dist < WINDOW)
    negdist = jnp.where(valid, -dist.astype(F32), 0.0)
    maskadd = jnp.where(valid, 0.0, NEG)
    nheads = KV_HEADS * GROUP
    for hidx in range(nheads):
        bias_s[hidx] = (_head_slope(hidx) * LOG2E) * negdist + maskadd
    first = jnp.where(kj < WINDOW, jnp.where(s == 0, NEG, 0.0), 0.0)

    for n in range(nblk):
        r0 = n * WINDOW
        half = GROUP * WINDOW
        keys = kbuf[r0:r0 + 2 * WINDOW, :]
        sc_kv = [_dot_nt(qbuf[n, kv * half:(kv + 1) * half, :], keys) for kv in range(KV_HEADS)]
        ps, sink_terms = [], []
        for hidx in range(nheads):
            g = hidx % GROUP
            sc = sc_kv[hidx // GROUP][g * WINDOW:(g + 1) * WINDOW, :] + bias_s[hidx]
            if n == 0:
                sc = sc + first
            sink = sink_ref[hidx] * LOG2E
            m = jnp.maximum(jnp.max(sc, axis=-1, keepdims=True), sink)
            ps.append(jnp.exp2(sc - m).astype(BF16))
            sink_terms.append(jnp.exp2(sink - m))
        vals = vbuf[r0:r0 + 2 * WINDOW, :]
        pv_kv = [_dot(jnp.concatenate(ps[kv * GROUP:(kv + 1) * GROUP], axis=0), vals) for kv in range(KV_HEADS)]
        outs = []
        for hidx in range(nheads):
            g = hidx % GROUP
            pv_h = pv_kv[hidx // GROUP][g * WINDOW:(g + 1) * WINDOW, :]
            outs.append(pv_h[:, 0:nkv] / (pv_h[:, nkv:2 * nkv] + sink_terms[hidx]))
        for g in range(GROUP):
            og = jnp.where(lo, outs[g], outs[GROUP + g])
            mixbuf[r0:r0 + WINDOW, aw + g * nkv:aw + (g + 1) * nkv] = og.astype(BF16)

    kbuf[0:WINDOW, :] = kbuf[ts:ts + WINDOW, :]
    vbuf[0:WINDOW, :] = vbuf[ts:ts + WINDOW, :]

    out = _dot(mixbuf[...], wout_ref[...])
    y_ref[...] = x + _rms(out, gpost_ref[...])


def _mix_ab_prompt(x, gpre, gpost, win, cw, sinks, wout):
    b, seq, d = x.shape
    ts = MIX_TILE
    assert seq % ts == 0 and ts % WINDOW == 0, (seq, ts)
    aw = cw.shape[1]
    nkv = KV_HEADS * HEAD_DIM
    tile = pl.BlockSpec((None, ts, d), lambda i, j: (i, j, 0))
    return pl.pallas_call(
        _mix_ab_prompt_kernel,
        grid=(b, seq // ts),
        in_specs=[tile, _resident(gpre.shape), _resident(gpost.shape), _resident(win.shape),
                  _resident(cw.shape), pl.BlockSpec(memory_space=pltpu.SMEM), _resident(wout.shape)],
        out_specs=[tile,
                   pl.BlockSpec((None, 2, aw), lambda i, j: (i, 0, 0)),
                   pl.BlockSpec((None, WINDOW, nkv), lambda i, j: (i, 0, 0)),
                   pl.BlockSpec((None, WINDOW, nkv), lambda i, j: (i, 0, 0))],
        out_shape=[jax.ShapeDtypeStruct((b, seq, d), F32),
                   jax.ShapeDtypeStruct((b, 2, aw), F32),
                   jax.ShapeDtypeStruct((b, WINDOW, nkv), F32),
                   jax.ShapeDtypeStruct((b, WINDOW, nkv), F32)],
        scratch_shapes=[pltpu.VMEM((ts + SUBLANES, aw), F32),
                        pltpu.VMEM((ts // WINDOW, KV_HEADS * GROUP * WINDOW, nkv), BF16),
                        pltpu.VMEM((ts + WINDOW, nkv), BF16),
                        pltpu.VMEM((ts + WINDOW, 2 * nkv), BF16),
                        pltpu.VMEM((ts, d), BF16),
                        pltpu.VMEM((KV_HEADS * GROUP, WINDOW, 2 * WINDOW), F32)],
        compiler_params=_params(("arbitrary", "arbitrary")),
        name="mix_ab_prompt",
    )(x, gpre, gpost, win, cw, sinks.reshape(-1), wout)


def _ffn_prompt_kernel(x_ref, gpre_ref, gpost_ref, wup_ref, cw_ref, wdown_ref,
                       y_ref, st_ref, abuf, carry, gbuf):
    s = pl.program_id(1)
    rows = x_ref.shape[0]
    ff = cw_ref.shape[1]

    @pl.when(s == 0)
    def _():
        carry[...] = jnp.zeros(carry.shape, F32)

    x = x_ref[...]
    h = _rms(x, gpre_ref[...]).astype(BF16)
    for c in range(ff // FFN_CHUNK):
        c0 = c * FFN_CHUNK
        buf = abuf.at[c % 2]
        a = _dot(h, wup_ref[:, c0:c0 + FFN_CHUNK])
        b = _dot(h, wup_ref[:, ff + c0:ff + c0 + FFN_CHUNK])
        buf[0:SUBLANES, :] = carry[:, c0:c0 + FFN_CHUNK]
        buf[SUBLANES:SUBLANES + rows, :] = a
        cw = cw_ref[:, c0:c0 + FFN_CHUNK]
        conv = (cw[2:3, :] * a + cw[1:2, :] * buf[SUBLANES - 1:SUBLANES - 1 + rows, :]
                + cw[0:1, :] * buf[SUBLANES - 2:SUBLANES - 2 + rows, :])
        tail = a[rows - 2:rows, :]
        st_ref[:, c0:c0 + FFN_CHUNK] = tail
        carry[SUBLANES - 2:SUBLANES, c0:c0 + FFN_CHUNK] = tail
        gbuf[:, c0:c0 + FFN_CHUNK] = (_gelu_tanh(conv) * b).astype(BF16)
    out = _dot(gbuf[...], wdown_ref[...])
    y_ref[...] = x + _rms(out, gpost_ref[...])


def _ffn_prompt(layer, x, gpre, gpost, wup, cw, wdown):
    b, seq, d = x.shape
    ts = FFN_TILE
    ff = cw.shape[1]
    assert seq % ts == 0 and ff % FFN_CHUNK == 0, (seq, ts, ff)
    tile = pl.BlockSpec((None, ts, d), lambda i, j: (i, j, 0))
    layer_block = lambda w: pl.BlockSpec((None,) + w.shape[1:], lambda i, j: (layer, 0, 0),
                                         pipeline_mode=pl.Buffered(1))
    return pl.pallas_call(
        _ffn_prompt_kernel,
        grid=(b, seq // ts),
        in_specs=[tile, _resident(gpre.shape), _resident(gpost.shape), layer_block(wup),
                  _resident(cw.shape), layer_block(wdown)],
        out_specs=[tile, pl.BlockSpec((None, 2, ff), lambda i, j: (i, 0, 0))],
        out_shape=[jax.ShapeDtypeStruct((b, seq, d), F32), jax.ShapeDtypeStruct((b, 2, ff), F32)],
        scratch_shapes=[pltpu.VMEM((2, ts + SUBLANES, FFN_CHUNK), F32), pltpu.VMEM((SUBLANES, ff), F32),
                        pltpu.VMEM((ts, ff), BF16)],
        compiler_params=_params(("arbitrary", "arbitrary")),
        name="ffn_prompt",
    )(x, gpre, gpost, wup, cw, wdown)


def _segment_conv_taps(a, t, seg, prev0_ref, prev1_ref, abuf, fix1, fix2):
    rows, c = a.shape
    nseq = rows // seg
    am1, am2 = [], []
    for k in range(c // LANES):
        sl = slice(k * LANES, (k + 1) * LANES)
        abuf[k, SUBLANES:SUBLANES + rows, :] = a[:, sl]
        fix1[k, pl.ds(0, nseq, stride=seg), :] = prev1_ref[:, sl]
        fix2[k, pl.ds(0, nseq, stride=seg), :] = prev0_ref[:, sl]
        fix2[k, pl.ds(1, nseq, stride=seg), :] = prev1_ref[:, sl]
        am1.append(jnp.where(t >= 1, abuf[k, SUBLANES - 1:SUBLANES - 1 + rows, :], fix1[k]))
        am2.append(jnp.where(t >= 2, abuf[k, SUBLANES - 2:SUBLANES - 2 + rows, :], fix2[k]))
    return jnp.concatenate(am1, axis=1), jnp.concatenate(am2, axis=1)


def _segment_tails(abuf, nseq, seg):
    tail = lambda back: jnp.concatenate(
        [abuf[k, pl.ds(SUBLANES + seg - back, nseq, stride=seg), :] for k in range(abuf.shape[0])], axis=1)
    return tail(2), tail(1)


def _ffn_sample_kernel(seg, x_ref, prev0_ref, prev1_ref, gpre_ref, gpost_ref, wup_ref, cw_ref, wdown_ref,
                       y_ref, st0_ref, st1_ref, abuf, fix1, fix2, gbuf):
    rows = x_ref.shape[0]
    ff = cw_ref.shape[1]
    abuf[:, 0:SUBLANES, :] = jnp.zeros((abuf.shape[0], SUBLANES, LANES), F32)
    fix1[...] = jnp.zeros(fix1.shape, F32)
    fix2[...] = jnp.zeros(fix2.shape, F32)
    x = x_ref[...]
    h = _rms(x, gpre_ref[...]).astype(BF16)
    t = lax.broadcasted_iota(jnp.int32, (rows, 1), 0) % seg
    for c in range(ff // FFN_CHUNK):
        cs = slice(c * FFN_CHUNK, (c + 1) * FFN_CHUNK)
        a = _dot(h, wup_ref[:, cs])
        b = _dot(h, wup_ref[:, ff + c * FFN_CHUNK:ff + (c + 1) * FFN_CHUNK])
        am1, am2 = _segment_conv_taps(a, t, seg, prev0_ref.at[:, cs], prev1_ref.at[:, cs], abuf, fix1, fix2)
        st0_ref[:, cs], st1_ref[:, cs] = _segment_tails(abuf, rows // seg, seg)
        cw = cw_ref[:, cs]
        conv = cw[2:3, :] * a + cw[1:2, :] * am1 + cw[0:1, :] * am2
        gbuf[:, cs] = (_gelu_tanh(conv) * b).astype(BF16)
    out = _dot(gbuf[...], wdown_ref[...])
    y_ref[...] = x + _rms(out, gpost_ref[...])


def _ffn_sample(layer, x, prev, seg, gpre, gpost, wup, cw, wdown):
    rows, d = x.shape
    nseq = rows // seg
    ff = cw.shape[1]
    assert ff % FFN_CHUNK == 0 and FFN_CHUNK % LANES == 0, ff
    full = lambda shape: pl.BlockSpec(shape, lambda i: (0,) * len(shape))
    layer_block = lambda w: pl.BlockSpec((None,) + w.shape[1:], lambda i: (layer, 0, 0),
                                         pipeline_mode=pl.Buffered(1))
    nlb = FFN_CHUNK // LANES
    y, st0, st1 = pl.pallas_call(
        functools.partial(_ffn_sample_kernel, seg),
        grid=(1,),
        in_specs=[_resident(x.shape), _resident((nseq, ff)), _resident((nseq, ff)), _resident(gpre.shape),
                  _resident(gpost.shape), layer_block(wup), _resident(cw.shape), layer_block(wdown)],
        out_specs=[full((rows, d)), full((nseq, ff)), full((nseq, ff))],
        out_shape=[jax.ShapeDtypeStruct((rows, d), F32), jax.ShapeDtypeStruct((nseq, ff), F32),
                   jax.ShapeDtypeStruct((nseq, ff), F32)],
        scratch_shapes=[pltpu.VMEM((nlb, rows + SUBLANES, LANES), F32), pltpu.VMEM((nlb, rows, LANES), F32),
                        pltpu.VMEM((nlb, rows, LANES), F32), pltpu.VMEM((rows, ff), BF16)],
        compiler_params=_params(("arbitrary",)),
        name="ffn_sample",
    )(x, prev[:, 0], prev[:, 1], gpre, gpost, wup, cw, wdown)
    return y, jnp.stack([st0, st1], axis=1)


def _lower_bound_terms(logits_ref, layer):
    lg = logits_ref[...]
    e = jnp.exp(lg - jnp.max(lg, axis=0, keepdims=True))
    pr = e / jnp.sum(e, axis=0, keepdims=True)
    cum = pr[0:1, :]
    for r in range(1, layer + 1):
        cum = cum + pr[r:r + 1, :]
    lb = jnp.clip(cum - pr[0:1, :], 0.0, 1.0)
    return jnp.log(lb) * LOG2E, jnp.log1p(-lb) * LOG2E, 1.0 - lb


def _forget_gate(f, lb_terms):
    log2_lb, log2_1m_lb, one_m_lb = lb_terms
    f2 = f * LOG2E
    e = jnp.exp2(jnp.minimum(f2, -f2))
    w = 1.0 + e
    b = log2_1m_lb + (jnp.minimum(f2, 0.0) - jnp.log2(w))
    d = log2_lb - b
    log2_ft = jnp.maximum(log2_lb, b) + jnp.log2(1.0 + jnp.exp2(jnp.minimum(d, -d)))
    return log2_ft, one_m_lb * (jnp.where(f2 >= 0.0, e, 1.0) / w)


def _split_bf16(x):
    hi = x.astype(BF16)
    return hi, (x - hi.astype(F32)).astype(BF16)


def _hgrn_gate_out(x, h, ob, mixbuf, win_ref, ng_ref, wout_ref, gpost_ref):
    dkt = C_HEADS * C_DK
    g = _dot(h, win_ref[:, 3 * dkt:4 * dkt])
    ng = ng_ref[...]
    for hh in range(C_HEADS):
        sl = slice(hh * C_DK, (hh + 1) * C_DK)
        mixbuf[:, sl] = (_rms(ob[:, sl], ng) * (g[:, sl] * _sigmoid(g[:, sl]))).astype(BF16)
    out = _dot(mixbuf[...], wout_ref[...])
    return x + _rms(out, gpost_ref[...])


def _hgrn_exact_tile(q, k_in, log2_ft, vb, st, ob, sq, sk, slg):
    ts = sq.shape[0]
    sq[...] = q
    sk[...] = k_in
    slg[...] = log2_ft
    ob[...] = jnp.zeros(ob.shape, F32)
    grp = 2 * SUBLANES
    row = lax.broadcasted_iota(jnp.int32, (grp, 1), 0)

    def body(t, carry):
        t0 = pl.multiple_of(lax.bitwise_and(t, -grp), grp)
        sel = row == lax.bitwise_and(t, grp - 1)
        for hh in range(C_HEADS):
            sl = slice(hh * C_DK, (hh + 1) * C_DK)
            rows_of = lambda ref: jnp.where(sel, ref[pl.ds(t0, grp), sl].astype(F32), 0.0)
            pick = lambda ref: rows_of(ref).astype(BF16)
            log2_f = jnp.sum(rows_of(slg), axis=0, keepdims=True)
            s_new = st[hh] * jnp.exp2(log2_f) + _dot_tn(pick(vb), pick(sk))
            st[hh] = s_new
            ob[pl.ds(t0, grp), sl] += _dot_nt(pick(sq), s_new.astype(BF16))
        return carry

    lax.fori_loop(0, ts, body, 0)


def _hgrn_prompt_kernel(layer, x_ref, gpre_ref, gpost_ref, win_ref, lbl_ref, ng_ref, wout_ref,
                        y_ref, sout_ref, st, st0, qtb, ktb, qib, klb, eglb, vb, ob, mixbuf, sq, sk, slg):
    s = pl.program_id(1)
    ts = x_ref.shape[0]
    dkt = C_HEADS * C_DK
    ck = HGRN_CHUNK

    @pl.when(s == 0)
    def _():
        st[...] = jnp.zeros(st.shape, F32)

    x = x_ref[...]
    h = _rms(x, gpre_ref[...]).astype(BF16)
    f = _dot(h, win_ref[:, dkt:2 * dkt])
    log2_ft, k_in = _forget_gate(f, _lower_bound_terms(lbl_ref, layer))
    q = _dot(h, win_ref[:, 0:dkt])
    vb[...] = _dot(h, win_ref[:, 2 * dkt:3 * dkt]).astype(BF16)

    ri = lax.broadcasted_iota(jnp.int32, (ck, ck), 0)
    ci = lax.broadcasted_iota(jnp.int32, (ck, ck), 1)
    causal = ri >= ci
    tri = jnp.where(causal, 1.0, 0.0).astype(BF16)
    mid = ck // 2 - 1
    nck = ts // ck

    spread = jnp.zeros((1, dkt), F32)
    for c in range(nck):
        rows = slice(c * ck, (c + 1) * ck)
        hi, lo = _split_bf16(log2_ft[rows, :])
        gcum = _dot(tri, hi) + _dot(tri, lo)
        gr = gcum[mid:mid + 1, :]
        gl = gcum[ck - 1:ck, :]
        spread = jnp.maximum(spread, jnp.maximum(gcum[0:1, :] - gr, gr - gl))
        qc = q[rows, :]
        kc = k_in[rows, :]
        qtb[rows, :] = (qc * jnp.exp2(gcum - gr)).astype(BF16)
        ktb[rows, :] = (kc * jnp.exp2(gr - gcum)).astype(BF16)
        qib[rows, :] = (qc * jnp.exp2(gcum)).astype(BF16)
        klb[rows, :] = (kc * jnp.exp2(gl - gcum)).astype(BF16)
        eglb[c] = jnp.exp2(gl)

    st0[...] = st[...]
    for c in range(nck):
        rows = slice(c * ck, (c + 1) * ck)
        for hh in range(C_HEADS):
            sl = slice(hh * C_DK, (hh + 1) * C_DK)
            vh = vb[rows, sl]
            att = jnp.where(causal, _dot_nt(qtb[rows, sl], ktb[rows, sl]), 0.0).astype(BF16)
            st_h = st[hh]
            ob[rows, sl] = _dot(jnp.concatenate([qib[rows, sl], att], axis=1),
                                jnp.concatenate([st_h.astype(BF16).T, vh], axis=0))
            st[hh] = eglb[c, :, sl] * st_h + _dot_tn(vh, klb[rows, sl])

    y_ref[...] = _hgrn_gate_out(x, h, ob, mixbuf, win_ref, ng_ref, wout_ref, gpost_ref)

    @pl.when(jnp.max(spread) > DECAY_LOG2_LIMIT)
    def _():
        st[...] = st0[...]
        _hgrn_exact_tile(q, k_in, log2_ft, vb, st, ob, sq, sk, slg)
        y_ref[...] = _hgrn_gate_out(x, h, ob, mixbuf, win_ref, ng_ref, wout_ref, gpost_ref)

    @pl.when(s == pl.num_programs(1) - 1)
    def _():
        for hh in range(C_HEADS):
            sout_ref[hh] = st[hh].T


def _hgrn_prompt(layer, x, gpre, gpost, win, lbl, ng, wout):
    b, seq, d = x.shape
    ts = HGRN_TILE
    assert seq % ts == 0 and ts % HGRN_CHUNK == 0, (seq, ts)
    dkt = C_HEADS * C_DK
    tile = pl.BlockSpec((None, ts, d), lambda i, j: (i, j, 0))
    return pl.pallas_call(
        functools.partial(_hgrn_prompt_kernel, layer),
        grid=(b, seq // ts),
        in_specs=[tile, _resident(gpre.shape), _resident(gpost.shape), _resident(win.shape),
                  _resident(lbl.shape), _resident(ng.shape), _resident(wout.shape)],
        out_specs=[tile, pl.BlockSpec((None, C_HEADS, C_DK, C_DK), lambda i, j: (i, 0, 0, 0))],
        out_shape=[jax.ShapeDtypeStruct((b, seq, d), F32),
                   jax.ShapeDtypeStruct((b, C_HEADS, C_DK, C_DK), F32)],
        scratch_shapes=[pltpu.VMEM((C_HEADS, C_DK, C_DK), F32), pltpu.VMEM((C_HEADS, C_DK, C_DK), F32),
                        pltpu.VMEM((ts, dkt), BF16), pltpu.VMEM((ts, dkt), BF16),
                        pltpu.VMEM((ts, dkt), BF16), pltpu.VMEM((ts, dkt), BF16),
                        pltpu.VMEM((ts // HGRN_CHUNK, 1, dkt), F32),
                        pltpu.VMEM((ts, dkt), BF16), pltpu.VMEM((ts, dkt), F32),
                        pltpu.VMEM((ts, dkt), BF16),
                        pltpu.VMEM((ts, dkt), F32), pltpu.VMEM((ts, dkt), F32), pltpu.VMEM((ts, dkt), F32)],
        compiler_params=_params(("arbitrary", "arbitrary")),
        name="hgrn_prompt",
    )(x, gpre, gpost, win, lbl, ng, wout)


def _mix_ab_sample_kernel(seg, x_ref, prev0_ref, prev1_ref, kc_ref, vc_ref, gpre_ref, gpost_ref, win_ref,
                          cw_ref, sink_ref, wout_ref,
                          y_ref, conv0_ref, conv1_ref, kout_ref, vout_ref,
                          ubuf, fix1, fix2, qbuf, knew_s, vnew_s, yabuf, ybbuf, kall, vall, bias_s):
    i = pl.program_id(0)
    rows = x_ref.shape[0]
    aw = cw_ref.shape[1]
    nq = GROUP * KV_HEADS * HEAD_DIM
    nkv = KV_HEADS * HEAD_DIM
    pair = 2 * seg
    qrows = KV_HEADS * GROUP * pair
    tk = WINDOW + pair

    @pl.when(i == 0)
    def _():
        x = x_ref[...]
        h = _rms(x, gpre_ref[...]).astype(BF16)
        t = lax.broadcasted_iota(jnp.int32, (rows, 1), 0) % seg
        u = _dot(h, win_ref[:, 2 * aw:3 * aw]) * _dot(h, win_ref[:, 0:aw])
        ubuf[:, 0:SUBLANES, :] = jnp.zeros((ubuf.shape[0], SUBLANES, LANES), F32)
        fix1[...] = jnp.zeros(fix1.shape, F32)
        fix2[...] = jnp.zeros(fix2.shape, F32)
        um1, um2 = _segment_conv_taps(u, t, seg, prev0_ref, prev1_ref, ubuf, fix1, fix2)
        conv0_ref[...], conv1_ref[...] = _segment_tails(ubuf, rows // seg, seg)
        cw = cw_ref[...]
        conv = cw[2:3, :] * u + cw[1:2, :] * um1 + cw[0:1, :] * um2
        yabuf[...] = (_dot(h, win_ref[:, aw:2 * aw]) * conv).astype(BF16)
        q0 = 3 * aw
        qbuf[...] = _dot(h, win_ref[:, q0:q0 + nq]) * (HEAD_DIM ** -0.5 * LOG2E)
        kv_new = _dot(h, win_ref[:, q0 + nq:q0 + nq + 2 * nkv])
        knew_s[...] = kv_new[:, 0:nkv]
        vnew_s[...] = kv_new[:, nkv:2 * nkv]

        r = lax.broadcasted_iota(jnp.int32, (qrows, tk), 0)
        j = lax.broadcasted_iota(jnp.int32, (qrows, tk), 1)
        dist = r % seg + WINDOW - j
        valid = (dist >= 0) & (dist < WINDOW) & (j < WINDOW + seg)
        rc = lax.broadcasted_iota(jnp.int32, (qrows, 1), 0)
        head = (rc // (GROUP * pair)) * GROUP + (rc // pair) % GROUP
        slope = jnp.zeros((qrows, 1), F32)
        for hidx in range(KV_HEADS * GROUP):
            slope = jnp.where(head == hidx, _head_slope(hidx) * LOG2E, slope)
        bias_s[...] = jnp.where(valid, -slope * dist.astype(F32), NEG)
        vall[:, :, nkv:2 * nkv] = jnp.ones((2, tk, nkv), F32)

    nblk = kc_ref.shape[0]
    lane = lax.broadcasted_iota(jnp.int32, (1, nkv), 1)
    lo = lane < HEAD_DIM
    rq = lax.broadcasted_iota(jnp.int32, (qrows, 1), 0)
    first_seq = (rq // seg) % 2 == 0

    def body(p, carry):
        r0 = pl.multiple_of((i * nblk + 2 * p) * seg, pair)
        q8 = qbuf[pl.ds(r0, pair), :]
        qg = jnp.concatenate([q8[:, g * nkv:(g + 1) * nkv] for g in range(GROUP)], axis=0)
        qs = jnp.concatenate([jnp.where(lo, qg, 0.0), jnp.where(lo, 0.0, qg)], axis=0).astype(BF16)
        knew = knew_s[pl.ds(r0, pair), :]
        vnew = vnew_s[pl.ds(r0, pair), :]
        sc, ov = [], []
        for a in range(2):
            own_first = (lambda z: z) if a == 0 else (lambda z: jnp.concatenate([z[seg:], z[:seg]], axis=0))
            kall[a, 0:WINDOW, :] = kc_ref[2 * p + a]
            kall[a, WINDOW:tk, :] = own_first(knew)
            vall[a, 0:WINDOW, 0:nkv] = vc_ref[2 * p + a]
            vall[a, WINDOW:tk, 0:nkv] = own_first(vnew)
            kout_ref[2 * p + a] = kall[a, seg:seg + WINDOW, :]
            vout_ref[2 * p + a] = vall[a, seg:seg + WINDOW, 0:nkv]
            sc.append(_dot_nt(qs, kall[a].astype(BF16)))
        s2 = jnp.where(first_seq, sc[0], sc[1]) + bias_s[...]
        ps, sink_terms = [], []
        for hidx in range(KV_HEADS * GROUP):
            sh = s2[hidx * pair:(hidx + 1) * pair, :]
            sink = sink_ref[hidx] * LOG2E
            m = jnp.maximum(jnp.max(sh, axis=-1, keepdims=True), sink)
            ps.append(jnp.exp2(sh - m))
            sink_terms.append(jnp.broadcast_to(jnp.exp2(sink - m), (pair, nkv)))
        prb = jnp.concatenate(ps, axis=0).astype(BF16)
        for a in range(2):
            ov.append(_dot(prb, vall[a].astype(BF16)))
        pv = jnp.where(first_seq, ov[0], ov[1])
        o = pv[:, 0:nkv] / (pv[:, nkv:2 * nkv] + jnp.concatenate(sink_terms, axis=0))
        og = jnp.where(lo, o[0:GROUP * pair, :], o[GROUP * pair:, :])
        for g in range(GROUP):
            ybbuf[pl.ds(r0, pair), g * nkv:(g + 1) * nkv] = og[g * pair:(g + 1) * pair, :]
        return carry

    lax.fori_loop(0, nblk // 2, body, 0, unroll=True)

    @pl.when(i == pl.num_programs(0) - 1)
    def _():
        out = _dot(yabuf[...], wout_ref[0:aw, :]) + _dot(ybbuf[...].astype(BF16), wout_ref[aw:, :])
        y_ref[...] = x_ref[...] + _rms(out, gpost_ref[...])


def _mix_ab_sample(x, conv_prev, kcache, vcache, seg, gpre, gpost, win, cw, sinks, wout):
    rows, d = x.shape
    nseq = rows // seg
    aw = cw.shape[1]
    nq = GROUP * KV_HEADS * HEAD_DIM
    nkv = KV_HEADS * HEAD_DIM
    sb = SAMPLE_SEQ_BLOCK
    pair = 2 * seg
    qrows = KV_HEADS * GROUP * pair
    tk = WINDOW + pair
    full = lambda shape: pl.BlockSpec(shape, lambda i: (0,) * len(shape))
    cache = pl.BlockSpec((sb, WINDOW, nkv), lambda i: (i, 0, 0))
    prev0, prev1 = conv_prev[:, 0], conv_prev[:, 1]
    y, conv0, conv1, k_new, v_new = pl.pallas_call(
        functools.partial(_mix_ab_sample_kernel, seg),
        grid=(nseq // sb,),
        in_specs=[_resident(x.shape), _resident(prev0.shape), _resident(prev1.shape), cache, cache,
                  _resident(gpre.shape), _resident(gpost.shape), _resident(win.shape), _resident(cw.shape),
                  pl.BlockSpec(memory_space=pltpu.SMEM), _resident(wout.shape)],
        out_specs=[full((rows, d)), full((nseq, aw)), full((nseq, aw)), cache, cache],
        out_shape=[jax.ShapeDtypeStruct((rows, d), F32), jax.ShapeDtypeStruct((nseq, aw), F32),
                   jax.ShapeDtypeStruct((nseq, aw), F32), jax.ShapeDtypeStruct(kcache.shape, F32),
                   jax.ShapeDtypeStruct(vcache.shape, F32)],
        scratch_shapes=[pltpu.VMEM((aw // LANES, rows + SUBLANES, LANES), F32),
                        pltpu.VMEM((aw // LANES, rows, LANES), F32),
                        pltpu.VMEM((aw // LANES, rows, LANES), F32),
                        pltpu.VMEM((rows, nq), F32),
                        pltpu.VMEM((rows, nkv), F32),
                        pltpu.VMEM((rows, nkv), F32),
                        pltpu.VMEM((rows, aw), BF16),
                        pltpu.VMEM((rows, nq), F32),
                        pltpu.VMEM((2, tk, nkv), F32),
                        pltpu.VMEM((2, tk, 2 * nkv), F32),
                        pltpu.VMEM((qrows, tk), F32)],
        compiler_params=_params(("arbitrary",)),
        name="mix_ab_sample",
    )(x, prev0, prev1, kcache, vcache, gpre, gpost, win, cw, sinks.reshape(-1), wout)
    return y, jnp.stack([conv0, conv1], axis=1), k_new, v_new


def _hgrn_sample_kernel(layer, seg, x_ref, s0_ref, gpre_ref, gpost_ref, win_ref, lbl_ref, ng_ref, wout_ref,
                        y_ref, s1_ref, qb, kb, vb, gb, ob, mixbuf):
    i = pl.program_id(0)
    rows = x_ref.shape[0]
    dkt = C_HEADS * C_DK
    pair = 2 * seg

    @pl.when(i == 0)
    def _():
        x = x_ref[...]
        h = _rms(x, gpre_ref[...]).astype(BF16)
        f = _dot(h, win_ref[:, dkt:2 * dkt])
        log2_ft, kb[...] = _forget_gate(f, _lower_bound_terms(lbl_ref, layer))
        hi, lo = _split_bf16(log2_ft)
        ri = lax.broadcasted_iota(jnp.int32, (rows, rows), 0)
        ci = lax.broadcasted_iota(jnp.int32, (rows, rows), 1)
        tri = jnp.where((ri // seg == ci // seg) & (ri >= ci), 1.0, 0.0).astype(BF16)
        gb[...] = _dot(tri, hi) + _dot(tri, lo)
        qb[...] = _dot(h, win_ref[:, 0:dkt])
        vb[...] = _dot(h, win_ref[:, 2 * dkt:3 * dkt])

    nblk = s0_ref.shape[0]
    rr = lax.broadcasted_iota(jnp.int32, (pair, 1), 0)
    first_seq = rr < seg
    second_seq = rr >= seg
    ri = lax.broadcasted_iota(jnp.int32, (pair, pair), 0)
    ci = lax.broadcasted_iota(jnp.int32, (pair, pair), 1)
    key_s = [(ri // seg == ci // seg) & (ci % seg == sk) & (ri % seg >= sk) for sk in range(seg)]

    def body(p, carry):
        r0 = pl.multiple_of((i * nblk + 2 * p) * seg, pair)
        for hh in range(C_HEADS):
            sl = slice(hh * C_DK, (hh + 1) * C_DK)
            gh = gb[pl.ds(r0, pair), sl]
            g_row = lambda t: jnp.where(first_seq, gh[t:t + 1, :], gh[seg + t:seg + t + 1, :])
            gl = g_row(seg - 1)
            qh = qb[pl.ds(r0, pair), sl]
            kh = kb[pl.ds(r0, pair), sl]
            vh = vb[pl.ds(r0, pair), sl].astype(BF16)
            q_rel = [(qh * jnp.exp2(gh - g_row(sk))).astype(BF16) for sk in range(seg - 1)] + [qh.astype(BF16)]
            rel = _dot_nt(jnp.concatenate(q_rel, axis=0), kh.astype(BF16))
            att = jnp.where(key_s[0], rel[0:pair, :], 0.0)
            for sk in range(1, seg):
                att = att + jnp.where(key_s[sk], rel[sk * pair:(sk + 1) * pair, :], 0.0)
            att = att.astype(BF16)
            qi = (qh * jnp.exp2(gh)).astype(BF16)
            kl = kh * jnp.exp2(gl - gh)
            o_prev = []
            for a in range(2):
                s_a = s0_ref[2 * p + a, hh]
                o_prev.append(_dot(qi, s_a.astype(BF16)))
                kl_a = jnp.where(first_seq if a == 0 else second_seq, kl, 0.0).astype(BF16)
                e = jnp.exp2(gh[(a + 1) * seg - 1:(a + 1) * seg, :])
                e1 = e.astype(BF16).astype(F32)
                e2 = (e - e1).astype(BF16).astype(F32)
                e3 = (e - e1 - e2).astype(BF16).astype(F32)
                dec_l = jnp.where(rr == 0, e1, jnp.where(rr == 1, e2, jnp.where(rr == 2, e3, 0.0)))
                dec = _dot_tn(dec_l.astype(BF16), jnp.ones((pair, C_DK), BF16))
                s1_ref[2 * p + a, hh] = dec * s_a + _dot_tn(kl_a, vh)
            ob[pl.ds(r0, pair), sl] = _dot(att, vh) + jnp.where(first_seq, o_prev[0], o_prev[1])
        return carry

    lax.fori_loop(0, nblk // 2, body, 0, unroll=True)

    @pl.when(i == pl.num_programs(0) - 1)
    def _():
        x = x_ref[...]
        h = _rms(x, gpre_ref[...]).astype(BF16)
        y_ref[...] = _hgrn_gate_out(x, h, ob, mixbuf, win_ref, ng_ref, wout_ref, gpost_ref)


def _hgrn_sample(layer, x, s0, seg, gpre, gpost, win, lbl, ng, wout):
    rows, d = x.shape
    assert 2 * seg == SUBLANES, "two sequences per iteration must fill one f32 vreg of sublanes"
    nseq = rows // seg
    dkt = C_HEADS * C_DK
    sb = SAMPLE_SEQ_BLOCK
    full = lambda shape: pl.BlockSpec(shape, lambda i: (0,) * len(shape))
    state = pl.BlockSpec((sb, C_HEADS, C_DK, C_DK), lambda i: (i, 0, 0, 0))
    return pl.pallas_call(
        functools.partial(_hgrn_sample_kernel, layer, seg),
        grid=(nseq // sb,),
        in_specs=[_resident(x.shape), state, _resident(gpre.shape), _resident(gpost.shape), _resident(win.shape),
                  _resident(lbl.shape), _resident(ng.shape), _resident(wout.shape)],
        out_specs=[full((rows, d)), state],
        out_shape=[jax.ShapeDtypeStruct((rows, d), F32), jax.ShapeDtypeStruct(s0.shape, F32)],
        scratch_shapes=[pltpu.VMEM((rows, dkt), F32), pltpu.VMEM((rows, dkt), F32),
                        pltpu.VMEM((rows, dkt), F32), pltpu.VMEM((rows, dkt), F32),
                        pltpu.VMEM((rows, dkt), F32), pltpu.VMEM((rows, dkt), BF16)],
        compiler_params=_params(("arbitrary",)),
        name="hgrn_sample",
    )(x, s0, gpre, gpost, win, lbl, ng, wout)


def kernel(x_prompt, x_sample, state_conv_a, cache_swa_k, cache_swa_v, state_hgrn, state_ffn_conv, norm_mix_pre, norm_mix_post, norm_ffn_pre, norm_ffn_post, w_in_ab, conv_a_w, attn_sinks, w_out_ab, w_in_c, hgrn_lb_logits, hgrn_norm_g, w_out_c, w_ffn_up, ffn_conv_w, w_ffn_down):
    bp, _, d = x_prompt.shape
    ns, seg, _ = x_sample.shape
    aw = conv_a_w.shape[-1]
    nq = GROUP * KV_HEADS * HEAD_DIM
    nkv = KV_HEADS * HEAD_DIM

    q0 = 3 * aw
    win_ab = w_in_ab[0].astype(BF16)
    wq = win_ab[:, q0:q0 + nq].reshape(d, KV_HEADS, GROUP, HEAD_DIM).swapaxes(1, 2).reshape(d, nq)
    win_ab = jnp.concatenate([win_ab[:, :q0], wq, win_ab[:, q0 + nq:]], axis=1)
    wout_ab = w_out_ab[0].astype(BF16)
    wo = wout_ab[aw:].reshape(KV_HEADS, GROUP, HEAD_DIM, d).swapaxes(0, 1).reshape(nq, d)
    wout_ab = jnp.concatenate([wout_ab[:aw], wo], axis=0)
    win_c = w_in_c[0].astype(BF16)
    wout_c = w_out_c[0].astype(BF16)
    wup = w_ffn_up.astype(BF16)
    wdown = w_ffn_down.astype(BF16)
    row = lambda a: a.reshape(1, -1)
    sinks = row(attn_sinks[0])
    ng = row(hgrn_norm_g[0])

    xs = x_sample.reshape(ns * seg, d)
    kc = cache_swa_k[0].reshape(ns, WINDOW, nkv)
    vc = cache_swa_v[0].reshape(ns, WINDOW, nkv)
    s1, conv_s, k_s, v_s = _mix_ab_sample(xs, state_conv_a[0], kc, vc, seg, row(norm_mix_pre[0]),
                                          row(norm_mix_post[0]), win_ab, conv_a_w[0], sinks, wout_ab)
    s2, ffn_s0 = _ffn_sample(0, s1, state_ffn_conv[0], seg, row(norm_ffn_pre[0]), row(norm_ffn_post[0]),
                             wup, ffn_conv_w[0], wdown)
    s3, hgrn_s = _hgrn_sample(1, s2, state_hgrn[0], seg, row(norm_mix_pre[1]), row(norm_mix_post[1]),
                              win_c, hgrn_lb_logits, ng, wout_c)
    y_sample, ffn_s1 = _ffn_sample(1, s3, state_ffn_conv[1], seg, row(norm_ffn_pre[1]), row(norm_ffn_post[1]),
                                   wup, ffn_conv_w[1], wdown)

    x1, conv_p, k_p, v_p = _mix_ab_prompt(x_prompt, row(norm_mix_pre[0]), row(norm_mix_post[0]), win_ab,
                                          conv_a_w[0], sinks, wout_ab)
    x2, ffn_p0 = _ffn_prompt(0, x1, row(norm_ffn_pre[0]), row(norm_ffn_post[0]), wup, ffn_conv_w[0], wdown)
    x3, hgrn_p = _hgrn_prompt(1, x2, row(norm_mix_pre[1]), row(norm_mix_post[1]), win_c, hgrn_lb_logits,
                              ng, wout_c)
    y_prompt, ffn_p1 = _ffn_prompt(1, x3, row(norm_ffn_pre[1]), row(norm_ffn_post[1]), wup, ffn_conv_w[1], wdown)

    kvshape = (KV_HEADS, HEAD_DIM)
    return (y_prompt, y_sample.reshape(ns, seg, d),
            conv_p[None], conv_s[None],
            k_p.reshape(1, bp, WINDOW, *kvshape), k_s.reshape(1, ns, WINDOW, *kvshape),
            v_p.reshape(1, bp, WINDOW, *kvshape), v_s.reshape(1, ns, WINDOW, *kvshape),
            hgrn_p[None], hgrn_s[None],
            jnp.stack([ffn_p0, ffn_p1]), jnp.stack([ffn_s0, ffn_s1]))
```

```python
import functools
import math

import jax
import jax.numpy as jnp
from jax import lax
from jax.experimental import pallas as pl
from jax.experimental.pallas import tpu as pltpu

F32 = jnp.float32
BF16 = jnp.bfloat16

EPS = 1e-6
LOG2E = 1.4426950408889634
WINDOW = 128
KV_HEADS = 2
GROUP = 4
HEAD_DIM = 64
C_HEADS = 8
C_DK = 128
NEG = -1e30
SUBLANES = 8
LANES = 128
MIX_TILE = 1024
HGRN_TILE = 512
FFN_TILE = 1024
HGRN_CHUNK = 64
DECAY_LOG2_LIMIT = 100.0
FFN_CHUNK = 256
SAMPLE_SEQ_BLOCK = 8
VMEM_LIMIT_V7X = 56 * 2**20


def _dot(a, b):
    return jnp.dot(a, b, preferred_element_type=F32)


def _dot_nt(a, b):
    return lax.dot_general(a, b, (((1,), (1,)), ((), ())), preferred_element_type=F32)


def _dot_tn(a, b):
    return lax.dot_general(a, b, (((0,), (0,)), ((), ())), preferred_element_type=F32)


def _rms(x, g):
    return x * lax.rsqrt(jnp.mean(x * x, axis=-1, keepdims=True) + EPS) * g


def _gelu_tanh(x):
    c = math.sqrt(2.0 / math.pi)
    return x * (0.5 * (1.0 + jnp.tanh(c * (x + 0.044715 * (x * x * x)))))


def _sigmoid(x):
    return 1.0 / (1.0 + jnp.exp(-x))


def _resident(shape):
    nd = len(shape)
    return pl.BlockSpec(shape, lambda *_: (0,) * nd, pipeline_mode=pl.Buffered(1))


def _params(sem):
    return pltpu.CompilerParams(dimension_semantics=sem, vmem_limit_bytes=VMEM_LIMIT_V7X)


def _head_slope(h):
    return 2.0 ** (-8.0 * (h + 1) / (KV_HEADS * GROUP))


def _mix_ab_prompt_kernel(x_ref, gpre_ref, gpost_ref, win_ref, cw_ref, sink_ref, wout_ref,
                          y_ref, conv_ref, kout_ref, vout_ref,
                          ubuf, qbuf, kbuf, vbuf, mixbuf, bias_s):
    s = pl.program_id(1)
    ts = x_ref.shape[0]
    aw = cw_ref.shape[1]
    nq = GROUP * KV_HEADS * HEAD_DIM
    nkv = KV_HEADS * HEAD_DIM

    @pl.when(s == 0)
    def _():
        ubuf[0:SUBLANES, :] = jnp.zeros((SUBLANES, aw), F32)
        kbuf[0:WINDOW, :] = jnp.zeros((WINDOW, nkv), BF16)
        vbuf[0:WINDOW, 0:nkv] = jnp.zeros((WINDOW, nkv), BF16)
        vbuf[:, nkv:2 * nkv] = jnp.ones((ts + WINDOW, nkv), BF16)

    x = x_ref[...]
    h = _rms(x, gpre_ref[...]).astype(BF16)

    u = _dot(h, win_ref[:, 2 * aw:3 * aw]) * _dot(h, win_ref[:, 0:aw])
    ubuf[SUBLANES:SUBLANES + ts, :] = u
    cw = cw_ref[...]
    conv = (cw[2:3, :] * u + cw[1:2, :] * ubuf[SUBLANES - 1:SUBLANES - 1 + ts, :]
            + cw[0:1, :] * ubuf[SUBLANES - 2:SUBLANES - 2 + ts, :])
    mixbuf[:, 0:aw] = (_dot(h, win_ref[:, aw:2 * aw]) * conv).astype(BF16)
    tail = ubuf[ts + SUBLANES - 2:ts + SUBLANES, :]
    conv_ref[...] = tail
    ubuf[SUBLANES - 2:SUBLANES, :] = tail

    q0 = 3 * aw
    q = _dot(h, win_ref[:, q0:q0 + nq]) * (HEAD_DIM ** -0.5 * LOG2E)
    lane = lax.broadcasted_iota(jnp.int32, (1, nkv), 1)
    lo = lane < HEAD_DIM
    nblk = ts // WINDOW
    for n in range(nblk):
        for g in range(GROUP):
            qg = q[n * WINDOW:(n + 1) * WINDOW, g * nkv:(g + 1) * nkv]
            qbuf[n, g * WINDOW:(g + 1) * WINDOW, :] = jnp.where(lo, qg, 0.0).astype(BF16)
            qbuf[n, (GROUP + g) * WINDOW:(GROUP + g + 1) * WINDOW, :] = jnp.where(lo, 0.0, qg).astype(BF16)
    wkv = win_ref[:, q0 + nq:q0 + nq + 2 * nkv]
    kv_new = jnp.concatenate([_dot(h[0:ts // 2, :], wkv), _dot(h[ts // 2:, :], wkv)], axis=0)
    k = kv_new[:, 0:nkv]
    v = kv_new[:, nkv:2 * nkv]
    kbuf[WINDOW:WINDOW + ts, :] = k.astype(BF16)
    vbuf[WINDOW:WINDOW + ts, 0:nkv] = v.astype(BF16)
    kout_ref[...] = k[ts - WINDOW:, :]
    vout_ref[...] = v[ts - WINDOW:, :]

    qi = lax.broadcasted_iota(jnp.int32, (WINDOW, 2 * WINDOW), 0)
    kj = lax.broadcasted_iota(jnp.int32, (WINDOW, 2 * WINDOW), 1)
    dist = qi + WINDOW - kj
    valid = (dist >= 0) & (dist < WINDOW)
    negdist = jnp.where(valid, -dist.astype(F32), 0.0)
    maskadd = jnp.where(valid, 0.0, NEG)
    nheads = KV_HEADS * GROUP
    for hidx in range(nheads):
        bias_s[hidx] = (_head_slope(hidx) * LOG2E) * negdist + maskadd
    first = jnp.where(kj < WINDOW, jnp.where(s == 0, NEG, 0.0), 0.0)

    for n in range(nblk):
        r0 = n * WINDOW
        half = GROUP * WINDOW
        keys = kbuf[r0:r0 + 2 * WINDOW, :]
        sc_kv = [_dot_nt(qbuf[n, kv * half:(kv + 1) * half, :], keys) for kv in range(KV_HEADS)]
        ps, sink_terms = [], []
        for hidx in range(nheads):
            g = hidx % GROUP
            sc = sc_kv[hidx // GROUP][g * WINDOW:(g + 1) * WINDOW, :] + bias_s[hidx]
            if n == 0:
                sc = sc + first
            sink = sink_ref[hidx] * LOG2E
            m = jnp.maximum(jnp.max(sc, axis=-1, keepdims=True), sink)
            ps.append(jnp.exp2(sc - m).astype(BF16))
            sink_terms.append(jnp.exp2(sink - m))
        vals = vbuf[r0:r0 + 2 * WINDOW, :]
        pv_kv = [_dot(jnp.concatenate(ps[kv * GROUP:(kv + 1) * GROUP], axis=0), vals) for kv in range(KV_HEADS)]
        outs = []
        for hidx in range(nheads):
            g = hidx % GROUP
            pv_h = pv_kv[hidx // GROUP][g * WINDOW:(g + 1) * WINDOW, :]
            outs.append(pv_h[:, 0:nkv] / (pv_h[:, nkv:2 * nkv] + sink_terms[hidx]))
        for g in range(GROUP):
            og = jnp.where(lo, outs[g], outs[GROUP + g])
            mixbuf[r0:r0 + WINDOW, aw + g * nkv:aw + (g + 1) * nkv] = og.astype(BF16)

    kbuf[0:WINDOW, :] = kbuf[ts:ts + WINDOW, :]
    vbuf[0:WINDOW, :] = vbuf[ts:ts + WINDOW, :]

    out = _dot(mixbuf[...], wout_ref[...])
    y_ref[...] = x + _rms(out, gpost_ref[...])


def _mix_ab_prompt(x, gpre, gpost, win, cw, sinks, wout):
    b, seq, d = x.shape
    ts = MIX_TILE
    assert seq % ts == 0 and ts % WINDOW == 0, (seq, ts)
    aw = cw.shape[1]
    nkv = KV_HEADS * HEAD_DIM
    tile = pl.BlockSpec((None, ts, d), lambda i, j: (i, j, 0))
    return pl.pallas_call(
        _mix_ab_prompt_kernel,
        grid=(b, seq // ts),
        in_specs=[tile, _resident(gpre.shape), _resident(gpost.shape), _resident(win.shape),
                  _resident(cw.shape), pl.BlockSpec(memory_space=pltpu.SMEM), _resident(wout.shape)],
        out_specs=[tile,
                   pl.BlockSpec((None, 2, aw), lambda i, j: (i, 0, 0)),
                   pl.BlockSpec((None, WINDOW, nkv), lambda i, j: (i, 0, 0)),
                   pl.BlockSpec((None, WINDOW, nkv), lambda i, j: (i, 0, 0))],
        out_shape=[jax.ShapeDtypeStruct((b, seq, d), F32),
                   jax.ShapeDtypeStruct((b, 2, aw), F32),
                   jax.ShapeDtypeStruct((b, WINDOW, nkv), F32),
                   jax.ShapeDtypeStruct((b, WINDOW, nkv), F32)],
        scratch_shapes=[pltpu.VMEM((ts + SUBLANES, aw), F32),
                        pltpu.VMEM((ts // WINDOW, KV_HEADS * GROUP * WINDOW, nkv), BF16),
                        pltpu.VMEM((ts + WINDOW, nkv), BF16),
                        pltpu.VMEM((ts + WINDOW, 2 * nkv), BF16),
                        pltpu.VMEM((ts, d), BF16),
                        pltpu.VMEM((KV_HEADS * GROUP, WINDOW, 2 * WINDOW), F32)],
        compiler_params=_params(("arbitrary", "arbitrary")),
        name="mix_ab_prompt",
    )(x, gpre, gpost, win, cw, sinks.reshape(-1), wout)


def _ffn_prompt_kernel(x_ref, gpre_ref, gpost_ref, wup_ref, cw_ref, wdown_ref,
                       y_ref, st_ref, abuf, carry, gbuf):
    s = pl.program_id(1)
    rows = x_ref.shape[0]
    ff = cw_ref.shape[1]

    @pl.when(s == 0)
    def _():
        carry[...] = jnp.zeros(carry.shape, F32)

    x = x_ref[...]
    h = _rms(x, gpre_ref[...]).astype(BF16)
    for c in range(ff // FFN_CHUNK):
        c0 = c * FFN_CHUNK
        buf = abuf.at[c % 2]
        a = _dot(h, wup_ref[:, c0:c0 + FFN_CHUNK])
        b = _dot(h, wup_ref[:, ff + c0:ff + c0 + FFN_CHUNK])
        buf[0:SUBLANES, :] = carry[:, c0:c0 + FFN_CHUNK]
        buf[SUBLANES:SUBLANES + rows, :] = a
        cw = cw_ref[:, c0:c0 + FFN_CHUNK]
        conv = (cw[2:3, :] * a + cw[1:2, :] * buf[SUBLANES - 1:SUBLANES - 1 + rows, :]
                + cw[0:1, :] * buf[SUBLANES - 2:SUBLANES - 2 + rows, :])
        tail = a[rows - 2:rows, :]
        st_ref[:, c0:c0 + FFN_CHUNK] = tail
        carry[SUBLANES - 2:SUBLANES, c0:c0 + FFN_CHUNK] = tail
        gbuf[:, c0:c0 + FFN_CHUNK] = (_gelu_tanh(conv) * b).astype(BF16)
    for r0 in range(0, rows, rows // 2):
        rs = slice(r0, r0 + rows // 2)
        out = _dot(gbuf[rs, :], wdown_ref[...])
        y_ref[rs, :] = x[rs, :] + _rms(out, gpost_ref[...])


def _ffn_prompt(layer, x, gpre, gpost, wup, cw, wdown):
    b, seq, d = x.shape
    ts = FFN_TILE
    ff = cw.shape[1]
    assert seq % ts == 0 and ff % FFN_CHUNK == 0, (seq, ts, ff)
    tile = pl.BlockSpec((None, ts, d), lambda i, j: (i, j, 0))
    layer_block = lambda w: pl.BlockSpec((None,) + w.shape[1:], lambda i, j: (layer, 0, 0),
                                         pipeline_mode=pl.Buffered(1))
    return pl.pallas_call(
        _ffn_prompt_kernel,
        grid=(b, seq // ts),
        in_specs=[tile, _resident(gpre.shape), _resident(gpost.shape), layer_block(wup),
                  _resident(cw.shape), layer_block(wdown)],
        out_specs=[tile, pl.BlockSpec((None, 2, ff), lambda i, j: (i, 0, 0))],
        out_shape=[jax.ShapeDtypeStruct((b, seq, d), F32), jax.ShapeDtypeStruct((b, 2, ff), F32)],
        scratch_shapes=[pltpu.VMEM((2, ts + SUBLANES, FFN_CHUNK), F32), pltpu.VMEM((SUBLANES, ff), F32),
                        pltpu.VMEM((ts, ff), BF16)],
        compiler_params=_params(("arbitrary", "arbitrary")),
        name="ffn_prompt",
    )(x, gpre, gpost, wup, cw, wdown)


def _segment_conv_taps(a, t, seg, prev0_ref, prev1_ref, abuf, fix1, fix2):
    rows, c = a.shape
    nseq = rows // seg
    am1, am2 = [], []
    for k in range(c // LANES):
        sl = slice(k * LANES, (k + 1) * LANES)
        abuf[k, SUBLANES:SUBLANES + rows, :] = a[:, sl]
        fix1[k, pl.ds(0, nseq, stride=seg), :] = prev1_ref[:, sl]
        fix2[k, pl.ds(0, nseq, stride=seg), :] = prev0_ref[:, sl]
        fix2[k, pl.ds(1, nseq, stride=seg), :] = prev1_ref[:, sl]
        am1.append(jnp.where(t >= 1, abuf[k, SUBLANES - 1:SUBLANES - 1 + rows, :], fix1[k]))
        am2.append(jnp.where(t >= 2, abuf[k, SUBLANES - 2:SUBLANES - 2 + rows, :], fix2[k]))
    return jnp.concatenate(am1, axis=1), jnp.concatenate(am2, axis=1)


def _segment_tails(abuf, nseq, seg):
    tail = lambda back: jnp.concatenate(
        [abuf[k, pl.ds(SUBLANES + seg - back, nseq, stride=seg), :] for k in range(abuf.shape[0])], axis=1)
    return tail(2), tail(1)


def _ffn_sample_kernel(seg, x_ref, prev0_ref, prev1_ref, gpre_ref, gpost_ref, wup_ref, cw_ref, wdown_ref,
                       y_ref, st0_ref, st1_ref, abuf, fix1, fix2, gbuf):
    rows = x_ref.shape[0]
    ff = cw_ref.shape[1]
    abuf[:, 0:SUBLANES, :] = jnp.zeros((abuf.shape[0], SUBLANES, LANES), F32)
    fix1[...] = jnp.zeros(fix1.shape, F32)
    fix2[...] = jnp.zeros(fix2.shape, F32)
    x = x_ref[...]
    h = _rms(x, gpre_ref[...]).astype(BF16)
    t = lax.broadcasted_iota(jnp.int32, (rows, 1), 0) % seg
    for c in range(ff // FFN_CHUNK):
        cs = slice(c * FFN_CHUNK, (c + 1) * FFN_CHUNK)
        a = _dot(h, wup_ref[:, cs])
        b = _dot(h, wup_ref[:, ff + c * FFN_CHUNK:ff + (c + 1) * FFN_CHUNK])
        am1, am2 = _segment_conv_taps(a, t, seg, prev0_ref.at[:, cs], prev1_ref.at[:, cs], abuf, fix1, fix2)
        st0_ref[:, cs], st1_ref[:, cs] = _segment_tails(abuf, rows // seg, seg)
        cw = cw_ref[:, cs]
        conv = cw[2:3, :] * a + cw[1:2, :] * am1 + cw[0:1, :] * am2
        gbuf[:, cs] = (_gelu_tanh(conv) * b).astype(BF16)
    out = _dot(gbuf[...], wdown_ref[...])
    y_ref[...] = x + _rms(out, gpost_ref[...])


def _ffn_sample(layer, x, prev, seg, gpre, gpost, wup, cw, wdown):
    rows, d = x.shape
    nseq = rows // seg
    ff = cw.shape[1]
    assert ff % FFN_CHUNK == 0 and FFN_CHUNK % LANES == 0, ff
    full = lambda shape: pl.BlockSpec(shape, lambda i: (0,) * len(shape))
    layer_block = lambda w: pl.BlockSpec((None,) + w.shape[1:], lambda i: (layer, 0, 0),
                                         pipeline_mode=pl.Buffered(1))
    nlb = FFN_CHUNK // LANES
    y, st0, st1 = pl.pallas_call(
        functools.partial(_ffn_sample_kernel, seg),
        grid=(1,),
        in_specs=[_resident(x.shape), _resident((nseq, ff)), _resident((nseq, ff)), _resident(gpre.shape),
                  _resident(gpost.shape), layer_block(wup), _resident(cw.shape), layer_block(wdown)],
        out_specs=[full((rows, d)), full((nseq, ff)), full((nseq, ff))],
        out_shape=[jax.ShapeDtypeStruct((rows, d), F32), jax.ShapeDtypeStruct((nseq, ff), F32),
                   jax.ShapeDtypeStruct((nseq, ff), F32)],
        scratch_shapes=[pltpu.VMEM((nlb, rows + SUBLANES, LANES), F32), pltpu.VMEM((nlb, rows, LANES), F32),
                        pltpu.VMEM((nlb, rows, LANES), F32), pltpu.VMEM((rows, ff), BF16)],
        compiler_params=_params(("arbitrary",)),
        name="ffn_sample",
    )(x, prev[:, 0], prev[:, 1], gpre, gpost, wup, cw, wdown)
    return y, jnp.stack([st0, st1], axis=1)


def _lower_bound_terms(logits_ref, layer):
    lg = logits_ref[...]
    e = jnp.exp(lg - jnp.max(lg, axis=0, keepdims=True))
    pr = e / jnp.sum(e, axis=0, keepdims=True)
    cum = pr[0:1, :]
    for r in range(1, layer + 1):
        cum = cum + pr[r:r + 1, :]
    lb = jnp.clip(cum - pr[0:1, :], 0.0, 1.0)
    return jnp.log(lb) * LOG2E, jnp.log1p(-lb) * LOG2E, 1.0 - lb


def _forget_gate(f, lb_terms):
    log2_lb, log2_1m_lb, one_m_lb = lb_terms
    f2 = f * LOG2E
    e = jnp.exp2(jnp.minimum(f2, -f2))
    w = 1.0 + e
    b = log2_1m_lb + (jnp.minimum(f2, 0.0) - jnp.log2(w))
    d = log2_lb - b
    log2_ft = jnp.maximum(log2_lb, b) + jnp.log2(1.0 + jnp.exp2(jnp.minimum(d, -d)))
    return log2_ft, one_m_lb * (jnp.where(f2 >= 0.0, e, 1.0) / w)


def _split_bf16(x):
    hi = x.astype(BF16)
    return hi, (x - hi.astype(F32)).astype(BF16)


def _hgrn_gate_out(x, h, ob, mixbuf, win_ref, ng_ref, wout_ref, gpost_ref):
    dkt = C_HEADS * C_DK
    g = _dot(h, win_ref[:, 3 * dkt:4 * dkt])
    ng = ng_ref[...]
    for hh in range(C_HEADS):
        sl = slice(hh * C_DK, (hh + 1) * C_DK)
        mixbuf[:, sl] = (_rms(ob[:, sl], ng) * (g[:, sl] * _sigmoid(g[:, sl]))).astype(BF16)
    out = _dot(mixbuf[...], wout_ref[...])
    return x + _rms(out, gpost_ref[...])


def _hgrn_exact_tile(q, k_in, log2_ft, vb, st, ob, sq, sk, slg):
    ts = sq.shape[0]
    sq[...] = q
    sk[...] = k_in
    slg[...] = log2_ft
    ob[...] = jnp.zeros(ob.shape, F32)
    grp = 2 * SUBLANES
    row = lax.broadcasted_iota(jnp.int32, (grp, 1), 0)

    def body(t, carry):
        t0 = pl.multiple_of(lax.bitwise_and(t, -grp), grp)
        sel = row == lax.bitwise_and(t, grp - 1)
        for hh in range(C_HEADS):
            sl = slice(hh * C_DK, (hh + 1) * C_DK)
            rows_of = lambda ref: jnp.where(sel, ref[pl.ds(t0, grp), sl].astype(F32), 0.0)
            pick = lambda ref: rows_of(ref).astype(BF16)
            log2_f = jnp.sum(rows_of(slg), axis=0, keepdims=True)
            s_new = st[hh] * jnp.exp2(log2_f) + _dot_tn(pick(vb), pick(sk))
            st[hh] = s_new
            ob[pl.ds(t0, grp), sl] += _dot_nt(pick(sq), s_new.astype(BF16))
        return carry

    lax.fori_loop(0, ts, body, 0)


def _hgrn_prompt_kernel(layer, x_ref, gpre_ref, gpost_ref, win_ref, lbl_ref, ng_ref, wout_ref,
                        y_ref, sout_ref, st, st0, qtb, ktb, qib, klb, eglb, vb, ob, mixbuf, sq, sk, slg):
    s = pl.program_id(1)
    ts = x_ref.shape[0]
    dkt = C_HEADS * C_DK
    ck = HGRN_CHUNK

    @pl.when(s == 0)
    def _():
        st[...] = jnp.zeros(st.shape, F32)

    x = x_ref[...]
    h = _rms(x, gpre_ref[...]).astype(BF16)
    f = _dot(h, win_ref[:, dkt:2 * dkt])
    log2_ft, k_in = _forget_gate(f, _lower_bound_terms(lbl_ref, layer))
    q = _dot(h, win_ref[:, 0:dkt])
    vb[...] = _dot(h, win_ref[:, 2 * dkt:3 * dkt]).astype(BF16)

    ri = lax.broadcasted_iota(jnp.int32, (ck, ck), 0)
    ci = lax.broadcasted_iota(jnp.int32, (ck, ck), 1)
    causal = ri >= ci
    tri = jnp.where(causal, 1.0, 0.0).astype(BF16)
    mid = ck // 2 - 1
    nck = ts // ck

    spread = jnp.zeros((1, dkt), F32)
    for c in range(nck):
        rows = slice(c * ck, (c + 1) * ck)
        hi, lo = _split_bf16(log2_ft[rows, :])
        gcum = _dot(tri, hi) + _dot(tri, lo)
        gr = gcum[mid:mid + 1, :]
        gl = gcum[ck - 1:ck, :]
        spread = jnp.maximum(spread, jnp.maximum(gcum[0:1, :] - gr, gr - gl))
        qc = q[rows, :]
        kc = k_in[rows, :]
        qtb[rows, :] = (qc * jnp.exp2(gcum - gr)).astype(BF16)
        ktb[rows, :] = (kc * jnp.exp2(gr - gcum)).astype(BF16)
        qib[rows, :] = (qc * jnp.exp2(gcum)).astype(BF16)
        klb[rows, :] = (kc * jnp.exp2(gl - gcum)).astype(BF16)
        eglb[c] = jnp.exp2(gl)

    st0[...] = st[...]
    for c in range(nck):
        rows = slice(c * ck, (c + 1) * ck)
        for hh in range(C_HEADS):
            sl = slice(hh * C_DK, (hh + 1) * C_DK)
            vh = vb[rows, sl]
            att = jnp.where(causal, _dot_nt(qtb[rows, sl], ktb[rows, sl]), 0.0).astype(BF16)
            st_h = st[hh]
            ob[rows, sl] = _dot(jnp.concatenate([qib[rows, sl], att], axis=1),
                                jnp.concatenate([st_h.astype(BF16).T, vh], axis=0))
            st[hh] = eglb[c, :, sl] * st_h + _dot_tn(vh, klb[rows, sl])

    y_ref[...] = _hgrn_gate_out(x, h, ob, mixbuf, win_ref, ng_ref, wout_ref, gpost_ref)

    @pl.when(jnp.max(spread) > DECAY_LOG2_LIMIT)
    def _():
        st[...] = st0[...]
        _hgrn_exact_tile(q, k_in, log2_ft, vb, st, ob, sq, sk, slg)
        y_ref[...] = _hgrn_gate_out(x, h, ob, mixbuf, win_ref, ng_ref, wout_ref, gpost_ref)

    @pl.when(s == pl.num_programs(1) - 1)
    def _():
        for hh in range(C_HEADS):
            sout_ref[hh] = st[hh].T


def _hgrn_prompt(layer, x, gpre, gpost, win, lbl, ng, wout):
    b, seq, d = x.shape
    ts = HGRN_TILE
    assert seq % ts == 0 and ts % HGRN_CHUNK == 0, (seq, ts)
    dkt = C_HEADS * C_DK
    tile = pl.BlockSpec((None, ts, d), lambda i, j: (i, j, 0))
    return pl.pallas_call(
        functools.partial(_hgrn_prompt_kernel, layer),
        grid=(b, seq // ts),
        in_specs=[tile, _resident(gpre.shape), _resident(gpost.shape), _resident(win.shape),
                  _resident(lbl.shape), _resident(ng.shape), _resident(wout.shape)],
        out_specs=[tile, pl.BlockSpec((None, C_HEADS, C_DK, C_DK), lambda i, j: (i, 0, 0, 0))],
        out_shape=[jax.ShapeDtypeStruct((b, seq, d), F32),
                   jax.ShapeDtypeStruct((b, C_HEADS, C_DK, C_DK), F32)],
        scratch_shapes=[pltpu.VMEM((C_HEADS, C_DK, C_DK), F32), pltpu.VMEM((C_HEADS, C_DK, C_DK), F32),
                        pltpu.VMEM((ts, dkt), BF16), pltpu.VMEM((ts, dkt), BF16),
                        pltpu.VMEM((ts, dkt), BF16), pltpu.VMEM((ts, dkt), BF16),
                        pltpu.VMEM((ts // HGRN_CHUNK, 1, dkt), F32),
                        pltpu.VMEM((ts, dkt), BF16), pltpu.VMEM((ts, dkt), F32),
                        pltpu.VMEM((ts, dkt), BF16),
                        pltpu.VMEM((ts, dkt), F32), pltpu.VMEM((ts, dkt), F32), pltpu.VMEM((ts, dkt), F32)],
        compiler_params=_params(("arbitrary", "arbitrary")),
        name="hgrn_prompt",
    )(x, gpre, gpost, win, lbl, ng, wout)


def _mix_ab_sample_kernel(seg, x_ref, prev0_ref, prev1_ref, kc_ref, vc_ref, gpre_ref, gpost_ref, win_ref,
                          cw_ref, sink_ref, wout_ref,
                          y_ref, conv0_ref, conv1_ref, kout_ref, vout_ref,
                          ubuf, fix1, fix2, qbuf, knew_s, vnew_s, yabuf, ybbuf, kall, vall, bias_s):
    i = pl.program_id(0)
    rows = x_ref.shape[0]
    aw = cw_ref.shape[1]
    nq = GROUP * KV_HEADS * HEAD_DIM
    nkv = KV_HEADS * HEAD_DIM
    pair = 2 * seg
    qrows = KV_HEADS * GROUP * pair
    tk = WINDOW + pair

    @pl.when(i == 0)
    def _():
        x = x_ref[...]
        h = _rms(x, gpre_ref[...]).astype(BF16)
        t = lax.broadcasted_iota(jnp.int32, (rows, 1), 0) % seg
        u = _dot(h, win_ref[:, 2 * aw:3 * aw]) * _dot(h, win_ref[:, 0:aw])
        ubuf[:, 0:SUBLANES, :] = jnp.zeros((ubuf.shape[0], SUBLANES, LANES), F32)
        fix1[...] = jnp.zeros(fix1.shape, F32)
        fix2[...] = jnp.zeros(fix2.shape, F32)
        um1, um2 = _segment_conv_taps(u, t, seg, prev0_ref, prev1_ref, ubuf, fix1, fix2)
        conv0_ref[...], conv1_ref[...] = _segment_tails(ubuf, rows // seg, seg)
        cw = cw_ref[...]
        conv = cw[2:3, :] * u + cw[1:2, :] * um1 + cw[0:1, :] * um2
        yabuf[...] = (_dot(h, win_ref[:, aw:2 * aw]) * conv).astype(BF16)
        q0 = 3 * aw
        qbuf[...] = _dot(h, win_ref[:, q0:q0 + nq]) * (HEAD_DIM ** -0.5 * LOG2E)
        kv_new = _dot(h, win_ref[:, q0 + nq:q0 + nq + 2 * nkv])
        knew_s[...] = kv_new[:, 0:nkv]
        vnew_s[...] = kv_new[:, nkv:2 * nkv]

        r = lax.broadcasted_iota(jnp.int32, (qrows, tk), 0)
        j = lax.broadcasted_iota(jnp.int32, (qrows, tk), 1)
        dist = r % seg + WINDOW - j
        valid = (dist >= 0) & (dist < WINDOW) & (j < WINDOW + seg)
        rc = lax.broadcasted_iota(jnp.int32, (qrows, 1), 0)
        head = (rc // (GROUP * pair)) * GROUP + (rc // pair) % GROUP
        slope = jnp.zeros((qrows, 1), F32)
        for hidx in range(KV_HEADS * GROUP):
            slope = jnp.where(head == hidx, _head_slope(hidx) * LOG2E, slope)
        bias_s[...] = jnp.where(valid, -slope * dist.astype(F32), NEG)
        vall[:, :, nkv:2 * nkv] = jnp.ones((2, tk, nkv), F32)

    nblk = kc_ref.shape[0]
    lane = lax.broadcasted_iota(jnp.int32, (1, nkv), 1)
    lo = lane < HEAD_DIM
    rq = lax.broadcasted_iota(jnp.int32, (qrows, 1), 0)
    first_seq = (rq // seg) % 2 == 0

    def body(p, carry):
        r0 = pl.multiple_of((i * nblk + 2 * p) * seg, pair)
        q8 = qbuf[pl.ds(r0, pair), :]
        qg = jnp.concatenate([q8[:, g * nkv:(g + 1) * nkv] for g in range(GROUP)], axis=0)
        qs = jnp.concatenate([jnp.where(lo, qg, 0.0), jnp.where(lo, 0.0, qg)], axis=0).astype(BF16)
        knew = knew_s[pl.ds(r0, pair), :]
        vnew = vnew_s[pl.ds(r0, pair), :]
        sc, ov = [], []
        for a in range(2):
            own_first = (lambda z: z) if a == 0 else (lambda z: jnp.concatenate([z[seg:], z[:seg]], axis=0))
            kall[a, 0:WINDOW, :] = kc_ref[2 * p + a]
            kall[a, WINDOW:tk, :] = own_first(knew)
            vall[a, 0:WINDOW, 0:nkv] = vc_ref[2 * p + a]
            vall[a, WINDOW:tk, 0:nkv] = own_first(vnew)
            kout_ref[2 * p + a] = kall[a, seg:seg + WINDOW, :]
            vout_ref[2 * p + a] = vall[a, seg:seg + WINDOW, 0:nkv]
            sc.append(_dot_nt(qs, kall[a].astype(BF16)))
        s2 = jnp.where(first_seq, sc[0], sc[1]) + bias_s[...]
        ps, sink_terms = [], []
        for hidx in range(KV_HEADS * GROUP):
            sh = s2[hidx * pair:(hidx + 1) * pair, :]
            sink = sink_ref[hidx] * LOG2E
            m = jnp.maximum(jnp.max(sh, axis=-1, keepdims=True), sink)
            ps.append(jnp.exp2(sh - m))
            sink_terms.append(jnp.broadcast_to(jnp.exp2(sink - m), (pair, nkv)))
        prb = jnp.concatenate(ps, axis=0).astype(BF16)
        for a in range(2):
            ov.append(_dot(prb, vall[a].astype(BF16)))
        pv = jnp.where(first_seq, ov[0], ov[1])
        o = pv[:, 0:nkv] / (pv[:, nkv:2 * nkv] + jnp.concatenate(sink_terms, axis=0))
        og = jnp.where(lo, o[0:GROUP * pair, :], o[GROUP * pair:, :])
        for g in range(GROUP):
            ybbuf[pl.ds(r0, pair), g * nkv:(g + 1) * nkv] = og[g * pair:(g + 1) * pair, :]
        return carry

    lax.fori_loop(0, nblk // 2, body, 0, unroll=True)

    @pl.when(i == pl.num_programs(0) - 1)
    def _():
        out = _dot(yabuf[...], wout_ref[0:aw, :]) + _dot(ybbuf[...].astype(BF16), wout_ref[aw:, :])
        y_ref[...] = x_ref[...] + _rms(out, gpost_ref[...])


def _mix_ab_sample(x, conv_prev, kcache, vcache, seg, gpre, gpost, win, cw, sinks, wout):
    rows, d = x.shape
    nseq = rows // seg
    aw = cw.shape[1]
    nq = GROUP * KV_HEADS * HEAD_DIM
    nkv = KV_HEADS * HEAD_DIM
    sb = SAMPLE_SEQ_BLOCK
    pair = 2 * seg
    qrows = KV_HEADS * GROUP * pair
    tk = WINDOW + pair
    full = lambda shape: pl.BlockSpec(shape, lambda i: (0,) * len(shape))
    cache = pl.BlockSpec((sb, WINDOW, nkv), lambda i: (i, 0, 0))
    prev0, prev1 = conv_prev[:, 0], conv_prev[:, 1]
    y, conv0, conv1, k_new, v_new = pl.pallas_call(
        functools.partial(_mix_ab_sample_kernel, seg),
        grid=(nseq // sb,),
        in_specs=[_resident(x.shape), _resident(prev0.shape), _resident(prev1.shape), cache, cache,
                  _resident(gpre.shape), _resident(gpost.shape), _resident(win.shape), _resident(cw.shape),
                  pl.BlockSpec(memory_space=pltpu.SMEM), _resident(wout.shape)],
        out_specs=[full((rows, d)), full((nseq, aw)), full((nseq, aw)), cache, cache],
        out_shape=[jax.ShapeDtypeStruct((rows, d), F32), jax.ShapeDtypeStruct((nseq, aw), F32),
                   jax.ShapeDtypeStruct((nseq, aw), F32), jax.ShapeDtypeStruct(kcache.shape, F32),
                   jax.ShapeDtypeStruct(vcache.shape, F32)],
        scratch_shapes=[pltpu.VMEM((aw // LANES, rows + SUBLANES, LANES), F32),
                        pltpu.VMEM((aw // LANES, rows, LANES), F32),
                        pltpu.VMEM((aw // LANES, rows, LANES), F32),
                        pltpu.VMEM((rows, nq), F32),
                        pltpu.VMEM((rows, nkv), F32),
                        pltpu.VMEM((rows, nkv), F32),
                        pltpu.VMEM((rows, aw), BF16),
                        pltpu.VMEM((rows, nq), F32),
                        pltpu.VMEM((2, tk, nkv), F32),
                        pltpu.VMEM((2, tk, 2 * nkv), F32),
                        pltpu.VMEM((qrows, tk), F32)],
        compiler_params=_params(("arbitrary",)),
        name="mix_ab_sample",
    )(x, prev0, prev1, kcache, vcache, gpre, gpost, win, cw, sinks.reshape(-1), wout)
    return y, jnp.stack([conv0, conv1], axis=1), k_new, v_new


def _hgrn_sample_kernel(layer, seg, x_ref, s0_ref, gpre_ref, gpost_ref, win_ref, lbl_ref, ng_ref, wout_ref,
                        y_ref, s1_ref, qb, kb, vb, gb, ob, mixbuf):
    i = pl.program_id(0)
    rows = x_ref.shape[0]
    dkt = C_HEADS * C_DK
    pair = 2 * seg

    @pl.when(i == 0)
    def _():
        x = x_ref[...]
        h = _rms(x, gpre_ref[...]).astype(BF16)
        f = _dot(h, win_ref[:, dkt:2 * dkt])
        log2_ft, kb[...] = _forget_gate(f, _lower_bound_terms(lbl_ref, layer))
        hi, lo = _split_bf16(log2_ft)
        ri = lax.broadcasted_iota(jnp.int32, (rows, rows), 0)
        ci = lax.broadcasted_iota(jnp.int32, (rows, rows), 1)
        tri = jnp.where((ri // seg == ci // seg) & (ri >= ci), 1.0, 0.0).astype(BF16)
        gb[...] = _dot(tri, hi) + _dot(tri, lo)
        qb[...] = _dot(h, win_ref[:, 0:dkt])
        vb[...] = _dot(h, win_ref[:, 2 * dkt:3 * dkt])

    nblk = s0_ref.shape[0]
    rr = lax.broadcasted_iota(jnp.int32, (pair, 1), 0)
    first_seq = rr < seg
    second_seq = rr >= seg
    ri = lax.broadcasted_iota(jnp.int32, (pair, pair), 0)
    ci = lax.broadcasted_iota(jnp.int32, (pair, pair), 1)
    key_s = [(ri // seg == ci // seg) & (ci % seg == sk) & (ri % seg >= sk) for sk in range(seg)]

    def body(p, carry):
        r0 = pl.multiple_of((i * nblk + 2 * p) * seg, pair)
        for hh in range(C_HEADS):
            sl = slice(hh * C_DK, (hh + 1) * C_DK)
            gh = gb[pl.ds(r0, pair), sl]
            g_row = lambda t: jnp.where(first_seq, gh[t:t + 1, :], gh[seg + t:seg + t + 1, :])
            gl = g_row(seg - 1)
            qh = qb[pl.ds(r0, pair), sl]
            kh = kb[pl.ds(r0, pair), sl]
            vh = vb[pl.ds(r0, pair), sl].astype(BF16)
            q_rel = [(qh * jnp.exp2(gh - g_row(sk))).astype(BF16) for sk in range(seg - 1)] + [qh.astype(BF16)]
            rel = _dot_nt(jnp.concatenate(q_rel, axis=0), kh.astype(BF16))
            att = jnp.where(key_s[0], rel[0:pair, :], 0.0)
            for sk in range(1, seg):
                att = att + jnp.where(key_s[sk], rel[sk * pair:(sk + 1) * pair, :], 0.0)
            att = att.astype(BF16)
            qi = (qh * jnp.exp2(gh)).astype(BF16)
            kl = kh * jnp.exp2(gl - gh)
            o_prev = []
            for a in range(2):
                s_a = s0_ref[2 * p + a, hh]
                o_prev.append(_dot(qi, s_a.astype(BF16)))
                kl_a = jnp.where(first_seq if a == 0 else second_seq, kl, 0.0).astype(BF16)
                e = jnp.exp2(gh[(a + 1) * seg - 1:(a + 1) * seg, :])
                e1 = e.astype(BF16).astype(F32)
                e2 = (e - e1).astype(BF16).astype(F32)
                e3 = (e - e1 - e2).astype(BF16).astype(F32)
                dec_l = jnp.where(rr == 0, e1, jnp.where(rr == 1, e2, jnp.where(rr == 2, e3, 0.0)))
                dec = _dot_tn(dec_l.astype(BF16), jnp.ones((pair, C_DK), BF16))
                s1_ref[2 * p + a, hh] = dec * s_a + _dot_tn(kl_a, vh)
            ob[pl.ds(r0, pair), sl] = _dot(att, vh) + jnp.where(first_seq, o_prev[0], o_prev[1])
        return carry

    lax.fori_loop(0, nblk // 2, body, 0, unroll=True)

    @pl.when(i == pl.num_programs(0) - 1)
    def _():
        x = x_ref[...]
        h = _rms(x, gpre_ref[...]).astype(BF16)
        y_ref[...] = _hgrn_gate_out(x, h, ob, mixbuf, win_ref, ng_ref, wout_ref, gpost_ref)


def _hgrn_sample(layer, x, s0, seg, gpre, gpost, win, lbl, ng, wout):
    rows, d = x.shape
    assert 2 * seg == SUBLANES, "two sequences per iteration must fill one f32 vreg of sublanes"
    nseq = rows // seg
    dkt = C_HEADS * C_DK
    sb = SAMPLE_SEQ_BLOCK
    full = lambda shape: pl.BlockSpec(shape, lambda i: (0,) * len(shape))
    state = pl.BlockSpec((sb, C_HEADS, C_DK, C_DK), lambda i: (i, 0, 0, 0))
    return pl.pallas_call(
        functools.partial(_hgrn_sample_kernel, layer, seg),
        grid=(nseq // sb,),
        in_specs=[_resident(x.shape), state, _resident(gpre.shape), _resident(gpost.shape), _resident(win.shape),
                  _resident(lbl.shape), _resident(ng.shape), _resident(wout.shape)],
        out_specs=[full((rows, d)), state],
        out_shape=[jax.ShapeDtypeStruct((rows, d), F32), jax.ShapeDtypeStruct(s0.shape, F32)],
        scratch_shapes=[pltpu.VMEM((rows, dkt), F32), pltpu.VMEM((rows, dkt), F32),
                        pltpu.VMEM((rows, dkt), F32), pltpu.VMEM((rows, dkt), F32),
                        pltpu.VMEM((rows, dkt), F32), pltpu.VMEM((rows, dkt), BF16)],
        compiler_params=_params(("arbitrary",)),
        name="hgrn_sample",
    )(x, s0, gpre, gpost, win, lbl, ng, wout)


def kernel(x_prompt, x_sample, state_conv_a, cache_swa_k, cache_swa_v, state_hgrn, state_ffn_conv, norm_mix_pre, norm_mix_post, norm_ffn_pre, norm_ffn_post, w_in_ab, conv_a_w, attn_sinks, w_out_ab, w_in_c, hgrn_lb_logits, hgrn_norm_g, w_out_c, w_ffn_up, ffn_conv_w, w_ffn_down):
    bp, _, d = x_prompt.shape
    ns, seg, _ = x_sample.shape
    aw = conv_a_w.shape[-1]
    nq = GROUP * KV_HEADS * HEAD_DIM
    nkv = KV_HEADS * HEAD_DIM

    q0 = 3 * aw
    win_ab = w_in_ab[0].astype(BF16)
    wq = win_ab[:, q0:q0 + nq].reshape(d, KV_HEADS, GROUP, HEAD_DIM).swapaxes(1, 2).reshape(d, nq)
    win_ab = jnp.concatenate([win_ab[:, :q0], wq, win_ab[:, q0 + nq:]], axis=1)
    wout_ab = w_out_ab[0].astype(BF16)
    wo = wout_ab[aw:].reshape(KV_HEADS, GROUP, HEAD_DIM, d).swapaxes(0, 1).reshape(nq, d)
    wout_ab = jnp.concatenate([wout_ab[:aw], wo], axis=0)
    win_c = w_in_c[0].astype(BF16)
    wout_c = w_out_c[0].astype(BF16)
    wup = w_ffn_up.astype(BF16)
    wdown = w_ffn_down.astype(BF16)
    row = lambda a: a.reshape(1, -1)
    sinks = row(attn_sinks[0])
    ng = row(hgrn_norm_g[0])

    xs = x_sample.reshape(ns * seg, d)
    kc = cache_swa_k[0].reshape(ns, WINDOW, nkv)
    vc = cache_swa_v[0].reshape(ns, WINDOW, nkv)
    s1, conv_s, k_s, v_s = _mix_ab_sample(xs, state_conv_a[0], kc, vc, seg, row(norm_mix_pre[0]),
                                          row(norm_mix_post[0]), win_ab, conv_a_w[0], sinks, wout_ab)
    s2, ffn_s0 = _ffn_sample(0, s1, state_ffn_conv[0], seg, row(norm_ffn_pre[0]), row(norm_ffn_post[0]),
                             wup, ffn_conv_w[0], wdown)
    s3, hgrn_s = _hgrn_sample(1, s2, state_hgrn[0], seg, row(norm_mix_pre[1]), row(norm_mix_post[1]),
                              win_c, hgrn_lb_logits, ng, wout_c)
    y_sample, ffn_s1 = _ffn_sample(1, s3, state_ffn_conv[1], seg, row(norm_ffn_pre[1]), row(norm_ffn_post[1]),
                                   wup, ffn_conv_w[1], wdown)

    x1, conv_p, k_p, v_p = _mix_ab_prompt(x_prompt, row(norm_mix_pre[0]), row(norm_mix_post[0]), win_ab,
                                          conv_a_w[0], sinks, wout_ab)
    x2, ffn_p0 = _ffn_prompt(0, x1, row(norm_ffn_pre[0]), row(norm_ffn_post[0]), wup, ffn_conv_w[0], wdown)
    x3, hgrn_p = _hgrn_prompt(1, x2, row(norm_mix_pre[1]), row(norm_mix_post[1]), win_c, hgrn_lb_logits,
                              ng, wout_c)
    y_prompt, ffn_p1 = _ffn_prompt(1, x3, row(norm_ffn_pre[1]), row(norm_ffn_post[1]), wup, ffn_conv_w[1], wdown)

    kvshape = (KV_HEADS, HEAD_DIM)
    return (y_prompt, y_sample.reshape(ns, seg, d),
            conv_p[None], conv_s[None],
            k_p.reshape(1, bp, WINDOW, *kvshape), k_s.reshape(1, ns, WINDOW, *kvshape),
            v_p.reshape(1, bp, WINDOW, *kvshape), v_s.reshape(1, ns, WINDOW, *kvshape),
            hgrn_p[None], hgrn_s[None],
            jnp.stack([ffn_p0, ffn_p1]), jnp.stack([ffn_s0, ffn_s1]))
```
